```python
import numpy as np
import jax, jax.numpy as jnp
from jax import lax

D_MODEL = 1024
BATCH = 16
SEQ = 2048
DEPTH = 1

RET_HEADS = 4
RET_DK = 128
RET_DV = 256
RET_CHUNK = 128
NSA_HEADS = 8
NSA_KV_GROUPS = 2
NSA_GROUP_HEADS = NSA_HEADS // NSA_KV_GROUPS
NSA_DK = 64
CMP_BLOCK = 32
CMP_STRIDE = 16
CMP_HIDDEN = 256
SEL_BLOCK = 64
SEL_TOPK = 8
WINDOW = 512
D_FF = 2816
CONV_W = 3

EPS = 1e-6
NEG = -1e30
FORCE = 1e9

IN_SIZES = (RET_HEADS * RET_DK, RET_HEADS * RET_DK, RET_HEADS * RET_DV, RET_HEADS * RET_DV,
            NSA_HEADS * NSA_DK,
            NSA_KV_GROUPS * NSA_DK, NSA_KV_GROUPS * NSA_DK, NSA_KV_GROUPS * NSA_DK,
            NSA_KV_GROUPS * NSA_DK, NSA_KV_GROUPS * NSA_DK, NSA_KV_GROUPS * NSA_DK,
            3 * NSA_HEADS, D_MODEL, D_MODEL)
N_IN = sum(IN_SIZES)

kernel_name = "hybrid_retention_nsa_convffn"


def rmsnorm(x, g):
    xf = x.astype(jnp.float32)
    y = xf * lax.rsqrt(jnp.mean(xf * xf, axis=-1, keepdims=True) + EPS) * g.astype(jnp.float32)
    return y.astype(x.dtype)


def retention(q, k, v, g, gn_g):
    B, S = q.shape[0], q.shape[1]
    H, C = RET_HEADS, RET_CHUNK
    NCH = S // C
    lg = jnp.log1p(-jnp.exp2(-5.0 - jnp.arange(H, dtype=jnp.float32)))
    pos = jnp.arange(C, dtype=jnp.float32)
    diff = pos[:, None] - pos[None, :]
    dmask = jnp.where(diff >= 0, jnp.exp(lg[:, None, None] * jnp.maximum(diff, 0.0)), 0.0)
    zeta = jnp.exp(lg[:, None] * (C - 1 - pos))
    xi = jnp.exp(lg[:, None] * (pos + 1.0))

    def to_chunks(t):
        return t.reshape(B, NCH, C, H, t.shape[-1]).transpose(0, 3, 1, 2, 4)

    qc = to_chunks(q)
    kc = to_chunks(k) * (RET_DK ** -0.5)
    vc = to_chunks(v)
    inner = jnp.einsum('bhncd,bhnmd->bhncm', qc, kc) * dmask[:, None]
    o_inner = jnp.einsum('bhncm,bhnme->bhnce', inner, vc)
    kv = jnp.einsum('bhnmd,bhnme->nbhde', kc * zeta[:, None, :, None], vc)
    chunk_decay = jnp.exp(lg * C).astype(kv.dtype)[:, None, None]

    def step(R, kv_n):
        return chunk_decay * R + kv_n, R

    R0 = jnp.zeros(kv.shape[1:], kv.dtype)
    _, R_prev = lax.scan(step, R0, kv)
    o_cross = jnp.einsum('bhncd,nbhde->bhnce', qc * xi[:, None, :, None], R_prev)
    o = (o_inner + o_cross).transpose(0, 2, 3, 1, 4).reshape(B, S, H, RET_DV)
    of = o.astype(jnp.float32)
    mu = jnp.mean(of, axis=-1, keepdims=True)
    var = jnp.mean(jnp.square(of - mu), axis=-1, keepdims=True)
    on = ((of - mu) * lax.rsqrt(var + EPS)).reshape(B, S, H * RET_DV) * gn_g.astype(jnp.float32)
    return (jax.nn.silu(g.astype(jnp.float32)) * on).astype(q.dtype)


def compress(t, pos_emb, w1, b1, w2):
    B, G, S, dk = t.shape
    npc = CMP_BLOCK // CMP_STRIDE
    NP = S // CMP_STRIDE
    pieces = t.reshape(B, G, NP, CMP_STRIDE, dk)
    blocks = jnp.concatenate([pieces[:, :, i:NP - (npc - 1) + i] for i in range(npc)], axis=3)
    blocks = (blocks + pos_emb).reshape(B, G, NP - npc + 1, CMP_BLOCK * dk)
    return jax.nn.gelu(blocks @ w1 + b1) @ w2


def nsa(q, kcr, vcr, ks, vs, kw, vw, gates,
        cmp_pos_k, cmp_w1_k, cmp_b1_k, cmp_w2_k, cmp_pos_v, cmp_w1_v, cmp_b1_v, cmp_w2_v):
    B, S = q.shape[0], q.shape[1]
    G, R, dk, T = NSA_KV_GROUPS, NSA_GROUP_HEADS, NSA_DK, SEL_BLOCK
    NS = S // SEL_BLOCK
    NQ = S // T
    k_eff = min(SEL_TOPK, NS)
    scale = dk ** -0.5
    qh = q.reshape(B, S, G, R, dk).transpose(0, 2, 3, 1, 4)

    def kvh(t):
        return t.reshape(B, S, G, dk).transpose(0, 2, 1, 3)

    kcr, vcr, ks, vs, kw, vw = [kvh(t) for t in (kcr, vcr, ks, vs, kw, vw)]
    slopes = jnp.exp2(-(jnp.arange(NSA_HEADS, dtype=jnp.float32) + 1.0)).reshape(G, R)
    tpos = jnp.arange(S, dtype=jnp.float32)

    kcmp = compress(kcr, cmp_pos_k, cmp_w1_k, cmp_b1_k, cmp_w2_k)
    vcmp = compress(vcr, cmp_pos_v, cmp_w1_v, cmp_b1_v, cmp_w2_v)
    NC = kcmp.shape[2]
    c_start = jnp.arange(NC) * CMP_STRIDE
    c_end = (c_start + CMP_BLOCK - 1).astype(jnp.float32)
    dcmp = tpos[:, None] - c_end[None, :]
    cmask = dcmp >= 0
    s = jnp.einsum('bgrtd,bgcd->bgrtc', qh, kcmp).astype(jnp.float32) * scale
    s = jnp.where(cmask, s - slopes[None, :, :, None, None] * dcmp, NEG)
    p_cmp = jax.nn.softmax(s, axis=-1) * cmask
    o_cmp = jnp.einsum('bgrtc,bgcd->bgrtd', p_cmp.astype(vcmp.dtype), vcmp)

    j_start = jnp.arange(NS) * SEL_BLOCK
    overlap = ((c_start[:, None] < j_start[None, :] + SEL_BLOCK) &
               (c_start[:, None] + CMP_BLOCK > j_start[None, :])).astype(jnp.float32)
    imp = jnp.einsum('bgrtc,cj->bgtj', p_cmp, overlap)
    cur = (jnp.arange(S) // SEL_BLOCK)[:, None]
    jj = jnp.arange(NS)[None, :]
    imp = jnp.where((jj == 0) | (jj == cur) | (jj == cur - 1), FORCE, imp)
    imp = jnp.where(jj > cur, -FORCE, imp)
    _, sel_idx = lax.top_k(imp, k_eff)

    ks_blk = ks.reshape(B, G, NS, SEL_BLOCK, dk)
    vs_blk = vs.reshape(B, G, NS, SEL_BLOCK, dk)
    kw_pad = jnp.pad(kw, ((0, 0), (0, 0), (WINDOW, 0), (0, 0)))
    vw_pad = jnp.pad(vw, ((0, 0), (0, 0), (WINDOW, 0), (0, 0)))
    bi = jnp.arange(B)[:, None, None, None]
    gi = jnp.arange(G)[None, :, None, None]
    sl = slopes[None, :, :, None]

    def block_fn(n):
        t0 = n * T
        qb = lax.dynamic_slice_in_dim(qh, t0, T, axis=3)
        tb = t0 + jnp.arange(T)
        idb = lax.dynamic_slice_in_dim(sel_idx, t0, T, axis=2)
        kb = ks_blk[bi, gi, idb]
        vb = vs_blk[bi, gi, idb]
        kpos = idb[..., None] * SEL_BLOCK + jnp.arange(SEL_BLOCK)
        dsel = (tb[None, None, :, None, None] - kpos).astype(jnp.float32)[:, :, None]
        ss = jnp.einsum('bgrtd,bgtnsd->bgrtns', qb, kb).astype(jnp.float32) * scale
        ss = jnp.where(dsel >= 0, ss - sl[..., None, None] * dsel, NEG)
        ps = jax.nn.softmax(ss.reshape(B, G, R, T, -1), axis=-1).reshape(ss.shape)
        o_sel = jnp.einsum('bgrtns,bgtnsd->bgrtd', ps.astype(vb.dtype), vb)
        kwb = lax.dynamic_slice_in_dim(kw_pad, t0, WINDOW + T, axis=2)
        vwb = lax.dynamic_slice_in_dim(vw_pad, t0, WINDOW + T, axis=2)
        wpos = t0 - WINDOW + jnp.arange(WINDOW + T)
        dw = tb[:, None] - wpos[None, :]
        wmask = (dw >= 0) & (dw < WINDOW) & (wpos[None, :] >= 0)
        sw = jnp.einsum('bgrtd,bgsd->bgrts', qb, kwb).astype(jnp.float32) * scale
        sw = jnp.where(wmask, sw - sl[..., None] * dw.astype(jnp.float32), NEG)
        pw = jax.nn.softmax(sw, axis=-1)
        o_win = jnp.einsum('bgrts,bgsd->bgrtd', pw.astype(vwb.dtype), vwb)
        return o_sel, o_win

    o_sel, o_win = lax.map(block_fn, jnp.arange(NQ))
    o_sel = o_sel.transpose(1, 2, 3, 0, 4, 5).reshape(B, G, R, S, dk)
    o_win = o_win.transpose(1, 2, 3, 0, 4, 5).reshape(B, G, R, S, dk)

    gt = jax.nn.sigmoid(gates.reshape(B, S, 3, G, R).transpose(2, 0, 3, 4, 1))[..., None]
    o = gt[0] * o_cmp + gt[1] * o_sel + gt[2] * o_win
    return o.transpose(0, 3, 1, 2, 4).reshape(B, S, NSA_HEADS * dk).astype(q.dtype)


def conv_ffn(h, w_up, conv_w, conv_b, w_down):
    S = h.shape[1]
    a, b = jnp.split(h @ w_up, 2, axis=-1)
    a_pad = jnp.pad(a, ((0, 0), (CONV_W - 1, 0), (0, 0)))
    ac = conv_b + sum(conv_w[i] * a_pad[:, i:i + S] for i in range(CONV_W))
    return (jax.nn.gelu(ac) * b) @ w_down


def setup_inputs(seed: int = 0) -> dict:
    key = jax.random.key(seed)
    ks = jax.random.split(key, 24)
    f32 = jnp.float32

    def nrm(k, shape, fan_in):
        return jax.random.normal(k, shape, f32) * (fan_in ** -0.5)

    def gain(k, shape):
        return 1.0 + 0.01 * jax.random.normal(k, shape, f32)

    L = DEPTH
    return {
        "x": jax.random.normal(ks[0], (BATCH, SEQ, D_MODEL), f32),
        "norm_mix": gain(ks[1], (L, D_MODEL)),
        "w_in": nrm(ks[2], (L, D_MODEL, N_IN), D_MODEL),
        "ret_gn_g": gain(ks[3], (L, RET_HEADS * RET_DV)),
        "cmp_pos_k": 0.02 * jax.random.normal(ks[4], (L, CMP_BLOCK, NSA_DK), f32),
        "cmp_w1_k": nrm(ks[5], (L, CMP_BLOCK * NSA_DK, CMP_HIDDEN), CMP_BLOCK * NSA_DK),
        "cmp_b1_k": 0.01 * jax.random.normal(ks[6], (L, CMP_HIDDEN), f32),
        "cmp_w2_k": nrm(ks[7], (L, CMP_HIDDEN, NSA_DK), CMP_HIDDEN),
        "cmp_pos_v": 0.02 * jax.random.normal(ks[8], (L, CMP_BLOCK, NSA_DK), f32),
        "cmp_w1_v": nrm(ks[9], (L, CMP_BLOCK * NSA_DK, CMP_HIDDEN), CMP_BLOCK * NSA_DK),
        "cmp_b1_v": 0.01 * jax.random.normal(ks[10], (L, CMP_HIDDEN), f32),
        "cmp_w2_v": nrm(ks[11], (L, CMP_HIDDEN, NSA_DK), CMP_HIDDEN),
        "w_ret_o": nrm(ks[12], (L, RET_HEADS * RET_DV, D_MODEL), RET_HEADS * RET_DV),
        "w_nsa_o": nrm(ks[13], (L, NSA_HEADS * NSA_DK, D_MODEL), NSA_HEADS * NSA_DK),
        "w_out": nrm(ks[14], (L, D_MODEL, D_MODEL), D_MODEL),
        "norm_ffn": gain(ks[15], (L, D_MODEL)),
        "w_up": nrm(ks[16], (L, D_MODEL, 2 * D_FF), D_MODEL),
        "conv_w": nrm(ks[17], (L, CONV_W, D_FF), CONV_W),
        "conv_b": 0.01 * jax.random.normal(ks[18], (L, D_FF), f32),
        "w_down": nrm(ks[19], (L, D_FF, D_MODEL), D_FF),
        "norm_final": gain(ks[20], (D_MODEL,)),
    }


def reference(x, norm_mix, w_in, ret_gn_g, cmp_pos_k, cmp_w1_k, cmp_b1_k, cmp_w2_k,
              cmp_pos_v, cmp_w1_v, cmp_b1_v, cmp_w2_v, w_ret_o, w_nsa_o, w_out,
              norm_ffn, w_up, conv_w, conv_b, w_down, norm_final):
    B, S, _ = x.shape
    split_at = list(np.cumsum(IN_SIZES)[:-1])
    for l in range(DEPTH):
        h = rmsnorm(x, norm_mix[l])
        proj = h @ w_in[l]
        (rq, rk, rv, rg, nq, kcr, vcr, ksl, vsl, kwn, vwn, ngate,
         ga, gb) = jnp.split(proj, split_at, axis=-1)
        y_a = retention(rq.reshape(B, S, RET_HEADS, RET_DK), rk.reshape(B, S, RET_HEADS, RET_DK),
                        rv.reshape(B, S, RET_HEADS, RET_DV), rg, ret_gn_g[l]) @ w_ret_o[l]
        y_b = nsa(nq, kcr, vcr, ksl, vsl, kwn, vwn, ngate,
                  cmp_pos_k[l], cmp_w1_k[l], cmp_b1_k[l], cmp_w2_k[l],
                  cmp_pos_v[l], cmp_w1_v[l], cmp_b1_v[l], cmp_w2_v[l]) @ w_nsa_o[l]
        merged = jax.nn.sigmoid(ga) * y_a + jax.nn.sigmoid(gb) * y_b
        x = x + merged @ w_out[l]
        x = x + conv_ffn(rmsnorm(x, norm_ffn[l]), w_up[l], conv_w[l], conv_b[l], w_down[l])
    return rmsnorm(x, norm_final)
```

```python
import functools
import math

import numpy as np
import jax
import jax.numpy as jnp
from jax import lax
from jax.experimental import pallas as pl
from jax.experimental.pallas import tpu as pltpu

F32 = jnp.float32
BF16 = jnp.bfloat16

D_MODEL = 1024
RET_HEADS = 4
RET_DK = 128
RET_DV = 256
NSA_HEADS = 8
NSA_GROUPS = 2
NSA_R = NSA_HEADS // NSA_GROUPS
NSA_DK = 64
CMP_BLOCK = 32
CMP_STRIDE = 16
CMP_HIDDEN = 256
SEL_BLOCK = 64
SEL_TOPK = 8
WINDOW = 512
D_FF = 2816
CONV_W = 3
EPS = 1e-6
NEG = -1e30
FORCE = 1e9
LOWEST = -3e38

IN_SIZES = (RET_HEADS * RET_DK, RET_HEADS * RET_DK, RET_HEADS * RET_DV, RET_HEADS * RET_DV,
            NSA_HEADS * NSA_DK,
            NSA_GROUPS * NSA_DK, NSA_GROUPS * NSA_DK, NSA_GROUPS * NSA_DK,
            NSA_GROUPS * NSA_DK, NSA_GROUPS * NSA_DK, NSA_GROUPS * NSA_DK,
            3 * NSA_HEADS, D_MODEL, D_MODEL)

LANES = 128
VMEM_LIMIT = 56 * 1024 * 1024

PB_RQ, PB_RK, PB_RV, PB_NQ, PB_KS, PB_VS, PB_KW, PB_VW = 0, 512, 1024, 2048, 2560, 2688, 2816, 2944
PB_W = 3072
PF_RG, PF_GA, PF_GB, PF_KCR, PF_VCR, PF_NG = 0, 1024, 2048, 3072, 3200, 3328
PF_W = 3456

NT_DIMS = (((1,), (1,)), ((), ()))
TN_DIMS = (((0,), (0,)), ((), ()))


def _params(*sem):
    return pltpu.CompilerParams(dimension_semantics=sem, vmem_limit_bytes=VMEM_LIMIT)


def _rms(x, g):
    return x * lax.rsqrt(jnp.mean(x * x, axis=-1, keepdims=True) + EPS) * g


def _inproj_kernel(x_ref, g_ref, w_ref, pb_ref, pf_ref):
    hb = _rms(x_ref[...], g_ref[...]).astype(BF16)
    for c0 in range(0, PB_W, 512):
        pb_ref[:, c0:c0 + 512] = jnp.dot(
            hb, w_ref[:, c0:c0 + 512], preferred_element_type=F32).astype(BF16)
    for c0 in range(0, PF_W, 512):
        c1 = min(c0 + 512, PF_W)
        pf_ref[:, c0:c1] = jnp.dot(hb, w_ref[:, PB_W + c0:PB_W + c1], preferred_element_type=F32)


def _inproj(x2, g, wp, tm):
    M = x2.shape[0]
    return pl.pallas_call(
        _inproj_kernel,
        grid=(M // tm,),
        in_specs=[pl.BlockSpec((tm, D_MODEL), lambda i: (i, 0)),
                  pl.BlockSpec((1, D_MODEL), lambda i: (0, 0)),
                  pl.BlockSpec((D_MODEL, PB_W + PF_W), lambda i: (0, 0), pipeline_mode=pl.Buffered(1))],
        out_specs=[pl.BlockSpec((tm, PB_W), lambda i: (i, 0)),
                   pl.BlockSpec((tm, PF_W), lambda i: (i, 0))],
        out_shape=[jax.ShapeDtypeStruct((M, PB_W), BF16), jax.ShapeDtypeStruct((M, PF_W), F32)],
        compiler_params=_params("arbitrary"),
        name="inproj",
    )(x2, g, wp)


def _ret_kernel(q_ref, k_ref, v_ref, g_ref, gn_ref, o_ref, r_ref, *, C):
    @pl.when(pl.program_id(1) == 0)
    def _():
        r_ref[...] = jnp.zeros_like(r_ref)

    diff = (lax.broadcasted_iota(jnp.int32, (C, C), 0)
            - lax.broadcasted_iota(jnp.int32, (C, C), 1)).astype(F32)
    pos = lax.broadcasted_iota(jnp.int32, (C, 1), 0).astype(F32)
    scale = RET_DK ** -0.5
    for h in range(RET_HEADS):
        lg = float(np.log1p(-np.exp2(np.float32(-5.0 - h))))
        dmask = jnp.where(diff >= 0, jnp.exp(lg * jnp.maximum(diff, 0.0)), 0.0) * scale
        q = q_ref[0, :, h * RET_DK:(h + 1) * RET_DK]
        k = k_ref[0, :, h * RET_DK:(h + 1) * RET_DK]
        v = v_ref[0, :, h * RET_DV:(h + 1) * RET_DV]
        s = lax.dot_general(q, k, NT_DIMS, preferred_element_type=F32)
        o = jnp.dot((s * dmask).astype(BF16), v, preferred_element_type=F32)
        state = r_ref[h]
        xi = jnp.exp(lg * (pos + 1.0))
        o = o + xi * jnp.dot(q, state.astype(BF16), preferred_element_type=F32)
        zeta = jnp.exp(lg * (C - 1.0 - pos)) * scale
        kz = (k.astype(F32) * zeta).astype(BF16)
        kv = lax.dot_general(kz, v, TN_DIMS, preferred_element_type=F32)
        r_ref[h] = math.exp(lg * C) * state + kv
        mu = jnp.mean(o, axis=-1, keepdims=True)
        d = o - mu
        var = jnp.mean(d * d, axis=-1, keepdims=True)
        on = d * lax.rsqrt(var + EPS) * gn_ref[:, h * RET_DV:(h + 1) * RET_DV]
        g = g_ref[0, :, h * RET_DV:(h + 1) * RET_DV]
        o_ref[0, :, h * RET_DV:(h + 1) * RET_DV] = (g * jax.nn.sigmoid(g) * on).astype(BF16)


def _retention(pb3, pf3, gn_g, C):
    B, S, _ = pb3.shape
    HV = RET_HEADS * RET_DV
    HK = RET_HEADS * RET_DK
    return pl.pallas_call(
        functools.partial(_ret_kernel, C=C),
        grid=(B, S // C),
        in_specs=[pl.BlockSpec((1, C, HK), lambda b, n: (b, n, PB_RQ // HK)),
                  pl.BlockSpec((1, C, HK), lambda b, n: (b, n, PB_RK // HK)),
                  pl.BlockSpec((1, C, HV), lambda b, n: (b, n, PB_RV // HV)),
                  pl.BlockSpec((1, C, HV), lambda b, n: (b, n, PF_RG // HV)),
                  pl.BlockSpec((1, HV), lambda b, n: (0, 0))],
        out_specs=pl.BlockSpec((1, C, HV), lambda b, n: (b, n, 0)),
        out_shape=jax.ShapeDtypeStruct((B, S, HV), BF16),
        scratch_shapes=[pltpu.VMEM((RET_HEADS, RET_DK, RET_DV), F32)],
        compiler_params=_params("arbitrary", "arbitrary"),
        name="retention",
    )(pb3, pb3, pb3, pf3, gn_g)


def _compress_kernel(kin_ref, vin_ref, posk_ref, posv_ref, w1k_ref, w1v_ref, b1k_ref, b1v_ref,
                     w2k_ref, w2v_ref, ko_ref, vo_ref, *, NP):
    GD = NSA_GROUPS * NSA_DK
    npc = CMP_BLOCK // CMP_STRIDE
    row = lax.broadcasted_iota(jnp.int32, (NP, GD), 0)
    for in_ref, pos_ref, w1_ref, b1_ref, w2_ref, out_ref in (
            (kin_ref, posk_ref, w1k_ref, b1k_ref, w2k_ref, ko_ref),
            (vin_ref, posv_ref, w1v_ref, b1v_ref, w2v_ref, vo_ref)):
        parts = []
        for p in range(npc):
            acc = jnp.zeros((NP, NSA_GROUPS * CMP_HIDDEN), F32)
            for i in range(CMP_STRIDE):
                tok = in_ref[0, pl.ds(i, NP, stride=CMP_STRIDE), :]
                j = p * CMP_STRIDE + i
                acc = acc + jnp.dot((tok + pos_ref[j:j + 1, :]).astype(BF16), w1_ref[j],
                                    preferred_element_type=F32)
            parts.append(acc)
        hidden = parts[0]
        for p in range(1, npc):
            hidden = hidden + pltpu.roll(parts[p], NP - p, axis=0)
        act = jax.nn.gelu(hidden + b1_ref[...]).astype(BF16)
        res = jnp.dot(act, w2_ref[...], preferred_element_type=F32)
        res = jnp.where(row < NP - (npc - 1), res, 0.0)
        for g in range(NSA_GROUPS):
            out_ref[0, g] = res[:, g * NSA_DK:(g + 1) * NSA_DK]


def _compress(pf3, posk, posv, w1k, w1v, b1k, b1v, w2k, w2v):
    B, S, _ = pf3.shape
    NP = S // CMP_STRIDE
    GD = NSA_GROUPS * NSA_DK
    GH = NSA_GROUPS * CMP_HIDDEN
    const2 = lambda b: (0, 0)
    const3 = lambda b: (0, 0, 0)
    out_sd = jax.ShapeDtypeStruct((B, NSA_GROUPS, NP, NSA_DK), F32)
    out_spec = pl.BlockSpec((1, NSA_GROUPS, NP, NSA_DK), lambda b: (b, 0, 0, 0))
    return pl.pallas_call(
        functools.partial(_compress_kernel, NP=NP),
        grid=(B,),
        in_specs=[pl.BlockSpec((1, S, GD), lambda b: (b, 0, PF_KCR // GD)),
                  pl.BlockSpec((1, S, GD), lambda b: (b, 0, PF_VCR // GD)),
                  pl.BlockSpec((CMP_BLOCK, GD), const2), pl.BlockSpec((CMP_BLOCK, GD), const2),
                  pl.BlockSpec((CMP_BLOCK, GD, GH), const3), pl.BlockSpec((CMP_BLOCK, GD, GH), const3),
                  pl.BlockSpec((1, GH), const2), pl.BlockSpec((1, GH), const2),
                  pl.BlockSpec((GH, GD), const2), pl.BlockSpec((GH, GD), const2)],
        out_specs=[out_spec, out_spec],
        out_shape=[out_sd, out_sd],
        compiler_params=_params("arbitrary"),
        name="compress",
    )(pf3, pf3, posk, posv, w1k, w1v, b1k, b1v, w2k, w2v)


def _cmpattn_kernel(q_ref, kc_ref, vc_ref, gate_ref, ocmp_ref, sel_ref, *, tq, NP, NS):
    NC = NP - (CMP_BLOCK // CMP_STRIDE - 1)
    scale = NSA_DK ** -0.5
    t0 = pl.program_id(1) * tq
    trow = t0 + lax.broadcasted_iota(jnp.int32, (tq, NP), 0)
    cidx = lax.broadcasted_iota(jnp.int32, (tq, NP), 1)
    dcmp = (trow - (cidx * CMP_STRIDE + (CMP_BLOCK - 1))).astype(F32)
    cmask = (dcmp >= 0) & (cidx < NC)
    gate = gate_ref[0]
    ratio = SEL_BLOCK // CMP_STRIDE
    cc = lax.broadcasted_iota(jnp.int32, (NP, LANES), 0)
    jj = lax.broadcasted_iota(jnp.int32, (NP, LANES), 1)
    overlap = ((cc * CMP_STRIDE < jj * SEL_BLOCK + SEL_BLOCK)
               & (cc * CMP_STRIDE + CMP_BLOCK > jj * SEL_BLOCK) & (jj < NS)).astype(F32)
    del ratio
    lane = lax.broadcasted_iota(jnp.int32, (tq, LANES), 1)
    lane_f = lane.astype(F32)
    cur = jnp.right_shift(t0 + lax.broadcasted_iota(jnp.int32, (tq, LANES), 0),
                          SEL_BLOCK.bit_length() - 1)
    for g in range(NSA_GROUPS):
        kc = kc_ref[0, g]
        vc = vc_ref[0, g].astype(BF16)
        psum = jnp.zeros((tq, NP), F32)
        for r in range(NSA_R):
            hh = g * NSA_R + r
            q = q_ref[0, :, hh * NSA_DK:(hh + 1) * NSA_DK].astype(F32)
            s = lax.dot_general(q, kc, NT_DIMS, precision=lax.Precision.HIGHEST,
                                preferred_element_type=F32) * scale
            s = jnp.where(cmask, s - (2.0 ** -(hh + 1)) * dcmp, NEG)
            e = jnp.exp(s - jnp.max(s, axis=-1, keepdims=True))
            p = jnp.where(cmask, e / jnp.sum(e, axis=-1, keepdims=True), 0.0)
            o = jnp.dot(p.astype(BF16), vc, preferred_element_type=F32)
            ocmp_ref[0, :, hh * NSA_DK:(hh + 1) * NSA_DK] = jax.nn.sigmoid(gate[:, hh:hh + 1]) * o
            psum = psum + p
        imp = jnp.dot(psum, overlap, precision=lax.Precision.HIGHEST, preferred_element_type=F32)
        imp = jnp.where((lane == 0) | (lane == cur) | (lane == cur - 1), FORCE, imp)
        imp = jnp.where(lane > cur, -FORCE, imp)
        work = jnp.where(lane < NS, imp, LOWEST)
        sel = jnp.zeros((tq, LANES), F32)
        for _ in range(min(SEL_TOPK, NS)):
            top = jnp.max(work, axis=-1, keepdims=True)
            first = jnp.min(jnp.where(work == top, lane_f, float(LANES)), axis=-1, keepdims=True)
            pick = lane_f == first
            sel = jnp.where(pick, 1.0, sel)
            work = jnp.where(pick, LOWEST, work)
        sel_ref[0, g] = sel.astype(BF16)


def _cmpattn(pb3, kcmp, vcmp, pf3, tq):
    B, S, _ = pb3.shape
    NP = S // CMP_STRIDE
    NS = S // SEL_BLOCK
    HD = NSA_HEADS * NSA_DK
    return pl.pallas_call(
        functools.partial(_cmpattn_kernel, tq=tq, NP=NP, NS=NS),
        grid=(B, S // tq),
        in_specs=[pl.BlockSpec((1, tq, HD), lambda b, i: (b, i, PB_NQ // HD)),
                  pl.BlockSpec((1, NSA_GROUPS, NP, NSA_DK), lambda b, i: (b, 0, 0, 0)),
                  pl.BlockSpec((1, NSA_GROUPS, NP, NSA_DK), lambda b, i: (b, 0, 0, 0)),
                  pl.BlockSpec((1, tq, LANES), lambda b, i: (b, i, PF_NG // LANES))],
        out_specs=[pl.BlockSpec((1, tq, HD), lambda b, i: (b, i, 0)),
                   pl.BlockSpec((1, NSA_GROUPS, tq, LANES), lambda b, i: (b, 0, i, 0))],
        out_shape=[jax.ShapeDtypeStruct((B, S, HD), F32),
                   jax.ShapeDtypeStruct((B, NSA_GROUPS, S, LANES), BF16)],
        compiler_params=_params("arbitrary", "arbitrary"),
        name="cmpattn",
    )(pb3, kcmp, vcmp, pf3)


def _selwin_kernel(q_ref, ks_ref, vs_ref, kw_ref, vw_ref, sel_ref, e_ref, gate_ref, ocmp_ref,
                   o_ref, *, tq, KC):
    scale = NSA_DK ** -0.5
    rows = NSA_R * tq
    WL = WINDOW + tq
    t0 = pl.program_id(1) * tq
    gate = gate_ref[0]
    lane = lax.broadcasted_iota(jnp.int32, (tq, LANES), 1)
    kstart = pl.multiple_of(jnp.maximum(t0 - WINDOW, 0), tq)
    nchunks = (t0 + tq + KC - 1) // KC

    def rel(nk, k0):
        tr = t0 + (lax.broadcasted_iota(jnp.int32, (rows, nk), 0) & (tq - 1))
        return tr - (k0 + lax.broadcasted_iota(jnp.int32, (rows, nk), 1))

    for g in range(NSA_GROUPS):
        qs = []
        for r in range(NSA_R):
            hh = g * NSA_R + r
            blk = q_ref[0, :, (hh // 2) * LANES:(hh // 2 + 1) * LANES]
            if (hh % 2) != g:
                blk = jnp.concatenate([blk[:, NSA_DK:], blk[:, :NSA_DK]], axis=1)
            keep = (lane >= g * NSA_DK) & (lane < (g + 1) * NSA_DK)
            qs.append(jnp.where(keep, blk, jnp.zeros_like(blk)))
        qg = jnp.concatenate(qs, axis=0)
        head = jnp.right_shift(lax.broadcasted_iota(jnp.int32, (rows, 1), 0), tq.bit_length() - 1)
        slope = jnp.exp2(-(head + (g * NSA_R + 1)).astype(F32))

        dw = rel(WL, kstart)
        s = lax.dot_general(qg, kw_ref[0, pl.ds(kstart, WL), :], NT_DIMS,
                            preferred_element_type=F32) * scale
        s = jnp.where((dw >= 0) & (dw < WINDOW), s - slope * dw.astype(F32), NEG)
        p = jnp.exp(s - jnp.max(s, axis=-1, keepdims=True))
        o_win = jnp.dot(p.astype(BF16), vw_ref[0, pl.ds(kstart, WL), :],
                        preferred_element_type=F32) / jnp.sum(p, axis=-1, keepdims=True)

        sel = sel_ref[0, g]

        def body(c, carry):
            m, l, acc = carry
            k0 = pl.multiple_of(c * KC, KC)
            dsel = rel(KC, k0)
            chosen = jnp.dot(sel, e_ref[:, pl.ds(k0, KC)], preferred_element_type=F32)
            chosen = jnp.concatenate([chosen] * NSA_R, axis=0)
            valid = (dsel >= 0) & (chosen > 0.5)
            sc = lax.dot_general(qg, ks_ref[0, pl.ds(k0, KC), :], NT_DIMS,
                                 preferred_element_type=F32) * scale
            sc = jnp.where(valid, sc - slope * dsel.astype(F32), NEG)
            m_new = jnp.maximum(m, jnp.max(sc, axis=-1, keepdims=True))
            alpha = jnp.exp(m - m_new)
            pc = jnp.where(valid, jnp.exp(sc - m_new), 0.0)
            l = alpha * l + jnp.sum(pc, axis=-1, keepdims=True)
            acc = alpha * acc + jnp.dot(pc.astype(BF16), vs_ref[0, pl.ds(k0, KC), :],
                                        preferred_element_type=F32)
            return m_new, l, acc

        init = (jnp.full((rows, 1), NEG, F32), jnp.zeros((rows, 1), F32),
                jnp.zeros((rows, LANES), F32))
        _, l, acc = lax.fori_loop(0, nchunks, body, init)
        o_sel = acc / l

        for r in range(NSA_R):
            hh = g * NSA_R + r
            g_sel = jax.nn.sigmoid(gate[:, NSA_HEADS + hh:NSA_HEADS + hh + 1])
            g_win = jax.nn.sigmoid(gate[:, 2 * NSA_HEADS + hh:2 * NSA_HEADS + hh + 1])
            piece = (g_sel * o_sel[r * tq:(r + 1) * tq, g * NSA_DK:(g + 1) * NSA_DK]
                     + g_win * o_win[r * tq:(r + 1) * tq, g * NSA_DK:(g + 1) * NSA_DK])
            o_ref[0, :, hh * NSA_DK:(hh + 1) * NSA_DK] = (
                ocmp_ref[0, :, hh * NSA_DK:(hh + 1) * NSA_DK] + piece).astype(BF16)


def _selwin(pb3, sel, emat, pf3, ocmp, tq, KC):
    B, S, _ = pb3.shape
    HD = NSA_HEADS * NSA_DK
    kv_spec = lambda off: pl.BlockSpec((1, S, LANES), lambda b, i: (b, 0, off // LANES))
    return pl.pallas_call(
        functools.partial(_selwin_kernel, tq=tq, KC=KC),
        grid=(B, S // tq),
        in_specs=[pl.BlockSpec((1, tq, HD), lambda b, i: (b, i, PB_NQ // HD)),
                  kv_spec(PB_KS), kv_spec(PB_VS), kv_spec(PB_KW), kv_spec(PB_VW),
                  pl.BlockSpec((1, NSA_GROUPS, tq, LANES), lambda b, i: (b, 0, i, 0)),
                  pl.BlockSpec((LANES, S), lambda b, i: (0, 0)),
                  pl.BlockSpec((1, tq, LANES), lambda b, i: (b, i, PF_NG // LANES)),
                  pl.BlockSpec((1, tq, HD), lambda b, i: (b, i, 0))],
        out_specs=pl.BlockSpec((1, tq, HD), lambda b, i: (b, i, 0)),
        out_shape=jax.ShapeDtypeStruct((B, S, HD), BF16),
        compiler_params=_params("arbitrary", "arbitrary"),
        name="selwin",
    )(pb3, pb3, pb3, pb3, pb3, sel, emat, pf3, ocmp)


def _merge_kernel(x_ref, ya_ref, yb_ref, ga_ref, gb_ref, wa_ref, wb_ref, wo_ref, g_ref,
                  x1_ref, h2_ref):
    y_a = jnp.dot(ya_ref[...], wa_ref[...], preferred_element_type=F32)
    y_b = jnp.dot(yb_ref[...], wb_ref[...], preferred_element_type=F32)
    merged = jax.nn.sigmoid(ga_ref[...]) * y_a + jax.nn.sigmoid(gb_ref[...]) * y_b
    x1 = x_ref[...] + jnp.dot(merged.astype(BF16), wo_ref[...], preferred_element_type=F32)
    x1_ref[...] = x1
    h2_ref[...] = _rms(x1, g_ref[...]).astype(BF16)


def _merge(x2, ya, yb, pf, wa, wb, wo, g, tm):
    M = x2.shape[0]
    HV = RET_HEADS * RET_DV
    HD = NSA_HEADS * NSA_DK
    const = lambda i: (0, 0)
    return pl.pallas_call(
        _merge_kernel,
        grid=(M // tm,),
        in_specs=[pl.BlockSpec((tm, D_MODEL), lambda i: (i, 0)),
                  pl.BlockSpec((tm, HV), lambda i: (i, 0)),
                  pl.BlockSpec((tm, HD), lambda i: (i, 0)),
                  pl.BlockSpec((tm, D_MODEL), lambda i: (i, PF_GA // D_MODEL)),
                  pl.BlockSpec((tm, D_MODEL), lambda i: (i, PF_GB // D_MODEL)),
                  pl.BlockSpec((HV, D_MODEL), const), pl.BlockSpec((HD, D_MODEL), const),
                  pl.BlockSpec((D_MODEL, D_MODEL), const), pl.BlockSpec((1, D_MODEL), const)],
        out_specs=[pl.BlockSpec((tm, D_MODEL), lambda i: (i, 0)),
                   pl.BlockSpec((tm, D_MODEL), lambda i: (i, 0))],
        out_shape=[jax.ShapeDtypeStruct((M, D_MODEL), F32), jax.ShapeDtypeStruct((M, D_MODEL), BF16)],
        compiler_params=_params("arbitrary"),
        name="merge",
    )(x2, ya, yb, pf, pf, wa, wb, wo, g)


def _ffn_kernel(h_ref, x1_ref, wa_ref, wb_ref, cw_ref, cb_ref, wd_ref, g_ref, o_ref, *, S):
    j = pl.program_id(1)
    h = h_ref[0]
    a = jnp.dot(h, wa_ref[...], preferred_element_type=F32)
    b = jnp.dot(h, wb_ref[...], preferred_element_type=F32)
    row = lax.broadcasted_iota(jnp.int32, a.shape, 0)
    ac = cb_ref[...] + cw_ref[CONV_W - 1:CONV_W, :] * a
    for d in range(1, CONV_W):
        shifted = jnp.where(row >= d, pltpu.roll(a, d, axis=0), 0.0)
        ac = ac + cw_ref[CONV_W - 1 - d:CONV_W - d, :] * shifted
    u = (jax.nn.gelu(ac) * b).astype(BF16)
    part = jnp.dot(u, wd_ref[...], preferred_element_type=F32)

    @pl.when(j == 0)
    def _():
        o_ref[0] = x1_ref[0] + part

    @pl.when(j > 0)
    def _():
        o_ref[0] = o_ref[0] + part

    @pl.when(j == pl.num_programs(1) - 1)
    def _():
        o_ref[0] = _rms(o_ref[0], g_ref[...])


def _ffn(h2, x1, w_up, conv_w, conv_b, w_down, g, tf):
    B, S, _ = h2.shape
    nf = D_FF // tf
    return pl.pallas_call(
        functools.partial(_ffn_kernel, S=S),
        grid=(B, nf),
        in_specs=[pl.BlockSpec((1, S, D_MODEL), lambda b, j: (b, 0, 0)),
                  pl.BlockSpec((1, S, D_MODEL), lambda b, j: (b, 0, 0), pipeline_mode=pl.Buffered(1)),
                  pl.BlockSpec((D_MODEL, tf), lambda b, j: (0, j)),
                  pl.BlockSpec((D_MODEL, tf), lambda b, j: (0, nf + j)),
                  pl.BlockSpec((CONV_W, tf), lambda b, j: (0, j)),
                  pl.BlockSpec((1, tf), lambda b, j: (0, j)),
                  pl.BlockSpec((tf, D_MODEL), lambda b, j: (j, 0)),
                  pl.BlockSpec((1, D_MODEL), lambda b, j: (0, 0))],
        out_specs=pl.BlockSpec((1, S, D_MODEL), lambda b, j: (b, 0, 0)),
        out_shape=jax.ShapeDtypeStruct((B, S, D_MODEL), F32),
        compiler_params=_params("arbitrary", "arbitrary"),
        name="convffn",
    )(h2, x1, w_up, w_up, conv_w, conv_b, w_down, g)


def _block_diag_groups(w):
    z = jnp.zeros_like(w)
    return jnp.concatenate([jnp.concatenate([w, z], axis=-1), jnp.concatenate([z, w], axis=-1)], axis=-2)


def kernel(x, norm_mix, w_in, ret_gn_g, cmp_pos_k, cmp_w1_k, cmp_b1_k, cmp_w2_k, cmp_pos_v, cmp_w1_v,
           cmp_b1_v, cmp_w2_v, w_ret_o, w_nsa_o, w_out, norm_ffn, w_up, conv_w, conv_b, w_down,
           norm_final):
    B, S, D = x.shape
    assert D == D_MODEL and NSA_GROUPS == 2 and norm_mix.shape[0] == 1
    assert S % 512 == 0 and S >= 1024 and S // CMP_STRIDE <= LANES and S // SEL_BLOCK <= LANES
    M = B * S
    x2 = x.reshape(M, D)

    off = np.concatenate([[0], np.cumsum(IN_SIZES)])
    w = w_in[0]
    seg = lambda i: w[:, off[i]:off[i + 1]]
    wp = jnp.concatenate(
        [seg(0), seg(1), seg(2), seg(4), seg(7), seg(8), seg(9), seg(10),
         seg(3), seg(12), seg(13), seg(5), seg(6), seg(11),
         jnp.zeros((D, PF_W - PF_NG - IN_SIZES[11]), F32)], axis=1).astype(BF16)

    def cmp_weights(pos, w1, b1, w2):
        return (jnp.tile(pos, (1, NSA_GROUPS)),
                _block_diag_groups(w1.reshape(CMP_BLOCK, NSA_DK, CMP_HIDDEN)).astype(BF16),
                jnp.tile(b1, NSA_GROUPS)[None, :],
                _block_diag_groups(w2).astype(BF16))

    posk, w1k, b1k, w2k = cmp_weights(cmp_pos_k[0], cmp_w1_k[0], cmp_b1_k[0], cmp_w2_k[0])
    posv, w1v, b1v, w2v = cmp_weights(cmp_pos_v[0], cmp_w1_v[0], cmp_b1_v[0], cmp_w2_v[0])
    emat = (jnp.arange(S)[None, :] // SEL_BLOCK == jnp.arange(LANES)[:, None]).astype(BF16)

    pb, pf = _inproj(x2, norm_mix, wp, tm=512)
    pb3 = pb.reshape(B, S, PB_W)
    pf3 = pf.reshape(B, S, PF_W)

    ya = _retention(pb3, pf3, ret_gn_g, C=128)
    kcmp, vcmp = _compress(pf3, posk, posv, w1k, w1v, b1k, b1v, w2k, w2v)
    ocmp, sel = _cmpattn(pb3, kcmp, vcmp, pf3, tq=256)
    yb = _selwin(pb3, sel, emat, pf3, ocmp, tq=128, KC=512)

    x1, h2 = _merge(x2, ya.reshape(M, -1), yb.reshape(M, -1), pf,
                    w_ret_o[0].astype(BF16), w_nsa_o[0].astype(BF16), w_out[0].astype(BF16),
                    norm_ffn, tm=512)
    out = _ffn(h2.reshape(B, S, D), x1.reshape(B, S, D), w_up[0].astype(BF16), conv_w[0],
               conv_b, w_down[0].astype(BF16), norm_final[None, :], tf=256)
    return out
```

```python
import functools
import math

import numpy as np
import jax
import jax.numpy as jnp
from jax import lax
from jax.experimental import pallas as pl
from jax.experimental.pallas import tpu as pltpu

F32 = jnp.float32
BF16 = jnp.bfloat16

D_MODEL = 1024
RET_HEADS = 4
RET_DK = 128
RET_DV = 256
NSA_HEADS = 8
NSA_GROUPS = 2
NSA_R = NSA_HEADS // NSA_GROUPS
NSA_DK = 64
CMP_BLOCK = 32
CMP_STRIDE = 16
CMP_HIDDEN = 256
SEL_BLOCK = 64
SEL_TOPK = 8
WINDOW = 512
D_FF = 2816
CONV_W = 3
EPS = 1e-6
NEG = -1e30
FORCE = 1e9
LOWEST = -3e38

IN_SIZES = (RET_HEADS * RET_DK, RET_HEADS * RET_DK, RET_HEADS * RET_DV, RET_HEADS * RET_DV,
            NSA_HEADS * NSA_DK,
            NSA_GROUPS * NSA_DK, NSA_GROUPS * NSA_DK, NSA_GROUPS * NSA_DK,
            NSA_GROUPS * NSA_DK, NSA_GROUPS * NSA_DK, NSA_GROUPS * NSA_DK,
            3 * NSA_HEADS, D_MODEL, D_MODEL)

LANES = 128
VMEM_LIMIT = 56 * 1024 * 1024

PB_RQ, PB_RK, PB_RV, PB_NQ, PB_KS, PB_VS, PB_KW, PB_VW = 0, 512, 1024, 2048, 2560, 2688, 2816, 2944
PB_W = 3072
PF_RG, PF_GA, PF_GB, PF_KCR, PF_VCR, PF_NG = 0, 1024, 2048, 3072, 3200, 3328
PF_W = 3456

NT_DIMS = (((1,), (1,)), ((), ()))
TN_DIMS = (((0,), (0,)), ((), ()))


def _params(*sem):
    return pltpu.CompilerParams(dimension_semantics=sem, vmem_limit_bytes=VMEM_LIMIT)


def _rms(x, g):
    return x * lax.rsqrt(jnp.mean(x * x, axis=-1, keepdims=True) + EPS) * g


def _inproj_kernel(x_ref, g_ref, w_ref, pb_ref, pf_ref):
    hb = _rms(x_ref[...], g_ref[...]).astype(BF16)
    for c0 in range(0, PB_W, 512):
        pb_ref[:, c0:c0 + 512] = jnp.dot(
            hb, w_ref[:, c0:c0 + 512], preferred_element_type=F32).astype(BF16)
    for c0 in range(0, PF_W, 512):
        c1 = min(c0 + 512, PF_W)
        pf_ref[:, c0:c1] = jnp.dot(hb, w_ref[:, PB_W + c0:PB_W + c1], preferred_element_type=F32)


def _inproj(x2, g, wp, tm):
    M = x2.shape[0]
    return pl.pallas_call(
        _inproj_kernel,
        grid=(M // tm,),
        in_specs=[pl.BlockSpec((tm, D_MODEL), lambda i: (i, 0)),
                  pl.BlockSpec((1, D_MODEL), lambda i: (0, 0)),
                  pl.BlockSpec((D_MODEL, PB_W + PF_W), lambda i: (0, 0), pipeline_mode=pl.Buffered(1))],
        out_specs=[pl.BlockSpec((tm, PB_W), lambda i: (i, 0)),
                   pl.BlockSpec((tm, PF_W), lambda i: (i, 0))],
        out_shape=[jax.ShapeDtypeStruct((M, PB_W), BF16), jax.ShapeDtypeStruct((M, PF_W), F32)],
        compiler_params=_params("arbitrary"),
        name="inproj",
    )(x2, g, wp)


def _ret_kernel(q_ref, k_ref, v_ref, g_ref, gn_ref, o_ref, r_ref, *, C):
    @pl.when(pl.program_id(1) == 0)
    def _():
        r_ref[...] = jnp.zeros_like(r_ref)

    diff = (lax.broadcasted_iota(jnp.int32, (C, C), 0)
            - lax.broadcasted_iota(jnp.int32, (C, C), 1)).astype(F32)
    pos = lax.broadcasted_iota(jnp.int32, (C, 1), 0).astype(F32)
    scale = RET_DK ** -0.5
    for h in range(RET_HEADS):
        lg = float(np.log1p(-np.exp2(np.float32(-5.0 - h))))
        dmask = jnp.where(diff >= 0, jnp.exp(lg * jnp.maximum(diff, 0.0)), 0.0) * scale
        q = q_ref[0, :, h * RET_DK:(h + 1) * RET_DK]
        k = k_ref[0, :, h * RET_DK:(h + 1) * RET_DK]
        v = v_ref[0, :, h * RET_DV:(h + 1) * RET_DV]
        s = lax.dot_general(q, k, NT_DIMS, preferred_element_type=F32)
        o = jnp.dot((s * dmask).astype(BF16), v, preferred_element_type=F32)
        state = r_ref[h]
        xi = jnp.exp(lg * (pos + 1.0))
        o = o + xi * jnp.dot(q, state.astype(BF16), preferred_element_type=F32)
        zeta = jnp.exp(lg * (C - 1.0 - pos)) * scale
        kz = (k.astype(F32) * zeta).astype(BF16)
        kv = lax.dot_general(kz, v, TN_DIMS, preferred_element_type=F32)
        r_ref[h] = math.exp(lg * C) * state + kv
        mu = jnp.mean(o, axis=-1, keepdims=True)
        d = o - mu
        var = jnp.mean(d * d, axis=-1, keepdims=True)
        on = d * lax.rsqrt(var + EPS) * gn_ref[:, h * RET_DV:(h + 1) * RET_DV]
        g = g_ref[0, :, h * RET_DV:(h + 1) * RET_DV]
        o_ref[0, :, h * RET_DV:(h + 1) * RET_DV] = (g * jax.nn.sigmoid(g) * on).astype(BF16)


def _retention(pb3, pf3, gn_g, C):
    B, S, _ = pb3.shape
    HV = RET_HEADS * RET_DV
    HK = RET_HEADS * RET_DK
    return pl.pallas_call(
        functools.partial(_ret_kernel, C=C),
        grid=(B, S // C),
        in_specs=[pl.BlockSpec((1, C, HK), lambda b, n: (b, n, PB_RQ // HK)),
                  pl.BlockSpec((1, C, HK), lambda b, n: (b, n, PB_RK // HK)),
                  pl.BlockSpec((1, C, HV), lambda b, n: (b, n, PB_RV // HV)),
                  pl.BlockSpec((1, C, HV), lambda b, n: (b, n, PF_RG // HV)),
                  pl.BlockSpec((1, HV), lambda b, n: (0, 0))],
        out_specs=pl.BlockSpec((1, C, HV), lambda b, n: (b, n, 0)),
        out_shape=jax.ShapeDtypeStruct((B, S, HV), BF16),
        scratch_shapes=[pltpu.VMEM((RET_HEADS, RET_DK, RET_DV), F32)],
        compiler_params=_params("arbitrary", "arbitrary"),
        name="retention",
    )(pb3, pb3, pb3, pf3, gn_g)


def _compress_kernel(kin_ref, vin_ref, posk_ref, posv_ref, w1k_ref, w1v_ref, b1k_ref, b1v_ref,
                     w2k_ref, w2v_ref, ko_ref, vo_ref, *, NP):
    GD = NSA_GROUPS * NSA_DK
    npc = CMP_BLOCK // CMP_STRIDE
    row = lax.broadcasted_iota(jnp.int32, (NP, GD), 0)
    for in_ref, pos_ref, w1_ref, b1_ref, w2_ref, out_ref in (
            (kin_ref, posk_ref, w1k_ref, b1k_ref, w2k_ref, ko_ref),
            (vin_ref, posv_ref, w1v_ref, b1v_ref, w2v_ref, vo_ref)):
        parts = []
        for p in range(npc):
            acc = jnp.zeros((NP, NSA_GROUPS * CMP_HIDDEN), F32)
            for i in range(CMP_STRIDE):
                tok = in_ref[0, pl.ds(i, NP, stride=CMP_STRIDE), :]
                j = p * CMP_STRIDE + i
                acc = acc + jnp.dot((tok + pos_ref[j:j + 1, :]).astype(BF16), w1_ref[j],
                                    preferred_element_type=F32)
            parts.append(acc)
        hidden = parts[0]
        for p in range(1, npc):
            hidden = hidden + pltpu.roll(parts[p], NP - p, axis=0)
        act = jax.nn.gelu(hidden + b1_ref[...]).astype(BF16)
        res = jnp.dot(act, w2_ref[...], preferred_element_type=F32)
        res = jnp.where(row < NP - (npc - 1), res, 0.0)
        for g in range(NSA_GROUPS):
            out_ref[0, g] = res[:, g * NSA_DK:(g + 1) * NSA_DK]


def _compress(pf3, posk, posv, w1k, w1v, b1k, b1v, w2k, w2v):
    B, S, _ = pf3.shape
    NP = S // CMP_STRIDE
    GD = NSA_GROUPS * NSA_DK
    GH = NSA_GROUPS * CMP_HIDDEN
    const2 = lambda b: (0, 0)
    const3 = lambda b: (0, 0, 0)
    out_sd = jax.ShapeDtypeStruct((B, NSA_GROUPS, NP, NSA_DK), F32)
    out_spec = pl.BlockSpec((1, NSA_GROUPS, NP, NSA_DK), lambda b: (b, 0, 0, 0))
    return pl.pallas_call(
        functools.partial(_compress_kernel, NP=NP),
        grid=(B,),
        in_specs=[pl.BlockSpec((1, S, GD), lambda b: (b, 0, PF_KCR // GD)),
                  pl.BlockSpec((1, S, GD), lambda b: (b, 0, PF_VCR // GD)),
                  pl.BlockSpec((CMP_BLOCK, GD), const2), pl.BlockSpec((CMP_BLOCK, GD), const2),
                  pl.BlockSpec((CMP_BLOCK, GD, GH), const3), pl.BlockSpec((CMP_BLOCK, GD, GH), const3),
                  pl.BlockSpec((1, GH), const2), pl.BlockSpec((1, GH), const2),
                  pl.BlockSpec((GH, GD), const2), pl.BlockSpec((GH, GD), const2)],
        out_specs=[out_spec, out_spec],
        out_shape=[out_sd, out_sd],
        compiler_params=_params("arbitrary"),
        name="compress",
    )(pf3, pf3, posk, posv, w1k, w1v, b1k, b1v, w2k, w2v)


SEL_LANE0 = 64
SEL_LANES = 32
POS_HI_LANE, POS_LO_LANE, PAD_LANE = 96, 97, 98
ONES_LANE = 64
MASK_BIG = 1e30


def _cmpattn_kernel(q_ref, kc_ref, vc_ref, gate_ref, ocmp_ref, sel_ref, *, tq, NP, NS):
    NC = NP - (CMP_BLOCK // CMP_STRIDE - 1)
    t0 = pl.program_id(1) * tq
    trow = t0 + lax.broadcasted_iota(jnp.int32, (tq, NP), 0)
    cidx = lax.broadcasted_iota(jnp.int32, (tq, NP), 1)
    dcmp = (trow - (cidx * CMP_STRIDE + (CMP_BLOCK - 1))).astype(F32)
    cmask = (dcmp >= 0) & (cidx < NC)
    gate = gate_ref[0]
    cc = lax.broadcasted_iota(jnp.int32, (NP, LANES), 0)
    jj = lax.broadcasted_iota(jnp.int32, (NP, LANES), 1) - SEL_LANE0
    overlap = ((cc * CMP_STRIDE < jj * SEL_BLOCK + SEL_BLOCK)
               & (cc * CMP_STRIDE + CMP_BLOCK > jj * SEL_BLOCK) & (jj >= 0) & (jj < NS)).astype(F32)
    lane = lax.broadcasted_iota(jnp.int32, (tq, LANES), 1)
    lane_f = lane.astype(F32)
    blk = lane - SEL_LANE0
    is_blk = (blk >= 0) & (blk < NS)
    cur = jnp.right_shift(t0 + lax.broadcasted_iota(jnp.int32, (tq, LANES), 0),
                          SEL_BLOCK.bit_length() - 1)
    for g in range(NSA_GROUPS):
        kc = kc_ref[0, g]
        vc = vc_ref[0, g].astype(BF16)
        psum = jnp.zeros((tq, NP), F32)
        for r in range(NSA_R):
            hh = g * NSA_R + r
            q = q_ref[0, :, hh * NSA_DK:(hh + 1) * NSA_DK].astype(F32)
            s = lax.dot_general(q, kc, NT_DIMS, precision=lax.Precision.HIGHEST,
                                preferred_element_type=F32)
            s = jnp.where(cmask, s - (2.0 ** -(hh + 1)) * dcmp, NEG)
            e = jnp.exp(s - jnp.max(s, axis=-1, keepdims=True))
            p = jnp.where(cmask, e / jnp.sum(e, axis=-1, keepdims=True), 0.0)
            o = jnp.dot(p.astype(BF16), vc, preferred_element_type=F32)
            ocmp_ref[0, :, hh * NSA_DK:(hh + 1) * NSA_DK] = jax.nn.sigmoid(gate[:, hh:hh + 1]) * o
            psum = psum + p
        imp = jnp.dot(psum, overlap, precision=lax.Precision.HIGHEST, preferred_element_type=F32)
        imp = jnp.where((blk == 0) | (blk == cur) | (blk == cur - 1), FORCE, imp)
        imp = jnp.where(blk > cur, -FORCE, imp)
        work = jnp.where(is_blk, imp, LOWEST)
        sel = jnp.zeros((tq, LANES), F32)
        for _ in range(min(SEL_TOPK, NS)):
            top = jnp.max(work, axis=-1, keepdims=True)
            first = jnp.min(jnp.where(work == top, lane_f, float(LANES)), axis=-1, keepdims=True)
            pick = lane_f == first
            sel = jnp.where(pick, 1.0, sel)
            work = jnp.where(pick, LOWEST, work)
        sel_ref[0, g] = jnp.where(is_blk & (blk <= cur), sel - 1.0, jnp.where(is_blk, -1.0, 0.0)).astype(BF16)


def _cmpattn(pb3, kcmp, vcmp, pf3, tq):
    B, S, _ = pb3.shape
    NP = S // CMP_STRIDE
    NS = S // SEL_BLOCK
    HD = NSA_HEADS * NSA_DK
    return pl.pallas_call(
        functools.partial(_cmpattn_kernel, tq=tq, NP=NP, NS=NS),
        grid=(B, S // tq),
        in_specs=[pl.BlockSpec((1, tq, HD), lambda b, i: (b, i, PB_NQ // HD)),
                  pl.BlockSpec((1, NSA_GROUPS, NP, NSA_DK), lambda b, i: (b, 0, 0, 0)),
                  pl.BlockSpec((1, NSA_GROUPS, NP, NSA_DK), lambda b, i: (b, 0, 0, 0)),
                  pl.BlockSpec((1, tq, LANES), lambda b, i: (b, i, PF_NG // LANES))],
        out_specs=[pl.BlockSpec((1, tq, HD), lambda b, i: (b, i, 0)),
                   pl.BlockSpec((1, NSA_GROUPS, tq, LANES), lambda b, i: (b, 0, i, 0))],
        out_shape=[jax.ShapeDtypeStruct((B, S, HD), F32),
                   jax.ShapeDtypeStruct((B, NSA_GROUPS, S, LANES), BF16)],
        compiler_params=_params("arbitrary", "arbitrary"),
        name="cmpattn",
    )(pb3, kcmp, vcmp, pf3)


def _selwin_kernel(q_ref, ks_ref, vs_ref, kw_ref, vw_ref, sel_ref, wmask_ref, dmask_ref, gate_ref, ocmp_ref,
                   o_ref, ksx, vsx, kwx, vwx, *, tq, KC, S):
    WL = WINDOW + tq
    RB = 512
    i = pl.program_id(1)
    t0 = pl.multiple_of(i * tq, tq)
    sel_shift = SEL_BLOCK.bit_length() - 1

    @pl.when(i == 0)
    def _build():
        lane = lax.broadcasted_iota(jnp.int32, (RB, LANES), 1)
        is_head = lane < NSA_DK
        pad_k = jnp.where(lane == PAD_LANE, MASK_BIG, 0.0).astype(BF16)
        for g in range(NSA_GROUPS):
            kwx[g, 0:WINDOW, :] = pad_k[:WINDOW]
            vwx[g, 0:WINDOW, :] = jnp.zeros((WINDOW, LANES), BF16)
        for piece in range(S // RB):
            r0 = piece * RB
            pos = r0 + lax.broadcasted_iota(jnp.int32, (RB, LANES), 0)
            alibi = jnp.where(lane == POS_HI_LANE, jnp.right_shift(pos, sel_shift),
                              jnp.where(lane == POS_LO_LANE, pos & (SEL_BLOCK - 1), 0)).astype(F32)
            onehot = jnp.where(lane - SEL_LANE0 == jnp.right_shift(pos, sel_shift), MASK_BIG, 0.0)
            k_extra_win = alibi.astype(BF16)
            k_extra_sel = (alibi + onehot).astype(BF16)
            v_extra = jnp.where(lane == ONES_LANE, 1.0, 0.0).astype(BF16)
            for g in range(NSA_GROUPS):
                def head_lanes(ref):
                    t = ref[0, r0:r0 + RB, :]
                    return t if g == 0 else jnp.concatenate([t[:, NSA_DK:], t[:, :NSA_DK]], axis=1)
                ksx[g, r0:r0 + RB, :] = jnp.where(is_head, head_lanes(ks_ref), k_extra_sel)
                vsx[g, r0:r0 + RB, :] = jnp.where(is_head, head_lanes(vs_ref), v_extra)
                kwx[g, WINDOW + r0:WINDOW + r0 + RB, :] = jnp.where(is_head, head_lanes(kw_ref), k_extra_win)
                vwx[g, WINDOW + r0:WINDOW + r0 + RB, :] = jnp.where(is_head, head_lanes(vw_ref), v_extra)

    gate = gate_ref[0]
    lane = lax.broadcasted_iota(jnp.int32, (tq, LANES), 1)
    is_head = lane < NSA_DK
    first_blk = jnp.right_shift(t0, sel_shift)
    n_past = (t0 + KC - 1) // KC
    wmask = wmask_ref[...][None]
    dmask = dmask_ref[...][None]

    def masked_rows(s, mask):
        n = s.shape[-1]
        return (s.reshape(NSA_R, tq, n) + mask).reshape(NSA_R * tq, n)

    def normalise(acc):
        return acc[:, :NSA_DK] / acc[:, ONES_LANE:ONES_LANE + 1]

    for g in range(NSA_GROUPS):
        sel_past = jnp.where((lane >= SEL_LANE0 + first_blk) & (lane < SEL_LANE0 + SEL_LANES),
                             -1.0, sel_ref[0, g].astype(F32))
        q_past, q_diag = [], []
        for r in range(NSA_R):
            hh = g * NSA_R + r
            qh = q_ref[0, :, (hh // 2) * LANES:(hh // 2 + 1) * LANES].astype(F32)
            if hh % 2:
                qh = jnp.concatenate([qh[:, NSA_DK:], qh[:, :NSA_DK]], axis=1)
            slope = 2.0 ** -(hh + 1)
            extra = jnp.where(lane == POS_HI_LANE, SEL_BLOCK * slope,
                              jnp.where(lane == POS_LO_LANE, slope, jnp.where(lane == PAD_LANE, -1.0, 0.0)))
            q_diag.append(jnp.where(is_head, qh, extra).astype(BF16))
            q_past.append(jnp.where(is_head, qh, extra + sel_past).astype(BF16))
        q_diag = jnp.concatenate(q_diag, axis=0)
        q_past = jnp.concatenate(q_past, axis=0)

        s = masked_rows(lax.dot_general(q_diag, kwx[g, pl.ds(t0, WL), :], NT_DIMS,
                                        preferred_element_type=F32), wmask)
        p = jnp.exp(s - jnp.max(s, axis=-1, keepdims=True))
        o_win = normalise(jnp.dot(p.astype(BF16), vwx[g, pl.ds(t0, WL), :], preferred_element_type=F32))

        s = masked_rows(lax.dot_general(q_diag, ksx[g, pl.ds(t0, tq), :], NT_DIMS,
                                        preferred_element_type=F32), dmask)
        m0 = jnp.max(s, axis=-1, keepdims=True)
        acc0 = jnp.dot(jnp.exp(s - m0).astype(BF16), vsx[g, pl.ds(t0, tq), :], preferred_element_type=F32)

        def body(c, carry):
            m, acc = carry
            k0 = pl.multiple_of(c * KC, KC)
            sc = lax.dot_general(q_past, ksx[g, pl.ds(k0, KC), :], NT_DIMS, preferred_element_type=F32)
            m_new = jnp.maximum(m, jnp.max(sc, axis=-1, keepdims=True))
            pc = jnp.exp(sc - m_new).astype(BF16)
            acc = jnp.exp(m - m_new) * acc + jnp.dot(pc, vsx[g, pl.ds(k0, KC), :],
                                                     preferred_element_type=F32)
            return m_new, acc

        _, acc = lax.fori_loop(0, n_past, body, (m0, acc0))
        o_sel = normalise(acc)

        for r in range(NSA_R):
            hh = g * NSA_R + r
            g_sel = jax.nn.sigmoid(gate[:, NSA_HEADS + hh:NSA_HEADS + hh + 1])
            g_win = jax.nn.sigmoid(gate[:, 2 * NSA_HEADS + hh:2 * NSA_HEADS + hh + 1])
            piece = g_sel * o_sel[r * tq:(r + 1) * tq] + g_win * o_win[r * tq:(r + 1) * tq]
            o_ref[0, :, hh * NSA_DK:(hh + 1) * NSA_DK] = (
                ocmp_ref[0, :, hh * NSA_DK:(hh + 1) * NSA_DK] + piece).astype(BF16)


def _selwin(pb3, sel, pf3, ocmp, tq, KC):
    B, S, _ = pb3.shape
    HD = NSA_HEADS * NSA_DK
    WL = WINDOW + tq
    tr = jnp.arange(tq)[:, None]
    kc = jnp.arange(WL)[None, :]
    wmask = jnp.where((kc > tr) & (kc <= tr + WINDOW), 0.0, NEG).astype(F32)
    dmask = jnp.where(jnp.arange(tq)[None, :] <= tr, 0.0, NEG).astype(F32)
    kv_spec = lambda off: pl.BlockSpec((1, S, LANES), lambda b, i: (b, 0, off // LANES))
    return pl.pallas_call(
        functools.partial(_selwin_kernel, tq=tq, KC=KC, S=S),
        grid=(B, S // tq),
        in_specs=[pl.BlockSpec((1, tq, HD), lambda b, i: (b, i, PB_NQ // HD)),
                  kv_spec(PB_KS), kv_spec(PB_VS), kv_spec(PB_KW), kv_spec(PB_VW),
                  pl.BlockSpec((1, NSA_GROUPS, tq, LANES), lambda b, i: (b, 0, i, 0)),
                  pl.BlockSpec((tq, WL), lambda b, i: (0, 0)),
                  pl.BlockSpec((tq, tq), lambda b, i: (0, 0)),
                  pl.BlockSpec((1, tq, LANES), lambda b, i: (b, i, PF_NG // LANES)),
                  pl.BlockSpec((1, tq, HD), lambda b, i: (b, i, 0))],
        out_specs=pl.BlockSpec((1, tq, HD), lambda b, i: (b, i, 0)),
        out_shape=jax.ShapeDtypeStruct((B, S, HD), BF16),
        scratch_shapes=[pltpu.VMEM((NSA_GROUPS, S, LANES), BF16), pltpu.VMEM((NSA_GROUPS, S, LANES), BF16),
                        pltpu.VMEM((NSA_GROUPS, WINDOW + S, LANES), BF16),
                        pltpu.VMEM((NSA_GROUPS, WINDOW + S, LANES), BF16)],
        compiler_params=_params("arbitrary", "arbitrary"),
        name="selwin",
    )(pb3, pb3, pb3, pb3, pb3, sel, wmask, dmask, pf3, ocmp)


def _merge_kernel(x_ref, ya_ref, yb_ref, ga_ref, gb_ref, wa_ref, wb_ref, wo_ref, g_ref,
                  x1_ref, h2_ref):
    y_a = jnp.dot(ya_ref[...], wa_ref[...], preferred_element_type=F32)
    y_b = jnp.dot(yb_ref[...], wb_ref[...], preferred_element_type=F32)
    merged = jax.nn.sigmoid(ga_ref[...]) * y_a + jax.nn.sigmoid(gb_ref[...]) * y_b
    x1 = x_ref[...] + jnp.dot(merged.astype(BF16), wo_ref[...], preferred_element_type=F32)
    x1_ref[...] = x1
    h2_ref[...] = _rms(x1, g_ref[...]).astype(BF16)


def _merge(x2, ya, yb, pf, wa, wb, wo, g, tm):
    M = x2.shape[0]
    HV = RET_HEADS * RET_DV
    HD = NSA_HEADS * NSA_DK
    const = lambda i: (0, 0)
    return pl.pallas_call(
        _merge_kernel,
        grid=(M // tm,),
        in_specs=[pl.BlockSpec((tm, D_MODEL), lambda i: (i, 0)),
                  pl.BlockSpec((tm, HV), lambda i: (i, 0)),
                  pl.BlockSpec((tm, HD), lambda i: (i, 0)),
                  pl.BlockSpec((tm, D_MODEL), lambda i: (i, PF_GA // D_MODEL)),
                  pl.BlockSpec((tm, D_MODEL), lambda i: (i, PF_GB // D_MODEL)),
                  pl.BlockSpec((HV, D_MODEL), const), pl.BlockSpec((HD, D_MODEL), const),
                  pl.BlockSpec((D_MODEL, D_MODEL), const), pl.BlockSpec((1, D_MODEL), const)],
        out_specs=[pl.BlockSpec((tm, D_MODEL), lambda i: (i, 0)),
                   pl.BlockSpec((tm, D_MODEL), lambda i: (i, 0))],
        out_shape=[jax.ShapeDtypeStruct((M, D_MODEL), F32), jax.ShapeDtypeStruct((M, D_MODEL), BF16)],
        compiler_params=_params("arbitrary"),
        name="merge",
    )(x2, ya, yb, pf, pf, wa, wb, wo, g)


def _ffn_kernel(h_ref, x1_ref, wa_ref, wb_ref, cw_ref, cb_ref, wd_ref, g_ref, o_ref, *, S):
    j = pl.program_id(1)
    h = h_ref[0]
    a = jnp.dot(h, wa_ref[...], preferred_element_type=F32)
    b = jnp.dot(h, wb_ref[...], preferred_element_type=F32)
    row = lax.broadcasted_iota(jnp.int32, a.shape, 0)
    ac = cb_ref[...] + cw_ref[CONV_W - 1:CONV_W, :] * a
    for d in range(1, CONV_W):
        shifted = jnp.where(row >= d, pltpu.roll(a, d, axis=0), 0.0)
        ac = ac + cw_ref[CONV_W - 1 - d:CONV_W - d, :] * shifted
    u = (jax.nn.gelu(ac) * b).astype(BF16)
    part = jnp.dot(u, wd_ref[...], preferred_element_type=F32)

    @pl.when(j == 0)
    def _():
        o_ref[0] = x1_ref[0] + part

    @pl.when(j > 0)
    def _():
        o_ref[0] = o_ref[0] + part

    @pl.when(j == pl.num_programs(1) - 1)
    def _():
        o_ref[0] = _rms(o_ref[0], g_ref[...])


def _ffn(h2, x1, w_up, conv_w, conv_b, w_down, g, tf):
    B, S, _ = h2.shape
    nf = D_FF // tf
    return pl.pallas_call(
        functools.partial(_ffn_kernel, S=S),
        grid=(B, nf),
        in_specs=[pl.BlockSpec((1, S, D_MODEL), lambda b, j: (b, 0, 0)),
                  pl.BlockSpec((1, S, D_MODEL), lambda b, j: (b, 0, 0), pipeline_mode=pl.Buffered(1)),
                  pl.BlockSpec((D_MODEL, tf), lambda b, j: (0, j)),
                  pl.BlockSpec((D_MODEL, tf), lambda b, j: (0, nf + j)),
                  pl.BlockSpec((CONV_W, tf), lambda b, j: (0, j)),
                  pl.BlockSpec((1, tf), lambda b, j: (0, j)),
                  pl.BlockSpec((tf, D_MODEL), lambda b, j: (j, 0)),
                  pl.BlockSpec((1, D_MODEL), lambda b, j: (0, 0))],
        out_specs=pl.BlockSpec((1, S, D_MODEL), lambda b, j: (b, 0, 0)),
        out_shape=jax.ShapeDtypeStruct((B, S, D_MODEL), F32),
        compiler_params=_params("arbitrary", "arbitrary"),
        name="convffn",
    )(h2, x1, w_up, w_up, conv_w, conv_b, w_down, g)


def _block_diag_groups(w):
    z = jnp.zeros_like(w)
    return jnp.concatenate([jnp.concatenate([w, z], axis=-1), jnp.concatenate([z, w], axis=-1)], axis=-2)


def kernel(x, norm_mix, w_in, ret_gn_g, cmp_pos_k, cmp_w1_k, cmp_b1_k, cmp_w2_k, cmp_pos_v, cmp_w1_v,
           cmp_b1_v, cmp_w2_v, w_ret_o, w_nsa_o, w_out, norm_ffn, w_up, conv_w, conv_b, w_down,
           norm_final):
    B, S, D = x.shape
    assert D == D_MODEL and NSA_GROUPS == 2 and norm_mix.shape[0] == 1
    assert S % 512 == 0 and S >= 1024 and S // CMP_STRIDE <= LANES and S // SEL_BLOCK <= SEL_LANES
    M = B * S
    x2 = x.reshape(M, D)

    off = np.concatenate([[0], np.cumsum(IN_SIZES)])
    w = w_in[0]
    seg = lambda i: w[:, off[i]:off[i + 1]]
    wp = jnp.concatenate(
        [seg(0), seg(1), seg(2), seg(4) * (NSA_DK ** -0.5), seg(7), seg(8), seg(9), seg(10),
         seg(3), seg(12), seg(13), seg(5), seg(6), seg(11),
         jnp.zeros((D, PF_W - PF_NG - IN_SIZES[11]), F32)], axis=1).astype(BF16)

    def cmp_weights(pos, w1, b1, w2):
        return (jnp.tile(pos, (1, NSA_GROUPS)),
                _block_diag_groups(w1.reshape(CMP_BLOCK, NSA_DK, CMP_HIDDEN)).astype(BF16),
                jnp.tile(b1, NSA_GROUPS)[None, :],
                _block_diag_groups(w2).astype(BF16))

    posk, w1k, b1k, w2k = cmp_weights(cmp_pos_k[0], cmp_w1_k[0], cmp_b1_k[0], cmp_w2_k[0])
    posv, w1v, b1v, w2v = cmp_weights(cmp_pos_v[0], cmp_w1_v[0], cmp_b1_v[0], cmp_w2_v[0])

    pb, pf = _inproj(x2, norm_mix, wp, tm=512)
    pb3 = pb.reshape(B, S, PB_W)
    pf3 = pf.reshape(B, S, PF_W)

    ya = _retention(pb3, pf3, ret_gn_g, C=128)
    kcmp, vcmp = _compress(pf3, posk, posv, w1k, w1v, b1k, b1v, w2k, w2v)
    ocmp, sel = _cmpattn(pb3, kcmp, vcmp, pf3, tq=256)
    yb = _selwin(pb3, sel, pf3, ocmp, tq=128, KC=512)

    x1, h2 = _merge(x2, ya.reshape(M, -1), yb.reshape(M, -1), pf,
                    w_ret_o[0].astype(BF16), w_nsa_o[0].astype(BF16), w_out[0].astype(BF16),
                    norm_ffn, tm=512)
    out = _ffn(h2.reshape(B, S, D), x1.reshape(B, S, D), w_up[0].astype(BF16), conv_w[0],
               conv_b, w_down[0].astype(BF16), norm_final[None, :], tf=256)
    return out
```

```python
import functools
import math

import numpy as np
import jax
import jax.numpy as jnp
from jax import lax
from jax.experimental import pallas as pl
from jax.experimental.pallas import tpu as pltpu

F32 = jnp.float32
BF16 = jnp.bfloat16

D_MODEL = 1024
RET_HEADS = 4
RET_DK = 128
RET_DV = 256
NSA_HEADS = 8
NSA_GROUPS = 2
NSA_R = NSA_HEADS // NSA_GROUPS
NSA_DK = 64
CMP_BLOCK = 32
CMP_STRIDE = 16
CMP_HIDDEN = 256
SEL_BLOCK = 64
SEL_TOPK = 8
WINDOW = 512
D_FF = 2816
CONV_W = 3
EPS = 1e-6
NEG = -1e30
FORCE = 1e9
LOWEST = -3e38

IN_SIZES = (RET_HEADS * RET_DK, RET_HEADS * RET_DK, RET_HEADS * RET_DV, RET_HEADS * RET_DV,
            NSA_HEADS * NSA_DK,
            NSA_GROUPS * NSA_DK, NSA_GROUPS * NSA_DK, NSA_GROUPS * NSA_DK,
            NSA_GROUPS * NSA_DK, NSA_GROUPS * NSA_DK, NSA_GROUPS * NSA_DK,
            3 * NSA_HEADS, D_MODEL, D_MODEL)

LANES = 128
VMEM_LIMIT = 56 * 1024 * 1024

PB_RQ, PB_RK, PB_RV, PB_NQ, PB_KS, PB_VS, PB_KW, PB_VW = 0, 512, 1024, 2048, 2560, 2688, 2816, 2944
PB_W = 3072
PF_RG, PF_GA, PF_GB, PF_KCR, PF_VCR, PF_NG = 0, 1024, 2048, 3072, 3200, 3328
PF_W = 3456

NT_DIMS = (((1,), (1,)), ((), ()))
TN_DIMS = (((0,), (0,)), ((), ()))


def _params(*sem):
    return pltpu.CompilerParams(dimension_semantics=sem, vmem_limit_bytes=VMEM_LIMIT)


def _rms(x, g):
    return x * lax.rsqrt(jnp.mean(x * x, axis=-1, keepdims=True) + EPS) * g


def _split3(x):
    hi = x.astype(BF16)
    rest = x - hi.astype(F32)
    mid = rest.astype(BF16)
    return hi, mid, (rest - mid.astype(F32)).astype(BF16)


def _inproj_kernel(x_ref, g_ref, w_ref, pb_ref, pf_ref):
    hb = _rms(x_ref[...], g_ref[...]).astype(BF16)
    for c0 in range(0, PB_W, 512):
        pb_ref[:, c0:c0 + 512] = jnp.dot(
            hb, w_ref[:, c0:c0 + 512], preferred_element_type=F32).astype(BF16)
    for c0 in range(0, PF_W, 512):
        c1 = min(c0 + 512, PF_W)
        pf_ref[:, c0:c1] = jnp.dot(hb, w_ref[:, PB_W + c0:PB_W + c1], preferred_element_type=F32)


def _inproj(x2, g, wp, tm):
    M = x2.shape[0]
    return pl.pallas_call(
        _inproj_kernel,
        grid=(M // tm,),
        in_specs=[pl.BlockSpec((tm, D_MODEL), lambda i: (i, 0)),
                  pl.BlockSpec((1, D_MODEL), lambda i: (0, 0)),
                  pl.BlockSpec((D_MODEL, PB_W + PF_W), lambda i: (0, 0), pipeline_mode=pl.Buffered(1))],
        out_specs=[pl.BlockSpec((tm, PB_W), lambda i: (i, 0)),
                   pl.BlockSpec((tm, PF_W), lambda i: (i, 0))],
        out_shape=[jax.ShapeDtypeStruct((M, PB_W), BF16), jax.ShapeDtypeStruct((M, PF_W), F32)],
        compiler_params=_params("arbitrary"),
        name="inproj",
    )(x2, g, wp)


def _ret_kernel(q_ref, k_ref, v_ref, g_ref, gn_ref, o_ref, r_ref, dmask_ref, *, C):
    log_gamma = [float(np.log1p(-np.exp2(np.float32(-5.0 - h)))) for h in range(RET_HEADS)]
    scale = RET_DK ** -0.5

    @pl.when((pl.program_id(0) == 0) & (pl.program_id(1) == 0))
    def _():
        diff = (lax.broadcasted_iota(jnp.int32, (C, C), 0)
                - lax.broadcasted_iota(jnp.int32, (C, C), 1)).astype(F32)
        for h in range(RET_HEADS):
            dmask_ref[h] = jnp.where(diff >= 0, jnp.exp(log_gamma[h] * jnp.maximum(diff, 0.0)), 0.0) * scale

    @pl.when(pl.program_id(1) == 0)
    def _():
        r_ref[...] = jnp.zeros_like(r_ref)

    pos = lax.broadcasted_iota(jnp.int32, (C, 1), 0).astype(F32)
    for h in range(RET_HEADS):
        lg = log_gamma[h]
        q = q_ref[0, :, h * RET_DK:(h + 1) * RET_DK]
        k = k_ref[0, :, h * RET_DK:(h + 1) * RET_DK]
        v = v_ref[0, :, h * RET_DV:(h + 1) * RET_DV]
        s = lax.dot_general(q, k, NT_DIMS, preferred_element_type=F32)
        o = jnp.dot((s * dmask_ref[h]).astype(BF16), v, preferred_element_type=F32)
        state = r_ref[h]
        xi = jnp.exp(lg * (pos + 1.0))
        o = o + xi * jnp.dot(q, state.astype(BF16), preferred_element_type=F32)
        zeta = jnp.exp(lg * (C - 1.0 - pos)) * scale
        kz = (k.astype(F32) * zeta).astype(BF16)
        kv = lax.dot_general(kz, v, TN_DIMS, preferred_element_type=F32)
        r_ref[h] = math.exp(lg * C) * state + kv
        mu = jnp.mean(o, axis=-1, keepdims=True)
        d = o - mu
        var = jnp.mean(d * d, axis=-1, keepdims=True)
        on = d * lax.rsqrt(var + EPS) * gn_ref[:, h * RET_DV:(h + 1) * RET_DV]
        g = g_ref[0, :, h * RET_DV:(h + 1) * RET_DV]
        o_ref[0, :, h * RET_DV:(h + 1) * RET_DV] = (g * jax.nn.sigmoid(g) * on).astype(BF16)


def _retention(pb3, pf3, gn_g, C):
    B, S, _ = pb3.shape
    HV = RET_HEADS * RET_DV
    HK = RET_HEADS * RET_DK
    return pl.pallas_call(
        functools.partial(_ret_kernel, C=C),
        grid=(B, S // C),
        in_specs=[pl.BlockSpec((1, C, HK), lambda b, n: (b, n, PB_RQ // HK)),
                  pl.BlockSpec((1, C, HK), lambda b, n: (b, n, PB_RK // HK)),
                  pl.BlockSpec((1, C, HV), lambda b, n: (b, n, PB_RV // HV)),
                  pl.BlockSpec((1, C, HV), lambda b, n: (b, n, PF_RG // HV)),
                  pl.BlockSpec((1, HV), lambda b, n: (0, 0))],
        out_specs=pl.BlockSpec((1, C, HV), lambda b, n: (b, n, 0)),
        out_shape=jax.ShapeDtypeStruct((B, S, HV), BF16),
        scratch_shapes=[pltpu.VMEM((RET_HEADS, RET_DK, RET_DV), F32), pltpu.VMEM((RET_HEADS, C, C), F32)],
        compiler_params=_params("arbitrary", "arbitrary"),
        name="retention",
    )(pb3, pb3, pb3, pf3, gn_g)


def _compress_kernel(kin_ref, vin_ref, posk_ref, posv_ref, w1k_ref, w1v_ref, b1k_ref, b1v_ref,
                     w2k_ref, w2v_ref, ko_ref, vo_ref, *, NP):
    GD = NSA_GROUPS * NSA_DK
    npc = CMP_BLOCK // CMP_STRIDE
    row = lax.broadcasted_iota(jnp.int32, (NP, GD), 0)
    for in_ref, pos_ref, w1_ref, b1_ref, w2_ref, out_ref in (
            (kin_ref, posk_ref, w1k_ref, b1k_ref, w2k_ref, ko_ref),
            (vin_ref, posv_ref, w1v_ref, b1v_ref, w2v_ref, vo_ref)):
        parts = []
        for p in range(npc):
            acc = jnp.zeros((NP, NSA_GROUPS * CMP_HIDDEN), F32)
            for i in range(CMP_STRIDE):
                tok = in_ref[0, pl.ds(i, NP, stride=CMP_STRIDE), :]
                j = p * CMP_STRIDE + i
                acc = acc + jnp.dot((tok + pos_ref[j:j + 1, :]).astype(BF16), w1_ref[j],
                                    preferred_element_type=F32)
            parts.append(acc)
        hidden = parts[0]
        for p in range(1, npc):
            hidden = hidden + pltpu.roll(parts[p], NP - p, axis=0)
        act = jax.nn.gelu(hidden + b1_ref[...]).astype(BF16)
        res = jnp.dot(act, w2_ref[...], preferred_element_type=F32)
        res = jnp.where(row < NP - (npc - 1), res, 0.0)
        if out_ref is vo_ref:
            res = res.T
            for g in range(NSA_GROUPS):
                out_ref[0, g] = res[g * NSA_DK:(g + 1) * NSA_DK, :]
        else:
            for g in range(NSA_GROUPS):
                out_ref[0, g] = res[:, g * NSA_DK:(g + 1) * NSA_DK]


def _compress(pf3, posk, posv, w1k, w1v, b1k, b1v, w2k, w2v):
    B, S, _ = pf3.shape
    NP = S // CMP_STRIDE
    GD = NSA_GROUPS * NSA_DK
    GH = NSA_GROUPS * CMP_HIDDEN
    const2 = lambda b: (0, 0)
    const3 = lambda b: (0, 0, 0)
    k_shape = (NSA_GROUPS, NP, NSA_DK)
    v_shape = (NSA_GROUPS, NSA_DK, NP)
    return pl.pallas_call(
        functools.partial(_compress_kernel, NP=NP),
        grid=(B,),
        in_specs=[pl.BlockSpec((1, S, GD), lambda b: (b, 0, PF_KCR // GD)),
                  pl.BlockSpec((1, S, GD), lambda b: (b, 0, PF_VCR // GD)),
                  pl.BlockSpec((CMP_BLOCK, GD), const2), pl.BlockSpec((CMP_BLOCK, GD), const2),
                  pl.BlockSpec((CMP_BLOCK, GD, GH), const3), pl.BlockSpec((CMP_BLOCK, GD, GH), const3),
                  pl.BlockSpec((1, GH), const2), pl.BlockSpec((1, GH), const2),
                  pl.BlockSpec((GH, GD), const2), pl.BlockSpec((GH, GD), const2)],
        out_specs=[pl.BlockSpec((1,) + k_shape, lambda b: (b, 0, 0, 0)),
                   pl.BlockSpec((1,) + v_shape, lambda b: (b, 0, 0, 0))],
        out_shape=[jax.ShapeDtypeStruct((B,) + k_shape, F32), jax.ShapeDtypeStruct((B,) + v_shape, F32)],
        compiler_params=_params("arbitrary"),
        name="compress",
    )(pf3, pf3, posk, posv, w1k, w1v, b1k, b1v, w2k, w2v)


SEL_LANE0 = 64
SEL_LANES = 32
POS_HI_LANE, POS_LO_LANE, PAD_LANE = 96, 97, 98
ONES_LANE = 64
MASK_BIG = 1e30


def _cmpattn_kernel(q_ref, kc_ref, vct_ref, gate_ref, ocmp_ref, sel_ref, *, tq, NP, NS):
    NC = NP - (CMP_BLOCK // CMP_STRIDE - 1)
    t0 = pl.program_id(1) * tq
    tcol = t0 + lax.broadcasted_iota(jnp.int32, (NP, tq), 1)
    cidx = lax.broadcasted_iota(jnp.int32, (NP, tq), 0)
    dcmp = (tcol - (cidx * CMP_STRIDE + (CMP_BLOCK - 1))).astype(F32)
    cmask = (dcmp >= 0) & (cidx < NC)
    jj = lax.broadcasted_iota(jnp.int32, (SEL_LANES, NP), 0)
    cc = lax.broadcasted_iota(jnp.int32, (SEL_LANES, NP), 1)
    overlap_t = ((cc * CMP_STRIDE < jj * SEL_BLOCK + SEL_BLOCK)
                 & (cc * CMP_STRIDE + CMP_BLOCK > jj * SEL_BLOCK) & (jj < NS)).astype(BF16)
    blk = lax.broadcasted_iota(jnp.int32, (SEL_LANES, tq), 0)
    blk_f = blk.astype(F32)
    is_blk = blk < NS
    cur = jnp.right_shift(t0 + lax.broadcasted_iota(jnp.int32, (SEL_LANES, tq), 1),
                          SEL_BLOCK.bit_length() - 1)
    o_t = []
    for g in range(NSA_GROUPS):
        kc_parts = _split3(kc_ref[0, g])
        vct = vct_ref[0, g].astype(BF16)
        psum = jnp.zeros((NP, tq), F32)
        for r in range(NSA_R):
            hh = g * NSA_R + r
            q = q_ref[0, :, hh * NSA_DK:(hh + 1) * NSA_DK]
            s = sum(lax.dot_general(part, q, NT_DIMS, preferred_element_type=F32) for part in kc_parts)
            s = jnp.where(cmask, s - (2.0 ** -(hh + 1)) * dcmp, NEG)
            e = jnp.exp(s - jnp.max(s, axis=0, keepdims=True))
            p = jnp.where(cmask, e / jnp.sum(e, axis=0, keepdims=True), 0.0)
            o_t.append(jnp.dot(vct, p.astype(BF16), preferred_element_type=F32))
            psum = psum + p
        imp = sum(jnp.dot(overlap_t, part, preferred_element_type=F32) for part in _split3(psum))
        imp = jnp.where((blk == 0) | (blk == cur) | (blk == cur - 1), FORCE, imp)
        imp = jnp.where(blk > cur, -FORCE, imp)
        work = jnp.where(is_blk, imp, LOWEST)
        sel = jnp.zeros((SEL_LANES, tq), F32)
        for _ in range(min(SEL_TOPK, NS)):
            top = jnp.max(work, axis=0, keepdims=True)
            first = jnp.min(jnp.where(work == top, blk_f, float(SEL_LANES)), axis=0, keepdims=True)
            pick = blk_f == first
            sel = jnp.where(pick, 1.0, sel)
            work = jnp.where(pick, LOWEST, work)
        sel = jnp.where(is_blk & (blk <= cur), sel - 1.0, jnp.where(is_blk, -1.0, 0.0))
        placed = jnp.concatenate([jnp.zeros((SEL_LANE0, tq), F32), sel,
                                  jnp.zeros((LANES - SEL_LANE0 - SEL_LANES, tq), F32)], axis=0)
        sel_ref[0, g] = placed.T.astype(BF16)
    o = jnp.concatenate(o_t, axis=0).T
    gate = jax.nn.sigmoid(gate_ref[0])
    for hh in range(NSA_HEADS):
        cols = slice(hh * NSA_DK, (hh + 1) * NSA_DK)
        ocmp_ref[0, :, cols] = gate[:, hh:hh + 1] * o[:, cols]


def _cmpattn(pb3, kcmp, vcmp, pf3, tq):
    B, S, _ = pb3.shape
    NP = S // CMP_STRIDE
    NS = S // SEL_BLOCK
    HD = NSA_HEADS * NSA_DK
    return pl.pallas_call(
        functools.partial(_cmpattn_kernel, tq=tq, NP=NP, NS=NS),
        grid=(B, S // tq),
        in_specs=[pl.BlockSpec((1, tq, HD), lambda b, i: (b, i, PB_NQ // HD)),
                  pl.BlockSpec((1, NSA_GROUPS, NP, NSA_DK), lambda b, i: (b, 0, 0, 0)),
                  pl.BlockSpec((1, NSA_GROUPS, NSA_DK, NP), lambda b, i: (b, 0, 0, 0)),
                  pl.BlockSpec((1, tq, LANES), lambda b, i: (b, i, PF_NG // LANES))],
        out_specs=[pl.BlockSpec((1, tq, HD), lambda b, i: (b, i, 0)),
                   pl.BlockSpec((1, NSA_GROUPS, tq, LANES), lambda b, i: (b, 0, i, 0))],
        out_shape=[jax.ShapeDtypeStruct((B, S, HD), F32),
                   jax.ShapeDtypeStruct((B, NSA_GROUPS, S, LANES), BF16)],
        compiler_params=_params("arbitrary", "arbitrary"),
        name="cmpattn",
    )(pb3, kcmp, vcmp, pf3)


def _selwin_kernel(q_ref, ks_ref, vs_ref, kw_ref, vw_ref, sel_ref, wmask_ref, dmask_ref, gate_ref, ocmp_ref,
                   o_ref, ksx, vsx, kwx, vwx, *, tq, KC, S):
    WL = WINDOW + tq
    RB = 512
    i = pl.program_id(1)
    t0 = pl.multiple_of(i * tq, tq)
    sel_shift = SEL_BLOCK.bit_length() - 1

    @pl.when(i == 0)
    def _build():
        lane = lax.broadcasted_iota(jnp.int32, (RB, LANES), 1)
        is_head = lane < NSA_DK
        pad_k = jnp.where(lane == PAD_LANE, MASK_BIG, 0.0).astype(BF16)
        for g in range(NSA_GROUPS):
            kwx[g, 0:WINDOW, :] = pad_k[:WINDOW]
            vwx[g, 0:WINDOW, :] = jnp.zeros((WINDOW, LANES), BF16)
        for piece in range(S // RB):
            r0 = piece * RB
            pos = r0 + lax.broadcasted_iota(jnp.int32, (RB, LANES), 0)
            alibi = jnp.where(lane == POS_HI_LANE, jnp.right_shift(pos, sel_shift),
                              jnp.where(lane == POS_LO_LANE, pos & (SEL_BLOCK - 1), 0)).astype(F32)
            onehot = jnp.where(lane - SEL_LANE0 == jnp.right_shift(pos, sel_shift), MASK_BIG, 0.0)
            k_extra_win = alibi.astype(BF16)
            k_extra_sel = (alibi + onehot).astype(BF16)
            v_extra = jnp.where(lane == ONES_LANE, 1.0, 0.0).astype(BF16)
            for g in range(NSA_GROUPS):
                def head_lanes(ref):
                    t = ref[0, r0:r0 + RB, :]
                    return t if g == 0 else jnp.concatenate([t[:, NSA_DK:], t[:, :NSA_DK]], axis=1)
                ksx[g, r0:r0 + RB, :] = jnp.where(is_head, head_lanes(ks_ref), k_extra_sel)
                vsx[g, r0:r0 + RB, :] = jnp.where(is_head, head_lanes(vs_ref), v_extra)
                kwx[g, WINDOW + r0:WINDOW + r0 + RB, :] = jnp.where(is_head, head_lanes(kw_ref), k_extra_win)
                vwx[g, WINDOW + r0:WINDOW + r0 + RB, :] = jnp.where(is_head, head_lanes(vw_ref), v_extra)

    gate = jax.nn.sigmoid(gate_ref[0])
    lane = lax.broadcasted_iota(jnp.int32, (tq, LANES), 1)
    is_head = lane < NSA_DK
    diag_chunk = t0 // KC
    wmask = wmask_ref[...][None]
    dmask = dmask_ref[i % (KC // tq)][None]

    def masked_rows(s, mask):
        n = s.shape[-1]
        return (s.reshape(NSA_R, tq, n) + mask).reshape(NSA_R * tq, n)

    def normalise(acc):
        return acc[:, :NSA_DK] / acc[:, ONES_LANE:ONES_LANE + 1]

    def query_rows(g):
        sel = sel_ref[0, g].astype(F32)
        rows = []
        for r in range(NSA_R):
            hh = g * NSA_R + r
            qh = q_ref[0, :, (hh // 2) * LANES:(hh // 2 + 1) * LANES].astype(F32)
            if hh % 2:
                qh = jnp.concatenate([qh[:, NSA_DK:], qh[:, :NSA_DK]], axis=1)
            slope = 2.0 ** -(hh + 1)
            extra = jnp.where(lane == POS_HI_LANE, SEL_BLOCK * slope,
                              jnp.where(lane == POS_LO_LANE, slope, jnp.where(lane == PAD_LANE, -1.0, sel)))
            rows.append(jnp.where(is_head, qh, extra).astype(BF16))
        return jnp.concatenate(rows, axis=0)

    qs = [query_rows(g) for g in range(NSA_GROUPS)]

    o_win = []
    for g in range(NSA_GROUPS):
        s = masked_rows(lax.dot_general(qs[g], kwx[g, pl.ds(t0, WL), :], NT_DIMS,
                                        preferred_element_type=F32), wmask)
        p = jnp.exp(s - jnp.max(s, axis=-1, keepdims=True))
        o_win.append(normalise(jnp.dot(p.astype(BF16), vwx[g, pl.ds(t0, WL), :],
                                       preferred_element_type=F32)))

    def chunk_scores(g, c):
        k0 = pl.multiple_of(c * KC, KC)
        return lax.dot_general(qs[g], ksx[g, pl.ds(k0, KC), :], NT_DIMS, preferred_element_type=F32)

    def chunk_values(g, c, p):
        k0 = pl.multiple_of(c * KC, KC)
        return jnp.dot(p.astype(BF16), vsx[g, pl.ds(k0, KC), :], preferred_element_type=F32)

    state = []
    for g in range(NSA_GROUPS):
        s = masked_rows(chunk_scores(g, diag_chunk), dmask)
        m = jnp.max(s, axis=-1, keepdims=True)
        state += [m, chunk_values(g, diag_chunk, jnp.exp(s - m))]

    def body(c, carry):
        out = []
        for g in range(NSA_GROUPS):
            m, acc = carry[2 * g], carry[2 * g + 1]
            s = chunk_scores(g, c)
            m_new = jnp.maximum(m, jnp.max(s, axis=-1, keepdims=True))
            out += [m_new, jnp.exp(m - m_new) * acc + chunk_values(g, c, jnp.exp(s - m_new))]
        return tuple(out)

    state = lax.fori_loop(0, diag_chunk, body, tuple(state))

    for g in range(NSA_GROUPS):
        o_sel = normalise(state[2 * g + 1])
        for r in range(NSA_R):
            hh = g * NSA_R + r
            g_sel = gate[:, NSA_HEADS + hh:NSA_HEADS + hh + 1]
            g_win = gate[:, 2 * NSA_HEADS + hh:2 * NSA_HEADS + hh + 1]
            piece = g_sel * o_sel[r * tq:(r + 1) * tq] + g_win * o_win[g][r * tq:(r + 1) * tq]
            o_ref[0, :, hh * NSA_DK:(hh + 1) * NSA_DK] = (
                ocmp_ref[0, :, hh * NSA_DK:(hh + 1) * NSA_DK] + piece).astype(BF16)


def _selwin(pb3, sel, pf3, ocmp, tq, KC):
    B, S, _ = pb3.shape
    HD = NSA_HEADS * NSA_DK
    WL = WINDOW + tq
    tr = jnp.arange(tq)[:, None]
    kc = jnp.arange(WL)[None, :]
    wmask = jnp.where((kc > tr) & (kc <= tr + WINDOW), 0.0, NEG).astype(F32)
    off = (jnp.arange(KC // tq) * tq)[:, None, None]
    dmask = jnp.where(jnp.arange(KC)[None, None, :] <= off + tr[None], 0.0, NEG).astype(F32)
    kv_spec = lambda off: pl.BlockSpec((1, S, LANES), lambda b, i: (b, 0, off // LANES))
    return pl.pallas_call(
        functools.partial(_selwin_kernel, tq=tq, KC=KC, S=S),
        grid=(B, S // tq),
        in_specs=[pl.BlockSpec((1, tq, HD), lambda b, i: (b, i, PB_NQ // HD)),
                  kv_spec(PB_KS), kv_spec(PB_VS), kv_spec(PB_KW), kv_spec(PB_VW),
                  pl.BlockSpec((1, NSA_GROUPS, tq, LANES), lambda b, i: (b, 0, i, 0)),
                  pl.BlockSpec((tq, WL), lambda b, i: (0, 0)),
                  pl.BlockSpec((KC // tq, tq, KC), lambda b, i: (0, 0, 0)),
                  pl.BlockSpec((1, tq, LANES), lambda b, i: (b, i, PF_NG // LANES)),
                  pl.BlockSpec((1, tq, HD), lambda b, i: (b, i, 0))],
        out_specs=pl.BlockSpec((1, tq, HD), lambda b, i: (b, i, 0)),
        out_shape=jax.ShapeDtypeStruct((B, S, HD), BF16),
        scratch_shapes=[pltpu.VMEM((NSA_GROUPS, S, LANES), BF16), pltpu.VMEM((NSA_GROUPS, S, LANES), BF16),
                        pltpu.VMEM((NSA_GROUPS, WINDOW + S, LANES), BF16),
                        pltpu.VMEM((NSA_GROUPS, WINDOW + S, LANES), BF16)],
        compiler_params=_params("arbitrary", "arbitrary"),
        name="selwin",
    )(pb3, pb3, pb3, pb3, pb3, sel, wmask, dmask, pf3, ocmp)


def _merge_kernel(x_ref, ya_ref, yb_ref, ga_ref, gb_ref, wa_ref, wb_ref, wo_ref, g_ref,
                  x1_ref, h2_ref):
    y_a = jnp.dot(ya_ref[...], wa_ref[...], preferred_element_type=F32)
    y_b = jnp.dot(yb_ref[...], wb_ref[...], preferred_element_type=F32)
    merged = jax.nn.sigmoid(ga_ref[...]) * y_a + jax.nn.sigmoid(gb_ref[...]) * y_b
    x1 = x_ref[...] + jnp.dot(merged.astype(BF16), wo_ref[...], preferred_element_type=F32)
    x1_ref[...] = x1
    h2_ref[...] = _rms(x1, g_ref[...]).astype(BF16)


def _merge(x2, ya, yb, pf, wa, wb, wo, g, tm):
    M = x2.shape[0]
    HV = RET_HEADS * RET_DV
    HD = NSA_HEADS * NSA_DK
    const = lambda i: (0, 0)
    return pl.pallas_call(
        _merge_kernel,
        grid=(M // tm,),
        in_specs=[pl.BlockSpec((tm, D_MODEL), lambda i: (i, 0)),
                  pl.BlockSpec((tm, HV), lambda i: (i, 0)),
                  pl.BlockSpec((tm, HD), lambda i: (i, 0)),
                  pl.BlockSpec((tm, D_MODEL), lambda i: (i, PF_GA // D_MODEL)),
                  pl.BlockSpec((tm, D_MODEL), lambda i: (i, PF_GB // D_MODEL)),
                  pl.BlockSpec((HV, D_MODEL), const), pl.BlockSpec((HD, D_MODEL), const),
                  pl.BlockSpec((D_MODEL, D_MODEL), const), pl.BlockSpec((1, D_MODEL), const)],
        out_specs=[pl.BlockSpec((tm, D_MODEL), lambda i: (i, 0)),
                   pl.BlockSpec((tm, D_MODEL), lambda i: (i, 0))],
        out_shape=[jax.ShapeDtypeStruct((M, D_MODEL), F32), jax.ShapeDtypeStruct((M, D_MODEL), BF16)],
        compiler_params=_params("arbitrary"),
        name="merge",
    )(x2, ya, yb, pf, pf, wa, wb, wo, g)


SUBLANES = 8


def _ffn_kernel(h_ref, x1_ref, wup_ref, cw_ref, cb_ref, wd_ref, g_ref, o_ref, tail_ref, u_ref, *, tm, tf):
    @pl.when(pl.program_id(1) == 0)
    def _():
        tail_ref[...] = jnp.zeros_like(tail_ref)

    h = h_ref[0]
    top = lax.broadcasted_iota(jnp.int32, (SUBLANES, tf), 0)
    for c in range(D_FF // tf):
        cols = slice(c * tf, (c + 1) * tf)
        a = jnp.dot(h, wup_ref[:, cols], preferred_element_type=F32)
        b = jnp.dot(h, wup_ref[:, D_FF + c * tf:D_FF + (c + 1) * tf], preferred_element_type=F32)
        tail = tail_ref[c]
        tail_ref[c] = a[tm - SUBLANES:]
        ac = cb_ref[:, cols] + cw_ref[CONV_W - 1:CONV_W, cols] * a
        for d in range(1, CONV_W):
            sh = pltpu.roll(a, d, axis=0)
            head = jnp.where(top < d, pltpu.roll(tail, d, axis=0), sh[:SUBLANES])
            sh = jnp.concatenate([head, sh[SUBLANES:]], axis=0)
            ac = ac + cw_ref[CONV_W - 1 - d:CONV_W - d, cols] * sh
        u_ref[:, cols] = (jax.nn.gelu(ac) * b).astype(BF16)
    y = jnp.dot(u_ref[...], wd_ref[...], preferred_element_type=F32)
    o_ref[0] = _rms(x1_ref[0] + y, g_ref[...])


def _ffn(h2, x1, w_up, conv_w, conv_b, w_down, g, tm, tf):
    B, S, _ = h2.shape
    const = lambda b, i: (0, 0)
    once = pl.Buffered(1)
    return pl.pallas_call(
        functools.partial(_ffn_kernel, tm=tm, tf=tf),
        grid=(B, S // tm),
        in_specs=[pl.BlockSpec((1, tm, D_MODEL), lambda b, i: (b, i, 0)),
                  pl.BlockSpec((1, tm, D_MODEL), lambda b, i: (b, i, 0)),
                  pl.BlockSpec((D_MODEL, 2 * D_FF), const, pipeline_mode=once),
                  pl.BlockSpec((CONV_W, D_FF), const, pipeline_mode=once),
                  pl.BlockSpec((1, D_FF), const, pipeline_mode=once),
                  pl.BlockSpec((D_FF, D_MODEL), const, pipeline_mode=once),
                  pl.BlockSpec((1, D_MODEL), const, pipeline_mode=once)],
        out_specs=pl.BlockSpec((1, tm, D_MODEL), lambda b, i: (b, i, 0)),
        out_shape=jax.ShapeDtypeStruct((B, S, D_MODEL), F32),
        scratch_shapes=[pltpu.VMEM((D_FF // tf, SUBLANES, tf), F32), pltpu.VMEM((tm, D_FF), BF16)],
        compiler_params=_params("arbitrary", "arbitrary"),
        name="convffn",
    )(h2, x1, w_up, conv_w, conv_b, w_down, g)


def _block_diag_groups(w):
    z = jnp.zeros_like(w)
    return jnp.concatenate([jnp.concatenate([w, z], axis=-1), jnp.concatenate([z, w], axis=-1)], axis=-2)


def kernel(x, norm_mix, w_in, ret_gn_g, cmp_pos_k, cmp_w1_k, cmp_b1_k, cmp_w2_k, cmp_pos_v, cmp_w1_v,
           cmp_b1_v, cmp_w2_v, w_ret_o, w_nsa_o, w_out, norm_ffn, w_up, conv_w, conv_b, w_down,
           norm_final):
    B, S, D = x.shape
    assert D == D_MODEL and NSA_GROUPS == 2 and norm_mix.shape[0] == 1
    assert S % 512 == 0 and S >= 1024 and S // CMP_STRIDE <= LANES and S // SEL_BLOCK <= SEL_LANES
    M = B * S
    x2 = x.reshape(M, D)

    off = np.concatenate([[0], np.cumsum(IN_SIZES)])
    w = w_in[0]
    seg = lambda i: w[:, off[i]:off[i + 1]]
    wp = jnp.concatenate(
        [seg(0), seg(1), seg(2), seg(4) * (NSA_DK ** -0.5), seg(7), seg(8), seg(9), seg(10),
         seg(3), seg(12), seg(13), seg(5), seg(6), seg(11),
         jnp.zeros((D, PF_W - PF_NG - IN_SIZES[11]), F32)], axis=1).astype(BF16)

    def cmp_weights(pos, w1, b1, w2):
        return (jnp.tile(pos, (1, NSA_GROUPS)),
                _block_diag_groups(w1.reshape(CMP_BLOCK, NSA_DK, CMP_HIDDEN)).astype(BF16),
                jnp.tile(b1, NSA_GROUPS)[None, :],
                _block_diag_groups(w2).astype(BF16))

    posk, w1k, b1k, w2k = cmp_weights(cmp_pos_k[0], cmp_w1_k[0], cmp_b1_k[0], cmp_w2_k[0])
    posv, w1v, b1v, w2v = cmp_weights(cmp_pos_v[0], cmp_w1_v[0], cmp_b1_v[0], cmp_w2_v[0])

    pb, pf = _inproj(x2, norm_mix, wp, tm=512)
    pb3 = pb.reshape(B, S, PB_W)
    pf3 = pf.reshape(B, S, PF_W)

    ya = _retention(pb3, pf3, ret_gn_g, C=256)
    kcmp, vcmp = _compress(pf3, posk, posv, w1k, w1v, b1k, b1v, w2k, w2v)
    ocmp, sel = _cmpattn(pb3, kcmp, vcmp, pf3, tq=256)
    yb = _selwin(pb3, sel, pf3, ocmp, tq=128, KC=512)

    x1, h2 = _merge(x2, ya.reshape(M, -1), yb.reshape(M, -1), pf,
                    w_ret_o[0].astype(BF16), w_nsa_o[0].astype(BF16), w_out[0].astype(BF16),
                    norm_ffn, tm=512)
    out = _ffn(h2.reshape(B, S, D), x1.reshape(B, S, D), w_up[0].astype(BF16), conv_w[0],
               conv_b, w_down[0].astype(BF16), norm_final[None, :], tm=512, tf=256)
    return out
```

```python
import functools
import math

import numpy as np
import jax
import jax.numpy as jnp
from jax import lax
from jax.experimental import pallas as pl
from jax.experimental.pallas import tpu as pltpu

F32 = jnp.float32
BF16 = jnp.bfloat16

D_MODEL = 1024
RET_HEADS = 4
RET_DK = 128
RET_DV = 256
NSA_HEADS = 8
NSA_GROUPS = 2
NSA_R = NSA_HEADS // NSA_GROUPS
NSA_DK = 64
CMP_BLOCK = 32
CMP_STRIDE = 16
CMP_HIDDEN = 256
SEL_BLOCK = 64
SEL_TOPK = 8
WINDOW = 512
D_FF = 2816
CONV_W = 3
EPS = 1e-6
NEG = -1e30
FORCE = 1e9
LOWEST = -3e38

IN_SIZES = (RET_HEADS * RET_DK, RET_HEADS * RET_DK, RET_HEADS * RET_DV, RET_HEADS * RET_DV,
            NSA_HEADS * NSA_DK,
            NSA_GROUPS * NSA_DK, NSA_GROUPS * NSA_DK, NSA_GROUPS * NSA_DK,
            NSA_GROUPS * NSA_DK, NSA_GROUPS * NSA_DK, NSA_GROUPS * NSA_DK,
            3 * NSA_HEADS, D_MODEL, D_MODEL)

LANES = 128
VMEM_LIMIT = 56 * 1024 * 1024

PB_RQ, PB_RK, PB_RV, PB_NQ, PB_KS, PB_VS, PB_KW, PB_VW = 0, 512, 1024, 2048, 2560, 2688, 2816, 2944
PB_RG, PB_GA, PB_GB = 3072, 4096, 5120
PB_W = 6144
PF_KCR, PF_VCR, PF_NG = 0, 128, 256
PF_W = 384

NT_DIMS = (((1,), (1,)), ((), ()))
TN_DIMS = (((0,), (0,)), ((), ()))


def _params(*sem):
    return pltpu.CompilerParams(dimension_semantics=sem, vmem_limit_bytes=VMEM_LIMIT)


def _rms(x, g):
    return x * lax.rsqrt(jnp.mean(x * x, axis=-1, keepdims=True) + EPS) * g


def _split3(x):
    hi = x.astype(BF16)
    rest = x - hi.astype(F32)
    mid = rest.astype(BF16)
    return hi, mid, (rest - mid.astype(F32)).astype(BF16)


def _inproj_kernel(x_ref, g_ref, w_ref, pb_ref, pf_ref):
    hb = _rms(x_ref[...], g_ref[...]).astype(BF16)
    for c0 in range(0, PB_W, 512):
        pb_ref[:, c0:c0 + 512] = jnp.dot(
            hb, w_ref[:, c0:c0 + 512], preferred_element_type=F32).astype(BF16)
    for c0 in range(0, PF_W, 512):
        c1 = min(c0 + 512, PF_W)
        pf_ref[:, c0:c1] = jnp.dot(hb, w_ref[:, PB_W + c0:PB_W + c1], preferred_element_type=F32)


def _inproj(x2, g, wp, tm):
    M = x2.shape[0]
    return pl.pallas_call(
        _inproj_kernel,
        grid=(M // tm,),
        in_specs=[pl.BlockSpec((tm, D_MODEL), lambda i: (i, 0)),
                  pl.BlockSpec((1, D_MODEL), lambda i: (0, 0)),
                  pl.BlockSpec((D_MODEL, PB_W + PF_W), lambda i: (0, 0), pipeline_mode=pl.Buffered(1))],
        out_specs=[pl.BlockSpec((tm, PB_W), lambda i: (i, 0)),
                   pl.BlockSpec((tm, PF_W), lambda i: (i, 0))],
        out_shape=[jax.ShapeDtypeStruct((M, PB_W), BF16), jax.ShapeDtypeStruct((M, PF_W), F32)],
        compiler_params=_params("arbitrary"),
        name="inproj",
    )(x2, g, wp)


def _ret_kernel(q_ref, k_ref, v_ref, g_ref, gn_ref, o_ref, r_ref, dmask_ref, *, C):
    log_gamma = [float(np.log1p(-np.exp2(np.float32(-5.0 - h)))) for h in range(RET_HEADS)]
    scale = RET_DK ** -0.5

    @pl.when((pl.program_id(0) == 0) & (pl.program_id(1) == 0))
    def _():
        diff = (lax.broadcasted_iota(jnp.int32, (C, C), 0)
                - lax.broadcasted_iota(jnp.int32, (C, C), 1)).astype(F32)
        for h in range(RET_HEADS):
            dmask_ref[h] = jnp.where(diff >= 0, jnp.exp(log_gamma[h] * jnp.maximum(diff, 0.0)), 0.0) * scale

    @pl.when(pl.program_id(1) == 0)
    def _():
        r_ref[...] = jnp.zeros_like(r_ref)

    pos = lax.broadcasted_iota(jnp.int32, (C, 1), 0).astype(F32)
    for h in range(RET_HEADS):
        lg = log_gamma[h]
        q = q_ref[0, :, h * RET_DK:(h + 1) * RET_DK]
        k = k_ref[0, :, h * RET_DK:(h + 1) * RET_DK]
        v = v_ref[0, :, h * RET_DV:(h + 1) * RET_DV]
        s = lax.dot_general(q, k, NT_DIMS, preferred_element_type=F32)
        o = jnp.dot((s * dmask_ref[h]).astype(BF16), v, preferred_element_type=F32)
        state = r_ref[h]
        xi = jnp.exp(lg * (pos + 1.0))
        o = o + xi * jnp.dot(q, state.astype(BF16), preferred_element_type=F32)
        zeta = jnp.exp(lg * (C - 1.0 - pos)) * scale
        kz = (k.astype(F32) * zeta).astype(BF16)
        kv = lax.dot_general(kz, v, TN_DIMS, preferred_element_type=F32)
        r_ref[h] = math.exp(lg * C) * state + kv
        mu = jnp.mean(o, axis=-1, keepdims=True)
        d = o - mu
        var = jnp.mean(d * d, axis=-1, keepdims=True)
        on = d * lax.rsqrt(var + EPS) * gn_ref[:, h * RET_DV:(h + 1) * RET_DV]
        g = g_ref[0, :, h * RET_DV:(h + 1) * RET_DV].astype(F32)
        o_ref[0, :, h * RET_DV:(h + 1) * RET_DV] = (g * jax.nn.sigmoid(g) * on).astype(BF16)


def _retention(pb3, gn_g, C):
    B, S, _ = pb3.shape
    HV = RET_HEADS * RET_DV
    HK = RET_HEADS * RET_DK
    return pl.pallas_call(
        functools.partial(_ret_kernel, C=C),
        grid=(B, S // C),
        in_specs=[pl.BlockSpec((1, C, HK), lambda b, n: (b, n, PB_RQ // HK)),
                  pl.BlockSpec((1, C, HK), lambda b, n: (b, n, PB_RK // HK)),
                  pl.BlockSpec((1, C, HV), lambda b, n: (b, n, PB_RV // HV)),
                  pl.BlockSpec((1, C, HV), lambda b, n: (b, n, PB_RG // HV)),
                  pl.BlockSpec((1, HV), lambda b, n: (0, 0))],
        out_specs=pl.BlockSpec((1, C, HV), lambda b, n: (b, n, 0)),
        out_shape=jax.ShapeDtypeStruct((B, S, HV), BF16),
        scratch_shapes=[pltpu.VMEM((RET_HEADS, RET_DK, RET_DV), F32), pltpu.VMEM((RET_HEADS, C, C), F32)],
        compiler_params=_params("arbitrary", "arbitrary"),
        name="retention",
    )(pb3, pb3, pb3, pb3, gn_g)


def _compress_kernel(kin_ref, vin_ref, posk_ref, posv_ref, w1k_ref, w1v_ref, b1k_ref, b1v_ref,
                     w2k_ref, w2v_ref, ko_ref, vo_ref, *, NP):
    GD = NSA_GROUPS * NSA_DK
    npc = CMP_BLOCK // CMP_STRIDE
    row = lax.broadcasted_iota(jnp.int32, (NP, GD), 0)
    for in_ref, pos_ref, w1_ref, b1_ref, w2_ref, out_ref in (
            (kin_ref, posk_ref, w1k_ref, b1k_ref, w2k_ref, ko_ref),
            (vin_ref, posv_ref, w1v_ref, b1v_ref, w2v_ref, vo_ref)):
        parts = []
        for p in range(npc):
            acc = jnp.zeros((NP, NSA_GROUPS * CMP_HIDDEN), F32)
            for i in range(CMP_STRIDE):
                tok = in_ref[0, pl.ds(i, NP, stride=CMP_STRIDE), :]
                j = p * CMP_STRIDE + i
                acc = acc + jnp.dot((tok + pos_ref[j:j + 1, :]).astype(BF16), w1_ref[j],
                                    preferred_element_type=F32)
            parts.append(acc)
        hidden = parts[0]
        for p in range(1, npc):
            hidden = hidden + pltpu.roll(parts[p], NP - p, axis=0)
        act = jax.nn.gelu(hidden + b1_ref[...]).astype(BF16)
        res = jnp.dot(act, w2_ref[...], preferred_element_type=F32)
        res = jnp.where(row < NP - (npc - 1), res, 0.0)
        if out_ref is vo_ref:
            res = res.T
            for g in range(NSA_GROUPS):
                out_ref[0, g] = res[g * NSA_DK:(g + 1) * NSA_DK, :]
        else:
            for g in range(NSA_GROUPS):
                out_ref[0, g] = res[:, g * NSA_DK:(g + 1) * NSA_DK]


def _compress(pf3, posk, posv, w1k, w1v, b1k, b1v, w2k, w2v):
    B, S, _ = pf3.shape
    NP = S // CMP_STRIDE
    GD = NSA_GROUPS * NSA_DK
    GH = NSA_GROUPS * CMP_HIDDEN
    const2 = lambda b: (0, 0)
    const3 = lambda b: (0, 0, 0)
    k_shape = (NSA_GROUPS, NP, NSA_DK)
    v_shape = (NSA_GROUPS, NSA_DK, NP)
    return pl.pallas_call(
        functools.partial(_compress_kernel, NP=NP),
        grid=(B,),
        in_specs=[pl.BlockSpec((1, S, GD), lambda b: (b, 0, PF_KCR // GD)),
                  pl.BlockSpec((1, S, GD), lambda b: (b, 0, PF_VCR // GD)),
                  pl.BlockSpec((CMP_BLOCK, GD), const2), pl.BlockSpec((CMP_BLOCK, GD), const2),
                  pl.BlockSpec((CMP_BLOCK, GD, GH), const3), pl.BlockSpec((CMP_BLOCK, GD, GH), const3),
                  pl.BlockSpec((1, GH), const2), pl.BlockSpec((1, GH), const2),
                  pl.BlockSpec((GH, GD), const2), pl.BlockSpec((GH, GD), const2)],
        out_specs=[pl.BlockSpec((1,) + k_shape, lambda b: (b, 0, 0, 0)),
                   pl.BlockSpec((1,) + v_shape, lambda b: (b, 0, 0, 0))],
        out_shape=[jax.ShapeDtypeStruct((B,) + k_shape, F32), jax.ShapeDtypeStruct((B,) + v_shape, F32)],
        compiler_params=_params("arbitrary"),
        name="compress",
    )(pf3, pf3, posk, posv, w1k, w1v, b1k, b1v, w2k, w2v)


SEL_LANE0 = 64
SEL_LANES = 32
POS_HI_LANE, POS_LO_LANE, PAD_LANE = 96, 97, 98
ONES_LANE = 64
MASK_BIG = 1e30


def _cmpattn_kernel(q_ref, kc_ref, vct_ref, gate_ref, ocmp_ref, sel_ref, *, tq, NP, NS):
    NC = NP - (CMP_BLOCK // CMP_STRIDE - 1)
    t0 = pl.program_id(1) * tq
    tcol = t0 + lax.broadcasted_iota(jnp.int32, (NP, tq), 1)
    cidx = lax.broadcasted_iota(jnp.int32, (NP, tq), 0)
    dcmp = (tcol - (cidx * CMP_STRIDE + (CMP_BLOCK - 1))).astype(F32)
    cmask = (dcmp >= 0) & (cidx < NC)
    jj = lax.broadcasted_iota(jnp.int32, (SEL_LANES, NP), 0)
    cc = lax.broadcasted_iota(jnp.int32, (SEL_LANES, NP), 1)
    overlap_t = ((cc * CMP_STRIDE < jj * SEL_BLOCK + SEL_BLOCK)
                 & (cc * CMP_STRIDE + CMP_BLOCK > jj * SEL_BLOCK) & (jj < NS)).astype(BF16)
    blk = lax.broadcasted_iota(jnp.int32, (SEL_LANES, tq), 0)
    blk_f = blk.astype(F32)
    is_blk = blk < NS
    cur = jnp.right_shift(t0 + lax.broadcasted_iota(jnp.int32, (SEL_LANES, tq), 1),
                          SEL_BLOCK.bit_length() - 1)
    o_t = []
    for g in range(NSA_GROUPS):
        kc_parts = _split3(kc_ref[0, g])
        vct = vct_ref[0, g].astype(BF16)
        psum = jnp.zeros((NP, tq), F32)
        for r in range(NSA_R):
            hh = g * NSA_R + r
            q = q_ref[0, :, hh * NSA_DK:(hh + 1) * NSA_DK]
            s = sum(lax.dot_general(part, q, NT_DIMS, preferred_element_type=F32) for part in kc_parts)
            s = jnp.where(cmask, s - (2.0 ** -(hh + 1)) * dcmp, NEG)
            e = jnp.exp(s - jnp.max(s, axis=0, keepdims=True))
            p = jnp.where(cmask, e / jnp.sum(e, axis=0, keepdims=True), 0.0)
            o_t.append(jnp.dot(vct, p.astype(BF16), preferred_element_type=F32))
            psum = psum + p
        imp = sum(jnp.dot(overlap_t, part, preferred_element_type=F32) for part in _split3(psum))
        imp = jnp.where((blk == 0) | (blk == cur) | (blk == cur - 1), FORCE, imp)
        imp = jnp.where(blk > cur, -FORCE, imp)
        work = jnp.where(is_blk, imp, LOWEST)
        sel = jnp.zeros((SEL_LANES, tq), F32)
        for _ in range(min(SEL_TOPK, NS)):
            top = jnp.max(work, axis=0, keepdims=True)
            first = jnp.min(jnp.where(work == top, blk_f, float(SEL_LANES)), axis=0, keepdims=True)
            pick = blk_f == first
            sel = jnp.where(pick, 1.0, sel)
            work = jnp.where(pick, LOWEST, work)
        sel = jnp.where(is_blk & (blk <= cur), sel - 1.0, jnp.where(is_blk, -1.0, 0.0))
        placed = jnp.concatenate([jnp.zeros((SEL_LANE0, tq), F32), sel,
                                  jnp.zeros((LANES - SEL_LANE0 - SEL_LANES, tq), F32)], axis=0)
        sel_ref[0, g] = placed.T.astype(BF16)
    o = jnp.concatenate(o_t, axis=0).T
    gate = jax.nn.sigmoid(gate_ref[0])
    for hh in range(NSA_HEADS):
        cols = slice(hh * NSA_DK, (hh + 1) * NSA_DK)
        ocmp_ref[0, :, cols] = gate[:, hh:hh + 1] * o[:, cols]


def _cmpattn(pb3, kcmp, vcmp, pf3, tq):
    B, S, _ = pb3.shape
    NP = S // CMP_STRIDE
    NS = S // SEL_BLOCK
    HD = NSA_HEADS * NSA_DK
    return pl.pallas_call(
        functools.partial(_cmpattn_kernel, tq=tq, NP=NP, NS=NS),
        grid=(B, S // tq),
        in_specs=[pl.BlockSpec((1, tq, HD), lambda b, i: (b, i, PB_NQ // HD)),
                  pl.BlockSpec((1, NSA_GROUPS, NP, NSA_DK), lambda b, i: (b, 0, 0, 0)),
                  pl.BlockSpec((1, NSA_GROUPS, NSA_DK, NP), lambda b, i: (b, 0, 0, 0)),
                  pl.BlockSpec((1, tq, LANES), lambda b, i: (b, i, PF_NG // LANES))],
        out_specs=[pl.BlockSpec((1, tq, HD), lambda b, i: (b, i, 0)),
                   pl.BlockSpec((1, NSA_GROUPS, tq, LANES), lambda b, i: (b, 0, i, 0))],
        out_shape=[jax.ShapeDtypeStruct((B, S, HD), F32),
                   jax.ShapeDtypeStruct((B, NSA_GROUPS, S, LANES), BF16)],
        compiler_params=_params("arbitrary", "arbitrary"),
        name="cmpattn",
    )(pb3, kcmp, vcmp, pf3)


def _selwin_kernel(q_ref, ks_ref, vs_ref, kw_ref, vw_ref, sel_ref, wmask_ref, dmask_ref, gate_ref, ocmp_ref,
                   o_ref, ksx, vsx, kwx, vwx, *, tq, KC, S):
    WL = WINDOW + tq
    RB = 512
    i = pl.program_id(1)
    t0 = pl.multiple_of(i * tq, tq)
    sel_shift = SEL_BLOCK.bit_length() - 1

    @pl.when(i == 0)
    def _build():
        lane = lax.broadcasted_iota(jnp.int32, (RB, LANES), 1)
        is_head = lane < NSA_DK
        pad_k = jnp.where(lane == PAD_LANE, MASK_BIG, 0.0).astype(BF16)
        for g in range(NSA_GROUPS):
            kwx[g, 0:WINDOW, :] = pad_k[:WINDOW]
            vwx[g, 0:WINDOW, :] = jnp.zeros((WINDOW, LANES), BF16)
        for piece in range(S // RB):
            r0 = piece * RB
            pos = r0 + lax.broadcasted_iota(jnp.int32, (RB, LANES), 0)
            alibi = jnp.where(lane == POS_HI_LANE, jnp.right_shift(pos, sel_shift),
                              jnp.where(lane == POS_LO_LANE, pos & (SEL_BLOCK - 1), 0)).astype(F32)
            onehot = jnp.where(lane - SEL_LANE0 == jnp.right_shift(pos, sel_shift), MASK_BIG, 0.0)
            k_extra_win = alibi.astype(BF16)
            k_extra_sel = (alibi + onehot).astype(BF16)
            v_extra = jnp.where(lane == ONES_LANE, 1.0, 0.0).astype(BF16)
            for g in range(NSA_GROUPS):
                def head_lanes(ref):
                    t = ref[0, r0:r0 + RB, :]
                    return t if g == 0 else jnp.concatenate([t[:, NSA_DK:], t[:, :NSA_DK]], axis=1)
                ksx[g, r0:r0 + RB, :] = jnp.where(is_head, head_lanes(ks_ref), k_extra_sel)
                vsx[g, r0:r0 + RB, :] = jnp.where(is_head, head_lanes(vs_ref), v_extra)
                kwx[g, WINDOW + r0:WINDOW + r0 + RB, :] = jnp.where(is_head, head_lanes(kw_ref), k_extra_win)
                vwx[g, WINDOW + r0:WINDOW + r0 + RB, :] = jnp.where(is_head, head_lanes(vw_ref), v_extra)

    gate = jax.nn.sigmoid(gate_ref[0])
    lane = lax.broadcasted_iota(jnp.int32, (tq, LANES), 1)
    is_head = lane < NSA_DK
    diag_chunk = t0 // KC
    wmask = wmask_ref[...][None]
    dmask = dmask_ref[i % (KC // tq)][None]

    def masked_rows(s, mask):
        n = s.shape[-1]
        return (s.reshape(NSA_R, tq, n) + mask).reshape(NSA_R * tq, n)

    def normalise(acc):
        return acc[:, :NSA_DK] / acc[:, ONES_LANE:ONES_LANE + 1]

    def query_rows(g):
        sel = sel_ref[0, g].astype(F32)
        rows = []
        for r in range(NSA_R):
            hh = g * NSA_R + r
            qh = q_ref[0, :, (hh // 2) * LANES:(hh // 2 + 1) * LANES].astype(F32)
            if hh % 2:
                qh = jnp.concatenate([qh[:, NSA_DK:], qh[:, :NSA_DK]], axis=1)
            slope = 2.0 ** -(hh + 1)
            extra = jnp.where(lane == POS_HI_LANE, SEL_BLOCK * slope,
                              jnp.where(lane == POS_LO_LANE, slope, jnp.where(lane == PAD_LANE, -1.0, sel)))
            rows.append(jnp.where(is_head, qh, extra).astype(BF16))
        return jnp.concatenate(rows, axis=0)

    qs = [query_rows(g) for g in range(NSA_GROUPS)]

    o_win = []
    for g in range(NSA_GROUPS):
        s = masked_rows(lax.dot_general(qs[g], kwx[g, pl.ds(t0, WL), :], NT_DIMS,
                                        preferred_element_type=F32), wmask)
        p = jnp.exp(s - jnp.max(s, axis=-1, keepdims=True))
        o_win.append(normalise(jnp.dot(p.astype(BF16), vwx[g, pl.ds(t0, WL), :],
                                       preferred_element_type=F32)))

    def chunk_scores(g, c):
        k0 = pl.multiple_of(c * KC, KC)
        return lax.dot_general(qs[g], ksx[g, pl.ds(k0, KC), :], NT_DIMS, preferred_element_type=F32)

    def chunk_values(g, c, p):
        k0 = pl.multiple_of(c * KC, KC)
        return jnp.dot(p.astype(BF16), vsx[g, pl.ds(k0, KC), :], preferred_element_type=F32)

    state = []
    for g in range(NSA_GROUPS):
        s = masked_rows(chunk_scores(g, diag_chunk), dmask)
        m = jnp.max(s, axis=-1, keepdims=True)
        state += [m, chunk_values(g, diag_chunk, jnp.exp(s - m))]

    def body(c, carry):
        out = []
        for g in range(NSA_GROUPS):
            m, acc = carry[2 * g], carry[2 * g + 1]
            s = chunk_scores(g, c)
            m_new = jnp.maximum(m, jnp.max(s, axis=-1, keepdims=True))
            out += [m_new, jnp.exp(m - m_new) * acc + chunk_values(g, c, jnp.exp(s - m_new))]
        return tuple(out)

    state = lax.fori_loop(0, diag_chunk, body, tuple(state))

    for g in range(NSA_GROUPS):
        o_sel = normalise(state[2 * g + 1])
        for r in range(NSA_R):
            hh = g * NSA_R + r
            g_sel = gate[:, NSA_HEADS + hh:NSA_HEADS + hh + 1]
            g_win = gate[:, 2 * NSA_HEADS + hh:2 * NSA_HEADS + hh + 1]
            piece = g_sel * o_sel[r * tq:(r + 1) * tq] + g_win * o_win[g][r * tq:(r + 1) * tq]
            o_ref[0, :, hh * NSA_DK:(hh + 1) * NSA_DK] = (
                ocmp_ref[0, :, hh * NSA_DK:(hh + 1) * NSA_DK] + piece).astype(BF16)


def _selwin(pb3, sel, pf3, ocmp, tq, KC):
    B, S, _ = pb3.shape
    HD = NSA_HEADS * NSA_DK
    WL = WINDOW + tq
    tr = jnp.arange(tq)[:, None]
    kc = jnp.arange(WL)[None, :]
    wmask = jnp.where((kc > tr) & (kc <= tr + WINDOW), 0.0, NEG).astype(F32)
    off = (jnp.arange(KC // tq) * tq)[:, None, None]
    dmask = jnp.where(jnp.arange(KC)[None, None, :] <= off + tr[None], 0.0, NEG).astype(F32)
    kv_spec = lambda off: pl.BlockSpec((1, S, LANES), lambda b, i: (b, 0, off // LANES))
    return pl.pallas_call(
        functools.partial(_selwin_kernel, tq=tq, KC=KC, S=S),
        grid=(B, S // tq),
        in_specs=[pl.BlockSpec((1, tq, HD), lambda b, i: (b, i, PB_NQ // HD)),
                  kv_spec(PB_KS), kv_spec(PB_VS), kv_spec(PB_KW), kv_spec(PB_VW),
                  pl.BlockSpec((1, NSA_GROUPS, tq, LANES), lambda b, i: (b, 0, i, 0)),
                  pl.BlockSpec((tq, WL), lambda b, i: (0, 0)),
                  pl.BlockSpec((KC // tq, tq, KC), lambda b, i: (0, 0, 0)),
                  pl.BlockSpec((1, tq, LANES), lambda b, i: (b, i, PF_NG // LANES)),
                  pl.BlockSpec((1, tq, HD), lambda b, i: (b, i, 0))],
        out_specs=pl.BlockSpec((1, tq, HD), lambda b, i: (b, i, 0)),
        out_shape=jax.ShapeDtypeStruct((B, S, HD), BF16),
        scratch_shapes=[pltpu.VMEM((NSA_GROUPS, S, LANES), BF16), pltpu.VMEM((NSA_GROUPS, S, LANES), BF16),
                        pltpu.VMEM((NSA_GROUPS, WINDOW + S, LANES), BF16),
                        pltpu.VMEM((NSA_GROUPS, WINDOW + S, LANES), BF16)],
        compiler_params=_params("arbitrary", "arbitrary"),
        name="selwin",
    )(pb3, pb3, pb3, pb3, pb3, sel, wmask, dmask, pf3, ocmp)


SUBLANES = 8


def _mix_ffn_kernel(x_ref, ya_ref, yb_ref, ga_ref, gb_ref, wa_ref, wb_ref, wo_ref, gn_ref,
                    wup_ref, cw_ref, cb_ref, wd_ref, gf_ref, o_ref, tail_ref, u_ref, *, tm, tf):
    @pl.when(pl.program_id(1) == 0)
    def _():
        tail_ref[...] = jnp.zeros_like(tail_ref)

    y_a = jnp.dot(ya_ref[0], wa_ref[...], preferred_element_type=F32)
    y_b = jnp.dot(yb_ref[0], wb_ref[...], preferred_element_type=F32)
    merged = (jax.nn.sigmoid(ga_ref[0].astype(F32)) * y_a + jax.nn.sigmoid(gb_ref[0].astype(F32)) * y_b)
    x1 = x_ref[0] + jnp.dot(merged.astype(BF16), wo_ref[...], preferred_element_type=F32)
    o_ref[0] = x1
    h = _rms(x1, gn_ref[...]).astype(BF16)

    top = lax.broadcasted_iota(jnp.int32, (SUBLANES, tf), 0)
    for c in range(D_FF // tf):
        cols = slice(c * tf, (c + 1) * tf)
        a = jnp.dot(h, wup_ref[:, cols], preferred_element_type=F32)
        b = jnp.dot(h, wup_ref[:, D_FF + c * tf:D_FF + (c + 1) * tf], preferred_element_type=F32)
        tail = tail_ref[c]
        tail_ref[c] = a[tm - SUBLANES:]
        ac = cb_ref[:, cols] + cw_ref[CONV_W - 1:CONV_W, cols] * a
        for d in range(1, CONV_W):
            sh = pltpu.roll(a, d, axis=0)
            head = jnp.where(top < d, pltpu.roll(tail, d, axis=0), sh[:SUBLANES])
            sh = jnp.concatenate([head, sh[SUBLANES:]], axis=0)
            ac = ac + cw_ref[CONV_W - 1 - d:CONV_W - d, cols] * sh
        u_ref[:, cols] = (jax.nn.gelu(ac) * b).astype(BF16)
    y = jnp.dot(u_ref[...], wd_ref[...], preferred_element_type=F32)
    o_ref[0] = _rms(o_ref[0] + y, gf_ref[...])


def _mix_ffn(x, ya, yb, pb3, wa, wb, wo, gn, w_up, conv_w, conv_b, w_down, gf, tm, tf):
    B, S, _ = x.shape
    HV = RET_HEADS * RET_DV
    HD = NSA_HEADS * NSA_DK
    rows = lambda width, col=0: pl.BlockSpec((1, tm, width), lambda b, i: (b, i, col))
    whole = lambda *shape: pl.BlockSpec(shape, lambda b, i: (0,) * len(shape), pipeline_mode=pl.Buffered(1))
    return pl.pallas_call(
        functools.partial(_mix_ffn_kernel, tm=tm, tf=tf),
        grid=(B, S // tm),
        in_specs=[rows(D_MODEL), rows(HV), rows(HD),
                  rows(D_MODEL, PB_GA // D_MODEL), rows(D_MODEL, PB_GB // D_MODEL),
                  whole(HV, D_MODEL), whole(HD, D_MODEL), whole(D_MODEL, D_MODEL), whole(1, D_MODEL),
                  whole(D_MODEL, 2 * D_FF), whole(CONV_W, D_FF), whole(1, D_FF), whole(D_FF, D_MODEL),
                  whole(1, D_MODEL)],
        out_specs=rows(D_MODEL),
        out_shape=jax.ShapeDtypeStruct((B, S, D_MODEL), F32),
        scratch_shapes=[pltpu.VMEM((D_FF // tf, SUBLANES, tf), F32), pltpu.VMEM((tm, D_FF), BF16)],
        compiler_params=_params("arbitrary", "arbitrary"),
        name="mix_ffn",
    )(x, ya, yb, pb3, pb3, wa, wb, wo, gn, w_up, conv_w, conv_b, w_down, gf)


def _block_diag_groups(w):
    z = jnp.zeros_like(w)
    return jnp.concatenate([jnp.concatenate([w, z], axis=-1), jnp.concatenate([z, w], axis=-1)], axis=-2)


def kernel(x, norm_mix, w_in, ret_gn_g, cmp_pos_k, cmp_w1_k, cmp_b1_k, cmp_w2_k, cmp_pos_v, cmp_w1_v,
           cmp_b1_v, cmp_w2_v, w_ret_o, w_nsa_o, w_out, norm_ffn, w_up, conv_w, conv_b, w_down,
           norm_final):
    B, S, D = x.shape
    assert D == D_MODEL and NSA_GROUPS == 2 and norm_mix.shape[0] == 1
    assert S % 512 == 0 and S >= 1024 and S // CMP_STRIDE <= LANES and S // SEL_BLOCK <= SEL_LANES
    M = B * S
    x2 = x.reshape(M, D)

    off = np.concatenate([[0], np.cumsum(IN_SIZES)])
    w = w_in[0]
    seg = lambda i: w[:, off[i]:off[i + 1]]
    wp = jnp.concatenate(
        [seg(0), seg(1), seg(2), seg(4) * (NSA_DK ** -0.5), seg(7), seg(8), seg(9), seg(10),
         seg(3), seg(12), seg(13),
         seg(5), seg(6), seg(11), jnp.zeros((D, PF_W - PF_NG - IN_SIZES[11]), F32)], axis=1).astype(BF16)

    def cmp_weights(pos, w1, b1, w2):
        return (jnp.tile(pos, (1, NSA_GROUPS)),
                _block_diag_groups(w1.reshape(CMP_BLOCK, NSA_DK, CMP_HIDDEN)).astype(BF16),
                jnp.tile(b1, NSA_GROUPS)[None, :],
                _block_diag_groups(w2).astype(BF16))

    posk, w1k, b1k, w2k = cmp_weights(cmp_pos_k[0], cmp_w1_k[0], cmp_b1_k[0], cmp_w2_k[0])
    posv, w1v, b1v, w2v = cmp_weights(cmp_pos_v[0], cmp_w1_v[0], cmp_b1_v[0], cmp_w2_v[0])

    pb, pf = _inproj(x2, norm_mix, wp, tm=512)
    pb3 = pb.reshape(B, S, PB_W)
    pf3 = pf.reshape(B, S, PF_W)

    ya = _retention(pb3, ret_gn_g, C=256)
    kcmp, vcmp = _compress(pf3, posk, posv, w1k, w1v, b1k, b1v, w2k, w2v)
    ocmp, sel = _cmpattn(pb3, kcmp, vcmp, pf3, tq=256)
    yb = _selwin(pb3, sel, pf3, ocmp, tq=256, KC=512)

    return _mix_ffn(x, ya, yb, pb3, w_ret_o[0].astype(BF16), w_nsa_o[0].astype(BF16),
                    w_out[0].astype(BF16), norm_ffn, w_up[0].astype(BF16), conv_w[0], conv_b,
                    w_down[0].astype(BF16), norm_final[None, :], tm=512, tf=256)
```

```python
import functools
import math

import numpy as np
import jax
import jax.numpy as jnp
from jax import lax
from jax.experimental import pallas as pl
from jax.experimental.pallas import tpu as pltpu

F32 = jnp.float32
BF16 = jnp.bfloat16

D_MODEL = 1024
RET_HEADS = 4
RET_DK = 128
RET_DV = 256
NSA_HEADS = 8
NSA_GROUPS = 2
NSA_R = NSA_HEADS // NSA_GROUPS
NSA_DK = 64
CMP_BLOCK = 32
CMP_STRIDE = 16
CMP_HIDDEN = 256
SEL_BLOCK = 64
SEL_TOPK = 8
WINDOW = 512
D_FF = 2816
CONV_W = 3
EPS = 1e-6
NEG = -1e30
FORCE = 1e9
LOWEST = -3e38

IN_SIZES = (RET_HEADS * RET_DK, RET_HEADS * RET_DK, RET_HEADS * RET_DV, RET_HEADS * RET_DV,
            NSA_HEADS * NSA_DK,
            NSA_GROUPS * NSA_DK, NSA_GROUPS * NSA_DK, NSA_GROUPS * NSA_DK,
            NSA_GROUPS * NSA_DK, NSA_GROUPS * NSA_DK, NSA_GROUPS * NSA_DK,
            3 * NSA_HEADS, D_MODEL, D_MODEL)

LANES = 128
VMEM_LIMIT = 56 * 1024 * 1024

PB_RQ, PB_RK, PB_RV, PB_NQ, PB_KS, PB_VS, PB_KW, PB_VW = 0, 512, 1024, 2048, 2560, 2688, 2816, 2944
PB_RG, PB_GA, PB_GB = 3072, 4096, 5120
PB_W = 6144
PF_KCR, PF_VCR, PF_NG = 0, 128, 256
PF_W = 384

NT_DIMS = (((1,), (1,)), ((), ()))
TN_DIMS = (((0,), (0,)), ((), ()))


def _params(*sem):
    return pltpu.CompilerParams(dimension_semantics=sem, vmem_limit_bytes=VMEM_LIMIT)


def _rms(x, g):
    return x * lax.rsqrt(jnp.mean(x * x, axis=-1, keepdims=True) + EPS) * g


def _split3(x):
    hi = x.astype(BF16)
    rest = x - hi.astype(F32)
    mid = rest.astype(BF16)
    return hi, mid, (rest - mid.astype(F32)).astype(BF16)


def _inproj_kernel(x_ref, g_ref, w_ref, pb_ref, pf_ref):
    hb = _rms(x_ref[...], g_ref[...]).astype(BF16)
    for c0 in range(0, PB_W, 512):
        pb_ref[:, c0:c0 + 512] = jnp.dot(
            hb, w_ref[:, c0:c0 + 512], preferred_element_type=F32).astype(BF16)
    for c0 in range(0, PF_W, 512):
        c1 = min(c0 + 512, PF_W)
        pf_ref[:, c0:c1] = jnp.dot(hb, w_ref[:, PB_W + c0:PB_W + c1], preferred_element_type=F32)


def _inproj(x2, g, wp, tm):
    M = x2.shape[0]
    return pl.pallas_call(
        _inproj_kernel,
        grid=(M // tm,),
        in_specs=[pl.BlockSpec((tm, D_MODEL), lambda i: (i, 0)),
                  pl.BlockSpec((1, D_MODEL), lambda i: (0, 0)),
                  pl.BlockSpec((D_MODEL, PB_W + PF_W), lambda i: (0, 0), pipeline_mode=pl.Buffered(1))],
        out_specs=[pl.BlockSpec((tm, PB_W), lambda i: (i, 0)),
                   pl.BlockSpec((tm, PF_W), lambda i: (i, 0))],
        out_shape=[jax.ShapeDtypeStruct((M, PB_W), BF16), jax.ShapeDtypeStruct((M, PF_W), F32)],
        compiler_params=_params("arbitrary"),
        name="inproj",
    )(x2, g, wp)


def _ret_kernel(q_ref, k_ref, v_ref, g_ref, gn_ref, o_ref, r_ref, dmask_ref, *, C):
    log_gamma = [float(np.log1p(-np.exp2(np.float32(-5.0 - h)))) for h in range(RET_HEADS)]
    scale = RET_DK ** -0.5

    @pl.when((pl.program_id(0) == 0) & (pl.program_id(1) == 0))
    def _():
        diff = (lax.broadcasted_iota(jnp.int32, (C, C), 0)
                - lax.broadcasted_iota(jnp.int32, (C, C), 1)).astype(F32)
        for h in range(RET_HEADS):
            dmask_ref[h] = jnp.where(diff >= 0, jnp.exp(log_gamma[h] * jnp.maximum(diff, 0.0)), 0.0) * scale

    @pl.when(pl.program_id(1) == 0)
    def _():
        r_ref[...] = jnp.zeros_like(r_ref)

    pos = lax.broadcasted_iota(jnp.int32, (C, 1), 0).astype(F32)
    for h in range(RET_HEADS):
        lg = log_gamma[h]
        q = q_ref[0, :, h * RET_DK:(h + 1) * RET_DK]
        k = k_ref[0, :, h * RET_DK:(h + 1) * RET_DK]
        v = v_ref[0, :, h * RET_DV:(h + 1) * RET_DV]
        s = lax.dot_general(q, k, NT_DIMS, preferred_element_type=F32)
        o = jnp.dot((s * dmask_ref[h]).astype(BF16), v, preferred_element_type=F32)
        state = r_ref[h]
        xi = jnp.exp(lg * (pos + 1.0))
        o = o + xi * jnp.dot(q, state.astype(BF16), preferred_element_type=F32)
        zeta = jnp.exp(lg * (C - 1.0 - pos)) * scale
        kz = (k.astype(F32) * zeta).astype(BF16)
        kv = lax.dot_general(kz, v, TN_DIMS, preferred_element_type=F32)
        r_ref[h] = math.exp(lg * C) * state + kv
        mu = jnp.mean(o, axis=-1, keepdims=True)
        d = o - mu
        var = jnp.mean(d * d, axis=-1, keepdims=True)
        on = d * lax.rsqrt(var + EPS) * gn_ref[:, h * RET_DV:(h + 1) * RET_DV]
        g = g_ref[0, :, h * RET_DV:(h + 1) * RET_DV].astype(F32)
        o_ref[0, :, h * RET_DV:(h + 1) * RET_DV] = (g * jax.nn.sigmoid(g) * on).astype(BF16)


def _retention(pb3, gn_g, C):
    B, S, _ = pb3.shape
    HV = RET_HEADS * RET_DV
    HK = RET_HEADS * RET_DK
    return pl.pallas_call(
        functools.partial(_ret_kernel, C=C),
        grid=(B, S // C),
        in_specs=[pl.BlockSpec((1, C, HK), lambda b, n: (b, n, PB_RQ // HK)),
                  pl.BlockSpec((1, C, HK), lambda b, n: (b, n, PB_RK // HK)),
                  pl.BlockSpec((1, C, HV), lambda b, n: (b, n, PB_RV // HV)),
                  pl.BlockSpec((1, C, HV), lambda b, n: (b, n, PB_RG // HV)),
                  pl.BlockSpec((1, HV), lambda b, n: (0, 0))],
        out_specs=pl.BlockSpec((1, C, HV), lambda b, n: (b, n, 0)),
        out_shape=jax.ShapeDtypeStruct((B, S, HV), BF16),
        scratch_shapes=[pltpu.VMEM((RET_HEADS, RET_DK, RET_DV), F32), pltpu.VMEM((RET_HEADS, C, C), F32)],
        compiler_params=_params("arbitrary", "arbitrary"),
        name="retention",
    )(pb3, pb3, pb3, pb3, gn_g)


def _compress_kernel(kin_ref, vin_ref, posk_ref, posv_ref, w1k_ref, w1v_ref, b1k_ref, b1v_ref,
                     w2k_ref, w2v_ref, ko_ref, vo_ref, *, NP):
    GD = NSA_GROUPS * NSA_DK
    npc = CMP_BLOCK // CMP_STRIDE
    row = lax.broadcasted_iota(jnp.int32, (NP, GD), 0)
    for in_ref, pos_ref, w1_ref, b1_ref, w2_ref, out_ref in (
            (kin_ref, posk_ref, w1k_ref, b1k_ref, w2k_ref, ko_ref),
            (vin_ref, posv_ref, w1v_ref, b1v_ref, w2v_ref, vo_ref)):
        parts = []
        for p in range(npc):
            acc = jnp.zeros((NP, NSA_GROUPS * CMP_HIDDEN), F32)
            for i in range(CMP_STRIDE):
                tok = in_ref[0, pl.ds(i, NP, stride=CMP_STRIDE), :]
                j = p * CMP_STRIDE + i
                acc = acc + jnp.dot((tok + pos_ref[j:j + 1, :]).astype(BF16), w1_ref[j],
                                    preferred_element_type=F32)
            parts.append(acc)
        hidden = parts[0]
        for p in range(1, npc):
            hidden = hidden + pltpu.roll(parts[p], NP - p, axis=0)
        act = jax.nn.gelu(hidden + b1_ref[...]).astype(BF16)
        res = jnp.dot(act, w2_ref[...], preferred_element_type=F32)
        res = jnp.where(row < NP - (npc - 1), res, 0.0)
        if out_ref is vo_ref:
            res = res.T
            for g in range(NSA_GROUPS):
                out_ref[0, g] = res[g * NSA_DK:(g + 1) * NSA_DK, :]
        else:
            for g in range(NSA_GROUPS):
                out_ref[0, g] = res[:, g * NSA_DK:(g + 1) * NSA_DK]


def _compress(pf3, posk, posv, w1k, w1v, b1k, b1v, w2k, w2v):
    B, S, _ = pf3.shape
    NP = S // CMP_STRIDE
    GD = NSA_GROUPS * NSA_DK
    GH = NSA_GROUPS * CMP_HIDDEN
    const2 = lambda b: (0, 0)
    const3 = lambda b: (0, 0, 0)
    k_shape = (NSA_GROUPS, NP, NSA_DK)
    v_shape = (NSA_GROUPS, NSA_DK, NP)
    return pl.pallas_call(
        functools.partial(_compress_kernel, NP=NP),
        grid=(B,),
        in_specs=[pl.BlockSpec((1, S, GD), lambda b: (b, 0, PF_KCR // GD)),
                  pl.BlockSpec((1, S, GD), lambda b: (b, 0, PF_VCR // GD)),
                  pl.BlockSpec((CMP_BLOCK, GD), const2), pl.BlockSpec((CMP_BLOCK, GD), const2),
                  pl.BlockSpec((CMP_BLOCK, GD, GH), const3), pl.BlockSpec((CMP_BLOCK, GD, GH), const3),
                  pl.BlockSpec((1, GH), const2), pl.BlockSpec((1, GH), const2),
                  pl.BlockSpec((GH, GD), const2), pl.BlockSpec((GH, GD), const2)],
        out_specs=[pl.BlockSpec((1,) + k_shape, lambda b: (b, 0, 0, 0)),
                   pl.BlockSpec((1,) + v_shape, lambda b: (b, 0, 0, 0))],
        out_shape=[jax.ShapeDtypeStruct((B,) + k_shape, F32), jax.ShapeDtypeStruct((B,) + v_shape, F32)],
        compiler_params=_params("arbitrary"),
        name="compress",
    )(pf3, pf3, posk, posv, w1k, w1v, b1k, b1v, w2k, w2v)


SEL_LANE0 = 64
SEL_LANES = 32
POS_HI_LANE, POS_LO_LANE, PAD_LANE = 96, 97, 98
ONES_LANE = 64
MASK_BIG = 1e30


def _cmpattn_kernel(q_ref, kc_ref, vct_ref, gate_ref, ocmp_ref, sel_ref, *, tq, NP, NS):
    NC = NP - (CMP_BLOCK // CMP_STRIDE - 1)
    t0 = pl.program_id(1) * tq
    tcol = t0 + lax.broadcasted_iota(jnp.int32, (NP, tq), 1)
    cidx = lax.broadcasted_iota(jnp.int32, (NP, tq), 0)
    dcmp = (tcol - (cidx * CMP_STRIDE + (CMP_BLOCK - 1))).astype(F32)
    cmask = (dcmp >= 0) & (cidx < NC)
    jj = lax.broadcasted_iota(jnp.int32, (SEL_LANES, NP), 0)
    cc = lax.broadcasted_iota(jnp.int32, (SEL_LANES, NP), 1)
    overlap_t = ((cc * CMP_STRIDE < jj * SEL_BLOCK + SEL_BLOCK)
                 & (cc * CMP_STRIDE + CMP_BLOCK > jj * SEL_BLOCK) & (jj < NS)).astype(BF16)
    blk = lax.broadcasted_iota(jnp.int32, (SEL_LANES, tq), 0)
    blk_f = blk.astype(F32)
    is_blk = blk < NS
    cur = jnp.right_shift(t0 + lax.broadcasted_iota(jnp.int32, (SEL_LANES, tq), 1),
                          SEL_BLOCK.bit_length() - 1)
    o_t = []
    for g in range(NSA_GROUPS):
        kc_parts = _split3(kc_ref[0, g])
        vct = vct_ref[0, g].astype(BF16)
        psum = jnp.zeros((NP, tq), F32)
        for r in range(NSA_R):
            hh = g * NSA_R + r
            q = q_ref[0, :, hh * NSA_DK:(hh + 1) * NSA_DK]
            s = sum(lax.dot_general(part, q, NT_DIMS, preferred_element_type=F32) for part in kc_parts)
            s = jnp.where(cmask, s - (2.0 ** -(hh + 1)) * dcmp, NEG)
            e = jnp.exp(s - jnp.max(s, axis=0, keepdims=True))
            p = jnp.where(cmask, e / jnp.sum(e, axis=0, keepdims=True), 0.0)
            o_t.append(jnp.dot(vct, p.astype(BF16), preferred_element_type=F32))
            psum = psum + p
        imp = sum(jnp.dot(overlap_t, part, preferred_element_type=F32) for part in _split3(psum))
        imp = jnp.where((blk == 0) | (blk == cur) | (blk == cur - 1), FORCE, imp)
        imp = jnp.where(blk > cur, -FORCE, imp)
        work = jnp.where(is_blk, imp, LOWEST)
        sel = jnp.zeros((SEL_LANES, tq), F32)
        for _ in range(min(SEL_TOPK, NS)):
            top = jnp.max(work, axis=0, keepdims=True)
            first = jnp.min(jnp.where(work == top, blk_f, float(SEL_LANES)), axis=0, keepdims=True)
            pick = blk_f == first
            sel = jnp.where(pick, 1.0, sel)
            work = jnp.where(pick, LOWEST, work)
        sel = jnp.where(is_blk & (blk <= cur), sel - 1.0, jnp.where(is_blk, -1.0, 0.0))
        placed = jnp.concatenate([jnp.zeros((SEL_LANE0, tq), F32), sel,
                                  jnp.zeros((LANES - SEL_LANE0 - SEL_LANES, tq), F32)], axis=0)
        sel_ref[0, g] = placed.T.astype(BF16)
    o = jnp.concatenate(o_t, axis=0).T
    gate = jax.nn.sigmoid(gate_ref[0])
    for hh in range(NSA_HEADS):
        cols = slice(hh * NSA_DK, (hh + 1) * NSA_DK)
        ocmp_ref[0, :, cols] = gate[:, hh:hh + 1] * o[:, cols]


def _cmpattn(pb3, kcmp, vcmp, pf3, tq):
    B, S, _ = pb3.shape
    NP = S // CMP_STRIDE
    NS = S // SEL_BLOCK
    HD = NSA_HEADS * NSA_DK
    return pl.pallas_call(
        functools.partial(_cmpattn_kernel, tq=tq, NP=NP, NS=NS),
        grid=(B, S // tq),
        in_specs=[pl.BlockSpec((1, tq, HD), lambda b, i: (b, i, PB_NQ // HD)),
                  pl.BlockSpec((1, NSA_GROUPS, NP, NSA_DK), lambda b, i: (b, 0, 0, 0)),
                  pl.BlockSpec((1, NSA_GROUPS, NSA_DK, NP), lambda b, i: (b, 0, 0, 0)),
                  pl.BlockSpec((1, tq, LANES), lambda b, i: (b, i, PF_NG // LANES))],
        out_specs=[pl.BlockSpec((1, tq, HD), lambda b, i: (b, i, 0)),
                   pl.BlockSpec((1, NSA_GROUPS, tq, LANES), lambda b, i: (b, 0, i, 0))],
        out_shape=[jax.ShapeDtypeStruct((B, S, HD), F32),
                   jax.ShapeDtypeStruct((B, NSA_GROUPS, S, LANES), BF16)],
        compiler_params=_params("arbitrary", "arbitrary"),
        name="cmpattn",
    )(pb3, kcmp, vcmp, pf3)


def _selwin_kernel(q_ref, ks_ref, vs_ref, kw_ref, vw_ref, sel_ref, wmask_ref, dmask_ref, gate_ref, ocmp_ref,
                   o_ref, ksx, vsx, kwx, vwx, *, tq, KC, S):
    WL = WINDOW + tq
    RB = 512
    i = pl.program_id(1)
    t0 = pl.multiple_of(i * tq, tq)
    sel_shift = SEL_BLOCK.bit_length() - 1

    @pl.when(i == 0)
    def _build():
        lane = lax.broadcasted_iota(jnp.int32, (RB, LANES), 1)
        is_head = lane < NSA_DK
        pad_k = jnp.where(lane == PAD_LANE, MASK_BIG, 0.0).astype(BF16)
        for g in range(NSA_GROUPS):
            kwx[g, 0:WINDOW, :] = pad_k[:WINDOW]
            vwx[g, 0:WINDOW, :] = jnp.zeros((WINDOW, LANES), BF16)
        for piece in range(S // RB):
            r0 = piece * RB
            pos = r0 + lax.broadcasted_iota(jnp.int32, (RB, LANES), 0)
            alibi = jnp.where(lane == POS_HI_LANE, jnp.right_shift(pos, sel_shift),
                              jnp.where(lane == POS_LO_LANE, pos & (SEL_BLOCK - 1), 0)).astype(F32)
            onehot = jnp.where(lane - SEL_LANE0 == jnp.right_shift(pos, sel_shift), MASK_BIG, 0.0)
            k_extra_win = alibi.astype(BF16)
            k_extra_sel = (alibi + onehot).astype(BF16)
            v_extra = jnp.where(lane == ONES_LANE, 1.0, 0.0).astype(BF16)
            for g in range(NSA_GROUPS):
                def head_lanes(ref):
                    t = ref[0, r0:r0 + RB, :]
                    return t if g == 0 else jnp.concatenate([t[:, NSA_DK:], t[:, :NSA_DK]], axis=1)
                ksx[g, r0:r0 + RB, :] = jnp.where(is_head, head_lanes(ks_ref), k_extra_sel)
                vsx[g, r0:r0 + RB, :] = jnp.where(is_head, head_lanes(vs_ref), v_extra)
                kwx[g, WINDOW + r0:WINDOW + r0 + RB, :] = jnp.where(is_head, head_lanes(kw_ref), k_extra_win)
                vwx[g, WINDOW + r0:WINDOW + r0 + RB, :] = jnp.where(is_head, head_lanes(vw_ref), v_extra)

    gate = jax.nn.sigmoid(gate_ref[0])
    lane = lax.broadcasted_iota(jnp.int32, (tq, LANES), 1)
    is_head = lane < NSA_DK
    diag_chunk = t0 // KC
    wmask = wmask_ref[...][None]
    dmask = dmask_ref[i % (KC // tq)][None]

    def masked_rows(s, mask):
        n = s.shape[-1]
        return (s.reshape(NSA_R, tq, n) + mask).reshape(NSA_R * tq, n)

    def normalise(acc):
        return acc[:, :NSA_DK] / acc[:, ONES_LANE:ONES_LANE + 1]

    def query_rows(g):
        sel = sel_ref[0, g].astype(F32)
        rows = []
        for r in range(NSA_R):
            hh = g * NSA_R + r
            qh = q_ref[0, :, (hh // 2) * LANES:(hh // 2 + 1) * LANES].astype(F32)
            if hh % 2:
                qh = jnp.concatenate([qh[:, NSA_DK:], qh[:, :NSA_DK]], axis=1)
            slope = 2.0 ** -(hh + 1)
            extra = jnp.where(lane == POS_HI_LANE, SEL_BLOCK * slope,
                              jnp.where(lane == POS_LO_LANE, slope, jnp.where(lane == PAD_LANE, -1.0, sel)))
            rows.append(jnp.where(is_head, qh, extra).astype(BF16))
        return jnp.concatenate(rows, axis=0)

    def tile(n_past):
        past = n_past * KC
        for g in range(NSA_GROUPS):
            q = query_rows(g)
            s = masked_rows(lax.dot_general(q, kwx[g, pl.ds(t0, WL), :], NT_DIMS,
                                            preferred_element_type=F32), wmask)
            p = jnp.exp(s - jnp.max(s, axis=-1, keepdims=True))
            o_win = normalise(jnp.dot(p.astype(BF16), vwx[g, pl.ds(t0, WL), :],
                                      preferred_element_type=F32))

            s_own = masked_rows(lax.dot_general(q, ksx[g, past:past + KC, :], NT_DIMS,
                                                preferred_element_type=F32), dmask)
            m = jnp.max(s_own, axis=-1, keepdims=True)
            if n_past:
                s_past = lax.dot_general(q, ksx[g, 0:past, :], NT_DIMS, preferred_element_type=F32)
                m = jnp.maximum(m, jnp.max(s_past, axis=-1, keepdims=True))
            acc = jnp.dot(jnp.exp(s_own - m).astype(BF16), vsx[g, past:past + KC, :],
                          preferred_element_type=F32)
            if n_past:
                acc = acc + jnp.dot(jnp.exp(s_past - m).astype(BF16), vsx[g, 0:past, :],
                                    preferred_element_type=F32)
            o_sel = normalise(acc)

            for r in range(NSA_R):
                hh = g * NSA_R + r
                g_sel = gate[:, NSA_HEADS + hh:NSA_HEADS + hh + 1]
                g_win = gate[:, 2 * NSA_HEADS + hh:2 * NSA_HEADS + hh + 1]
                piece = g_sel * o_sel[r * tq:(r + 1) * tq] + g_win * o_win[r * tq:(r + 1) * tq]
                o_ref[0, :, hh * NSA_DK:(hh + 1) * NSA_DK] = (
                    ocmp_ref[0, :, hh * NSA_DK:(hh + 1) * NSA_DK] + piece).astype(BF16)

    for n_past in range(S // KC):
        pl.when(diag_chunk == n_past)(functools.partial(tile, n_past))


def _selwin(pb3, sel, pf3, ocmp, tq, KC):
    B, S, _ = pb3.shape
    HD = NSA_HEADS * NSA_DK
    WL = WINDOW + tq
    tr = jnp.arange(tq)[:, None]
    kc = jnp.arange(WL)[None, :]
    wmask = jnp.where((kc > tr) & (kc <= tr + WINDOW), 0.0, NEG).astype(F32)
    off = (jnp.arange(KC // tq) * tq)[:, None, None]
    dmask = jnp.where(jnp.arange(KC)[None, None, :] <= off + tr[None], 0.0, NEG).astype(F32)
    kv_spec = lambda off: pl.BlockSpec((1, S, LANES), lambda b, i: (b, 0, off // LANES))
    return pl.pallas_call(
        functools.partial(_selwin_kernel, tq=tq, KC=KC, S=S),
        grid=(B, S // tq),
        in_specs=[pl.BlockSpec((1, tq, HD), lambda b, i: (b, i, PB_NQ // HD)),
                  kv_spec(PB_KS), kv_spec(PB_VS), kv_spec(PB_KW), kv_spec(PB_VW),
                  pl.BlockSpec((1, NSA_GROUPS, tq, LANES), lambda b, i: (b, 0, i, 0)),
                  pl.BlockSpec((tq, WL), lambda b, i: (0, 0)),
                  pl.BlockSpec((KC // tq, tq, KC), lambda b, i: (0, 0, 0)),
                  pl.BlockSpec((1, tq, LANES), lambda b, i: (b, i, PF_NG // LANES)),
                  pl.BlockSpec((1, tq, HD), lambda b, i: (b, i, 0))],
        out_specs=pl.BlockSpec((1, tq, HD), lambda b, i: (b, i, 0)),
        out_shape=jax.ShapeDtypeStruct((B, S, HD), BF16),
        scratch_shapes=[pltpu.VMEM((NSA_GROUPS, S, LANES), BF16), pltpu.VMEM((NSA_GROUPS, S, LANES), BF16),
                        pltpu.VMEM((NSA_GROUPS, WINDOW + S, LANES), BF16),
                        pltpu.VMEM((NSA_GROUPS, WINDOW + S, LANES), BF16)],
        compiler_params=_params("arbitrary", "arbitrary"),
        name="selwin",
    )(pb3, pb3, pb3, pb3, pb3, sel, wmask, dmask, pf3, ocmp)


SUBLANES = 8


def _mix_ffn_kernel(x_ref, ya_ref, yb_ref, ga_ref, gb_ref, wa_ref, wb_ref, wo_ref, gn_ref,
                    wup_ref, cw_ref, cb_ref, wd_ref, gf_ref, o_ref, tail_ref, u_ref, *, tm, tf):
    @pl.when(pl.program_id(1) == 0)
    def _():
        tail_ref[...] = jnp.zeros_like(tail_ref)

    y_a = jnp.dot(ya_ref[0], wa_ref[...], preferred_element_type=F32)
    y_b = jnp.dot(yb_ref[0], wb_ref[...], preferred_element_type=F32)
    merged = (jax.nn.sigmoid(ga_ref[0].astype(F32)) * y_a + jax.nn.sigmoid(gb_ref[0].astype(F32)) * y_b)
    x1 = x_ref[0] + jnp.dot(merged.astype(BF16), wo_ref[...], preferred_element_type=F32)
    o_ref[0] = x1
    h = _rms(x1, gn_ref[...]).astype(BF16)

    top = lax.broadcasted_iota(jnp.int32, (SUBLANES, tf), 0)
    for c in range(D_FF // tf):
        cols = slice(c * tf, (c + 1) * tf)
        a = jnp.dot(h, wup_ref[:, cols], preferred_element_type=F32)
        b = jnp.dot(h, wup_ref[:, D_FF + c * tf:D_FF + (c + 1) * tf], preferred_element_type=F32)
        tail = tail_ref[c]
        tail_ref[c] = a[tm - SUBLANES:]
        ac = cb_ref[:, cols] + cw_ref[CONV_W - 1:CONV_W, cols] * a
        for d in range(1, CONV_W):
            sh = pltpu.roll(a, d, axis=0)
            head = jnp.where(top < d, pltpu.roll(tail, d, axis=0), sh[:SUBLANES])
            sh = jnp.concatenate([head, sh[SUBLANES:]], axis=0)
            ac = ac + cw_ref[CONV_W - 1 - d:CONV_W - d, cols] * sh
        u_ref[:, cols] = (jax.nn.gelu(ac) * b).astype(BF16)
    y = jnp.dot(u_ref[...], wd_ref[...], preferred_element_type=F32)
    o_ref[0] = _rms(o_ref[0] + y, gf_ref[...])


def _mix_ffn(x, ya, yb, pb3, wa, wb, wo, gn, w_up, conv_w, conv_b, w_down, gf, tm, tf):
    B, S, _ = x.shape
    HV = RET_HEADS * RET_DV
    HD = NSA_HEADS * NSA_DK
    rows = lambda width, col=0: pl.BlockSpec((1, tm, width), lambda b, i: (b, i, col))
    whole = lambda *shape: pl.BlockSpec(shape, lambda b, i: (0,) * len(shape), pipeline_mode=pl.Buffered(1))
    return pl.pallas_call(
        functools.partial(_mix_ffn_kernel, tm=tm, tf=tf),
        grid=(B, S // tm),
        in_specs=[rows(D_MODEL), rows(HV), rows(HD),
                  rows(D_MODEL, PB_GA // D_MODEL), rows(D_MODEL, PB_GB // D_MODEL),
                  whole(HV, D_MODEL), whole(HD, D_MODEL), whole(D_MODEL, D_MODEL), whole(1, D_MODEL),
                  whole(D_MODEL, 2 * D_FF), whole(CONV_W, D_FF), whole(1, D_FF), whole(D_FF, D_MODEL),
                  whole(1, D_MODEL)],
        out_specs=rows(D_MODEL),
        out_shape=jax.ShapeDtypeStruct((B, S, D_MODEL), F32),
        scratch_shapes=[pltpu.VMEM((D_FF // tf, SUBLANES, tf), F32), pltpu.VMEM((tm, D_FF), BF16)],
        compiler_params=_params("arbitrary", "arbitrary"),
        name="mix_ffn",
    )(x, ya, yb, pb3, pb3, wa, wb, wo, gn, w_up, conv_w, conv_b, w_down, gf)


def _block_diag_groups(w):
    z = jnp.zeros_like(w)
    return jnp.concatenate([jnp.concatenate([w, z], axis=-1), jnp.concatenate([z, w], axis=-1)], axis=-2)


def kernel(x, norm_mix, w_in, ret_gn_g, cmp_pos_k, cmp_w1_k, cmp_b1_k, cmp_w2_k, cmp_pos_v, cmp_w1_v,
           cmp_b1_v, cmp_w2_v, w_ret_o, w_nsa_o, w_out, norm_ffn, w_up, conv_w, conv_b, w_down,
           norm_final):
    B, S, D = x.shape
    assert D == D_MODEL and NSA_GROUPS == 2 and norm_mix.shape[0] == 1
    assert S % 512 == 0 and S >= 1024 and S // CMP_STRIDE <= LANES and S // SEL_BLOCK <= SEL_LANES
    M = B * S
    x2 = x.reshape(M, D)

    off = np.concatenate([[0], np.cumsum(IN_SIZES)])
    w = w_in[0].astype(BF16)
    seg = lambda i: w[:, off[i]:off[i + 1]]
    wp = jnp.concatenate(
        [seg(0), seg(1), seg(2), seg(4) * (NSA_DK ** -0.5), seg(7), seg(8), seg(9), seg(10),
         seg(3), seg(12), seg(13),
         seg(5), seg(6), seg(11), jnp.zeros((D, PF_W - PF_NG - IN_SIZES[11]), BF16)], axis=1)

    def cmp_weights(pos, w1, b1, w2):
        return (jnp.tile(pos, (1, NSA_GROUPS)),
                _block_diag_groups(w1.reshape(CMP_BLOCK, NSA_DK, CMP_HIDDEN)).astype(BF16),
                jnp.tile(b1, NSA_GROUPS)[None, :],
                _block_diag_groups(w2).astype(BF16))

    posk, w1k, b1k, w2k = cmp_weights(cmp_pos_k[0], cmp_w1_k[0], cmp_b1_k[0], cmp_w2_k[0])
    posv, w1v, b1v, w2v = cmp_weights(cmp_pos_v[0], cmp_w1_v[0], cmp_b1_v[0], cmp_w2_v[0])

    pb, pf = _inproj(x2, norm_mix, wp, tm=512)
    pb3 = pb.reshape(B, S, PB_W)
    pf3 = pf.reshape(B, S, PF_W)

    ya = _retention(pb3, ret_gn_g, C=256)
    kcmp, vcmp = _compress(pf3, posk, posv, w1k, w1v, b1k, b1v, w2k, w2v)
    ocmp, sel = _cmpattn(pb3, kcmp, vcmp, pf3, tq=1024)
    yb = _selwin(pb3, sel, pf3, ocmp, tq=256, KC=512)

    return _mix_ffn(x, ya, yb, pb3, w_ret_o[0].astype(BF16), w_nsa_o[0].astype(BF16),
                    w_out[0].astype(BF16), norm_ffn, w_up[0].astype(BF16), conv_w[0], conv_b,
                    w_down[0].astype(BF16), norm_final[None, :], tm=512, tf=256)
```

```python
import functools
import math

import numpy as np
import jax
import jax.numpy as jnp
from jax import lax
from jax.experimental import pallas as pl
from jax.experimental.pallas import tpu as pltpu

F32 = jnp.float32
BF16 = jnp.bfloat16

D_MODEL = 1024
RET_HEADS = 4
RET_DK = 128
RET_DV = 256
NSA_HEADS = 8
NSA_GROUPS = 2
NSA_R = NSA_HEADS // NSA_GROUPS
NSA_DK = 64
CMP_BLOCK = 32
CMP_STRIDE = 16
CMP_HIDDEN = 256
SEL_BLOCK = 64
SEL_TOPK = 8
WINDOW = 512
D_FF = 2816
CONV_W = 3
EPS = 1e-6
NEG = -1e30
FORCE = 1e9
LOWEST = -3e38

IN_SIZES = (RET_HEADS * RET_DK, RET_HEADS * RET_DK, RET_HEADS * RET_DV, RET_HEADS * RET_DV,
            NSA_HEADS * NSA_DK,
            NSA_GROUPS * NSA_DK, NSA_GROUPS * NSA_DK, NSA_GROUPS * NSA_DK,
            NSA_GROUPS * NSA_DK, NSA_GROUPS * NSA_DK, NSA_GROUPS * NSA_DK,
            3 * NSA_HEADS, D_MODEL, D_MODEL)

LANES = 128
VMEM_LIMIT = 56 * 1024 * 1024

PB_RQ, PB_RK, PB_RV, PB_NQ, PB_KS, PB_VS, PB_KW, PB_VW = 0, 512, 1024, 2048, 2560, 2688, 2816, 2944
PB_RG, PB_GA, PB_GB = 3072, 4096, 5120
PB_W = 6144
PF_KCR, PF_VCR, PF_NG = 0, 128, 256
PF_W = 384

NT_DIMS = (((1,), (1,)), ((), ()))
TN_DIMS = (((0,), (0,)), ((), ()))


def _params(*sem):
    return pltpu.CompilerParams(dimension_semantics=sem, vmem_limit_bytes=VMEM_LIMIT)


def _rms(x, g):
    return x * lax.rsqrt(jnp.mean(x * x, axis=-1, keepdims=True) + EPS) * g


def _split3(x):
    hi = x.astype(BF16)
    rest = x - hi.astype(F32)
    mid = rest.astype(BF16)
    return hi, mid, (rest - mid.astype(F32)).astype(BF16)


def _inproj_kernel(x_ref, g_ref, w_ref, pb_ref, pf_ref):
    hb = _rms(x_ref[...], g_ref[...]).astype(BF16)
    for c0 in range(0, PB_W, 512):
        pb_ref[:, c0:c0 + 512] = jnp.dot(
            hb, w_ref[:, c0:c0 + 512], preferred_element_type=F32).astype(BF16)
    for c0 in range(0, PF_W, 512):
        c1 = min(c0 + 512, PF_W)
        pf_ref[:, c0:c1] = jnp.dot(hb, w_ref[:, PB_W + c0:PB_W + c1], preferred_element_type=F32)


def _inproj(x2, g, wp, tm):
    M = x2.shape[0]
    return pl.pallas_call(
        _inproj_kernel,
        grid=(M // tm,),
        in_specs=[pl.BlockSpec((tm, D_MODEL), lambda i: (i, 0)),
                  pl.BlockSpec((1, D_MODEL), lambda i: (0, 0)),
                  pl.BlockSpec((D_MODEL, PB_W + PF_W), lambda i: (0, 0), pipeline_mode=pl.Buffered(1))],
        out_specs=[pl.BlockSpec((tm, PB_W), lambda i: (i, 0)),
                   pl.BlockSpec((tm, PF_W), lambda i: (i, 0))],
        out_shape=[jax.ShapeDtypeStruct((M, PB_W), BF16), jax.ShapeDtypeStruct((M, PF_W), F32)],
        compiler_params=_params("arbitrary"),
        name="inproj",
    )(x2, g, wp)


def _ret_kernel(q_ref, k_ref, v_ref, g_ref, gn_ref, o_ref, r_ref, dmask_ref, xi_ref, zeta_ref, *, C):
    log_gamma = [float(np.log1p(-np.exp2(np.float32(-5.0 - h)))) for h in range(RET_HEADS)]
    scale = RET_DK ** -0.5

    @pl.when((pl.program_id(0) == 0) & (pl.program_id(1) == 0))
    def _():
        diff = (lax.broadcasted_iota(jnp.int32, (C, C), 0)
                - lax.broadcasted_iota(jnp.int32, (C, C), 1)).astype(F32)
        pos_v = lax.broadcasted_iota(jnp.int32, (C, RET_DV), 0).astype(F32)
        pos_k = lax.broadcasted_iota(jnp.int32, (C, RET_DK), 0).astype(F32)
        for h in range(RET_HEADS):
            lg = log_gamma[h]
            dmask_ref[h] = jnp.where(diff >= 0, jnp.exp(lg * jnp.maximum(diff, 0.0)), 0.0) * scale
            xi_ref[h] = jnp.exp(lg * (pos_v + 1.0))
            zeta_ref[h] = jnp.exp(lg * (C - 1.0 - pos_k)) * scale

    @pl.when(pl.program_id(1) == 0)
    def _():
        r_ref[...] = jnp.zeros_like(r_ref)

    for h in range(RET_HEADS):
        lg = log_gamma[h]
        q = q_ref[0, :, h * RET_DK:(h + 1) * RET_DK]
        k = k_ref[0, :, h * RET_DK:(h + 1) * RET_DK]
        v = v_ref[0, :, h * RET_DV:(h + 1) * RET_DV]
        s = lax.dot_general(q, k, NT_DIMS, preferred_element_type=F32)
        o = jnp.dot((s * dmask_ref[h]).astype(BF16), v, preferred_element_type=F32)
        state = r_ref[h]
        o = o + xi_ref[h] * jnp.dot(q, state.astype(BF16), preferred_element_type=F32)
        kz = (k.astype(F32) * zeta_ref[h]).astype(BF16)
        kv = lax.dot_general(kz, v, TN_DIMS, preferred_element_type=F32)
        r_ref[h] = math.exp(lg * C) * state + kv
        mu = jnp.mean(o, axis=-1, keepdims=True)
        d = o - mu
        var = jnp.mean(d * d, axis=-1, keepdims=True)
        on = d * lax.rsqrt(var + EPS) * gn_ref[:, h * RET_DV:(h + 1) * RET_DV]
        g = g_ref[0, :, h * RET_DV:(h + 1) * RET_DV].astype(F32)
        o_ref[0, :, h * RET_DV:(h + 1) * RET_DV] = (g * jax.nn.sigmoid(g) * on).astype(BF16)


def _retention(pb3, gn_g, C):
    B, S, _ = pb3.shape
    HV = RET_HEADS * RET_DV
    HK = RET_HEADS * RET_DK
    return pl.pallas_call(
        functools.partial(_ret_kernel, C=C),
        grid=(B, S // C),
        in_specs=[pl.BlockSpec((1, C, HK), lambda b, n: (b, n, PB_RQ // HK)),
                  pl.BlockSpec((1, C, HK), lambda b, n: (b, n, PB_RK // HK)),
                  pl.BlockSpec((1, C, HV), lambda b, n: (b, n, PB_RV // HV)),
                  pl.BlockSpec((1, C, HV), lambda b, n: (b, n, PB_RG // HV)),
                  pl.BlockSpec((1, HV), lambda b, n: (0, 0))],
        out_specs=pl.BlockSpec((1, C, HV), lambda b, n: (b, n, 0)),
        out_shape=jax.ShapeDtypeStruct((B, S, HV), BF16),
        scratch_shapes=[pltpu.VMEM((RET_HEADS, RET_DK, RET_DV), F32), pltpu.VMEM((RET_HEADS, C, C), F32),
                        pltpu.VMEM((RET_HEADS, C, RET_DV), F32), pltpu.VMEM((RET_HEADS, C, RET_DK), F32)],
        compiler_params=_params("arbitrary", "arbitrary"),
        name="retention",
    )(pb3, pb3, pb3, pb3, gn_g)


def _compress_kernel(kin_ref, vin_ref, posk_ref, posv_ref, w1k_ref, w1v_ref, b1k_ref, b1v_ref,
                     w2k_ref, w2v_ref, ko_ref, vo_ref, *, NP):
    GD = NSA_GROUPS * NSA_DK
    npc = CMP_BLOCK // CMP_STRIDE
    row = lax.broadcasted_iota(jnp.int32, (NP, GD), 0)
    for in_ref, pos_ref, w1_ref, b1_ref, w2_ref, out_ref in (
            (kin_ref, posk_ref, w1k_ref, b1k_ref, w2k_ref, ko_ref),
            (vin_ref, posv_ref, w1v_ref, b1v_ref, w2v_ref, vo_ref)):
        parts = []
        for p in range(npc):
            acc = jnp.zeros((NP, NSA_GROUPS * CMP_HIDDEN), F32)
            for i in range(CMP_STRIDE):
                tok = in_ref[0, pl.ds(i, NP, stride=CMP_STRIDE), :]
                j = p * CMP_STRIDE + i
                acc = acc + jnp.dot((tok + pos_ref[j:j + 1, :]).astype(BF16), w1_ref[j],
                                    preferred_element_type=F32)
            parts.append(acc)
        hidden = parts[0]
        for p in range(1, npc):
            hidden = hidden + pltpu.roll(parts[p], NP - p, axis=0)
        act = jax.nn.gelu(hidden + b1_ref[...]).astype(BF16)
        res = jnp.dot(act, w2_ref[...], preferred_element_type=F32)
        res = jnp.where(row < NP - (npc - 1), res, 0.0)
        if out_ref is vo_ref:
            res = res.T
            for g in range(NSA_GROUPS):
                out_ref[0, g] = res[g * NSA_DK:(g + 1) * NSA_DK, :]
        else:
            for g in range(NSA_GROUPS):
                out_ref[0, g] = res[:, g * NSA_DK:(g + 1) * NSA_DK]


def _compress(pf3, posk, posv, w1k, w1v, b1k, b1v, w2k, w2v):
    B, S, _ = pf3.shape
    NP = S // CMP_STRIDE
    GD = NSA_GROUPS * NSA_DK
    GH = NSA_GROUPS * CMP_HIDDEN
    const2 = lambda b: (0, 0)
    const3 = lambda b: (0, 0, 0)
    k_shape = (NSA_GROUPS, NP, NSA_DK)
    v_shape = (NSA_GROUPS, NSA_DK, NP)
    return pl.pallas_call(
        functools.partial(_compress_kernel, NP=NP),
        grid=(B,),
        in_specs=[pl.BlockSpec((1, S, GD), lambda b: (b, 0, PF_KCR // GD)),
                  pl.BlockSpec((1, S, GD), lambda b: (b, 0, PF_VCR // GD)),
                  pl.BlockSpec((CMP_BLOCK, GD), const2), pl.BlockSpec((CMP_BLOCK, GD), const2),
                  pl.BlockSpec((CMP_BLOCK, GD, GH), const3), pl.BlockSpec((CMP_BLOCK, GD, GH), const3),
                  pl.BlockSpec((1, GH), const2), pl.BlockSpec((1, GH), const2),
                  pl.BlockSpec((GH, GD), const2), pl.BlockSpec((GH, GD), const2)],
        out_specs=[pl.BlockSpec((1,) + k_shape, lambda b: (b, 0, 0, 0)),
                   pl.BlockSpec((1,) + v_shape, lambda b: (b, 0, 0, 0))],
        out_shape=[jax.ShapeDtypeStruct((B,) + k_shape, F32), jax.ShapeDtypeStruct((B,) + v_shape, F32)],
        compiler_params=_params("arbitrary"),
        name="compress",
    )(pf3, pf3, posk, posv, w1k, w1v, b1k, b1v, w2k, w2v)


SEL_LANE0 = 64
SEL_LANES = 32
POS_HI_LANE, POS_LO_LANE, PAD_LANE = 96, 97, 98
ONES_LANE = 64
MASK_BIG = 1e30


def _cmpattn_kernel(q_ref, kc_ref, vct_ref, gate_ref, ocmp_ref, sel_ref, *, tq, NP, NS):
    NC = NP - (CMP_BLOCK // CMP_STRIDE - 1)
    tile = pl.program_id(1)

    def run(rows):
        _cmpattn_tile(q_ref, kc_ref, vct_ref, gate_ref, ocmp_ref, sel_ref, tile * tq,
                      tq=tq, NP=rows, NC=NC, NS=NS)

    for v in range(pl.cdiv(NP * CMP_STRIDE, tq)):
        pl.when(tile == v)(functools.partial(run, min(NP, (v + 1) * tq // CMP_STRIDE)))


def _cmpattn_tile(q_ref, kc_ref, vct_ref, gate_ref, ocmp_ref, sel_ref, t0, *, tq, NP, NC, NS):
    tcol = t0 + lax.broadcasted_iota(jnp.int32, (NP, tq), 1)
    cidx = lax.broadcasted_iota(jnp.int32, (NP, tq), 0)
    dcmp = (tcol - (cidx * CMP_STRIDE + (CMP_BLOCK - 1))).astype(F32)
    cmask = (dcmp >= 0) & (cidx < NC)
    jj = lax.broadcasted_iota(jnp.int32, (SEL_LANES, NP), 0)
    cc = lax.broadcasted_iota(jnp.int32, (SEL_LANES, NP), 1)
    overlap_t = ((cc * CMP_STRIDE < jj * SEL_BLOCK + SEL_BLOCK)
                 & (cc * CMP_STRIDE + CMP_BLOCK > jj * SEL_BLOCK) & (jj < NS)).astype(BF16)
    blk = lax.broadcasted_iota(jnp.int32, (SEL_LANES, tq), 0)
    blk_f = blk.astype(F32)
    is_blk = blk < NS
    cur = jnp.right_shift(t0 + lax.broadcasted_iota(jnp.int32, (SEL_LANES, tq), 1),
                          SEL_BLOCK.bit_length() - 1)
    o_t = []
    for g in range(NSA_GROUPS):
        kc_parts = _split3(kc_ref[0, g, :NP, :])
        vct = vct_ref[0, g, :, :NP].astype(BF16)
        psum = jnp.zeros((NP, tq), F32)
        for r in range(NSA_R):
            hh = g * NSA_R + r
            q = q_ref[0, :, hh * NSA_DK:(hh + 1) * NSA_DK]
            s = sum(lax.dot_general(part, q, NT_DIMS, preferred_element_type=F32) for part in kc_parts)
            s = jnp.where(cmask, s - (2.0 ** -(hh + 1)) * dcmp, NEG)
            e = jnp.exp(s - jnp.max(s, axis=0, keepdims=True))
            p = jnp.where(cmask, e / jnp.sum(e, axis=0, keepdims=True), 0.0)
            o_t.append(jnp.dot(vct, p.astype(BF16), preferred_element_type=F32))
            psum = psum + p
        imp = sum(jnp.dot(overlap_t, part, preferred_element_type=F32) for part in _split3(psum))
        imp = jnp.where((blk == 0) | (blk == cur) | (blk == cur - 1), FORCE, imp)
        imp = jnp.where(blk > cur, -FORCE, imp)
        work = jnp.where(is_blk, imp, LOWEST)
        sel = jnp.zeros((SEL_LANES, tq), F32)
        for _ in range(min(SEL_TOPK, NS)):
            top = jnp.max(work, axis=0, keepdims=True)
            first = jnp.min(jnp.where(work == top, blk_f, float(SEL_LANES)), axis=0, keepdims=True)
            pick = blk_f == first
            sel = jnp.where(pick, 1.0, sel)
            work = jnp.where(pick, LOWEST, work)
        sel = jnp.where(is_blk & (blk <= cur), sel - 1.0, jnp.where(is_blk, -1.0, 0.0))
        placed = jnp.concatenate([jnp.zeros((SEL_LANE0, tq), F32), sel,
                                  jnp.zeros((LANES - SEL_LANE0 - SEL_LANES, tq), F32)], axis=0)
        sel_ref[0, g] = placed.T.astype(BF16)
    o = jnp.concatenate(o_t, axis=0).T
    gate = jax.nn.sigmoid(gate_ref[0])
    for hh in range(NSA_HEADS):
        cols = slice(hh * NSA_DK, (hh + 1) * NSA_DK)
        ocmp_ref[0, :, cols] = gate[:, hh:hh + 1] * o[:, cols]


def _cmpattn(pb3, kcmp, vcmp, pf3, tq):
    B, S, _ = pb3.shape
    NP = S // CMP_STRIDE
    NS = S // SEL_BLOCK
    HD = NSA_HEADS * NSA_DK
    return pl.pallas_call(
        functools.partial(_cmpattn_kernel, tq=tq, NP=NP, NS=NS),
        grid=(B, S // tq),
        in_specs=[pl.BlockSpec((1, tq, HD), lambda b, i: (b, i, PB_NQ // HD)),
                  pl.BlockSpec((1, NSA_GROUPS, NP, NSA_DK), lambda b, i: (b, 0, 0, 0)),
                  pl.BlockSpec((1, NSA_GROUPS, NSA_DK, NP), lambda b, i: (b, 0, 0, 0)),
                  pl.BlockSpec((1, tq, LANES), lambda b, i: (b, i, PF_NG // LANES))],
        out_specs=[pl.BlockSpec((1, tq, HD), lambda b, i: (b, i, 0)),
                   pl.BlockSpec((1, NSA_GROUPS, tq, LANES), lambda b, i: (b, 0, i, 0))],
        out_shape=[jax.ShapeDtypeStruct((B, S, HD), F32),
                   jax.ShapeDtypeStruct((B, NSA_GROUPS, S, LANES), BF16)],
        compiler_params=_params("arbitrary", "arbitrary"),
        name="cmpattn",
    )(pb3, kcmp, vcmp, pf3)


def _selwin_kernel(q_ref, ks_ref, vs_ref, kw_ref, vw_ref, sel_ref, wmask_ref, dmask_ref, gate_ref, ocmp_ref,
                   o_ref, ksx, vsx, kwx, vwx, *, tq, KC, S):
    WL = WINDOW + tq
    RB = 512
    i = pl.program_id(1)
    t0 = pl.multiple_of(i * tq, tq)
    sel_shift = SEL_BLOCK.bit_length() - 1

    @pl.when(i == 0)
    def _build():
        lane = lax.broadcasted_iota(jnp.int32, (RB, LANES), 1)
        is_head = lane < NSA_DK
        pad_k = jnp.where(lane == PAD_LANE, MASK_BIG, 0.0).astype(BF16)
        for g in range(NSA_GROUPS):
            kwx[g, 0:WINDOW, :] = pad_k[:WINDOW]
            vwx[g, 0:WINDOW, :] = jnp.zeros((WINDOW, LANES), BF16)
        for piece in range(S // RB):
            r0 = piece * RB
            pos = r0 + lax.broadcasted_iota(jnp.int32, (RB, LANES), 0)
            alibi = jnp.where(lane == POS_HI_LANE, jnp.right_shift(pos, sel_shift),
                              jnp.where(lane == POS_LO_LANE, pos & (SEL_BLOCK - 1), 0)).astype(F32)
            onehot = jnp.where(lane - SEL_LANE0 == jnp.right_shift(pos, sel_shift), MASK_BIG, 0.0)
            k_extra_win = alibi.astype(BF16)
            k_extra_sel = (alibi + onehot).astype(BF16)
            v_extra = jnp.where(lane == ONES_LANE, 1.0, 0.0).astype(BF16)
            for g in range(NSA_GROUPS):
                def head_lanes(ref):
                    t = ref[0, r0:r0 + RB, :]
                    return t if g == 0 else jnp.concatenate([t[:, NSA_DK:], t[:, :NSA_DK]], axis=1)
                ksx[g, r0:r0 + RB, :] = jnp.where(is_head, head_lanes(ks_ref), k_extra_sel)
                vsx[g, r0:r0 + RB, :] = jnp.where(is_head, head_lanes(vs_ref), v_extra)
                kwx[g, WINDOW + r0:WINDOW + r0 + RB, :] = jnp.where(is_head, head_lanes(kw_ref), k_extra_win)
                vwx[g, WINDOW + r0:WINDOW + r0 + RB, :] = jnp.where(is_head, head_lanes(vw_ref), v_extra)

    gate = jax.nn.sigmoid(gate_ref[0])
    lane = lax.broadcasted_iota(jnp.int32, (tq, LANES), 1)
    is_head = lane < NSA_DK
    diag_chunk = t0 // KC
    wmask = wmask_ref[...][None]
    dmask = dmask_ref[i % (KC // tq)][None]

    def masked_rows(s, mask):
        n = s.shape[-1]
        return (s.reshape(NSA_R, tq, n) + mask).reshape(NSA_R * tq, n)

    def normalise(acc):
        return acc[:, :NSA_DK] / acc[:, ONES_LANE:ONES_LANE + 1]

    def scores(q, k):
        half = q.shape[0] // 2
        return jnp.concatenate([lax.dot_general(q[:half], k, NT_DIMS, preferred_element_type=F32),
                                lax.dot_general(q[half:], k, NT_DIMS, preferred_element_type=F32)], axis=0)

    def weighted_values(p, v):
        half = p.shape[0] // 2
        p = p.astype(BF16)
        return jnp.concatenate([jnp.dot(p[:half], v, preferred_element_type=F32),
                                jnp.dot(p[half:], v, preferred_element_type=F32)], axis=0)

    def query_rows(g):
        sel = sel_ref[0, g].astype(F32)
        rows = []
        for r in range(NSA_R):
            hh = g * NSA_R + r
            qh = q_ref[0, :, (hh // 2) * LANES:(hh // 2 + 1) * LANES].astype(F32)
            if hh % 2:
                qh = jnp.concatenate([qh[:, NSA_DK:], qh[:, :NSA_DK]], axis=1)
            slope = 2.0 ** -(hh + 1)
            extra = jnp.where(lane == POS_HI_LANE, SEL_BLOCK * slope,
                              jnp.where(lane == POS_LO_LANE, slope, jnp.where(lane == PAD_LANE, -1.0, sel)))
            rows.append(jnp.where(is_head, qh, extra).astype(BF16))
        return jnp.concatenate(rows, axis=0)

    def tile(n_past):
        past = n_past * KC
        for g in range(NSA_GROUPS):
            q = query_rows(g)
            s = masked_rows(scores(q, kwx[g, pl.ds(t0, WL), :]), wmask)
            p = jnp.exp(s - jnp.max(s, axis=-1, keepdims=True))
            o_win = normalise(weighted_values(p, vwx[g, pl.ds(t0, WL), :]))

            s_own = masked_rows(scores(q, ksx[g, past:past + KC, :]), dmask)
            m = jnp.max(s_own, axis=-1, keepdims=True)
            if n_past:
                s_past = scores(q, ksx[g, 0:past, :])
                m = jnp.maximum(m, jnp.max(s_past, axis=-1, keepdims=True))
            acc = weighted_values(jnp.exp(s_own - m), vsx[g, past:past + KC, :])
            if n_past:
                acc = acc + weighted_values(jnp.exp(s_past - m), vsx[g, 0:past, :])
            o_sel = normalise(acc)

            for r in range(NSA_R):
                hh = g * NSA_R + r
                g_sel = gate[:, NSA_HEADS + hh:NSA_HEADS + hh + 1]
                g_win = gate[:, 2 * NSA_HEADS + hh:2 * NSA_HEADS + hh + 1]
                piece = g_sel * o_sel[r * tq:(r + 1) * tq] + g_win * o_win[r * tq:(r + 1) * tq]
                o_ref[0, :, hh * NSA_DK:(hh + 1) * NSA_DK] = (
                    ocmp_ref[0, :, hh * NSA_DK:(hh + 1) * NSA_DK] + piece).astype(BF16)

    for n_past in range(S // KC):
        pl.when(diag_chunk == n_past)(functools.partial(tile, n_past))


def _selwin(pb3, sel, pf3, ocmp, tq, KC):
    B, S, _ = pb3.shape
    HD = NSA_HEADS * NSA_DK
    WL = WINDOW + tq
    tr = jnp.arange(tq)[:, None]
    kc = jnp.arange(WL)[None, :]
    wmask = jnp.where((kc > tr) & (kc <= tr + WINDOW), 0.0, NEG).astype(F32)
    off = (jnp.arange(KC // tq) * tq)[:, None, None]
    dmask = jnp.where(jnp.arange(KC)[None, None, :] <= off + tr[None], 0.0, NEG).astype(F32)
    kv_spec = lambda off: pl.BlockSpec((1, S, LANES), lambda b, i: (b, 0, off // LANES))
    return pl.pallas_call(
        functools.partial(_selwin_kernel, tq=tq, KC=KC, S=S),
        grid=(B, S // tq),
        in_specs=[pl.BlockSpec((1, tq, HD), lambda b, i: (b, i, PB_NQ // HD)),
                  kv_spec(PB_KS), kv_spec(PB_VS), kv_spec(PB_KW), kv_spec(PB_VW),
                  pl.BlockSpec((1, NSA_GROUPS, tq, LANES), lambda b, i: (b, 0, i, 0)),
                  pl.BlockSpec((tq, WL), lambda b, i: (0, 0)),
                  pl.BlockSpec((KC // tq, tq, KC), lambda b, i: (0, 0, 0)),
                  pl.BlockSpec((1, tq, LANES), lambda b, i: (b, i, PF_NG // LANES)),
                  pl.BlockSpec((1, tq, HD), lambda b, i: (b, i, 0))],
        out_specs=pl.BlockSpec((1, tq, HD), lambda b, i: (b, i, 0)),
        out_shape=jax.ShapeDtypeStruct((B, S, HD), BF16),
        scratch_shapes=[pltpu.VMEM((NSA_GROUPS, S, LANES), BF16), pltpu.VMEM((NSA_GROUPS, S, LANES), BF16),
                        pltpu.VMEM((NSA_GROUPS, WINDOW + S, LANES), BF16),
                        pltpu.VMEM((NSA_GROUPS, WINDOW + S, LANES), BF16)],
        compiler_params=_params("arbitrary", "arbitrary"),
        name="selwin",
    )(pb3, pb3, pb3, pb3, pb3, sel, wmask, dmask, pf3, ocmp)


SUBLANES = 8


def _mix_ffn_kernel(x_ref, ya_ref, yb_ref, ga_ref, gb_ref, wa_ref, wb_ref, wo_ref, gn_ref,
                    wup_ref, cw_ref, cb_ref, wd_ref, gf_ref, o_ref, tail_ref, u_ref, *, tm, tf):
    @pl.when(pl.program_id(1) == 0)
    def _():
        tail_ref[...] = jnp.zeros_like(tail_ref)

    y_a = jnp.dot(ya_ref[0], wa_ref[...], preferred_element_type=F32)
    y_b = jnp.dot(yb_ref[0], wb_ref[...], preferred_element_type=F32)
    merged = (jax.nn.sigmoid(ga_ref[0].astype(F32)) * y_a + jax.nn.sigmoid(gb_ref[0].astype(F32)) * y_b)
    x1 = x_ref[0] + jnp.dot(merged.astype(BF16), wo_ref[...], preferred_element_type=F32)
    o_ref[0] = x1
    h = _rms(x1, gn_ref[...]).astype(BF16)

    top = lax.broadcasted_iota(jnp.int32, (SUBLANES, tf), 0)
    for c in range(D_FF // tf):
        cols = slice(c * tf, (c + 1) * tf)
        a = jnp.dot(h, wup_ref[:, cols], preferred_element_type=F32)
        b = jnp.dot(h, wup_ref[:, D_FF + c * tf:D_FF + (c + 1) * tf], preferred_element_type=F32)
        tail = tail_ref[c]
        tail_ref[c] = a[tm - SUBLANES:]
        ac = cb_ref[:, cols] + cw_ref[CONV_W - 1:CONV_W, cols] * a
        for d in range(1, CONV_W):
            sh = pltpu.roll(a, d, axis=0)
            head = jnp.where(top < d, pltpu.roll(tail, d, axis=0), sh[:SUBLANES])
            sh = jnp.concatenate([head, sh[SUBLANES:]], axis=0)
            ac = ac + cw_ref[CONV_W - 1 - d:CONV_W - d, cols] * sh
        u_ref[:, cols] = (jax.nn.gelu(ac) * b).astype(BF16)
    y = jnp.dot(u_ref[...], wd_ref[...], preferred_element_type=F32)
    o_ref[0] = _rms(o_ref[0] + y, gf_ref[...])


def _mix_ffn(x, ya, yb, pb3, wa, wb, wo, gn, w_up, conv_w, conv_b, w_down, gf, tm, tf):
    B, S, _ = x.shape
    HV = RET_HEADS * RET_DV
    HD = NSA_HEADS * NSA_DK
    rows = lambda width, col=0: pl.BlockSpec((1, tm, width), lambda b, i: (b, i, col))
    whole = lambda *shape: pl.BlockSpec(shape, lambda b, i: (0,) * len(shape), pipeline_mode=pl.Buffered(1))
    return pl.pallas_call(
        functools.partial(_mix_ffn_kernel, tm=tm, tf=tf),
        grid=(B, S // tm),
        in_specs=[rows(D_MODEL), rows(HV), rows(HD),
                  rows(D_MODEL, PB_GA // D_MODEL), rows(D_MODEL, PB_GB // D_MODEL),
                  whole(HV, D_MODEL), whole(HD, D_MODEL), whole(D_MODEL, D_MODEL), whole(1, D_MODEL),
                  whole(D_MODEL, 2 * D_FF), whole(CONV_W, D_FF), whole(1, D_FF), whole(D_FF, D_MODEL),
                  whole(1, D_MODEL)],
        out_specs=rows(D_MODEL),
        out_shape=jax.ShapeDtypeStruct((B, S, D_MODEL), F32),
        scratch_shapes=[pltpu.VMEM((D_FF // tf, SUBLANES, tf), F32), pltpu.VMEM((tm, D_FF), BF16)],
        compiler_params=_params("arbitrary", "arbitrary"),
        name="mix_ffn",
    )(x, ya, yb, pb3, pb3, wa, wb, wo, gn, w_up, conv_w, conv_b, w_down, gf)


def _block_diag_groups(w):
    z = jnp.zeros_like(w)
    return jnp.concatenate([jnp.concatenate([w, z], axis=-1), jnp.concatenate([z, w], axis=-1)], axis=-2)


def kernel(x, norm_mix, w_in, ret_gn_g, cmp_pos_k, cmp_w1_k, cmp_b1_k, cmp_w2_k, cmp_pos_v, cmp_w1_v,
           cmp_b1_v, cmp_w2_v, w_ret_o, w_nsa_o, w_out, norm_ffn, w_up, conv_w, conv_b, w_down,
           norm_final):
    B, S, D = x.shape
    assert D == D_MODEL and NSA_GROUPS == 2 and norm_mix.shape[0] == 1
    assert S % 512 == 0 and S >= 1024 and S // CMP_STRIDE <= LANES and S // SEL_BLOCK <= SEL_LANES
    M = B * S
    x2 = x.reshape(M, D)

    off = np.concatenate([[0], np.cumsum(IN_SIZES)])
    w = w_in[0].astype(BF16)
    seg = lambda i: w[:, off[i]:off[i + 1]]
    wp = jnp.concatenate(
        [seg(0), seg(1), seg(2), seg(4) * (NSA_DK ** -0.5), seg(7), seg(8), seg(9), seg(10),
         seg(3), seg(12), seg(13),
         seg(5), seg(6), seg(11), jnp.zeros((D, PF_W - PF_NG - IN_SIZES[11]), BF16)], axis=1)

    def cmp_weights(pos, w1, b1, w2):
        return (jnp.tile(pos, (1, NSA_GROUPS)),
                _block_diag_groups(w1.reshape(CMP_BLOCK, NSA_DK, CMP_HIDDEN)).astype(BF16),
                jnp.tile(b1, NSA_GROUPS)[None, :],
                _block_diag_groups(w2).astype(BF16))

    posk, w1k, b1k, w2k = cmp_weights(cmp_pos_k[0], cmp_w1_k[0], cmp_b1_k[0], cmp_w2_k[0])
    posv, w1v, b1v, w2v = cmp_weights(cmp_pos_v[0], cmp_w1_v[0], cmp_b1_v[0], cmp_w2_v[0])

    pb, pf = _inproj(x2, norm_mix, wp, tm=512)
    pb3 = pb.reshape(B, S, PB_W)
    pf3 = pf.reshape(B, S, PF_W)

    ya = _retention(pb3, ret_gn_g, C=256)
    kcmp, vcmp = _compress(pf3, posk, posv, w1k, w1v, b1k, b1v, w2k, w2v)
    ocmp, sel = _cmpattn(pb3, kcmp, vcmp, pf3, tq=1024)
    yb = _selwin(pb3, sel, pf3, ocmp, tq=256, KC=512)

    return _mix_ffn(x, ya, yb, pb3, w_ret_o[0].astype(BF16), w_nsa_o[0].astype(BF16),
                    w_out[0].astype(BF16), norm_ffn, w_up[0].astype(BF16), conv_w[0], conv_b,
                    w_down[0].astype(BF16), norm_final[None, :], tm=512, tf=256)
```

```python
import functools
import math

import numpy as np
import jax
import jax.numpy as jnp
from jax import lax
from jax.experimental import pallas as pl
from jax.experimental.pallas import tpu as pltpu

F32 = jnp.float32
BF16 = jnp.bfloat16

D_MODEL = 1024
RET_HEADS = 4
RET_DK = 128
RET_DV = 256
NSA_HEADS = 8
NSA_GROUPS = 2
NSA_R = NSA_HEADS // NSA_GROUPS
NSA_DK = 64
CMP_BLOCK = 32
CMP_STRIDE = 16
CMP_HIDDEN = 256
SEL_BLOCK = 64
SEL_TOPK = 8
WINDOW = 512
D_FF = 2816
CONV_W = 3
EPS = 1e-6
NEG = -1e30
FORCE = 1e9
LOWEST = -3e38

IN_SIZES = (RET_HEADS * RET_DK, RET_HEADS * RET_DK, RET_HEADS * RET_DV, RET_HEADS * RET_DV,
            NSA_HEADS * NSA_DK,
            NSA_GROUPS * NSA_DK, NSA_GROUPS * NSA_DK, NSA_GROUPS * NSA_DK,
            NSA_GROUPS * NSA_DK, NSA_GROUPS * NSA_DK, NSA_GROUPS * NSA_DK,
            3 * NSA_HEADS, D_MODEL, D_MODEL)

LANES = 128
VMEM_LIMIT = 56 * 1024 * 1024

PB_RQ, PB_RK, PB_RV, PB_NQ, PB_KS, PB_VS, PB_KW, PB_VW = 0, 512, 1024, 2048, 2560, 2688, 2816, 2944
PB_RG, PB_GA, PB_GB = 3072, 4096, 5120
PB_W = 6144
PF_KCR, PF_VCR, PF_NG = 0, 128, 256
PF_W = 384

NT_DIMS = (((1,), (1,)), ((), ()))
TN_DIMS = (((0,), (0,)), ((), ()))


def _params(*sem):
    return pltpu.CompilerParams(dimension_semantics=sem, vmem_limit_bytes=VMEM_LIMIT)


def _rms(x, g):
    return x * lax.rsqrt(jnp.mean(x * x, axis=-1, keepdims=True) + EPS) * g


def _split3(x):
    hi = x.astype(BF16)
    rest = x - hi.astype(F32)
    mid = rest.astype(BF16)
    return hi, mid, (rest - mid.astype(F32)).astype(BF16)


def _inproj_kernel(x_ref, g_ref, w_ref, pb_ref, pf_ref):
    hb = _rms(x_ref[...], g_ref[...]).astype(BF16)
    for c0 in range(0, PB_W, 512):
        pb_ref[:, c0:c0 + 512] = jnp.dot(
            hb, w_ref[:, c0:c0 + 512], preferred_element_type=F32).astype(BF16)
    for c0 in range(0, PF_W, 512):
        c1 = min(c0 + 512, PF_W)
        pf_ref[:, c0:c1] = jnp.dot(hb, w_ref[:, PB_W + c0:PB_W + c1], preferred_element_type=F32)


def _inproj(x2, g, wp, tm):
    M = x2.shape[0]
    return pl.pallas_call(
        _inproj_kernel,
        grid=(M // tm,),
        in_specs=[pl.BlockSpec((tm, D_MODEL), lambda i: (i, 0)),
                  pl.BlockSpec((1, D_MODEL), lambda i: (0, 0)),
                  pl.BlockSpec((D_MODEL, PB_W + PF_W), lambda i: (0, 0), pipeline_mode=pl.Buffered(1))],
        out_specs=[pl.BlockSpec((tm, PB_W), lambda i: (i, 0)),
                   pl.BlockSpec((tm, PF_W), lambda i: (i, 0))],
        out_shape=[jax.ShapeDtypeStruct((M, PB_W), BF16), jax.ShapeDtypeStruct((M, PF_W), F32)],
        compiler_params=_params("arbitrary"),
        name="inproj",
    )(x2, g, wp)


def _ret_kernel(q_ref, k_ref, v_ref, g_ref, gn_ref, o_ref, r_ref, dmask_ref, xi_ref, zeta_ref, *, C):
    log_gamma = [float(np.log1p(-np.exp2(np.float32(-5.0 - h)))) for h in range(RET_HEADS)]
    scale = RET_DK ** -0.5

    @pl.when((pl.program_id(0) == 0) & (pl.program_id(1) == 0))
    def _():
        diff = (lax.broadcasted_iota(jnp.int32, (C, C), 0)
                - lax.broadcasted_iota(jnp.int32, (C, C), 1)).astype(F32)
        pos_v = lax.broadcasted_iota(jnp.int32, (C, RET_DV), 0).astype(F32)
        pos_k = lax.broadcasted_iota(jnp.int32, (C, RET_DK), 0).astype(F32)
        for h in range(RET_HEADS):
            lg = log_gamma[h]
            dmask_ref[h] = jnp.where(diff >= 0, jnp.exp(lg * jnp.maximum(diff, 0.0)), 0.0) * scale
            xi_ref[h] = jnp.exp(lg * (pos_v + 1.0))
            zeta_ref[h] = jnp.exp(lg * (C - 1.0 - pos_k)) * scale

    @pl.when(pl.program_id(1) == 0)
    def _():
        r_ref[...] = jnp.zeros_like(r_ref)

    for h in range(RET_HEADS):
        lg = log_gamma[h]
        q = q_ref[0, :, h * RET_DK:(h + 1) * RET_DK]
        k = k_ref[0, :, h * RET_DK:(h + 1) * RET_DK]
        v = v_ref[0, :, h * RET_DV:(h + 1) * RET_DV]
        s = lax.dot_general(q, k, NT_DIMS, preferred_element_type=F32)
        o = jnp.dot((s * dmask_ref[h]).astype(BF16), v, preferred_element_type=F32)
        state = r_ref[h]
        o = o + xi_ref[h] * jnp.dot(q, state.astype(BF16), preferred_element_type=F32)
        kz = (k.astype(F32) * zeta_ref[h]).astype(BF16)
        kv = lax.dot_general(kz, v, TN_DIMS, preferred_element_type=F32)
        r_ref[h] = math.exp(lg * C) * state + kv
        mu = jnp.mean(o, axis=-1, keepdims=True)
        d = o - mu
        var = jnp.mean(d * d, axis=-1, keepdims=True)
        on = d * lax.rsqrt(var + EPS) * gn_ref[:, h * RET_DV:(h + 1) * RET_DV]
        g = g_ref[0, :, h * RET_DV:(h + 1) * RET_DV].astype(F32)
        o_ref[0, :, h * RET_DV:(h + 1) * RET_DV] = (g * jax.nn.sigmoid(g) * on).astype(BF16)


def _retention(pb3, gn_g, C):
    B, S, _ = pb3.shape
    HV = RET_HEADS * RET_DV
    HK = RET_HEADS * RET_DK
    return pl.pallas_call(
        functools.partial(_ret_kernel, C=C),
        grid=(B, S // C),
        in_specs=[pl.BlockSpec((1, C, HK), lambda b, n: (b, n, PB_RQ // HK)),
                  pl.BlockSpec((1, C, HK), lambda b, n: (b, n, PB_RK // HK)),
                  pl.BlockSpec((1, C, HV), lambda b, n: (b, n, PB_RV // HV)),
                  pl.BlockSpec((1, C, HV), lambda b, n: (b, n, PB_RG // HV)),
                  pl.BlockSpec((1, HV), lambda b, n: (0, 0))],
        out_specs=pl.BlockSpec((1, C, HV), lambda b, n: (b, n, 0)),
        out_shape=jax.ShapeDtypeStruct((B, S, HV), BF16),
        scratch_shapes=[pltpu.VMEM((RET_HEADS, RET_DK, RET_DV), F32), pltpu.VMEM((RET_HEADS, C, C), F32),
                        pltpu.VMEM((RET_HEADS, C, RET_DV), F32), pltpu.VMEM((RET_HEADS, C, RET_DK), F32)],
        compiler_params=_params("arbitrary", "arbitrary"),
        name="retention",
    )(pb3, pb3, pb3, pb3, gn_g)


def _compress_kernel(kin_ref, vin_ref, posk_ref, posv_ref, w1k_ref, w1v_ref, b1k_ref, b1v_ref,
                     w2k_ref, w2v_ref, ko_ref, vo_ref, *, NP):
    GD = NSA_GROUPS * NSA_DK
    npc = CMP_BLOCK // CMP_STRIDE
    row = lax.broadcasted_iota(jnp.int32, (NP, GD), 0)
    for in_ref, pos_ref, w1_ref, b1_ref, w2_ref, out_ref in (
            (kin_ref, posk_ref, w1k_ref, b1k_ref, w2k_ref, ko_ref),
            (vin_ref, posv_ref, w1v_ref, b1v_ref, w2v_ref, vo_ref)):
        parts = []
        for p in range(npc):
            acc = jnp.zeros((NP, NSA_GROUPS * CMP_HIDDEN), F32)
            for i in range(CMP_STRIDE):
                tok = in_ref[0, pl.ds(i, NP, stride=CMP_STRIDE), :]
                j = p * CMP_STRIDE + i
                acc = acc + jnp.dot((tok + pos_ref[j:j + 1, :]).astype(BF16), w1_ref[j],
                                    preferred_element_type=F32)
            parts.append(acc)
        hidden = parts[0]
        for p in range(1, npc):
            hidden = hidden + pltpu.roll(parts[p], NP - p, axis=0)
        act = jax.nn.gelu(hidden + b1_ref[...]).astype(BF16)
        res = jnp.dot(act, w2_ref[...], preferred_element_type=F32)
        res = jnp.where(row < NP - (npc - 1), res, 0.0)
        if out_ref is vo_ref:
            res = res.T
            for g in range(NSA_GROUPS):
                out_ref[0, g] = res[g * NSA_DK:(g + 1) * NSA_DK, :]
        else:
            for g in range(NSA_GROUPS):
                out_ref[0, g] = res[:, g * NSA_DK:(g + 1) * NSA_DK]


def _compress(pf3, posk, posv, w1k, w1v, b1k, b1v, w2k, w2v):
    B, S, _ = pf3.shape
    NP = S // CMP_STRIDE
    GD = NSA_GROUPS * NSA_DK
    GH = NSA_GROUPS * CMP_HIDDEN
    const2 = lambda b: (0, 0)
    const3 = lambda b: (0, 0, 0)
    k_shape = (NSA_GROUPS, NP, NSA_DK)
    v_shape = (NSA_GROUPS, NSA_DK, NP)
    return pl.pallas_call(
        functools.partial(_compress_kernel, NP=NP),
        grid=(B,),
        in_specs=[pl.BlockSpec((1, S, GD), lambda b: (b, 0, PF_KCR // GD)),
                  pl.BlockSpec((1, S, GD), lambda b: (b, 0, PF_VCR // GD)),
                  pl.BlockSpec((CMP_BLOCK, GD), const2), pl.BlockSpec((CMP_BLOCK, GD), const2),
                  pl.BlockSpec((CMP_BLOCK, GD, GH), const3), pl.BlockSpec((CMP_BLOCK, GD, GH), const3),
                  pl.BlockSpec((1, GH), const2), pl.BlockSpec((1, GH), const2),
                  pl.BlockSpec((GH, GD), const2), pl.BlockSpec((GH, GD), const2)],
        out_specs=[pl.BlockSpec((1,) + k_shape, lambda b: (b, 0, 0, 0)),
                   pl.BlockSpec((1,) + v_shape, lambda b: (b, 0, 0, 0))],
        out_shape=[jax.ShapeDtypeStruct((B,) + k_shape, F32), jax.ShapeDtypeStruct((B,) + v_shape, F32)],
        compiler_params=_params("arbitrary"),
        name="compress",
    )(pf3, pf3, posk, posv, w1k, w1v, b1k, b1v, w2k, w2v)


SEL_LANE0 = 64
SEL_LANES = 32
POS_HI_LANE, POS_LO_LANE = 96, 97
ONES_LANE = 64
MASK_BIG = 1e30


def _cmpattn_kernel(q_ref, kc_ref, vct_ref, gate_ref, ocmp_ref, sel_ref, *, tq, NP, NS):
    NC = NP - (CMP_BLOCK // CMP_STRIDE - 1)
    tile = pl.program_id(1)

    def run(rows):
        _cmpattn_tile(q_ref, kc_ref, vct_ref, gate_ref, ocmp_ref, sel_ref, tile * tq,
                      tq=tq, NP=rows, NC=NC, NS=NS)

    for v in range(pl.cdiv(NP * CMP_STRIDE, tq)):
        pl.when(tile == v)(functools.partial(run, min(NP, (v + 1) * tq // CMP_STRIDE)))


def _cmpattn_tile(q_ref, kc_ref, vct_ref, gate_ref, ocmp_ref, sel_ref, t0, *, tq, NP, NC, NS):
    tcol = t0 + lax.broadcasted_iota(jnp.int32, (NP, tq), 1)
    cidx = lax.broadcasted_iota(jnp.int32, (NP, tq), 0)
    dcmp = (tcol - (cidx * CMP_STRIDE + (CMP_BLOCK - 1))).astype(F32)
    cmask = (dcmp >= 0) & (cidx < NC)
    jj = lax.broadcasted_iota(jnp.int32, (SEL_LANES, NP), 0)
    cc = lax.broadcasted_iota(jnp.int32, (SEL_LANES, NP), 1)
    overlap_t = ((cc * CMP_STRIDE < jj * SEL_BLOCK + SEL_BLOCK)
                 & (cc * CMP_STRIDE + CMP_BLOCK > jj * SEL_BLOCK) & (jj < NS)).astype(BF16)
    blk = lax.broadcasted_iota(jnp.int32, (SEL_LANES, tq), 0)
    blk_f = blk.astype(F32)
    is_blk = blk < NS
    cur = jnp.right_shift(t0 + lax.broadcasted_iota(jnp.int32, (SEL_LANES, tq), 1),
                          SEL_BLOCK.bit_length() - 1)
    o_t = []
    for g in range(NSA_GROUPS):
        kc_parts = _split3(kc_ref[0, g, :NP, :])
        vct = vct_ref[0, g, :, :NP].astype(BF16)
        psum = jnp.zeros((NP, tq), F32)
        for r in range(NSA_R):
            hh = g * NSA_R + r
            q = q_ref[0, :, hh * NSA_DK:(hh + 1) * NSA_DK]
            s = sum(lax.dot_general(part, q, NT_DIMS, preferred_element_type=F32) for part in kc_parts)
            s = jnp.where(cmask, s - (2.0 ** -(hh + 1)) * dcmp, NEG)
            e = jnp.exp(s - jnp.max(s, axis=0, keepdims=True))
            p = jnp.where(cmask, e / jnp.sum(e, axis=0, keepdims=True), 0.0)
            o_t.append(jnp.dot(vct, p.astype(BF16), preferred_element_type=F32))
            psum = psum + p
        imp = sum(jnp.dot(overlap_t, part, preferred_element_type=F32) for part in _split3(psum))
        imp = jnp.where((blk == 0) | (blk == cur) | (blk == cur - 1), FORCE, imp)
        imp = jnp.where(blk > cur, -FORCE, imp)
        work = jnp.where(is_blk, imp, LOWEST)
        sel = jnp.zeros((SEL_LANES, tq), F32)
        for _ in range(min(SEL_TOPK, NS)):
            top = jnp.max(work, axis=0, keepdims=True)
            first = jnp.min(jnp.where(work == top, blk_f, float(SEL_LANES)), axis=0, keepdims=True)
            pick = blk_f == first
            sel = jnp.where(pick, 1.0, sel)
            work = jnp.where(pick, LOWEST, work)
        sel = jnp.where(is_blk & (blk <= cur), sel - 1.0, jnp.where(is_blk, -1.0, 0.0))
        placed = jnp.concatenate([jnp.zeros((SEL_LANE0, tq), F32), sel,
                                  jnp.zeros((LANES - SEL_LANE0 - SEL_LANES, tq), F32)], axis=0)
        sel_ref[0, g] = placed.T.astype(BF16)
    o = jnp.concatenate(o_t, axis=0).T
    gate = jax.nn.sigmoid(gate_ref[0])
    for hh in range(NSA_HEADS):
        cols = slice(hh * NSA_DK, (hh + 1) * NSA_DK)
        ocmp_ref[0, :, cols] = gate[:, hh:hh + 1] * o[:, cols]


def _cmpattn(pb3, kcmp, vcmp, pf3, tq):
    B, S, _ = pb3.shape
    NP = S // CMP_STRIDE
    NS = S // SEL_BLOCK
    HD = NSA_HEADS * NSA_DK
    return pl.pallas_call(
        functools.partial(_cmpattn_kernel, tq=tq, NP=NP, NS=NS),
        grid=(B, S // tq),
        in_specs=[pl.BlockSpec((1, tq, HD), lambda b, i: (b, i, PB_NQ // HD)),
                  pl.BlockSpec((1, NSA_GROUPS, NP, NSA_DK), lambda b, i: (b, 0, 0, 0)),
                  pl.BlockSpec((1, NSA_GROUPS, NSA_DK, NP), lambda b, i: (b, 0, 0, 0)),
                  pl.BlockSpec((1, tq, LANES), lambda b, i: (b, i, PF_NG // LANES))],
        out_specs=[pl.BlockSpec((1, tq, HD), lambda b, i: (b, i, 0)),
                   pl.BlockSpec((1, NSA_GROUPS, tq, LANES), lambda b, i: (b, 0, i, 0))],
        out_shape=[jax.ShapeDtypeStruct((B, S, HD), F32),
                   jax.ShapeDtypeStruct((B, NSA_GROUPS, S, LANES), BF16)],
        compiler_params=_params("arbitrary", "arbitrary"),
        name="cmpattn",
    )(pb3, kcmp, vcmp, pf3)


def _selwin_kernel(q_ref, ks_ref, vs_ref, kw_ref, vw_ref, sel_ref, wmask_ref, gate_ref, ocmp_ref,
                   o_ref, ksx, vsx, kwx, vwx, *, tq, S):
    WL = WINDOW + tq
    RB = 512
    i = pl.program_id(1)
    sel_shift = SEL_BLOCK.bit_length() - 1

    @pl.when(i == 0)
    def _build():
        lane = lax.broadcasted_iota(jnp.int32, (RB, LANES), 1)
        is_head = lane < NSA_DK
        for piece in range(S // RB):
            r0 = piece * RB
            pos = r0 + lax.broadcasted_iota(jnp.int32, (RB, LANES), 0)
            alibi = jnp.where(lane == POS_HI_LANE, jnp.right_shift(pos, sel_shift),
                              jnp.where(lane == POS_LO_LANE, pos & (SEL_BLOCK - 1), 0)).astype(F32)
            onehot = jnp.where(lane - SEL_LANE0 == jnp.right_shift(pos, sel_shift), MASK_BIG, 0.0)
            k_extra_win = alibi.astype(BF16)
            k_extra_sel = (alibi + onehot).astype(BF16)
            v_extra = jnp.where(lane == ONES_LANE, 1.0, 0.0).astype(BF16)
            for g in range(NSA_GROUPS):
                def head_lanes(ref):
                    t = ref[0, r0:r0 + RB, :]
                    return t if g == 0 else jnp.concatenate([t[:, NSA_DK:], t[:, :NSA_DK]], axis=1)
                ksx[g, r0:r0 + RB, :] = jnp.where(is_head, head_lanes(ks_ref), k_extra_sel)
                vsx[g, r0:r0 + RB, :] = jnp.where(is_head, head_lanes(vs_ref), v_extra)
                kwx[g, r0:r0 + RB, :] = jnp.where(is_head, head_lanes(kw_ref), k_extra_win)
                vwx[g, r0:r0 + RB, :] = jnp.where(is_head, head_lanes(vw_ref), v_extra)

    gate = jax.nn.sigmoid(gate_ref[0])
    lane = lax.broadcasted_iota(jnp.int32, (tq, LANES), 1)
    is_head = lane < NSA_DK

    def masked_rows(s, mask):
        n = s.shape[-1]
        return (s.reshape(NSA_R, tq, n) + mask).reshape(NSA_R * tq, n)

    def normalise(acc):
        return acc[:, :NSA_DK] / acc[:, ONES_LANE:ONES_LANE + 1]

    def scores(q, k):
        half = q.shape[0] // 2
        return jnp.concatenate([lax.dot_general(q[:half], k, NT_DIMS, preferred_element_type=F32),
                                lax.dot_general(q[half:], k, NT_DIMS, preferred_element_type=F32)], axis=0)

    def weighted_values(p, v):
        half = p.shape[0] // 2
        p = p.astype(BF16)
        return jnp.concatenate([jnp.dot(p[:half], v, preferred_element_type=F32),
                                jnp.dot(p[half:], v, preferred_element_type=F32)], axis=0)

    def query_rows(g):
        sel = sel_ref[0, g].astype(F32)
        rows = []
        for r in range(NSA_R):
            hh = g * NSA_R + r
            qh = q_ref[0, :, (hh // 2) * LANES:(hh // 2 + 1) * LANES].astype(F32)
            if hh % 2:
                qh = jnp.concatenate([qh[:, NSA_DK:], qh[:, :NSA_DK]], axis=1)
            slope = 2.0 ** -(hh + 1)
            extra = jnp.where(lane == POS_HI_LANE, SEL_BLOCK * slope, jnp.where(lane == POS_LO_LANE, slope, sel))
            rows.append(jnp.where(is_head, qh, extra).astype(BF16))
        return jnp.concatenate(rows, axis=0)

    def tile(t0):
        w0 = max(t0 - WINDOW, 0)
        end = t0 + tq
        wmask = wmask_ref[:, WL - (end - w0):][None]
        own_mask = wmask_ref[:, WL - tq:][None]
        for g in range(NSA_GROUPS):
            q = query_rows(g)
            s = masked_rows(scores(q, kwx[g, w0:end, :]), wmask)
            p = jnp.exp(s - jnp.max(s, axis=-1, keepdims=True))
            o_win = normalise(weighted_values(p, vwx[g, w0:end, :]))

            s = scores(q, ksx[g, 0:end, :])
            s_own = masked_rows(s[:, t0:], own_mask)
            s = jnp.concatenate([s[:, :t0], s_own], axis=1) if t0 else s_own
            p = jnp.exp(s - jnp.max(s, axis=-1, keepdims=True))
            o_sel = normalise(weighted_values(p, vsx[g, 0:end, :]))

            for r in range(NSA_R):
                hh = g * NSA_R + r
                g_sel = gate[:, NSA_HEADS + hh:NSA_HEADS + hh + 1]
                g_win = gate[:, 2 * NSA_HEADS + hh:2 * NSA_HEADS + hh + 1]
                piece = g_sel * o_sel[r * tq:(r + 1) * tq] + g_win * o_win[r * tq:(r + 1) * tq]
                o_ref[0, :, hh * NSA_DK:(hh + 1) * NSA_DK] = (
                    ocmp_ref[0, :, hh * NSA_DK:(hh + 1) * NSA_DK] + piece).astype(BF16)

    for v in range(S // tq):
        pl.when(i == v)(functools.partial(tile, v * tq))


def _selwin(pb3, sel, pf3, ocmp, tq):
    B, S, _ = pb3.shape
    HD = NSA_HEADS * NSA_DK
    WL = WINDOW + tq
    tr = jnp.arange(tq)[:, None]
    kc = jnp.arange(WL)[None, :]
    wmask = jnp.where((kc > tr) & (kc <= tr + WINDOW), 0.0, NEG).astype(F32)
    kv_spec = lambda off: pl.BlockSpec((1, S, LANES), lambda b, i: (b, 0, off // LANES))
    return pl.pallas_call(
        functools.partial(_selwin_kernel, tq=tq, S=S),
        grid=(B, S // tq),
        in_specs=[pl.BlockSpec((1, tq, HD), lambda b, i: (b, i, PB_NQ // HD)),
                  kv_spec(PB_KS), kv_spec(PB_VS), kv_spec(PB_KW), kv_spec(PB_VW),
                  pl.BlockSpec((1, NSA_GROUPS, tq, LANES), lambda b, i: (b, 0, i, 0)),
                  pl.BlockSpec((tq, WL), lambda b, i: (0, 0)),
                  pl.BlockSpec((1, tq, LANES), lambda b, i: (b, i, PF_NG // LANES)),
                  pl.BlockSpec((1, tq, HD), lambda b, i: (b, i, 0))],
        out_specs=pl.BlockSpec((1, tq, HD), lambda b, i: (b, i, 0)),
        out_shape=jax.ShapeDtypeStruct((B, S, HD), BF16),
        scratch_shapes=[pltpu.VMEM((NSA_GROUPS, S, LANES), BF16)] * 4,
        compiler_params=_params("arbitrary", "arbitrary"),
        name="selwin",
    )(pb3, pb3, pb3, pb3, pb3, sel, wmask, pf3, ocmp)


SUBLANES = 8


def _mix_ffn_kernel(x_ref, ya_ref, yb_ref, ga_ref, gb_ref, wa_ref, wb_ref, wo_ref, gn_ref,
                    wup_ref, cw_ref, cb_ref, wd_ref, gf_ref, o_ref, tail_ref, u_ref, *, tm, tf):
    @pl.when(pl.program_id(1) == 0)
    def _():
        tail_ref[...] = jnp.zeros_like(tail_ref)

    y_a = jnp.dot(ya_ref[0], wa_ref[...], preferred_element_type=F32)
    y_b = jnp.dot(yb_ref[0], wb_ref[...], preferred_element_type=F32)
    merged = (jax.nn.sigmoid(ga_ref[0].astype(F32)) * y_a + jax.nn.sigmoid(gb_ref[0].astype(F32)) * y_b)
    x1 = x_ref[0] + jnp.dot(merged.astype(BF16), wo_ref[...], preferred_element_type=F32)
    o_ref[0] = x1
    h = _rms(x1, gn_ref[...]).astype(BF16)

    top = lax.broadcasted_iota(jnp.int32, (SUBLANES, tf), 0)
    for c in range(D_FF // tf):
        cols = slice(c * tf, (c + 1) * tf)
        a = jnp.dot(h, wup_ref[:, cols], preferred_element_type=F32)
        b = jnp.dot(h, wup_ref[:, D_FF + c * tf:D_FF + (c + 1) * tf], preferred_element_type=F32)
        tail = tail_ref[c]
        tail_ref[c] = a[tm - SUBLANES:]
        ac = cb_ref[:, cols] + cw_ref[CONV_W - 1:CONV_W, cols] * a
        for d in range(1, CONV_W):
            sh = pltpu.roll(a, d, axis=0)
            head = jnp.where(top < d, pltpu.roll(tail, d, axis=0), sh[:SUBLANES])
            sh = jnp.concatenate([head, sh[SUBLANES:]], axis=0)
            ac = ac + cw_ref[CONV_W - 1 - d:CONV_W - d, cols] * sh
        u_ref[:, cols] = (jax.nn.gelu(ac) * b).astype(BF16)
    y = jnp.dot(u_ref[...], wd_ref[...], preferred_element_type=F32)
    o_ref[0] = _rms(o_ref[0] + y, gf_ref[...])


def _mix_ffn(x, ya, yb, pb3, wa, wb, wo, gn, w_up, conv_w, conv_b, w_down, gf, tm, tf):
    B, S, _ = x.shape
    HV = RET_HEADS * RET_DV
    HD = NSA_HEADS * NSA_DK
    rows = lambda width, col=0: pl.BlockSpec((1, tm, width), lambda b, i: (b, i, col))
    whole = lambda *shape: pl.BlockSpec(shape, lambda b, i: (0,) * len(shape), pipeline_mode=pl.Buffered(1))
    return pl.pallas_call(
        functools.partial(_mix_ffn_kernel, tm=tm, tf=tf),
        grid=(B, S // tm),
        in_specs=[rows(D_MODEL), rows(HV), rows(HD),
                  rows(D_MODEL, PB_GA // D_MODEL), rows(D_MODEL, PB_GB // D_MODEL),
                  whole(HV, D_MODEL), whole(HD, D_MODEL), whole(D_MODEL, D_MODEL), whole(1, D_MODEL),
                  whole(D_MODEL, 2 * D_FF), whole(CONV_W, D_FF), whole(1, D_FF), whole(D_FF, D_MODEL),
                  whole(1, D_MODEL)],
        out_specs=rows(D_MODEL),
        out_shape=jax.ShapeDtypeStruct((B, S, D_MODEL), F32),
        scratch_shapes=[pltpu.VMEM((D_FF // tf, SUBLANES, tf), F32), pltpu.VMEM((tm, D_FF), BF16)],
        compiler_params=_params("arbitrary", "arbitrary"),
        name="mix_ffn",
    )(x, ya, yb, pb3, pb3, wa, wb, wo, gn, w_up, conv_w, conv_b, w_down, gf)


def _block_diag_groups(w):
    z = jnp.zeros_like(w)
    return jnp.concatenate([jnp.concatenate([w, z], axis=-1), jnp.concatenate([z, w], axis=-1)], axis=-2)


def kernel(x, norm_mix, w_in, ret_gn_g, cmp_pos_k, cmp_w1_k, cmp_b1_k, cmp_w2_k, cmp_pos_v, cmp_w1_v,
           cmp_b1_v, cmp_w2_v, w_ret_o, w_nsa_o, w_out, norm_ffn, w_up, conv_w, conv_b, w_down,
           norm_final):
    B, S, D = x.shape
    assert D == D_MODEL and NSA_GROUPS == 2 and norm_mix.shape[0] == 1
    assert S % 512 == 0 and S >= 1024 and S // CMP_STRIDE <= LANES and S // SEL_BLOCK <= SEL_LANES
    M = B * S
    x2 = x.reshape(M, D)

    off = np.concatenate([[0], np.cumsum(IN_SIZES)])
    w = w_in[0].astype(BF16)
    seg = lambda i: w[:, off[i]:off[i + 1]]
    wp = jnp.concatenate(
        [seg(0), seg(1), seg(2), seg(4) * (NSA_DK ** -0.5), seg(7), seg(8), seg(9), seg(10),
         seg(3), seg(12), seg(13),
         seg(5), seg(6), seg(11), jnp.zeros((D, PF_W - PF_NG - IN_SIZES[11]), BF16)], axis=1)

    def cmp_weights(pos, w1, b1, w2):
        return (jnp.tile(pos, (1, NSA_GROUPS)),
                _block_diag_groups(w1.reshape(CMP_BLOCK, NSA_DK, CMP_HIDDEN)).astype(BF16),
                jnp.tile(b1, NSA_GROUPS)[None, :],
                _block_diag_groups(w2).astype(BF16))

    posk, w1k, b1k, w2k = cmp_weights(cmp_pos_k[0], cmp_w1_k[0], cmp_b1_k[0], cmp_w2_k[0])
    posv, w1v, b1v, w2v = cmp_weights(cmp_pos_v[0], cmp_w1_v[0], cmp_b1_v[0], cmp_w2_v[0])

    pb, pf = _inproj(x2, norm_mix, wp, tm=512)
    pb3 = pb.reshape(B, S, PB_W)
    pf3 = pf.reshape(B, S, PF_W)

    ya = _retention(pb3, ret_gn_g, C=256)
    kcmp, vcmp = _compress(pf3, posk, posv, w1k, w1v, b1k, b1v, w2k, w2v)
    ocmp, sel = _cmpattn(pb3, kcmp, vcmp, pf3, tq=1024)
    yb = _selwin(pb3, sel, pf3, ocmp, tq=256)

    return _mix_ffn(x, ya, yb, pb3, w_ret_o[0].astype(BF16), w_nsa_o[0].astype(BF16),
                    w_out[0].astype(BF16), norm_ffn, w_up[0].astype(BF16), conv_w[0], conv_b,
                    w_down[0].astype(BF16), norm_final[None, :], tm=512, tf=256)
```

```python
import functools
import math

import numpy as np
import jax
import jax.numpy as jnp
from jax import lax
from jax.experimental import pallas as pl
from jax.experimental.pallas import tpu as pltpu

F32 = jnp.float32
BF16 = jnp.bfloat16

D_MODEL = 1024
RET_HEADS = 4
RET_DK = 128
RET_DV = 256
NSA_HEADS = 8
NSA_GROUPS = 2
NSA_R = NSA_HEADS // NSA_GROUPS
NSA_DK = 64
CMP_BLOCK = 32
CMP_STRIDE = 16
CMP_HIDDEN = 256
SEL_BLOCK = 64
SEL_TOPK = 8
WINDOW = 512
D_FF = 2816
CONV_W = 3
EPS = 1e-6
NEG = -1e30
FORCE = 1e9
LOWEST = -3e38

IN_SIZES = (RET_HEADS * RET_DK, RET_HEADS * RET_DK, RET_HEADS * RET_DV, RET_HEADS * RET_DV,
            NSA_HEADS * NSA_DK,
            NSA_GROUPS * NSA_DK, NSA_GROUPS * NSA_DK, NSA_GROUPS * NSA_DK,
            NSA_GROUPS * NSA_DK, NSA_GROUPS * NSA_DK, NSA_GROUPS * NSA_DK,
            3 * NSA_HEADS, D_MODEL, D_MODEL)

LANES = 128
VMEM_LIMIT = 56 * 1024 * 1024

RT_RQ, RT_RK, RT_RV, RT_RG = 0, 512, 1024, 2048
RT_W = 3072
PB_NQ, PB_KS, PB_VS, PB_KW, PB_VW, PB_GA, PB_GB = 0, 512, 640, 768, 896, 1024, 2048
PB_W = 3072
PF_KCR, PF_VCR, PF_NG = 0, 128, 256
PF_W = 384

NT_DIMS = (((1,), (1,)), ((), ()))
TN_DIMS = (((0,), (0,)), ((), ()))


def _params(*sem):
    return pltpu.CompilerParams(dimension_semantics=sem, vmem_limit_bytes=VMEM_LIMIT)


def _rms(x, g):
    return x * lax.rsqrt(jnp.mean(x * x, axis=-1, keepdims=True) + EPS) * g


def _split3(x):
    hi = x.astype(BF16)
    rest = x - hi.astype(F32)
    mid = rest.astype(BF16)
    return hi, mid, (rest - mid.astype(F32)).astype(BF16)


def _inproj_ret_kernel(x_ref, g_ref, w_ref, gn_ref, pb_ref, pf_ref, ya_ref,
                       stash0, stash1, r_ref, dmask_ref, xi_ref, zeta_ref, *, tm, C, S):
    log_gamma = [float(np.log1p(-np.exp2(np.float32(-5.0 - h)))) for h in range(RET_HEADS)]
    scale = RET_DK ** -0.5
    i = pl.program_id(0)

    @pl.when(i == 0)
    def _():
        diff = (lax.broadcasted_iota(jnp.int32, (C, C), 0)
                - lax.broadcasted_iota(jnp.int32, (C, C), 1)).astype(F32)
        pos_v = lax.broadcasted_iota(jnp.int32, (C, RET_DV), 0).astype(F32)
        pos_k = lax.broadcasted_iota(jnp.int32, (C, RET_DK), 0).astype(F32)
        for h in range(RET_HEADS):
            lg = log_gamma[h]
            dmask_ref[h] = jnp.where(diff >= 0, jnp.exp(lg * jnp.maximum(diff, 0.0)), 0.0) * scale
            xi_ref[h] = jnp.exp(lg * (pos_v + 1.0))
            zeta_ref[h] = jnp.exp(lg * (C - 1.0 - pos_k)) * scale
        stash1[...] = jnp.zeros_like(stash1)
        r_ref[...] = jnp.zeros_like(r_ref)

    seq_start = lax.rem(i + (S // tm) - 1, S // tm) == 0

    def step(new_ref, old_ref):
        hb = _rms(x_ref[...], g_ref[...]).astype(BF16)
        for c0 in range(0, RT_W, 512):
            new_ref[:, c0:c0 + 512] = jnp.dot(
                hb, w_ref[:, c0:c0 + 512], preferred_element_type=F32).astype(BF16)
        for c0 in range(0, PB_W, 512):
            pb_ref[:, c0:c0 + 512] = jnp.dot(
                hb, w_ref[:, RT_W + c0:RT_W + c0 + 512], preferred_element_type=F32).astype(BF16)
        pf_ref[...] = jnp.dot(hb, w_ref[:, RT_W + PB_W:], preferred_element_type=F32)

        for cc in range(tm // C):
            rows = slice(cc * C, (cc + 1) * C)
            for h in range(RET_HEADS):
                lg = log_gamma[h]
                q = old_ref[rows, RT_RQ + h * RET_DK:RT_RQ + (h + 1) * RET_DK]
                k = old_ref[rows, RT_RK + h * RET_DK:RT_RK + (h + 1) * RET_DK]
                v = old_ref[rows, RT_RV + h * RET_DV:RT_RV + (h + 1) * RET_DV]
                g = old_ref[rows, RT_RG + h * RET_DV:RT_RG + (h + 1) * RET_DV].astype(F32)
                s = lax.dot_general(q, k, NT_DIMS, preferred_element_type=F32)
                o = jnp.dot((s * dmask_ref[h]).astype(BF16), v, preferred_element_type=F32)
                state = r_ref[h]
                if cc == 0:
                    state = jnp.where(seq_start, 0.0, state)
                o = o + xi_ref[h] * jnp.dot(q, state.astype(BF16), preferred_element_type=F32)
                kz = (k.astype(F32) * zeta_ref[h]).astype(BF16)
                kv = lax.dot_general(kz, v, TN_DIMS, preferred_element_type=F32)
                r_ref[h] = math.exp(lg * C) * state + kv
                mu = jnp.mean(o, axis=-1, keepdims=True)
                d = o - mu
                var = jnp.mean(d * d, axis=-1, keepdims=True)
                on = d * lax.rsqrt(var + EPS) * gn_ref[:, h * RET_DV:(h + 1) * RET_DV]
                ya_ref[rows, h * RET_DV:(h + 1) * RET_DV] = (g * jax.nn.sigmoid(g) * on).astype(BF16)

    pl.when(lax.rem(i, 2) == 0)(functools.partial(step, stash0, stash1))
    pl.when(lax.rem(i, 2) == 1)(functools.partial(step, stash1, stash0))


def _inproj_ret(x2, g, wp, gn_g, S, tm, C):
    M = x2.shape[0]
    nt = M // tm
    HV = RET_HEADS * RET_DV
    cur = lambda i: (jnp.minimum(i, nt - 1), 0)
    prev = lambda i: (jnp.maximum(i - 1, 0), 0)
    const = lambda i: (0, 0)
    return pl.pallas_call(
        functools.partial(_inproj_ret_kernel, tm=tm, C=C, S=S),
        grid=(nt + 1,),
        in_specs=[pl.BlockSpec((tm, D_MODEL), cur),
                  pl.BlockSpec((1, D_MODEL), const),
                  pl.BlockSpec((D_MODEL, RT_W + PB_W + PF_W), const, pipeline_mode=pl.Buffered(1)),
                  pl.BlockSpec((1, HV), const)],
        out_specs=[pl.BlockSpec((tm, PB_W), cur), pl.BlockSpec((tm, PF_W), cur), pl.BlockSpec((tm, HV), prev)],
        out_shape=[jax.ShapeDtypeStruct((M, PB_W), BF16), jax.ShapeDtypeStruct((M, PF_W), F32),
                   jax.ShapeDtypeStruct((M, HV), BF16)],
        scratch_shapes=[pltpu.VMEM((tm, RT_W), BF16), pltpu.VMEM((tm, RT_W), BF16),
                        pltpu.VMEM((RET_HEADS, RET_DK, RET_DV), F32), pltpu.VMEM((RET_HEADS, C, C), F32),
                        pltpu.VMEM((RET_HEADS, C, RET_DV), F32), pltpu.VMEM((RET_HEADS, C, RET_DK), F32)],
        compiler_params=_params("arbitrary"),
        name="inproj_ret",
    )(x2, g, wp, gn_g)


def _compress_kernel(kin_ref, vin_ref, posk_ref, posv_ref, w1k_ref, w1v_ref, b1k_ref, b1v_ref,
                     w2k_ref, w2v_ref, ko_ref, vo_ref, *, NP):
    GD = NSA_GROUPS * NSA_DK
    npc = CMP_BLOCK // CMP_STRIDE
    row = lax.broadcasted_iota(jnp.int32, (NP, GD), 0)
    for in_ref, pos_ref, w1_ref, b1_ref, w2_ref, out_ref in (
            (kin_ref, posk_ref, w1k_ref, b1k_ref, w2k_ref, ko_ref),
            (vin_ref, posv_ref, w1v_ref, b1v_ref, w2v_ref, vo_ref)):
        parts = []
        for p in range(npc):
            acc = jnp.zeros((NP, NSA_GROUPS * CMP_HIDDEN), F32)
            for i in range(CMP_STRIDE):
                tok = in_ref[0, pl.ds(i, NP, stride=CMP_STRIDE), :]
                j = p * CMP_STRIDE + i
                acc = acc + jnp.dot((tok + pos_ref[j:j + 1, :]).astype(BF16), w1_ref[j],
                                    preferred_element_type=F32)
            parts.append(acc)
        hidden = parts[0]
        for p in range(1, npc):
            hidden = hidden + pltpu.roll(parts[p], NP - p, axis=0)
        act = jax.nn.gelu(hidden + b1_ref[...]).astype(BF16)
        res = jnp.dot(act, w2_ref[...], preferred_element_type=F32)
        res = jnp.where(row < NP - (npc - 1), res, 0.0)
        if out_ref is vo_ref:
            res = res.T
            for g in range(NSA_GROUPS):
                out_ref[0, g] = res[g * NSA_DK:(g + 1) * NSA_DK, :]
        else:
            for g in range(NSA_GROUPS):
                out_ref[0, g] = res[:, g * NSA_DK:(g + 1) * NSA_DK]


def _compress(pf3, posk, posv, w1k, w1v, b1k, b1v, w2k, w2v):
    B, S, _ = pf3.shape
    NP = S // CMP_STRIDE
    GD = NSA_GROUPS * NSA_DK
    GH = NSA_GROUPS * CMP_HIDDEN
    const2 = lambda b: (0, 0)
    const3 = lambda b: (0, 0, 0)
    k_shape = (NSA_GROUPS, NP, NSA_DK)
    v_shape = (NSA_GROUPS, NSA_DK, NP)
    return pl.pallas_call(
        functools.partial(_compress_kernel, NP=NP),
        grid=(B,),
        in_specs=[pl.BlockSpec((1, S, GD), lambda b: (b, 0, PF_KCR // GD)),
                  pl.BlockSpec((1, S, GD), lambda b: (b, 0, PF_VCR // GD)),
                  pl.BlockSpec((CMP_BLOCK, GD), const2), pl.BlockSpec((CMP_BLOCK, GD), const2),
                  pl.BlockSpec((CMP_BLOCK, GD, GH), const3), pl.BlockSpec((CMP_BLOCK, GD, GH), const3),
                  pl.BlockSpec((1, GH), const2), pl.BlockSpec((1, GH), const2),
                  pl.BlockSpec((GH, GD), const2), pl.BlockSpec((GH, GD), const2)],
        out_specs=[pl.BlockSpec((1,) + k_shape, lambda b: (b, 0, 0, 0)),
                   pl.BlockSpec((1,) + v_shape, lambda b: (b, 0, 0, 0))],
        out_shape=[jax.ShapeDtypeStruct((B,) + k_shape, F32), jax.ShapeDtypeStruct((B,) + v_shape, F32)],
        compiler_params=_params("arbitrary"),
        name="compress",
    )(pf3, pf3, posk, posv, w1k, w1v, b1k, b1v, w2k, w2v)


SEL_LANE0 = 64
SEL_LANES = 32
POS_HI_LANE, POS_LO_LANE, PAD_LANE = 96, 97, 98
ONES_LANE = 64
MASK_BIG = 1e30


def _cmpattn_kernel(q_ref, kc_ref, vct_ref, gate_ref, ocmp_ref, sel_ref, *, tq, NP, NS):
    NC = NP - (CMP_BLOCK // CMP_STRIDE - 1)
    tile = pl.program_id(1)

    def run(rows):
        _cmpattn_tile(q_ref, kc_ref, vct_ref, gate_ref, ocmp_ref, sel_ref, tile * tq,
                      tq=tq, NP=rows, NC=NC, NS=NS)

    for v in range(pl.cdiv(NP * CMP_STRIDE, tq)):
        pl.when(tile == v)(functools.partial(run, min(NP, (v + 1) * tq // CMP_STRIDE)))


def _cmpattn_tile(q_ref, kc_ref, vct_ref, gate_ref, ocmp_ref, sel_ref, t0, *, tq, NP, NC, NS):
    tcol = t0 + lax.broadcasted_iota(jnp.int32, (NP, tq), 1)
    cidx = lax.broadcasted_iota(jnp.int32, (NP, tq), 0)
    dcmp = (tcol - (cidx * CMP_STRIDE + (CMP_BLOCK - 1))).astype(F32)
    cmask = (dcmp >= 0) & (cidx < NC)
    jj = lax.broadcasted_iota(jnp.int32, (SEL_LANES, NP), 0)
    cc = lax.broadcasted_iota(jnp.int32, (SEL_LANES, NP), 1)
    overlap_t = ((cc * CMP_STRIDE < jj * SEL_BLOCK + SEL_BLOCK)
                 & (cc * CMP_STRIDE + CMP_BLOCK > jj * SEL_BLOCK) & (jj < NS)).astype(BF16)
    blk = lax.broadcasted_iota(jnp.int32, (SEL_LANES, tq), 0)
    blk_f = blk.astype(F32)
    is_blk = blk < NS
    cur = jnp.right_shift(t0 + lax.broadcasted_iota(jnp.int32, (SEL_LANES, tq), 1),
                          SEL_BLOCK.bit_length() - 1)
    o_t = []
    for g in range(NSA_GROUPS):
        kc_parts = _split3(kc_ref[0, g, :NP, :])
        vct = vct_ref[0, g, :, :NP].astype(BF16)
        psum = jnp.zeros((NP, tq), F32)
        for r in range(NSA_R):
            hh = g * NSA_R + r
            q = q_ref[0, :, hh * NSA_DK:(hh + 1) * NSA_DK]
            s = sum(lax.dot_general(part, q, NT_DIMS, preferred_element_type=F32) for part in kc_parts)
            s = jnp.where(cmask, s - (2.0 ** -(hh + 1)) * dcmp, NEG)
            e = jnp.exp(s - jnp.max(s, axis=0, keepdims=True))
            p = jnp.where(cmask, e / jnp.sum(e, axis=0, keepdims=True), 0.0)
            o_t.append(jnp.dot(vct, p.astype(BF16), preferred_element_type=F32))
            psum = psum + p
        imp = sum(jnp.dot(overlap_t, part, preferred_element_type=F32) for part in _split3(psum))
        imp = jnp.where((blk == 0) | (blk == cur) | (blk == cur - 1), FORCE, imp)
        imp = jnp.where(blk > cur, -FORCE, imp)
        work = jnp.where(is_blk, imp, LOWEST)
        sel = jnp.zeros((SEL_LANES, tq), F32)
        for _ in range(min(SEL_TOPK, NS)):
            top = jnp.max(work, axis=0, keepdims=True)
            first = jnp.min(jnp.where(work == top, blk_f, float(SEL_LANES)), axis=0, keepdims=True)
            pick = blk_f == first
            sel = jnp.where(pick, 1.0, sel)
            work = jnp.where(pick, LOWEST, work)
        sel = jnp.where(is_blk & (blk <= cur), sel - 1.0, jnp.where(is_blk, -1.0, 0.0))
        placed = jnp.concatenate([jnp.zeros((SEL_LANE0, tq), F32), sel,
                                  jnp.zeros((LANES - SEL_LANE0 - SEL_LANES, tq), F32)], axis=0)
        sel_ref[0, g] = placed.T.astype(BF16)
    o = jnp.concatenate(o_t, axis=0).T
    gate = jax.nn.sigmoid(gate_ref[0])
    for hh in range(NSA_HEADS):
        cols = slice(hh * NSA_DK, (hh + 1) * NSA_DK)
        ocmp_ref[0, :, cols] = gate[:, hh:hh + 1] * o[:, cols]


def _cmpattn(pb3, kcmp, vcmp, pf3, tq):
    B, S, _ = pb3.shape
    NP = S // CMP_STRIDE
    NS = S // SEL_BLOCK
    HD = NSA_HEADS * NSA_DK
    return pl.pallas_call(
        functools.partial(_cmpattn_kernel, tq=tq, NP=NP, NS=NS),
        grid=(B, S // tq),
        in_specs=[pl.BlockSpec((1, tq, HD), lambda b, i: (b, i, PB_NQ // HD)),
                  pl.BlockSpec((1, NSA_GROUPS, NP, NSA_DK), lambda b, i: (b, 0, 0, 0)),
                  pl.BlockSpec((1, NSA_GROUPS, NSA_DK, NP), lambda b, i: (b, 0, 0, 0)),
                  pl.BlockSpec((1, tq, LANES), lambda b, i: (b, i, PF_NG // LANES))],
        out_specs=[pl.BlockSpec((1, tq, HD), lambda b, i: (b, i, 0)),
                   pl.BlockSpec((1, NSA_GROUPS, tq, LANES), lambda b, i: (b, 0, i, 0))],
        out_shape=[jax.ShapeDtypeStruct((B, S, HD), F32),
                   jax.ShapeDtypeStruct((B, NSA_GROUPS, S, LANES), BF16)],
        compiler_params=_params("arbitrary", "arbitrary"),
        name="cmpattn",
    )(pb3, kcmp, vcmp, pf3)


def _selwin_kernel(q_ref, ks_ref, vs_ref, kw_ref, vw_ref, sel_ref, wmask_ref, dmask_ref, gate_ref, ocmp_ref,
                   o_ref, ksx, vsx, kwx, vwx, *, tq, KC, S):
    WL = WINDOW + tq
    RB = 512
    i = pl.program_id(1)
    t0 = pl.multiple_of(i * tq, tq)
    sel_shift = SEL_BLOCK.bit_length() - 1

    @pl.when(i == 0)
    def _build():
        lane = lax.broadcasted_iota(jnp.int32, (RB, LANES), 1)
        is_head = lane < NSA_DK
        pad_k = jnp.where(lane == PAD_LANE, MASK_BIG, 0.0).astype(BF16)
        for g in range(NSA_GROUPS):
            kwx[g, 0:WINDOW, :] = pad_k[:WINDOW]
            vwx[g, 0:WINDOW, :] = jnp.zeros((WINDOW, LANES), BF16)
        for piece in range(S // RB):
            r0 = piece * RB
            pos = r0 + lax.broadcasted_iota(jnp.int32, (RB, LANES), 0)
            alibi = jnp.where(lane == POS_HI_LANE, jnp.right_shift(pos, sel_shift),
                              jnp.where(lane == POS_LO_LANE, pos & (SEL_BLOCK - 1), 0)).astype(F32)
            onehot = jnp.where(lane - SEL_LANE0 == jnp.right_shift(pos, sel_shift), MASK_BIG, 0.0)
            k_extra_win = alibi.astype(BF16)
            k_extra_sel = (alibi + onehot).astype(BF16)
            v_extra = jnp.where(lane == ONES_LANE, 1.0, 0.0).astype(BF16)
            for g in range(NSA_GROUPS):
                def head_lanes(ref):
                    t = ref[0, r0:r0 + RB, :]
                    return t if g == 0 else jnp.concatenate([t[:, NSA_DK:], t[:, :NSA_DK]], axis=1)
                ksx[g, r0:r0 + RB, :] = jnp.where(is_head, head_lanes(ks_ref), k_extra_sel)
                vsx[g, r0:r0 + RB, :] = jnp.where(is_head, head_lanes(vs_ref), v_extra)
                kwx[g, WINDOW + r0:WINDOW + r0 + RB, :] = jnp.where(is_head, head_lanes(kw_ref), k_extra_win)
                vwx[g, WINDOW + r0:WINDOW + r0 + RB, :] = jnp.where(is_head, head_lanes(vw_ref), v_extra)

    gate = jax.nn.sigmoid(gate_ref[0])
    lane = lax.broadcasted_iota(jnp.int32, (tq, LANES), 1)
    is_head = lane < NSA_DK
    diag_chunk = t0 // KC
    wmask = wmask_ref[...][None]
    dmask = dmask_ref[i % (KC // tq)][None]

    def masked_rows(s, mask):
        n = s.shape[-1]
        return (s.reshape(NSA_R, tq, n) + mask).reshape(NSA_R * tq, n)

    def normalise(acc):
        return acc[:, :NSA_DK] / acc[:, ONES_LANE:ONES_LANE + 1]

    def scores(q, k):
        half = q.shape[0] // 2
        return jnp.concatenate([lax.dot_general(q[:half], k, NT_DIMS, preferred_element_type=F32),
                                lax.dot_general(q[half:], k, NT_DIMS, preferred_element_type=F32)], axis=0)

    def weighted_values(p, v):
        half = p.shape[0] // 2
        p = p.astype(BF16)
        return jnp.concatenate([jnp.dot(p[:half], v, preferred_element_type=F32),
                                jnp.dot(p[half:], v, preferred_element_type=F32)], axis=0)

    def query_rows(g):
        sel = sel_ref[0, g].astype(F32)
        rows = []
        for r in range(NSA_R):
            hh = g * NSA_R + r
            qh = q_ref[0, :, (hh // 2) * LANES:(hh // 2 + 1) * LANES].astype(F32)
            if hh % 2:
                qh = jnp.concatenate([qh[:, NSA_DK:], qh[:, :NSA_DK]], axis=1)
            slope = 2.0 ** -(hh + 1)
            extra = jnp.where(lane == POS_HI_LANE, SEL_BLOCK * slope,
                              jnp.where(lane == POS_LO_LANE, slope, jnp.where(lane == PAD_LANE, -1.0, sel)))
            rows.append(jnp.where(is_head, qh, extra).astype(BF16))
        return jnp.concatenate(rows, axis=0)

    def tile(n_past):
        past = n_past * KC
        for g in range(NSA_GROUPS):
            q = query_rows(g)
            s = masked_rows(scores(q, kwx[g, pl.ds(t0, WL), :]), wmask)
            p = jnp.exp(s - jnp.max(s, axis=-1, keepdims=True))
            o_win = normalise(weighted_values(p, vwx[g, pl.ds(t0, WL), :]))

            s_own = masked_rows(scores(q, ksx[g, past:past + KC, :]), dmask)
            m = jnp.max(s_own, axis=-1, keepdims=True)
            if n_past:
                s_past = scores(q, ksx[g, 0:past, :])
                m = jnp.maximum(m, jnp.max(s_past, axis=-1, keepdims=True))
            acc = weighted_values(jnp.exp(s_own - m), vsx[g, past:past + KC, :])
            if n_past:
                acc = acc + weighted_values(jnp.exp(s_past - m), vsx[g, 0:past, :])
            o_sel = normalise(acc)

            for r in range(NSA_R):
                hh = g * NSA_R + r
                g_sel = gate[:, NSA_HEADS + hh:NSA_HEADS + hh + 1]
                g_win = gate[:, 2 * NSA_HEADS + hh:2 * NSA_HEADS + hh + 1]
                piece = g_sel * o_sel[r * tq:(r + 1) * tq] + g_win * o_win[r * tq:(r + 1) * tq]
                o_ref[0, :, hh * NSA_DK:(hh + 1) * NSA_DK] = (
                    ocmp_ref[0, :, hh * NSA_DK:(hh + 1) * NSA_DK] + piece).astype(BF16)

    for n_past in range(S // KC):
        pl.when(diag_chunk == n_past)(functools.partial(tile, n_past))


def _selwin(pb3, sel, pf3, ocmp, tq, KC):
    B, S, _ = pb3.shape
    HD = NSA_HEADS * NSA_DK
    WL = WINDOW + tq
    tr = jnp.arange(tq)[:, None]
    kc = jnp.arange(WL)[None, :]
    wmask = jnp.where((kc > tr) & (kc <= tr + WINDOW), 0.0, NEG).astype(F32)
    off = (jnp.arange(KC // tq) * tq)[:, None, None]
    dmask = jnp.where(jnp.arange(KC)[None, None, :] <= off + tr[None], 0.0, NEG).astype(F32)
    kv_spec = lambda off: pl.BlockSpec((1, S, LANES), lambda b, i: (b, 0, off // LANES))
    return pl.pallas_call(
        functools.partial(_selwin_kernel, tq=tq, KC=KC, S=S),
        grid=(B, S // tq),
        in_specs=[pl.BlockSpec((1, tq, HD), lambda b, i: (b, i, PB_NQ // HD)),
                  kv_spec(PB_KS), kv_spec(PB_VS), kv_spec(PB_KW), kv_spec(PB_VW),
                  pl.BlockSpec((1, NSA_GROUPS, tq, LANES), lambda b, i: (b, 0, i, 0)),
                  pl.BlockSpec((tq, WL), lambda b, i: (0, 0)),
                  pl.BlockSpec((KC // tq, tq, KC), lambda b, i: (0, 0, 0)),
                  pl.BlockSpec((1, tq, LANES), lambda b, i: (b, i, PF_NG // LANES)),
                  pl.BlockSpec((1, tq, HD), lambda b, i: (b, i, 0))],
        out_specs=pl.BlockSpec((1, tq, HD), lambda b, i: (b, i, 0)),
        out_shape=jax.ShapeDtypeStruct((B, S, HD), BF16),
        scratch_shapes=[pltpu.VMEM((NSA_GROUPS, S, LANES), BF16), pltpu.VMEM((NSA_GROUPS, S, LANES), BF16),
                        pltpu.VMEM((NSA_GROUPS, WINDOW + S, LANES), BF16),
                        pltpu.VMEM((NSA_GROUPS, WINDOW + S, LANES), BF16)],
        compiler_params=_params("arbitrary", "arbitrary"),
        name="selwin",
    )(pb3, pb3, pb3, pb3, pb3, sel, wmask, dmask, pf3, ocmp)


SUBLANES = 8


def _mix_ffn_kernel(x_ref, ya_ref, yb_ref, ga_ref, gb_ref, wa_ref, wb_ref, wo_ref, gn_ref,
                    wup_ref, cw_ref, cb_ref, wd_ref, gf_ref, o_ref, tail_ref, u_ref, *, tm, tf):
    @pl.when(pl.program_id(1) == 0)
    def _():
        tail_ref[...] = jnp.zeros_like(tail_ref)

    y_a = jnp.dot(ya_ref[0], wa_ref[...], preferred_element_type=F32)
    y_b = jnp.dot(yb_ref[0], wb_ref[...], preferred_element_type=F32)
    merged = (jax.nn.sigmoid(ga_ref[0].astype(F32)) * y_a + jax.nn.sigmoid(gb_ref[0].astype(F32)) * y_b)
    x1 = x_ref[0] + jnp.dot(merged.astype(BF16), wo_ref[...], preferred_element_type=F32)
    o_ref[0] = x1
    h = _rms(x1, gn_ref[...]).astype(BF16)

    top = lax.broadcasted_iota(jnp.int32, (SUBLANES, tf), 0)
    for c in range(D_FF // tf):
        cols = slice(c * tf, (c + 1) * tf)
        a = jnp.dot(h, wup_ref[:, cols], preferred_element_type=F32)
        b = jnp.dot(h, wup_ref[:, D_FF + c * tf:D_FF + (c + 1) * tf], preferred_element_type=F32)
        tail = tail_ref[c]
        tail_ref[c] = a[tm - SUBLANES:]
        ac = cb_ref[:, cols] + cw_ref[CONV_W - 1:CONV_W, cols] * a
        for d in range(1, CONV_W):
            sh = pltpu.roll(a, d, axis=0)
            head = jnp.where(top < d, pltpu.roll(tail, d, axis=0), sh[:SUBLANES])
            sh = jnp.concatenate([head, sh[SUBLANES:]], axis=0)
            ac = ac + cw_ref[CONV_W - 1 - d:CONV_W - d, cols] * sh
        u_ref[:, cols] = (jax.nn.gelu(ac) * b).astype(BF16)
    y = jnp.dot(u_ref[...], wd_ref[...], preferred_element_type=F32)
    o_ref[0] = _rms(o_ref[0] + y, gf_ref[...])


def _mix_ffn(x, ya, yb, pb3, wa, wb, wo, gn, w_up, conv_w, conv_b, w_down, gf, tm, tf):
    B, S, _ = x.shape
    HV = RET_HEADS * RET_DV
    HD = NSA_HEADS * NSA_DK
    rows = lambda width, col=0: pl.BlockSpec((1, tm, width), lambda b, i: (b, i, col))
    whole = lambda *shape: pl.BlockSpec(shape, lambda b, i: (0,) * len(shape), pipeline_mode=pl.Buffered(1))
    return pl.pallas_call(
        functools.partial(_mix_ffn_kernel, tm=tm, tf=tf),
        grid=(B, S // tm),
        in_specs=[rows(D_MODEL), rows(HV), rows(HD),
                  rows(D_MODEL, PB_GA // D_MODEL), rows(D_MODEL, PB_GB // D_MODEL),
                  whole(HV, D_MODEL), whole(HD, D_MODEL), whole(D_MODEL, D_MODEL), whole(1, D_MODEL),
                  whole(D_MODEL, 2 * D_FF), whole(CONV_W, D_FF), whole(1, D_FF), whole(D_FF, D_MODEL),
                  whole(1, D_MODEL)],
        out_specs=rows(D_MODEL),
        out_shape=jax.ShapeDtypeStruct((B, S, D_MODEL), F32),
        scratch_shapes=[pltpu.VMEM((D_FF // tf, SUBLANES, tf), F32), pltpu.VMEM((tm, D_FF), BF16)],
        compiler_params=_params("arbitrary", "arbitrary"),
        name="mix_ffn",
    )(x, ya, yb, pb3, pb3, wa, wb, wo, gn, w_up, conv_w, conv_b, w_down, gf)


def _block_diag_groups(w):
    z = jnp.zeros_like(w)
    return jnp.concatenate([jnp.concatenate([w, z], axis=-1), jnp.concatenate([z, w], axis=-1)], axis=-2)


def kernel(x, norm_mix, w_in, ret_gn_g, cmp_pos_k, cmp_w1_k, cmp_b1_k, cmp_w2_k, cmp_pos_v, cmp_w1_v,
           cmp_b1_v, cmp_w2_v, w_ret_o, w_nsa_o, w_out, norm_ffn, w_up, conv_w, conv_b, w_down,
           norm_final):
    B, S, D = x.shape
    assert D == D_MODEL and NSA_GROUPS == 2 and norm_mix.shape[0] == 1
    assert S % 512 == 0 and S >= 1024 and S // CMP_STRIDE <= LANES and S // SEL_BLOCK <= SEL_LANES
    M = B * S
    x2 = x.reshape(M, D)

    off = np.concatenate([[0], np.cumsum(IN_SIZES)])
    w = w_in[0].astype(BF16)
    seg = lambda i: w[:, off[i]:off[i + 1]]
    wp = jnp.concatenate(
        [seg(0), seg(1), seg(2), seg(3),
         seg(4) * (NSA_DK ** -0.5), seg(7), seg(8), seg(9), seg(10), seg(12), seg(13),
         seg(5), seg(6), seg(11), jnp.zeros((D, PF_W - PF_NG - IN_SIZES[11]), BF16)], axis=1)

    def cmp_weights(pos, w1, b1, w2):
        return (jnp.tile(pos, (1, NSA_GROUPS)),
                _block_diag_groups(w1.reshape(CMP_BLOCK, NSA_DK, CMP_HIDDEN)).astype(BF16),
                jnp.tile(b1, NSA_GROUPS)[None, :],
                _block_diag_groups(w2).astype(BF16))

    posk, w1k, b1k, w2k = cmp_weights(cmp_pos_k[0], cmp_w1_k[0], cmp_b1_k[0], cmp_w2_k[0])
    posv, w1v, b1v, w2v = cmp_weights(cmp_pos_v[0], cmp_w1_v[0], cmp_b1_v[0], cmp_w2_v[0])

    pb, pf, ya = _inproj_ret(x2, norm_mix, wp, ret_gn_g, S, tm=512, C=256)
    pb3 = pb.reshape(B, S, PB_W)
    pf3 = pf.reshape(B, S, PF_W)
    ya = ya.reshape(B, S, -1)

    kcmp, vcmp = _compress(pf3, posk, posv, w1k, w1v, b1k, b1v, w2k, w2v)
    ocmp, sel = _cmpattn(pb3, kcmp, vcmp, pf3, tq=1024)
    yb = _selwin(pb3, sel, pf3, ocmp, tq=256, KC=512)

    return _mix_ffn(x, ya, yb, pb3, w_ret_o[0].astype(BF16), w_nsa_o[0].astype(BF16),
                    w_out[0].astype(BF16), norm_ffn, w_up[0].astype(BF16), conv_w[0], conv_b,
                    w_down[0].astype(BF16), norm_final[None, :], tm=512, tf=256)
```

```python
import functools
import math

import numpy as np
import jax
import jax.numpy as jnp
from jax import lax
from jax.experimental import pallas as pl
from jax.experimental.pallas import tpu as pltpu

F32 = jnp.float32
BF16 = jnp.bfloat16

D_MODEL = 1024
RET_HEADS = 4
RET_DK = 128
RET_DV = 256
NSA_HEADS = 8
NSA_GROUPS = 2
NSA_R = NSA_HEADS // NSA_GROUPS
NSA_DK = 64
CMP_BLOCK = 32
CMP_STRIDE = 16
CMP_HIDDEN = 256
SEL_BLOCK = 64
SEL_TOPK = 8
WINDOW = 512
D_FF = 2816
CONV_W = 3
EPS = 1e-6
NEG = -1e30
FORCE = 1e9
LOWEST = -3e38

IN_SIZES = (RET_HEADS * RET_DK, RET_HEADS * RET_DK, RET_HEADS * RET_DV, RET_HEADS * RET_DV,
            NSA_HEADS * NSA_DK,
            NSA_GROUPS * NSA_DK, NSA_GROUPS * NSA_DK, NSA_GROUPS * NSA_DK,
            NSA_GROUPS * NSA_DK, NSA_GROUPS * NSA_DK, NSA_GROUPS * NSA_DK,
            3 * NSA_HEADS, D_MODEL, D_MODEL)

LANES = 128
VMEM_LIMIT = 56 * 1024 * 1024

RT_RQ, RT_RK, RT_RV, RT_RG = 0, 512, 1024, 2048
RT_W = 3072
PB_NQ, PB_KS, PB_VS, PB_KW, PB_VW, PB_GA, PB_GB = 0, 512, 640, 768, 896, 1024, 2048
PB_W = 3072
PF_KCR, PF_VCR, PF_NG = 0, 128, 256
PF_W = 384

NT_DIMS = (((1,), (1,)), ((), ()))
TN_DIMS = (((0,), (0,)), ((), ()))


def _params(*sem):
    return pltpu.CompilerParams(dimension_semantics=sem, vmem_limit_bytes=VMEM_LIMIT)


def _rms(x, g):
    return x * lax.rsqrt(jnp.mean(x * x, axis=-1, keepdims=True) + EPS) * g


def _split3(x):
    hi = x.astype(BF16)
    rest = x - hi.astype(F32)
    mid = rest.astype(BF16)
    return hi, mid, (rest - mid.astype(F32)).astype(BF16)


def _inproj_ret_kernel(x_ref, g_ref, w_ref, gn_ref, pb_ref, pf_ref, ya_ref,
                       stash0, stash1, r_ref, dmask_ref, xi_ref, zeta_ref, *, tm, C, S):
    log_gamma = [float(np.log1p(-np.exp2(np.float32(-5.0 - h)))) for h in range(RET_HEADS)]
    scale = RET_DK ** -0.5
    i = pl.program_id(0)

    @pl.when(i == 0)
    def _():
        diff = (lax.broadcasted_iota(jnp.int32, (C, C), 0)
                - lax.broadcasted_iota(jnp.int32, (C, C), 1)).astype(F32)
        pos_v = lax.broadcasted_iota(jnp.int32, (C, RET_DV), 0).astype(F32)
        pos_k = lax.broadcasted_iota(jnp.int32, (C, RET_DK), 0).astype(F32)
        for h in range(RET_HEADS):
            lg = log_gamma[h]
            dmask_ref[h] = jnp.where(diff >= 0, jnp.exp(lg * jnp.maximum(diff, 0.0)), 0.0) * scale
            xi_ref[h] = jnp.exp(lg * (pos_v + 1.0))
            zeta_ref[h] = jnp.exp(lg * (C - 1.0 - pos_k)) * scale
        stash1[...] = jnp.zeros_like(stash1)
        r_ref[...] = jnp.zeros_like(r_ref)

    seq_start = lax.rem(i + (S // tm) - 1, S // tm) == 0

    def step(new_ref, old_ref):
        hb = _rms(x_ref[...], g_ref[...]).astype(BF16)
        for c0 in range(0, RT_W, 512):
            new_ref[:, c0:c0 + 512] = jnp.dot(
                hb, w_ref[:, c0:c0 + 512], preferred_element_type=F32).astype(BF16)
        for c0 in range(0, PB_W, 512):
            pb_ref[:, c0:c0 + 512] = jnp.dot(
                hb, w_ref[:, RT_W + c0:RT_W + c0 + 512], preferred_element_type=F32).astype(BF16)
        pf_ref[...] = jnp.dot(hb, w_ref[:, RT_W + PB_W:], preferred_element_type=F32)

        for cc in range(tm // C):
            rows = slice(cc * C, (cc + 1) * C)
            for h in range(RET_HEADS):
                lg = log_gamma[h]
                q = old_ref[rows, RT_RQ + h * RET_DK:RT_RQ + (h + 1) * RET_DK]
                k = old_ref[rows, RT_RK + h * RET_DK:RT_RK + (h + 1) * RET_DK]
                v = old_ref[rows, RT_RV + h * RET_DV:RT_RV + (h + 1) * RET_DV]
                g = old_ref[rows, RT_RG + h * RET_DV:RT_RG + (h + 1) * RET_DV].astype(F32)
                s = lax.dot_general(q, k, NT_DIMS, preferred_element_type=F32)
                o = jnp.dot((s * dmask_ref[h]).astype(BF16), v, preferred_element_type=F32)
                state = r_ref[h]
                if cc == 0:
                    state = jnp.where(seq_start, 0.0, state)
                o = o + xi_ref[h] * jnp.dot(q, state.astype(BF16), preferred_element_type=F32)
                kz = (k.astype(F32) * zeta_ref[h]).astype(BF16)
                kv = lax.dot_general(kz, v, TN_DIMS, preferred_element_type=F32)
                r_ref[h] = math.exp(lg * C) * state + kv
                mu = jnp.mean(o, axis=-1, keepdims=True)
                d = o - mu
                var = jnp.mean(d * d, axis=-1, keepdims=True)
                on = d * lax.rsqrt(var + EPS) * gn_ref[:, h * RET_DV:(h + 1) * RET_DV]
                ya_ref[rows, h * RET_DV:(h + 1) * RET_DV] = (g * jax.nn.sigmoid(g) * on).astype(BF16)

    pl.when(lax.rem(i, 2) == 0)(functools.partial(step, stash0, stash1))
    pl.when(lax.rem(i, 2) == 1)(functools.partial(step, stash1, stash0))


def _inproj_ret(x2, g, wp, gn_g, S, tm, C):
    M = x2.shape[0]
    nt = M // tm
    HV = RET_HEADS * RET_DV
    cur = lambda i: (jnp.minimum(i, nt - 1), 0)
    prev = lambda i: (jnp.maximum(i - 1, 0), 0)
    const = lambda i: (0, 0)
    return pl.pallas_call(
        functools.partial(_inproj_ret_kernel, tm=tm, C=C, S=S),
        grid=(nt + 1,),
        in_specs=[pl.BlockSpec((tm, D_MODEL), cur),
                  pl.BlockSpec((1, D_MODEL), const),
                  pl.BlockSpec((D_MODEL, RT_W + PB_W + PF_W), const, pipeline_mode=pl.Buffered(1)),
                  pl.BlockSpec((1, HV), const)],
        out_specs=[pl.BlockSpec((tm, PB_W), cur), pl.BlockSpec((tm, PF_W), cur), pl.BlockSpec((tm, HV), prev)],
        out_shape=[jax.ShapeDtypeStruct((M, PB_W), BF16), jax.ShapeDtypeStruct((M, PF_W), F32),
                   jax.ShapeDtypeStruct((M, HV), BF16)],
        scratch_shapes=[pltpu.VMEM((tm, RT_W), BF16), pltpu.VMEM((tm, RT_W), BF16),
                        pltpu.VMEM((RET_HEADS, RET_DK, RET_DV), F32), pltpu.VMEM((RET_HEADS, C, C), F32),
                        pltpu.VMEM((RET_HEADS, C, RET_DV), F32), pltpu.VMEM((RET_HEADS, C, RET_DK), F32)],
        compiler_params=_params("arbitrary"),
        name="inproj_ret",
    )(x2, g, wp, gn_g)


def _compress_kernel(kin_ref, vin_ref, posk_ref, posv_ref, w1k_ref, w1v_ref, b1k_ref, b1v_ref,
                     w2k_ref, w2v_ref, ko_ref, vo_ref, *, NP):
    GD = NSA_GROUPS * NSA_DK
    npc = CMP_BLOCK // CMP_STRIDE
    row = lax.broadcasted_iota(jnp.int32, (NP, GD), 0)
    for in_ref, pos_ref, w1_ref, b1_ref, w2_ref, out_ref in (
            (kin_ref, posk_ref, w1k_ref, b1k_ref, w2k_ref, ko_ref),
            (vin_ref, posv_ref, w1v_ref, b1v_ref, w2v_ref, vo_ref)):
        parts = []
        for p in range(npc):
            acc = jnp.zeros((NP, NSA_GROUPS * CMP_HIDDEN), F32)
            for i in range(CMP_STRIDE):
                tok = in_ref[0, pl.ds(i, NP, stride=CMP_STRIDE), :]
                j = p * CMP_STRIDE + i
                acc = acc + jnp.dot((tok + pos_ref[j:j + 1, :]).astype(BF16), w1_ref[j],
                                    preferred_element_type=F32)
            parts.append(acc)
        hidden = parts[0]
        for p in range(1, npc):
            hidden = hidden + pltpu.roll(parts[p], NP - p, axis=0)
        act = jax.nn.gelu(hidden + b1_ref[...]).astype(BF16)
        res = jnp.dot(act, w2_ref[...], preferred_element_type=F32)
        res = jnp.where(row < NP - (npc - 1), res, 0.0)
        if out_ref is vo_ref:
            res = res.T
            for g in range(NSA_GROUPS):
                out_ref[0, g] = res[g * NSA_DK:(g + 1) * NSA_DK, :]
        else:
            for g in range(NSA_GROUPS):
                out_ref[0, g] = res[:, g * NSA_DK:(g + 1) * NSA_DK]


def _compress(pf3, posk, posv, w1k, w1v, b1k, b1v, w2k, w2v):
    B, S, _ = pf3.shape
    NP = S // CMP_STRIDE
    GD = NSA_GROUPS * NSA_DK
    GH = NSA_GROUPS * CMP_HIDDEN
    const2 = lambda b: (0, 0)
    const3 = lambda b: (0, 0, 0)
    k_shape = (NSA_GROUPS, NP, NSA_DK)
    v_shape = (NSA_GROUPS, NSA_DK, NP)
    return pl.pallas_call(
        functools.partial(_compress_kernel, NP=NP),
        grid=(B,),
        in_specs=[pl.BlockSpec((1, S, GD), lambda b: (b, 0, PF_KCR // GD)),
                  pl.BlockSpec((1, S, GD), lambda b: (b, 0, PF_VCR // GD)),
                  pl.BlockSpec((CMP_BLOCK, GD), const2), pl.BlockSpec((CMP_BLOCK, GD), const2),
                  pl.BlockSpec((CMP_BLOCK, GD, GH), const3), pl.BlockSpec((CMP_BLOCK, GD, GH), const3),
                  pl.BlockSpec((1, GH), const2), pl.BlockSpec((1, GH), const2),
                  pl.BlockSpec((GH, GD), const2), pl.BlockSpec((GH, GD), const2)],
        out_specs=[pl.BlockSpec((1,) + k_shape, lambda b: (b, 0, 0, 0)),
                   pl.BlockSpec((1,) + v_shape, lambda b: (b, 0, 0, 0))],
        out_shape=[jax.ShapeDtypeStruct((B,) + k_shape, F32), jax.ShapeDtypeStruct((B,) + v_shape, F32)],
        compiler_params=_params("arbitrary"),
        name="compress",
    )(pf3, pf3, posk, posv, w1k, w1v, b1k, b1v, w2k, w2v)


SEL_LANE0 = 64
SEL_LANES = 32
POS_HI_LANE, POS_LO_LANE, PAD_LANE = 96, 97, 98
ONES_LANE = 64
MASK_BIG = 1e30


def _cmpattn_kernel(q_ref, kc_ref, vct_ref, gate_ref, ocmp_ref, sel_ref, *, tq, NP, NS):
    NC = NP - (CMP_BLOCK // CMP_STRIDE - 1)
    tile = pl.program_id(1)

    def run(rows):
        _cmpattn_tile(q_ref, kc_ref, vct_ref, gate_ref, ocmp_ref, sel_ref, tile * tq,
                      tq=tq, NP=rows, NC=NC, NS=NS)

    for v in range(pl.cdiv(NP * CMP_STRIDE, tq)):
        pl.when(tile == v)(functools.partial(run, min(NP, (v + 1) * tq // CMP_STRIDE)))


def _cmpattn_tile(q_ref, kc_ref, vct_ref, gate_ref, ocmp_ref, sel_ref, t0, *, tq, NP, NC, NS):
    tcol = t0 + lax.broadcasted_iota(jnp.int32, (NP, tq), 1)
    cidx = lax.broadcasted_iota(jnp.int32, (NP, tq), 0)
    visible = (tcol >= cidx * CMP_STRIDE + (CMP_BLOCK - 1)) & (cidx < NC)
    mask_add = jnp.where(visible, 0.0, NEG)
    block_end = (lax.broadcasted_iota(jnp.int32, (NP, 1), 0) * CMP_STRIDE + (CMP_BLOCK - 1)).astype(F32)
    sees_any = ((t0 + lax.broadcasted_iota(jnp.int32, (1, tq), 1)) >= CMP_BLOCK - 1).astype(F32)
    jj = lax.broadcasted_iota(jnp.int32, (SEL_LANES, NP), 0)
    cc = lax.broadcasted_iota(jnp.int32, (SEL_LANES, NP), 1)
    overlap_t = ((cc * CMP_STRIDE < jj * SEL_BLOCK + SEL_BLOCK)
                 & (cc * CMP_STRIDE + CMP_BLOCK > jj * SEL_BLOCK) & (jj < NS)).astype(BF16)
    blk = lax.broadcasted_iota(jnp.int32, (SEL_LANES, tq), 0)
    blk_f = blk.astype(F32)
    is_blk = blk < NS
    cur = jnp.right_shift(t0 + lax.broadcasted_iota(jnp.int32, (SEL_LANES, tq), 1),
                          SEL_BLOCK.bit_length() - 1)
    o_t = []
    for g in range(NSA_GROUPS):
        kc_parts = _split3(kc_ref[0, g, :NP, :])
        vct = vct_ref[0, g, :, :NP].astype(BF16)
        psum = jnp.zeros((NP, tq), F32)
        for r in range(NSA_R):
            hh = g * NSA_R + r
            q = q_ref[0, :, hh * NSA_DK:(hh + 1) * NSA_DK]
            s = sum(lax.dot_general(part, q, NT_DIMS, preferred_element_type=F32) for part in kc_parts)
            s = s + (2.0 ** -(hh + 1)) * block_end + mask_add
            e = jnp.exp(s - jnp.max(s, axis=0, keepdims=True))
            p = e * (sees_any / jnp.sum(e, axis=0, keepdims=True))
            o_t.append(jnp.dot(vct, p.astype(BF16), preferred_element_type=F32))
            psum = psum + p
        imp = sum(jnp.dot(overlap_t, part, preferred_element_type=F32) for part in _split3(psum))
        imp = jnp.where((blk == 0) | (blk == cur) | (blk == cur - 1), FORCE, imp)
        imp = jnp.where(blk > cur, -FORCE, imp)
        work = jnp.where(is_blk, imp, LOWEST)
        sel = jnp.zeros((SEL_LANES, tq), F32)
        for _ in range(min(SEL_TOPK, NS)):
            top = jnp.max(work, axis=0, keepdims=True)
            first = jnp.min(jnp.where(work == top, blk_f, float(SEL_LANES)), axis=0, keepdims=True)
            pick = blk_f == first
            sel = jnp.where(pick, 1.0, sel)
            work = jnp.where(pick, LOWEST, work)
        sel = jnp.where(is_blk & (blk <= cur), sel - 1.0, jnp.where(is_blk, -1.0, 0.0))
        placed = jnp.concatenate([jnp.zeros((SEL_LANE0, tq), F32), sel,
                                  jnp.zeros((LANES - SEL_LANE0 - SEL_LANES, tq), F32)], axis=0)
        sel_ref[0, g] = placed.T.astype(BF16)
    o = jnp.concatenate(o_t, axis=0).T
    gate = jax.nn.sigmoid(gate_ref[0])
    for hh in range(NSA_HEADS):
        cols = slice(hh * NSA_DK, (hh + 1) * NSA_DK)
        ocmp_ref[0, :, cols] = gate[:, hh:hh + 1] * o[:, cols]


def _cmpattn(pb3, kcmp, vcmp, pf3, tq):
    B, S, _ = pb3.shape
    NP = S // CMP_STRIDE
    NS = S // SEL_BLOCK
    HD = NSA_HEADS * NSA_DK
    return pl.pallas_call(
        functools.partial(_cmpattn_kernel, tq=tq, NP=NP, NS=NS),
        grid=(B, S // tq),
        in_specs=[pl.BlockSpec((1, tq, HD), lambda b, i: (b, i, PB_NQ // HD)),
                  pl.BlockSpec((1, NSA_GROUPS, NP, NSA_DK), lambda b, i: (b, 0, 0, 0)),
                  pl.BlockSpec((1, NSA_GROUPS, NSA_DK, NP), lambda b, i: (b, 0, 0, 0)),
                  pl.BlockSpec((1, tq, LANES), lambda b, i: (b, i, PF_NG // LANES))],
        out_specs=[pl.BlockSpec((1, tq, HD), lambda b, i: (b, i, 0)),
                   pl.BlockSpec((1, NSA_GROUPS, tq, LANES), lambda b, i: (b, 0, i, 0))],
        out_shape=[jax.ShapeDtypeStruct((B, S, HD), F32),
                   jax.ShapeDtypeStruct((B, NSA_GROUPS, S, LANES), BF16)],
        compiler_params=_params("arbitrary", "arbitrary"),
        name="cmpattn",
    )(pb3, kcmp, vcmp, pf3)


def _selwin_kernel(q_ref, ks_ref, vs_ref, kw_ref, vw_ref, sel_ref, wmask_ref, dmask_ref, gate_ref, ocmp_ref,
                   o_ref, ksx, vsx, kwx, vwx, *, tq, KC, S):
    WL = WINDOW + tq
    RB = 512
    i = pl.program_id(1)
    t0 = pl.multiple_of(i * tq, tq)
    sel_shift = SEL_BLOCK.bit_length() - 1

    @pl.when(i == 0)
    def _build():
        lane = lax.broadcasted_iota(jnp.int32, (RB, LANES), 1)
        is_head = lane < NSA_DK
        pad_k = jnp.where(lane == PAD_LANE, MASK_BIG, 0.0).astype(BF16)
        for g in range(NSA_GROUPS):
            kwx[g, 0:WINDOW, :] = pad_k[:WINDOW]
            vwx[g, 0:WINDOW, :] = jnp.zeros((WINDOW, LANES), BF16)
        for piece in range(S // RB):
            r0 = piece * RB
            pos = r0 + lax.broadcasted_iota(jnp.int32, (RB, LANES), 0)
            alibi = jnp.where(lane == POS_HI_LANE, jnp.right_shift(pos, sel_shift),
                              jnp.where(lane == POS_LO_LANE, pos & (SEL_BLOCK - 1), 0)).astype(F32)
            onehot = jnp.where(lane - SEL_LANE0 == jnp.right_shift(pos, sel_shift), MASK_BIG, 0.0)
            k_extra_win = alibi.astype(BF16)
            k_extra_sel = (alibi + onehot).astype(BF16)
            v_extra = jnp.where(lane == ONES_LANE, 1.0, 0.0).astype(BF16)
            for g in range(NSA_GROUPS):
                def head_lanes(ref):
                    t = ref[0, r0:r0 + RB, :]
                    return t if g == 0 else jnp.concatenate([t[:, NSA_DK:], t[:, :NSA_DK]], axis=1)
                ksx[g, r0:r0 + RB, :] = jnp.where(is_head, head_lanes(ks_ref), k_extra_sel)
                vsx[g, r0:r0 + RB, :] = jnp.where(is_head, head_lanes(vs_ref), v_extra)
                kwx[g, WINDOW + r0:WINDOW + r0 + RB, :] = jnp.where(is_head, head_lanes(kw_ref), k_extra_win)
                vwx[g, WINDOW + r0:WINDOW + r0 + RB, :] = jnp.where(is_head, head_lanes(vw_ref), v_extra)

    gate = jax.nn.sigmoid(gate_ref[0])
    lane = lax.broadcasted_iota(jnp.int32, (tq, LANES), 1)
    is_head = lane < NSA_DK
    diag_chunk = t0 // KC
    wmask = wmask_ref[...][None]
    dmask = dmask_ref[i % (KC // tq)][None]

    def masked_rows(s, mask):
        n = s.shape[-1]
        return (s.reshape(NSA_R, tq, n) + mask).reshape(NSA_R * tq, n)

    def normalise(acc):
        return acc[:, :NSA_DK] / acc[:, ONES_LANE:ONES_LANE + 1]

    def scores(q, k):
        half = q.shape[0] // 2
        return jnp.concatenate([lax.dot_general(q[:half], k, NT_DIMS, preferred_element_type=F32),
                                lax.dot_general(q[half:], k, NT_DIMS, preferred_element_type=F32)], axis=0)

    def weighted_values(p, v):
        half = p.shape[0] // 2
        p = p.astype(BF16)
        return jnp.concatenate([jnp.dot(p[:half], v, preferred_element_type=F32),
                                jnp.dot(p[half:], v, preferred_element_type=F32)], axis=0)

    def query_rows(g):
        sel = sel_ref[0, g].astype(F32)
        rows = []
        for r in range(NSA_R):
            hh = g * NSA_R + r
            qh = q_ref[0, :, (hh // 2) * LANES:(hh // 2 + 1) * LANES].astype(F32)
            if hh % 2:
                qh = jnp.concatenate([qh[:, NSA_DK:], qh[:, :NSA_DK]], axis=1)
            slope = 2.0 ** -(hh + 1)
            extra = jnp.where(lane == POS_HI_LANE, SEL_BLOCK * slope,
                              jnp.where(lane == POS_LO_LANE, slope, jnp.where(lane == PAD_LANE, -1.0, sel)))
            rows.append(jnp.where(is_head, qh, extra).astype(BF16))
        return jnp.concatenate(rows, axis=0)

    def tile(n_past):
        past = n_past * KC
        for g in range(NSA_GROUPS):
            q = query_rows(g)
            s = masked_rows(scores(q, kwx[g, pl.ds(t0, WL), :]), wmask)
            p = jnp.exp(s - jnp.max(s, axis=-1, keepdims=True))
            o_win = normalise(weighted_values(p, vwx[g, pl.ds(t0, WL), :]))

            s_own = masked_rows(scores(q, ksx[g, past:past + KC, :]), dmask)
            m = jnp.max(s_own, axis=-1, keepdims=True)
            if n_past:
                s_past = scores(q, ksx[g, 0:past, :])
                m = jnp.maximum(m, jnp.max(s_past, axis=-1, keepdims=True))
            acc = weighted_values(jnp.exp(s_own - m), vsx[g, past:past + KC, :])
            if n_past:
                acc = acc + weighted_values(jnp.exp(s_past - m), vsx[g, 0:past, :])
            o_sel = normalise(acc)

            for r in range(NSA_R):
                hh = g * NSA_R + r
                g_sel = gate[:, NSA_HEADS + hh:NSA_HEADS + hh + 1]
                g_win = gate[:, 2 * NSA_HEADS + hh:2 * NSA_HEADS + hh + 1]
                piece = g_sel * o_sel[r * tq:(r + 1) * tq] + g_win * o_win[r * tq:(r + 1) * tq]
                o_ref[0, :, hh * NSA_DK:(hh + 1) * NSA_DK] = (
                    ocmp_ref[0, :, hh * NSA_DK:(hh + 1) * NSA_DK] + piece).astype(BF16)

    for n_past in range(S // KC):
        pl.when(diag_chunk == n_past)(functools.partial(tile, n_past))


def _selwin(pb3, sel, pf3, ocmp, tq, KC):
    B, S, _ = pb3.shape
    HD = NSA_HEADS * NSA_DK
    WL = WINDOW + tq
    tr = jnp.arange(tq)[:, None]
    kc = jnp.arange(WL)[None, :]
    wmask = jnp.where((kc > tr) & (kc <= tr + WINDOW), 0.0, NEG).astype(F32)
    off = (jnp.arange(KC // tq) * tq)[:, None, None]
    dmask = jnp.where(jnp.arange(KC)[None, None, :] <= off + tr[None], 0.0, NEG).astype(F32)
    kv_spec = lambda off: pl.BlockSpec((1, S, LANES), lambda b, i: (b, 0, off // LANES))
    return pl.pallas_call(
        functools.partial(_selwin_kernel, tq=tq, KC=KC, S=S),
        grid=(B, S // tq),
        in_specs=[pl.BlockSpec((1, tq, HD), lambda b, i: (b, i, PB_NQ // HD)),
                  kv_spec(PB_KS), kv_spec(PB_VS), kv_spec(PB_KW), kv_spec(PB_VW),
                  pl.BlockSpec((1, NSA_GROUPS, tq, LANES), lambda b, i: (b, 0, i, 0)),
                  pl.BlockSpec((tq, WL), lambda b, i: (0, 0)),
                  pl.BlockSpec((KC // tq, tq, KC), lambda b, i: (0, 0, 0)),
                  pl.BlockSpec((1, tq, LANES), lambda b, i: (b, i, PF_NG // LANES)),
                  pl.BlockSpec((1, tq, HD), lambda b, i: (b, i, 0))],
        out_specs=pl.BlockSpec((1, tq, HD), lambda b, i: (b, i, 0)),
        out_shape=jax.ShapeDtypeStruct((B, S, HD), BF16),
        scratch_shapes=[pltpu.VMEM((NSA_GROUPS, S, LANES), BF16), pltpu.VMEM((NSA_GROUPS, S, LANES), BF16),
                        pltpu.VMEM((NSA_GROUPS, WINDOW + S, LANES), BF16),
                        pltpu.VMEM((NSA_GROUPS, WINDOW + S, LANES), BF16)],
        compiler_params=_params("arbitrary", "arbitrary"),
        name="selwin",
    )(pb3, pb3, pb3, pb3, pb3, sel, wmask, dmask, pf3, ocmp)


SUBLANES = 8


def _mix_ffn_kernel(x_ref, ya_ref, yb_ref, ga_ref, gb_ref, wa_ref, wb_ref, wo_ref, gn_ref,
                    wup_ref, cw_ref, cb_ref, wd_ref, gf_ref, o_ref, tail_ref, u_ref, *, tm, tf):
    @pl.when(pl.program_id(1) == 0)
    def _():
        tail_ref[...] = jnp.zeros_like(tail_ref)

    y_a = jnp.dot(ya_ref[0], wa_ref[...], preferred_element_type=F32)
    y_b = jnp.dot(yb_ref[0], wb_ref[...], preferred_element_type=F32)
    merged = (jax.nn.sigmoid(ga_ref[0].astype(F32)) * y_a + jax.nn.sigmoid(gb_ref[0].astype(F32)) * y_b)
    x1 = x_ref[0] + jnp.dot(merged.astype(BF16), wo_ref[...], preferred_element_type=F32)
    o_ref[0] = x1
    h = _rms(x1, gn_ref[...]).astype(BF16)

    top = lax.broadcasted_iota(jnp.int32, (SUBLANES, tf), 0)
    for c in range(D_FF // tf):
        cols = slice(c * tf, (c + 1) * tf)
        a = jnp.dot(h, wup_ref[:, cols], preferred_element_type=F32)
        b = jnp.dot(h, wup_ref[:, D_FF + c * tf:D_FF + (c + 1) * tf], preferred_element_type=F32)
        tail = tail_ref[c]
        tail_ref[c] = a[tm - SUBLANES:]
        ac = cb_ref[:, cols] + cw_ref[CONV_W - 1:CONV_W, cols] * a
        for d in range(1, CONV_W):
            sh = pltpu.roll(a, d, axis=0)
            head = jnp.where(top < d, pltpu.roll(tail, d, axis=0), sh[:SUBLANES])
            sh = jnp.concatenate([head, sh[SUBLANES:]], axis=0)
            ac = ac + cw_ref[CONV_W - 1 - d:CONV_W - d, cols] * sh
        u_ref[:, cols] = (jax.nn.gelu(ac) * b).astype(BF16)
    y = jnp.dot(u_ref[...], wd_ref[...], preferred_element_type=F32)
    o_ref[0] = _rms(o_ref[0] + y, gf_ref[...])


def _mix_ffn(x, ya, yb, pb3, wa, wb, wo, gn, w_up, conv_w, conv_b, w_down, gf, tm, tf):
    B, S, _ = x.shape
    HV = RET_HEADS * RET_DV
    HD = NSA_HEADS * NSA_DK
    rows = lambda width, col=0: pl.BlockSpec((1, tm, width), lambda b, i: (b, i, col))
    whole = lambda *shape: pl.BlockSpec(shape, lambda b, i: (0,) * len(shape), pipeline_mode=pl.Buffered(1))
    return pl.pallas_call(
        functools.partial(_mix_ffn_kernel, tm=tm, tf=tf),
        grid=(B, S // tm),
        in_specs=[rows(D_MODEL), rows(HV), rows(HD),
                  rows(D_MODEL, PB_GA // D_MODEL), rows(D_MODEL, PB_GB // D_MODEL),
                  whole(HV, D_MODEL), whole(HD, D_MODEL), whole(D_MODEL, D_MODEL), whole(1, D_MODEL),
                  whole(D_MODEL, 2 * D_FF), whole(CONV_W, D_FF), whole(1, D_FF), whole(D_FF, D_MODEL),
                  whole(1, D_MODEL)],
        out_specs=rows(D_MODEL),
        out_shape=jax.ShapeDtypeStruct((B, S, D_MODEL), F32),
        scratch_shapes=[pltpu.VMEM((D_FF // tf, SUBLANES, tf), F32), pltpu.VMEM((tm, D_FF), BF16)],
        compiler_params=_params("arbitrary", "arbitrary"),
        name="mix_ffn",
    )(x, ya, yb, pb3, pb3, wa, wb, wo, gn, w_up, conv_w, conv_b, w_down, gf)


def _block_diag_groups(w):
    z = jnp.zeros_like(w)
    return jnp.concatenate([jnp.concatenate([w, z], axis=-1), jnp.concatenate([z, w], axis=-1)], axis=-2)


def kernel(x, norm_mix, w_in, ret_gn_g, cmp_pos_k, cmp_w1_k, cmp_b1_k, cmp_w2_k, cmp_pos_v, cmp_w1_v,
           cmp_b1_v, cmp_w2_v, w_ret_o, w_nsa_o, w_out, norm_ffn, w_up, conv_w, conv_b, w_down,
           norm_final):
    B, S, D = x.shape
    assert D == D_MODEL and NSA_GROUPS == 2 and norm_mix.shape[0] == 1
    assert S % 512 == 0 and S >= 1024 and S // CMP_STRIDE <= LANES and S // SEL_BLOCK <= SEL_LANES
    M = B * S
    x2 = x.reshape(M, D)

    off = np.concatenate([[0], np.cumsum(IN_SIZES)])
    wt = jnp.transpose(w_in[0]).astype(BF16)
    seg = lambda i: wt[off[i]:off[i + 1], :]
    wp = jnp.transpose(jnp.concatenate(
        [seg(0), seg(1), seg(2), seg(3),
         seg(4) * (NSA_DK ** -0.5), seg(7), seg(8), seg(9), seg(10), seg(12), seg(13),
         seg(5), seg(6), seg(11), jnp.zeros((PF_W - PF_NG - IN_SIZES[11], D), BF16)], axis=0))

    def cmp_weights(pos, w1, b1, w2):
        return (jnp.tile(pos, (1, NSA_GROUPS)),
                _block_diag_groups(w1.reshape(CMP_BLOCK, NSA_DK, CMP_HIDDEN)).astype(BF16),
                jnp.tile(b1, NSA_GROUPS)[None, :],
                _block_diag_groups(w2).astype(BF16))

    posk, w1k, b1k, w2k = cmp_weights(cmp_pos_k[0], cmp_w1_k[0], cmp_b1_k[0], cmp_w2_k[0])
    posv, w1v, b1v, w2v = cmp_weights(cmp_pos_v[0], cmp_w1_v[0], cmp_b1_v[0], cmp_w2_v[0])

    pb, pf, ya = _inproj_ret(x2, norm_mix, wp, ret_gn_g, S, tm=512, C=256)
    pb3 = pb.reshape(B, S, PB_W)
    pf3 = pf.reshape(B, S, PF_W)
    ya = ya.reshape(B, S, -1)

    kcmp, vcmp = _compress(pf3, posk, posv, w1k, w1v, b1k, b1v, w2k, w2v)
    ocmp, sel = _cmpattn(pb3, kcmp, vcmp, pf3, tq=1024)
    yb = _selwin(pb3, sel, pf3, ocmp, tq=256, KC=512)

    return _mix_ffn(x, ya, yb, pb3, w_ret_o[0].astype(BF16), w_nsa_o[0].astype(BF16),
                    w_out[0].astype(BF16), norm_ffn, w_up[0].astype(BF16), conv_w[0], conv_b,
                    w_down[0].astype(BF16), norm_final[None, :], tm=512, tf=256)
```

```python
import functools
import math

import numpy as np
import jax
import jax.numpy as jnp
from jax import lax
from jax.experimental import pallas as pl
from jax.experimental.pallas import tpu as pltpu

F32 = jnp.float32
BF16 = jnp.bfloat16

D_MODEL = 1024
RET_HEADS = 4
RET_DK = 128
RET_DV = 256
NSA_HEADS = 8
NSA_GROUPS = 2
NSA_R = NSA_HEADS // NSA_GROUPS
NSA_DK = 64
CMP_BLOCK = 32
CMP_STRIDE = 16
CMP_HIDDEN = 256
SEL_BLOCK = 64
SEL_TOPK = 8
WINDOW = 512
D_FF = 2816
CONV_W = 3
EPS = 1e-6
NEG = -1e30
FORCE = 1e9
LOWEST = -3e38

IN_SIZES = (RET_HEADS * RET_DK, RET_HEADS * RET_DK, RET_HEADS * RET_DV, RET_HEADS * RET_DV,
            NSA_HEADS * NSA_DK,
            NSA_GROUPS * NSA_DK, NSA_GROUPS * NSA_DK, NSA_GROUPS * NSA_DK,
            NSA_GROUPS * NSA_DK, NSA_GROUPS * NSA_DK, NSA_GROUPS * NSA_DK,
            3 * NSA_HEADS, D_MODEL, D_MODEL)

LANES = 128
VMEM_LIMIT = 56 * 1024 * 1024

RT_RQ, RT_RK, RT_RV, RT_RG = 0, 512, 1024, 2048
RT_W = 3072
PB_NQ, PB_KS, PB_VS, PB_KW, PB_VW, PB_GA, PB_GB = 0, 512, 640, 768, 896, 1024, 2048
PB_W = 3072
PF_KCR, PF_VCR, PF_NG = 0, 128, 256
PF_W = 384

NT_DIMS = (((1,), (1,)), ((), ()))
TN_DIMS = (((0,), (0,)), ((), ()))


def _params(*sem):
    return pltpu.CompilerParams(dimension_semantics=sem, vmem_limit_bytes=VMEM_LIMIT)


def _rms(x, g):
    return x * lax.rsqrt(jnp.mean(x * x, axis=-1, keepdims=True) + EPS) * g


def _split3(x):
    hi = x.astype(BF16)
    rest = x - hi.astype(F32)
    mid = rest.astype(BF16)
    return hi, mid, (rest - mid.astype(F32)).astype(BF16)


def _inproj_ret_kernel(x_ref, g_ref, w_ref, gn_ref, pb_ref, pf_ref, ya_ref,
                       stash0, stash1, r_ref, dmask_ref, xi_ref, zeta_ref, *, tm, C, S):
    log_gamma = [float(np.log1p(-np.exp2(np.float32(-5.0 - h)))) for h in range(RET_HEADS)]
    scale = RET_DK ** -0.5
    i = pl.program_id(0)

    @pl.when(i == 0)
    def _():
        diff = (lax.broadcasted_iota(jnp.int32, (C, C), 0)
                - lax.broadcasted_iota(jnp.int32, (C, C), 1)).astype(F32)
        pos_v = lax.broadcasted_iota(jnp.int32, (C, RET_DV), 0).astype(F32)
        pos_k = lax.broadcasted_iota(jnp.int32, (C, RET_DK), 0).astype(F32)
        for h in range(RET_HEADS):
            lg = log_gamma[h]
            dmask_ref[h] = jnp.where(diff >= 0, jnp.exp(lg * jnp.maximum(diff, 0.0)), 0.0) * scale
            xi_ref[h] = jnp.exp(lg * (pos_v + 1.0))
            zeta_ref[h] = jnp.exp(lg * (C - 1.0 - pos_k)) * scale
        stash1[...] = jnp.zeros_like(stash1)
        r_ref[...] = jnp.zeros_like(r_ref)

    seq_start = lax.rem(i + (S // tm) - 1, S // tm) == 0

    def step(new_ref, old_ref):
        hb = _rms(x_ref[...], g_ref[...]).astype(BF16)
        for c0 in range(0, RT_W, 512):
            new_ref[:, c0:c0 + 512] = jnp.dot(
                hb, w_ref[:, c0:c0 + 512], preferred_element_type=F32).astype(BF16)
        for c0 in range(0, PB_W, 512):
            pb_ref[:, c0:c0 + 512] = jnp.dot(
                hb, w_ref[:, RT_W + c0:RT_W + c0 + 512], preferred_element_type=F32).astype(BF16)
        pf_ref[...] = jnp.dot(hb, w_ref[:, RT_W + PB_W:], preferred_element_type=F32)

        for cc in range(tm // C):
            rows = slice(cc * C, (cc + 1) * C)
            for h in range(RET_HEADS):
                lg = log_gamma[h]
                q = old_ref[rows, RT_RQ + h * RET_DK:RT_RQ + (h + 1) * RET_DK]
                k = old_ref[rows, RT_RK + h * RET_DK:RT_RK + (h + 1) * RET_DK]
                v = old_ref[rows, RT_RV + h * RET_DV:RT_RV + (h + 1) * RET_DV]
                g = old_ref[rows, RT_RG + h * RET_DV:RT_RG + (h + 1) * RET_DV].astype(F32)
                s = lax.dot_general(q, k, NT_DIMS, preferred_element_type=F32)
                o = jnp.dot((s * dmask_ref[h]).astype(BF16), v, preferred_element_type=F32)
                state = r_ref[h]
                if cc == 0:
                    state = jnp.where(seq_start, 0.0, state)
                o = o + xi_ref[h] * jnp.dot(q, state.astype(BF16), preferred_element_type=F32)
                kz = (k.astype(F32) * zeta_ref[h]).astype(BF16)
                kv = lax.dot_general(kz, v, TN_DIMS, preferred_element_type=F32)
                r_ref[h] = math.exp(lg * C) * state + kv
                mu = jnp.mean(o, axis=-1, keepdims=True)
                d = o - mu
                var = jnp.mean(d * d, axis=-1, keepdims=True)
                on = d * lax.rsqrt(var + EPS) * gn_ref[:, h * RET_DV:(h + 1) * RET_DV]
                ya_ref[rows, h * RET_DV:(h + 1) * RET_DV] = (g * jax.nn.sigmoid(g) * on).astype(BF16)

    pl.when(lax.rem(i, 2) == 0)(functools.partial(step, stash0, stash1))
    pl.when(lax.rem(i, 2) == 1)(functools.partial(step, stash1, stash0))


def _inproj_ret(x2, g, wp, gn_g, S, tm, C):
    M = x2.shape[0]
    nt = M // tm
    HV = RET_HEADS * RET_DV
    cur = lambda i: (jnp.minimum(i, nt - 1), 0)
    prev = lambda i: (jnp.maximum(i - 1, 0), 0)
    const = lambda i: (0, 0)
    return pl.pallas_call(
        functools.partial(_inproj_ret_kernel, tm=tm, C=C, S=S),
        grid=(nt + 1,),
        in_specs=[pl.BlockSpec((tm, D_MODEL), cur),
                  pl.BlockSpec((1, D_MODEL), const),
                  pl.BlockSpec((D_MODEL, RT_W + PB_W + PF_W), const, pipeline_mode=pl.Buffered(1)),
                  pl.BlockSpec((1, HV), const)],
        out_specs=[pl.BlockSpec((tm, PB_W), cur), pl.BlockSpec((tm, PF_W), cur), pl.BlockSpec((tm, HV), prev)],
        out_shape=[jax.ShapeDtypeStruct((M, PB_W), BF16), jax.ShapeDtypeStruct((M, PF_W), F32),
                   jax.ShapeDtypeStruct((M, HV), BF16)],
        scratch_shapes=[pltpu.VMEM((tm, RT_W), BF16), pltpu.VMEM((tm, RT_W), BF16),
                        pltpu.VMEM((RET_HEADS, RET_DK, RET_DV), F32), pltpu.VMEM((RET_HEADS, C, C), F32),
                        pltpu.VMEM((RET_HEADS, C, RET_DV), F32), pltpu.VMEM((RET_HEADS, C, RET_DK), F32)],
        compiler_params=_params("arbitrary"),
        name="inproj_ret",
    )(x2, g, wp, gn_g)


def _compress_kernel(kin_ref, vin_ref, posk_ref, posv_ref, w1k_ref, w1v_ref, b1k_ref, b1v_ref,
                     w2k_ref, w2v_ref, ko_ref, vo_ref, *, NP):
    GD = NSA_GROUPS * NSA_DK
    npc = CMP_BLOCK // CMP_STRIDE
    row = lax.broadcasted_iota(jnp.int32, (NP, GD), 0)
    for in_ref, pos_ref, w1_ref, b1_ref, w2_ref, out_ref in (
            (kin_ref, posk_ref, w1k_ref, b1k_ref, w2k_ref, ko_ref),
            (vin_ref, posv_ref, w1v_ref, b1v_ref, w2v_ref, vo_ref)):
        parts = []
        for p in range(npc):
            acc = jnp.zeros((NP, NSA_GROUPS * CMP_HIDDEN), F32)
            for i in range(CMP_STRIDE):
                tok = in_ref[0, pl.ds(i, NP, stride=CMP_STRIDE), :]
                j = p * CMP_STRIDE + i
                acc = acc + jnp.dot((tok + pos_ref[j:j + 1, :]).astype(BF16), w1_ref[j],
                                    preferred_element_type=F32)
            parts.append(acc)
        hidden = parts[0]
        for p in range(1, npc):
            hidden = hidden + pltpu.roll(parts[p], NP - p, axis=0)
        act = jax.nn.gelu(hidden + b1_ref[...]).astype(BF16)
        res = jnp.dot(act, w2_ref[...], preferred_element_type=F32)
        res = jnp.where(row < NP - (npc - 1), res, 0.0)
        if out_ref is vo_ref:
            res = res.T
            for g in range(NSA_GROUPS):
                out_ref[0, g] = res[g * NSA_DK:(g + 1) * NSA_DK, :]
        else:
            for g in range(NSA_GROUPS):
                out_ref[0, g] = res[:, g * NSA_DK:(g + 1) * NSA_DK]


def _compress(pf3, posk, posv, w1k, w1v, b1k, b1v, w2k, w2v):
    B, S, _ = pf3.shape
    NP = S // CMP_STRIDE
    GD = NSA_GROUPS * NSA_DK
    GH = NSA_GROUPS * CMP_HIDDEN
    const2 = lambda b: (0, 0)
    const3 = lambda b: (0, 0, 0)
    k_shape = (NSA_GROUPS, NP, NSA_DK)
    v_shape = (NSA_GROUPS, NSA_DK, NP)
    return pl.pallas_call(
        functools.partial(_compress_kernel, NP=NP),
        grid=(B,),
        in_specs=[pl.BlockSpec((1, S, GD), lambda b: (b, 0, PF_KCR // GD)),
                  pl.BlockSpec((1, S, GD), lambda b: (b, 0, PF_VCR // GD)),
                  pl.BlockSpec((CMP_BLOCK, GD), const2), pl.BlockSpec((CMP_BLOCK, GD), const2),
                  pl.BlockSpec((CMP_BLOCK, GD, GH), const3), pl.BlockSpec((CMP_BLOCK, GD, GH), const3),
                  pl.BlockSpec((1, GH), const2), pl.BlockSpec((1, GH), const2),
                  pl.BlockSpec((GH, GD), const2), pl.BlockSpec((GH, GD), const2)],
        out_specs=[pl.BlockSpec((1,) + k_shape, lambda b: (b, 0, 0, 0)),
                   pl.BlockSpec((1,) + v_shape, lambda b: (b, 0, 0, 0))],
        out_shape=[jax.ShapeDtypeStruct((B,) + k_shape, F32), jax.ShapeDtypeStruct((B,) + v_shape, F32)],
        compiler_params=_params("arbitrary"),
        name="compress",
    )(pf3, pf3, posk, posv, w1k, w1v, b1k, b1v, w2k, w2v)


SEL_LANE0 = 64
SEL_LANES = 32
POS_HI_LANE, POS_LO_LANE, PAD_LANE = 96, 97, 98
ONES_LANE = 64
MASK_BIG = 1e30


def _cmpattn_kernel(q_ref, kc_ref, vct_ref, gate_ref, ocmp_ref, sel_ref, *, tq, NP, NS):
    NC = NP - (CMP_BLOCK // CMP_STRIDE - 1)
    tile = pl.program_id(1)

    def run(rows):
        _cmpattn_tile(q_ref, kc_ref, vct_ref, gate_ref, ocmp_ref, sel_ref, tile * tq,
                      tq=tq, NP=rows, NC=NC, NS=NS)

    for v in range(pl.cdiv(NP * CMP_STRIDE, tq)):
        pl.when(tile == v)(functools.partial(run, min(NP, (v + 1) * tq // CMP_STRIDE)))


def _cmpattn_tile(q_ref, kc_ref, vct_ref, gate_ref, ocmp_ref, sel_ref, t0, *, tq, NP, NC, NS):
    tcol = t0 + lax.broadcasted_iota(jnp.int32, (NP, tq), 1)
    cidx = lax.broadcasted_iota(jnp.int32, (NP, tq), 0)
    visible = (tcol >= cidx * CMP_STRIDE + (CMP_BLOCK - 1)) & (cidx < NC)
    mask_add = jnp.where(visible, 0.0, NEG)
    block_end = (lax.broadcasted_iota(jnp.int32, (NP, 1), 0) * CMP_STRIDE + (CMP_BLOCK - 1)).astype(F32)
    sees_any = ((t0 + lax.broadcasted_iota(jnp.int32, (1, tq), 1)) >= CMP_BLOCK - 1).astype(F32)
    jj = lax.broadcasted_iota(jnp.int32, (SEL_LANES, NP), 0)
    cc = lax.broadcasted_iota(jnp.int32, (SEL_LANES, NP), 1)
    overlap_t = ((cc * CMP_STRIDE < jj * SEL_BLOCK + SEL_BLOCK)
                 & (cc * CMP_STRIDE + CMP_BLOCK > jj * SEL_BLOCK) & (jj < NS)).astype(BF16)
    blk = lax.broadcasted_iota(jnp.int32, (SEL_LANES, tq), 0)
    blk_f = blk.astype(F32)
    is_blk = blk < NS
    cur = jnp.right_shift(t0 + lax.broadcasted_iota(jnp.int32, (SEL_LANES, tq), 1),
                          SEL_BLOCK.bit_length() - 1)
    o_t = []
    for g in range(NSA_GROUPS):
        kc_parts = _split3(kc_ref[0, g, :NP, :])
        vct = vct_ref[0, g, :, :NP].astype(BF16)
        psum = jnp.zeros((NP, tq), F32)
        for r in range(NSA_R):
            hh = g * NSA_R + r
            q = q_ref[0, :, hh * NSA_DK:(hh + 1) * NSA_DK]
            s = sum(lax.dot_general(part, q, NT_DIMS, preferred_element_type=F32) for part in kc_parts)
            s = s + (2.0 ** -(hh + 1)) * block_end + mask_add
            e = jnp.exp(s - jnp.max(s, axis=0, keepdims=True))
            p = e * (sees_any / jnp.sum(e, axis=0, keepdims=True))
            o_t.append(jnp.dot(vct, p.astype(BF16), preferred_element_type=F32))
            psum = psum + p
        imp = sum(jnp.dot(overlap_t, part, preferred_element_type=F32) for part in _split3(psum))
        imp = jnp.where((blk == 0) | (blk == cur) | (blk == cur - 1), FORCE, imp)
        imp = jnp.where(blk > cur, -FORCE, imp)
        work = jnp.where(is_blk, imp, LOWEST)
        sel = jnp.zeros((SEL_LANES, tq), F32)
        for _ in range(min(SEL_TOPK, NS)):
            top = jnp.max(work, axis=0, keepdims=True)
            first = jnp.min(jnp.where(work == top, blk_f, float(SEL_LANES)), axis=0, keepdims=True)
            pick = blk_f == first
            sel = jnp.where(pick, 1.0, sel)
            work = jnp.where(pick, LOWEST, work)
        sel = jnp.where(is_blk & (blk <= cur), sel - 1.0, jnp.where(is_blk, -1.0, 0.0))
        placed = jnp.concatenate([jnp.zeros((SEL_LANE0, tq), F32), sel,
                                  jnp.zeros((LANES - SEL_LANE0 - SEL_LANES, tq), F32)], axis=0)
        sel_ref[0, g] = placed.T.astype(BF16)
    o = jnp.concatenate(o_t, axis=0).T
    gate = jax.nn.sigmoid(gate_ref[0])
    for hh in range(NSA_HEADS):
        cols = slice(hh * NSA_DK, (hh + 1) * NSA_DK)
        ocmp_ref[0, :, cols] = gate[:, hh:hh + 1] * o[:, cols]


def _cmpattn(pb3, kcmp, vcmp, pf3, tq):
    B, S, _ = pb3.shape
    NP = S // CMP_STRIDE
    NS = S // SEL_BLOCK
    HD = NSA_HEADS * NSA_DK
    return pl.pallas_call(
        functools.partial(_cmpattn_kernel, tq=tq, NP=NP, NS=NS),
        grid=(B, S // tq),
        in_specs=[pl.BlockSpec((1, tq, HD), lambda b, i: (b, i, PB_NQ // HD)),
                  pl.BlockSpec((1, NSA_GROUPS, NP, NSA_DK), lambda b, i: (b, 0, 0, 0)),
                  pl.BlockSpec((1, NSA_GROUPS, NSA_DK, NP), lambda b, i: (b, 0, 0, 0)),
                  pl.BlockSpec((1, tq, LANES), lambda b, i: (b, i, PF_NG // LANES))],
        out_specs=[pl.BlockSpec((1, tq, HD), lambda b, i: (b, i, 0)),
                   pl.BlockSpec((1, NSA_GROUPS, tq, LANES), lambda b, i: (b, 0, i, 0))],
        out_shape=[jax.ShapeDtypeStruct((B, S, HD), F32),
                   jax.ShapeDtypeStruct((B, NSA_GROUPS, S, LANES), BF16)],
        compiler_params=_params("arbitrary", "arbitrary"),
        name="cmpattn",
    )(pb3, kcmp, vcmp, pf3)


def _selwin_kernel(q_ref, ks_ref, vs_ref, kw_ref, vw_ref, sel_ref, wmask_ref, dmask_ref, gate_ref, ocmp_ref,
                   o_ref, ksx, vsx, kwx, vwx, *, tq, KC, S):
    WL = WINDOW + tq
    RB = 512
    step = pl.program_id(1)
    sel_shift = SEL_BLOCK.bit_length() - 1

    @pl.when(step == 0)
    def _build():
        lane = lax.broadcasted_iota(jnp.int32, (RB, LANES), 1)
        is_head = lane < NSA_DK
        pad_k = jnp.where(lane == PAD_LANE, MASK_BIG, 0.0).astype(BF16)
        for g in range(NSA_GROUPS):
            kwx[g, 0:WINDOW, :] = pad_k[:WINDOW]
            vwx[g, 0:WINDOW, :] = jnp.zeros((WINDOW, LANES), BF16)
        for piece in range(S // RB):
            r0 = piece * RB
            pos = r0 + lax.broadcasted_iota(jnp.int32, (RB, LANES), 0)
            alibi = jnp.where(lane == POS_HI_LANE, jnp.right_shift(pos, sel_shift),
                              jnp.where(lane == POS_LO_LANE, pos & (SEL_BLOCK - 1), 0)).astype(F32)
            onehot = jnp.where(lane - SEL_LANE0 == jnp.right_shift(pos, sel_shift), MASK_BIG, 0.0)
            k_extra_win = alibi.astype(BF16)
            k_extra_sel = (alibi + onehot).astype(BF16)
            v_extra = jnp.where(lane == ONES_LANE, 1.0, 0.0).astype(BF16)
            for g in range(NSA_GROUPS):
                def head_lanes(ref):
                    t = ref[0, r0:r0 + RB, :]
                    return t if g == 0 else jnp.concatenate([t[:, NSA_DK:], t[:, :NSA_DK]], axis=1)
                ksx[g, r0:r0 + RB, :] = jnp.where(is_head, head_lanes(ks_ref), k_extra_sel)
                vsx[g, r0:r0 + RB, :] = jnp.where(is_head, head_lanes(vs_ref), v_extra)
                kwx[g, WINDOW + r0:WINDOW + r0 + RB, :] = jnp.where(is_head, head_lanes(kw_ref), k_extra_win)
                vwx[g, WINDOW + r0:WINDOW + r0 + RB, :] = jnp.where(is_head, head_lanes(vw_ref), v_extra)

    lane = lax.broadcasted_iota(jnp.int32, (tq, LANES), 1)
    is_head = lane < NSA_DK
    wmask = wmask_ref[...][None]

    def masked_rows(s, mask):
        n = s.shape[-1]
        return (s.reshape(NSA_R, tq, n) + mask).reshape(NSA_R * tq, n)

    def normalise(acc):
        return acc[:, :NSA_DK] / acc[:, ONES_LANE:ONES_LANE + 1]

    def scores(q, k):
        half = q.shape[0] // 2
        return jnp.concatenate([lax.dot_general(q[:half], k, NT_DIMS, preferred_element_type=F32),
                                lax.dot_general(q[half:], k, NT_DIMS, preferred_element_type=F32)], axis=0)

    def weighted_values(p, v):
        half = p.shape[0] // 2
        p = p.astype(BF16)
        return jnp.concatenate([jnp.dot(p[:half], v, preferred_element_type=F32),
                                jnp.dot(p[half:], v, preferred_element_type=F32)], axis=0)

    def query_rows(g, tile_rows):
        sel = sel_ref[0, g, tile_rows, :].astype(F32)
        rows = []
        for r in range(NSA_R):
            hh = g * NSA_R + r
            qh = q_ref[0, tile_rows, (hh // 2) * LANES:(hh // 2 + 1) * LANES].astype(F32)
            if hh % 2:
                qh = jnp.concatenate([qh[:, NSA_DK:], qh[:, :NSA_DK]], axis=1)
            slope = 2.0 ** -(hh + 1)
            extra = jnp.where(lane == POS_HI_LANE, SEL_BLOCK * slope,
                              jnp.where(lane == POS_LO_LANE, slope, jnp.where(lane == PAD_LANE, -1.0, sel)))
            rows.append(jnp.where(is_head, qh, extra).astype(BF16))
        return jnp.concatenate(rows, axis=0)

    def tile(n_past, k, carry):
        past = n_past * KC
        tile_rows = pl.ds(pl.multiple_of(k * tq, tq), tq)
        t0 = pl.multiple_of(past + k * tq, tq)
        gate = jax.nn.sigmoid(gate_ref[0, tile_rows, :])
        dmask = dmask_ref[k][None]
        for g in range(NSA_GROUPS):
            q = query_rows(g, tile_rows)
            s = masked_rows(scores(q, kwx[g, pl.ds(t0, WL), :]), wmask)
            p = jnp.exp(s - jnp.max(s, axis=-1, keepdims=True))
            o_win = normalise(weighted_values(p, vwx[g, pl.ds(t0, WL), :]))

            s_own = masked_rows(scores(q, ksx[g, past:past + KC, :]), dmask)
            m = jnp.max(s_own, axis=-1, keepdims=True)
            if n_past:
                s_past = scores(q, ksx[g, 0:past, :])
                m = jnp.maximum(m, jnp.max(s_past, axis=-1, keepdims=True))
            acc = weighted_values(jnp.exp(s_own - m), vsx[g, past:past + KC, :])
            if n_past:
                acc = acc + weighted_values(jnp.exp(s_past - m), vsx[g, 0:past, :])
            o_sel = normalise(acc)

            for r in range(NSA_R):
                hh = g * NSA_R + r
                g_sel = gate[:, NSA_HEADS + hh:NSA_HEADS + hh + 1]
                g_win = gate[:, 2 * NSA_HEADS + hh:2 * NSA_HEADS + hh + 1]
                piece = g_sel * o_sel[r * tq:(r + 1) * tq] + g_win * o_win[r * tq:(r + 1) * tq]
                o_ref[0, tile_rows, hh * NSA_DK:(hh + 1) * NSA_DK] = (
                    ocmp_ref[0, tile_rows, hh * NSA_DK:(hh + 1) * NSA_DK] + piece).astype(BF16)
        return carry

    def tiles(n_past):
        lax.fori_loop(0, KC // tq, functools.partial(tile, n_past), 0)

    for n_past in range(S // KC):
        pl.when(step == n_past)(functools.partial(tiles, n_past))


def _selwin(pb3, sel, pf3, ocmp, tq, KC):
    B, S, _ = pb3.shape
    HD = NSA_HEADS * NSA_DK
    WL = WINDOW + tq
    tr = jnp.arange(tq)[:, None]
    kc = jnp.arange(WL)[None, :]
    wmask = jnp.where((kc > tr) & (kc <= tr + WINDOW), 0.0, NEG).astype(F32)
    off = (jnp.arange(KC // tq) * tq)[:, None, None]
    dmask = jnp.where(jnp.arange(KC)[None, None, :] <= off + tr[None], 0.0, NEG).astype(F32)
    kv_spec = lambda off: pl.BlockSpec((1, S, LANES), lambda b, i: (b, 0, off // LANES))
    return pl.pallas_call(
        functools.partial(_selwin_kernel, tq=tq, KC=KC, S=S),
        grid=(B, S // KC),
        in_specs=[pl.BlockSpec((1, KC, HD), lambda b, i: (b, i, PB_NQ // HD)),
                  kv_spec(PB_KS), kv_spec(PB_VS), kv_spec(PB_KW), kv_spec(PB_VW),
                  pl.BlockSpec((1, NSA_GROUPS, KC, LANES), lambda b, i: (b, 0, i, 0)),
                  pl.BlockSpec((tq, WL), lambda b, i: (0, 0)),
                  pl.BlockSpec((KC // tq, tq, KC), lambda b, i: (0, 0, 0)),
                  pl.BlockSpec((1, KC, LANES), lambda b, i: (b, i, PF_NG // LANES)),
                  pl.BlockSpec((1, KC, HD), lambda b, i: (b, i, 0))],
        out_specs=pl.BlockSpec((1, KC, HD), lambda b, i: (b, i, 0)),
        out_shape=jax.ShapeDtypeStruct((B, S, HD), BF16),
        scratch_shapes=[pltpu.VMEM((NSA_GROUPS, S, LANES), BF16), pltpu.VMEM((NSA_GROUPS, S, LANES), BF16),
                        pltpu.VMEM((NSA_GROUPS, WINDOW + S, LANES), BF16),
                        pltpu.VMEM((NSA_GROUPS, WINDOW + S, LANES), BF16)],
        compiler_params=_params("arbitrary", "arbitrary"),
        name="selwin",
    )(pb3, pb3, pb3, pb3, pb3, sel, wmask, dmask, pf3, ocmp)


SUBLANES = 8


def _mix_ffn_kernel(x_ref, ya_ref, yb_ref, ga_ref, gb_ref, wa_ref, wb_ref, wo_ref, gn_ref,
                    wup_ref, cw_ref, cb_ref, wd_ref, gf_ref, o_ref, tail_ref, u_ref, *, tm, tf):
    @pl.when(pl.program_id(1) == 0)
    def _():
        tail_ref[...] = jnp.zeros_like(tail_ref)

    y_a = jnp.dot(ya_ref[0], wa_ref[...], preferred_element_type=F32)
    y_b = jnp.dot(yb_ref[0], wb_ref[...], preferred_element_type=F32)
    merged = (jax.nn.sigmoid(ga_ref[0].astype(F32)) * y_a + jax.nn.sigmoid(gb_ref[0].astype(F32)) * y_b)
    x1 = x_ref[0] + jnp.dot(merged.astype(BF16), wo_ref[...], preferred_element_type=F32)
    o_ref[0] = x1
    h = _rms(x1, gn_ref[...]).astype(BF16)

    top = lax.broadcasted_iota(jnp.int32, (SUBLANES, tf), 0)
    for c in range(D_FF // tf):
        cols = slice(c * tf, (c + 1) * tf)
        a = jnp.dot(h, wup_ref[:, cols], preferred_element_type=F32)
        b = jnp.dot(h, wup_ref[:, D_FF + c * tf:D_FF + (c + 1) * tf], preferred_element_type=F32)
        tail = tail_ref[c]
        tail_ref[c] = a[tm - SUBLANES:]
        ac = cb_ref[:, cols] + cw_ref[CONV_W - 1:CONV_W, cols] * a
        for d in range(1, CONV_W):
            sh = pltpu.roll(a, d, axis=0)
            head = jnp.where(top < d, pltpu.roll(tail, d, axis=0), sh[:SUBLANES])
            sh = jnp.concatenate([head, sh[SUBLANES:]], axis=0)
            ac = ac + cw_ref[CONV_W - 1 - d:CONV_W - d, cols] * sh
        u_ref[:, cols] = (jax.nn.gelu(ac) * b).astype(BF16)
    y = jnp.dot(u_ref[...], wd_ref[...], preferred_element_type=F32)
    o_ref[0] = _rms(o_ref[0] + y, gf_ref[...])


def _mix_ffn(x, ya, yb, pb3, wa, wb, wo, gn, w_up, conv_w, conv_b, w_down, gf, tm, tf):
    B, S, _ = x.shape
    HV = RET_HEADS * RET_DV
    HD = NSA_HEADS * NSA_DK
    rows = lambda width, col=0: pl.BlockSpec((1, tm, width), lambda b, i: (b, i, col))
    whole = lambda *shape: pl.BlockSpec(shape, lambda b, i: (0,) * len(shape), pipeline_mode=pl.Buffered(1))
    return pl.pallas_call(
        functools.partial(_mix_ffn_kernel, tm=tm, tf=tf),
        grid=(B, S // tm),
        in_specs=[rows(D_MODEL), rows(HV), rows(HD),
                  rows(D_MODEL, PB_GA // D_MODEL), rows(D_MODEL, PB_GB // D_MODEL),
                  whole(HV, D_MODEL), whole(HD, D_MODEL), whole(D_MODEL, D_MODEL), whole(1, D_MODEL),
                  whole(D_MODEL, 2 * D_FF), whole(CONV_W, D_FF), whole(1, D_FF), whole(D_FF, D_MODEL),
                  whole(1, D_MODEL)],
        out_specs=rows(D_MODEL),
        out_shape=jax.ShapeDtypeStruct((B, S, D_MODEL), F32),
        scratch_shapes=[pltpu.VMEM((D_FF // tf, SUBLANES, tf), F32), pltpu.VMEM((tm, D_FF), BF16)],
        compiler_params=_params("arbitrary", "arbitrary"),
        name="mix_ffn",
    )(x, ya, yb, pb3, pb3, wa, wb, wo, gn, w_up, conv_w, conv_b, w_down, gf)


def _block_diag_groups(w):
    z = jnp.zeros_like(w)
    return jnp.concatenate([jnp.concatenate([w, z], axis=-1), jnp.concatenate([z, w], axis=-1)], axis=-2)


def kernel(x, norm_mix, w_in, ret_gn_g, cmp_pos_k, cmp_w1_k, cmp_b1_k, cmp_w2_k, cmp_pos_v, cmp_w1_v,
           cmp_b1_v, cmp_w2_v, w_ret_o, w_nsa_o, w_out, norm_ffn, w_up, conv_w, conv_b, w_down,
           norm_final):
    B, S, D = x.shape
    assert D == D_MODEL and NSA_GROUPS == 2 and norm_mix.shape[0] == 1
    assert S % 512 == 0 and S >= 1024 and S // CMP_STRIDE <= LANES and S // SEL_BLOCK <= SEL_LANES
    M = B * S
    x2 = x.reshape(M, D)

    off = np.concatenate([[0], np.cumsum(IN_SIZES)])
    wt = jnp.transpose(w_in[0]).astype(BF16)
    seg = lambda i: wt[off[i]:off[i + 1], :]
    wp = jnp.transpose(jnp.concatenate(
        [seg(0), seg(1), seg(2), seg(3),
         seg(4) * (NSA_DK ** -0.5), seg(7), seg(8), seg(9), seg(10), seg(12), seg(13),
         seg(5), seg(6), seg(11), jnp.zeros((PF_W - PF_NG - IN_SIZES[11], D), BF16)], axis=0))

    def cmp_weights(pos, w1, b1, w2):
        return (jnp.tile(pos, (1, NSA_GROUPS)),
                _block_diag_groups(w1.reshape(CMP_BLOCK, NSA_DK, CMP_HIDDEN)).astype(BF16),
                jnp.tile(b1, NSA_GROUPS)[None, :],
                _block_diag_groups(w2).astype(BF16))

    posk, w1k, b1k, w2k = cmp_weights(cmp_pos_k[0], cmp_w1_k[0], cmp_b1_k[0], cmp_w2_k[0])
    posv, w1v, b1v, w2v = cmp_weights(cmp_pos_v[0], cmp_w1_v[0], cmp_b1_v[0], cmp_w2_v[0])

    pb, pf, ya = _inproj_ret(x2, norm_mix, wp, ret_gn_g, S, tm=512, C=256)
    pb3 = pb.reshape(B, S, PB_W)
    pf3 = pf.reshape(B, S, PF_W)
    ya = ya.reshape(B, S, -1)

    kcmp, vcmp = _compress(pf3, posk, posv, w1k, w1v, b1k, b1v, w2k, w2v)
    ocmp, sel = _cmpattn(pb3, kcmp, vcmp, pf3, tq=1024)
    yb = _selwin(pb3, sel, pf3, ocmp, tq=256, KC=512)

    return _mix_ffn(x, ya, yb, pb3, w_ret_o[0].astype(BF16), w_nsa_o[0].astype(BF16),
                    w_out[0].astype(BF16), norm_ffn, w_up[0].astype(BF16), conv_w[0], conv_b,
                    w_down[0].astype(BF16), norm_final[None, :], tm=512, tf=256)
```

```python
import functools
import math

import numpy as np
import jax
import jax.numpy as jnp
from jax import lax
from jax.experimental import pallas as pl
from jax.experimental.pallas import tpu as pltpu

F32 = jnp.float32
BF16 = jnp.bfloat16

D_MODEL = 1024
RET_HEADS = 4
RET_DK = 128
RET_DV = 256
NSA_HEADS = 8
NSA_GROUPS = 2
NSA_R = NSA_HEADS // NSA_GROUPS
NSA_DK = 64
CMP_BLOCK = 32
CMP_STRIDE = 16
CMP_HIDDEN = 256
SEL_BLOCK = 64
SEL_TOPK = 8
WINDOW = 512
D_FF = 2816
CONV_W = 3
EPS = 1e-6
NEG = -1e30
FORCE = 1e9
LOWEST = -3e38

IN_SIZES = (RET_HEADS * RET_DK, RET_HEADS * RET_DK, RET_HEADS * RET_DV, RET_HEADS * RET_DV,
            NSA_HEADS * NSA_DK,
            NSA_GROUPS * NSA_DK, NSA_GROUPS * NSA_DK, NSA_GROUPS * NSA_DK,
            NSA_GROUPS * NSA_DK, NSA_GROUPS * NSA_DK, NSA_GROUPS * NSA_DK,
            3 * NSA_HEADS, D_MODEL, D_MODEL)

LANES = 128
VMEM_LIMIT = 56 * 1024 * 1024

_W_OFF = np.concatenate([[0], np.cumsum(IN_SIZES)]).tolist()
W_NQ, W_KCR, W_KS, W_NG, W_GA = _W_OFF[4], _W_OFF[5], _W_OFF[7], _W_OFF[11], _W_OFF[12]

RT_RQ, RT_RK, RT_RV, RT_RG = 0, 512, 1024, 2048
RT_W = 3072
PB_NQ, PB_KS, PB_VS, PB_KW, PB_VW, PB_GA, PB_GB = 0, 512, 640, 768, 896, 1024, 2048
PB_W = 3072
PF_KCR, PF_VCR, PF_NG = 0, 128, 256
PF_W = 384

NT_DIMS = (((1,), (1,)), ((), ()))
TN_DIMS = (((0,), (0,)), ((), ()))


def _params(*sem):
    return pltpu.CompilerParams(dimension_semantics=sem, vmem_limit_bytes=VMEM_LIMIT)


def _rms(x, g):
    return x * lax.rsqrt(jnp.mean(x * x, axis=-1, keepdims=True) + EPS) * g


def _split3(x):
    hi = x.astype(BF16)
    rest = x - hi.astype(F32)
    mid = rest.astype(BF16)
    return hi, mid, (rest - mid.astype(F32)).astype(BF16)


def _inproj_ret_kernel(x_ref, g_ref, w_ref, wg_ref, gn_ref, pb_ref, pf_ref, ya_ref,
                       stash0, stash1, r_ref, dmask_ref, xi_ref, zeta_ref, *, tm, C, S):
    log_gamma = [float(np.log1p(-np.exp2(np.float32(-5.0 - h)))) for h in range(RET_HEADS)]
    scale = RET_DK ** -0.5
    i = pl.program_id(0)

    @pl.when(i == 0)
    def _():
        diff = (lax.broadcasted_iota(jnp.int32, (C, C), 0)
                - lax.broadcasted_iota(jnp.int32, (C, C), 1)).astype(F32)
        pos_v = lax.broadcasted_iota(jnp.int32, (C, RET_DV), 0).astype(F32)
        pos_k = lax.broadcasted_iota(jnp.int32, (C, RET_DK), 0).astype(F32)
        for h in range(RET_HEADS):
            lg = log_gamma[h]
            dmask_ref[h] = jnp.where(diff >= 0, jnp.exp(lg * jnp.maximum(diff, 0.0)), 0.0) * scale
            xi_ref[h] = jnp.exp(lg * (pos_v + 1.0))
            zeta_ref[h] = jnp.exp(lg * (C - 1.0 - pos_k)) * scale
        stash1[...] = jnp.zeros_like(stash1)
        r_ref[...] = jnp.zeros_like(r_ref)

    seq_start = lax.rem(i + (S // tm) - 1, S // tm) == 0

    def step(new_ref, old_ref):
        hb = _rms(x_ref[...], g_ref[...]).astype(BF16)
        for c0 in range(0, RT_W, 512):
            new_ref[:, c0:c0 + 512] = jnp.dot(
                hb, w_ref[:, c0:c0 + 512], preferred_element_type=F32).astype(BF16)
        def project(w, c0, c1):
            return jnp.dot(hb, w[:, c0:c1], preferred_element_type=F32)

        pb_ref[:, PB_NQ:PB_KS] = (project(w_ref, W_NQ, W_KCR) * (NSA_DK ** -0.5)).astype(BF16)
        pb_ref[:, PB_KS:PB_GA] = project(w_ref, W_KS, W_NG).astype(BF16)
        for c0 in range(0, PB_W - PB_GA, 512):
            pb_ref[:, PB_GA + c0:PB_GA + c0 + 512] = project(wg_ref, c0, c0 + 512).astype(BF16)
        pf_ref[:, PF_KCR:PF_NG] = project(w_ref, W_KCR, W_KS)
        pf_ref[:, PF_NG:PF_W] = project(w_ref, W_NG, W_NG + PF_W - PF_NG)

        for cc in range(tm // C):
            rows = slice(cc * C, (cc + 1) * C)
            for h in range(RET_HEADS):
                lg = log_gamma[h]
                q = old_ref[rows, RT_RQ + h * RET_DK:RT_RQ + (h + 1) * RET_DK]
                k = old_ref[rows, RT_RK + h * RET_DK:RT_RK + (h + 1) * RET_DK]
                v = old_ref[rows, RT_RV + h * RET_DV:RT_RV + (h + 1) * RET_DV]
                g = old_ref[rows, RT_RG + h * RET_DV:RT_RG + (h + 1) * RET_DV].astype(F32)
                s = lax.dot_general(q, k, NT_DIMS, preferred_element_type=F32)
                o = jnp.dot((s * dmask_ref[h]).astype(BF16), v, preferred_element_type=F32)
                state = r_ref[h]
                if cc == 0:
                    state = jnp.where(seq_start, 0.0, state)
                o = o + xi_ref[h] * jnp.dot(q, state.astype(BF16), preferred_element_type=F32)
                kz = (k.astype(F32) * zeta_ref[h]).astype(BF16)
                kv = lax.dot_general(kz, v, TN_DIMS, preferred_element_type=F32)
                r_ref[h] = math.exp(lg * C) * state + kv
                mu = jnp.mean(o, axis=-1, keepdims=True)
                d = o - mu
                var = jnp.mean(d * d, axis=-1, keepdims=True)
                on = d * lax.rsqrt(var + EPS) * gn_ref[:, h * RET_DV:(h + 1) * RET_DV]
                ya_ref[rows, h * RET_DV:(h + 1) * RET_DV] = (g * jax.nn.sigmoid(g) * on).astype(BF16)

    pl.when(lax.rem(i, 2) == 0)(functools.partial(step, stash0, stash1))
    pl.when(lax.rem(i, 2) == 1)(functools.partial(step, stash1, stash0))


def _inproj_ret(x2, g, w, w_gates, gn_g, S, tm, C):
    M = x2.shape[0]
    nt = M // tm
    HV = RET_HEADS * RET_DV
    cur = lambda i: (jnp.minimum(i, nt - 1), 0)
    prev = lambda i: (jnp.maximum(i - 1, 0), 0)
    const = lambda i: (0, 0)
    return pl.pallas_call(
        functools.partial(_inproj_ret_kernel, tm=tm, C=C, S=S),
        grid=(nt + 1,),
        in_specs=[pl.BlockSpec((tm, D_MODEL), cur),
                  pl.BlockSpec((1, D_MODEL), const),
                  pl.BlockSpec(w.shape, const, pipeline_mode=pl.Buffered(1)),
                  pl.BlockSpec(w_gates.shape, const, pipeline_mode=pl.Buffered(1)),
                  pl.BlockSpec((1, HV), const)],
        out_specs=[pl.BlockSpec((tm, PB_W), cur), pl.BlockSpec((tm, PF_W), cur), pl.BlockSpec((tm, HV), prev)],
        out_shape=[jax.ShapeDtypeStruct((M, PB_W), BF16), jax.ShapeDtypeStruct((M, PF_W), F32),
                   jax.ShapeDtypeStruct((M, HV), BF16)],
        scratch_shapes=[pltpu.VMEM((tm, RT_W), BF16), pltpu.VMEM((tm, RT_W), BF16),
                        pltpu.VMEM((RET_HEADS, RET_DK, RET_DV), F32), pltpu.VMEM((RET_HEADS, C, C), F32),
                        pltpu.VMEM((RET_HEADS, C, RET_DV), F32), pltpu.VMEM((RET_HEADS, C, RET_DK), F32)],
        compiler_params=_params("arbitrary"),
        name="inproj_ret",
    )(x2, g, w, w_gates, gn_g)


def _compress_kernel(kin_ref, vin_ref, posk_ref, posv_ref, w1k_ref, w1v_ref, b1k_ref, b1v_ref,
                     w2k_ref, w2v_ref, ko_ref, vo_ref, *, NP):
    GD = NSA_GROUPS * NSA_DK
    npc = CMP_BLOCK // CMP_STRIDE
    row = lax.broadcasted_iota(jnp.int32, (NP, GD), 0)
    for in_ref, pos_ref, w1_ref, b1_ref, w2_ref, out_ref in (
            (kin_ref, posk_ref, w1k_ref, b1k_ref, w2k_ref, ko_ref),
            (vin_ref, posv_ref, w1v_ref, b1v_ref, w2v_ref, vo_ref)):
        parts = []
        for p in range(npc):
            acc = jnp.zeros((NP, NSA_GROUPS * CMP_HIDDEN), F32)
            for i in range(CMP_STRIDE):
                tok = in_ref[0, pl.ds(i, NP, stride=CMP_STRIDE), :]
                j = p * CMP_STRIDE + i
                acc = acc + jnp.dot((tok + pos_ref[j:j + 1, :]).astype(BF16), w1_ref[j],
                                    preferred_element_type=F32)
            parts.append(acc)
        hidden = parts[0]
        for p in range(1, npc):
            hidden = hidden + pltpu.roll(parts[p], NP - p, axis=0)
        act = jax.nn.gelu(hidden + b1_ref[...]).astype(BF16)
        res = jnp.dot(act, w2_ref[...], preferred_element_type=F32)
        res = jnp.where(row < NP - (npc - 1), res, 0.0)
        if out_ref is vo_ref:
            res = res.T
            for g in range(NSA_GROUPS):
                out_ref[0, g] = res[g * NSA_DK:(g + 1) * NSA_DK, :]
        else:
            for g in range(NSA_GROUPS):
                out_ref[0, g] = res[:, g * NSA_DK:(g + 1) * NSA_DK]


def _compress(pf3, posk, posv, w1k, w1v, b1k, b1v, w2k, w2v):
    B, S, _ = pf3.shape
    NP = S // CMP_STRIDE
    GD = NSA_GROUPS * NSA_DK
    GH = NSA_GROUPS * CMP_HIDDEN
    const2 = lambda b: (0, 0)
    const3 = lambda b: (0, 0, 0)
    k_shape = (NSA_GROUPS, NP, NSA_DK)
    v_shape = (NSA_GROUPS, NSA_DK, NP)
    return pl.pallas_call(
        functools.partial(_compress_kernel, NP=NP),
        grid=(B,),
        in_specs=[pl.BlockSpec((1, S, GD), lambda b: (b, 0, PF_KCR // GD)),
                  pl.BlockSpec((1, S, GD), lambda b: (b, 0, PF_VCR // GD)),
                  pl.BlockSpec((CMP_BLOCK, GD), const2), pl.BlockSpec((CMP_BLOCK, GD), const2),
                  pl.BlockSpec((CMP_BLOCK, GD, GH), const3), pl.BlockSpec((CMP_BLOCK, GD, GH), const3),
                  pl.BlockSpec((1, GH), const2), pl.BlockSpec((1, GH), const2),
                  pl.BlockSpec((GH, GD), const2), pl.BlockSpec((GH, GD), const2)],
        out_specs=[pl.BlockSpec((1,) + k_shape, lambda b: (b, 0, 0, 0)),
                   pl.BlockSpec((1,) + v_shape, lambda b: (b, 0, 0, 0))],
        out_shape=[jax.ShapeDtypeStruct((B,) + k_shape, F32), jax.ShapeDtypeStruct((B,) + v_shape, F32)],
        compiler_params=_params("arbitrary"),
        name="compress",
    )(pf3, pf3, posk, posv, w1k, w1v, b1k, b1v, w2k, w2v)


SEL_LANE0 = 64
SEL_LANES = 32
POS_HI_LANE, POS_LO_LANE, PAD_LANE = 96, 97, 98
ONES_LANE = 64
MASK_BIG = 1e30


def _cmpattn_kernel(q_ref, kc_ref, vct_ref, gate_ref, ocmp_ref, sel_ref, *, tq, NP, NS):
    NC = NP - (CMP_BLOCK // CMP_STRIDE - 1)
    tile = pl.program_id(1)

    def run(rows):
        _cmpattn_tile(q_ref, kc_ref, vct_ref, gate_ref, ocmp_ref, sel_ref, tile * tq,
                      tq=tq, NP=rows, NC=NC, NS=NS)

    for v in range(pl.cdiv(NP * CMP_STRIDE, tq)):
        pl.when(tile == v)(functools.partial(run, min(NP, (v + 1) * tq // CMP_STRIDE)))


def _cmpattn_tile(q_ref, kc_ref, vct_ref, gate_ref, ocmp_ref, sel_ref, t0, *, tq, NP, NC, NS):
    tcol = t0 + lax.broadcasted_iota(jnp.int32, (NP, tq), 1)
    cidx = lax.broadcasted_iota(jnp.int32, (NP, tq), 0)
    visible = (tcol >= cidx * CMP_STRIDE + (CMP_BLOCK - 1)) & (cidx < NC)
    mask_add = jnp.where(visible, 0.0, NEG)
    block_end = (lax.broadcasted_iota(jnp.int32, (NP, 1), 0) * CMP_STRIDE + (CMP_BLOCK - 1)).astype(F32)
    sees_any = ((t0 + lax.broadcasted_iota(jnp.int32, (1, tq), 1)) >= CMP_BLOCK - 1).astype(F32)
    jj = lax.broadcasted_iota(jnp.int32, (SEL_LANES, NP), 0)
    cc = lax.broadcasted_iota(jnp.int32, (SEL_LANES, NP), 1)
    overlap_t = ((cc * CMP_STRIDE < jj * SEL_BLOCK + SEL_BLOCK)
                 & (cc * CMP_STRIDE + CMP_BLOCK > jj * SEL_BLOCK) & (jj < NS)).astype(BF16)
    blk = lax.broadcasted_iota(jnp.int32, (SEL_LANES, tq), 0)
    blk_f = blk.astype(F32)
    is_blk = blk < NS
    cur = jnp.right_shift(t0 + lax.broadcasted_iota(jnp.int32, (SEL_LANES, tq), 1),
                          SEL_BLOCK.bit_length() - 1)
    o_t = []
    for g in range(NSA_GROUPS):
        kc_parts = _split3(kc_ref[0, g, :NP, :])
        vct = vct_ref[0, g, :, :NP].astype(BF16)
        psum = jnp.zeros((NP, tq), F32)
        for r in range(NSA_R):
            hh = g * NSA_R + r
            q = q_ref[0, :, hh * NSA_DK:(hh + 1) * NSA_DK]
            s = sum(lax.dot_general(part, q, NT_DIMS, preferred_element_type=F32) for part in kc_parts)
            s = s + (2.0 ** -(hh + 1)) * block_end + mask_add
            e = jnp.exp(s - jnp.max(s, axis=0, keepdims=True))
            p = e * (sees_any / jnp.sum(e, axis=0, keepdims=True))
            o_t.append(jnp.dot(vct, p.astype(BF16), preferred_element_type=F32))
            psum = psum + p
        imp = sum(jnp.dot(overlap_t, part, preferred_element_type=F32) for part in _split3(psum))
        imp = jnp.where((blk == 0) | (blk == cur) | (blk == cur - 1), FORCE, imp)
        imp = jnp.where(blk > cur, -FORCE, imp)
        work = jnp.where(is_blk, imp, LOWEST)
        sel = jnp.zeros((SEL_LANES, tq), F32)
        for _ in range(min(SEL_TOPK, NS)):
            top = jnp.max(work, axis=0, keepdims=True)
            first = jnp.min(jnp.where(work == top, blk_f, float(SEL_LANES)), axis=0, keepdims=True)
            pick = blk_f == first
            sel = jnp.where(pick, 1.0, sel)
            work = jnp.where(pick, LOWEST, work)
        sel = jnp.where(is_blk & (blk <= cur), sel - 1.0, jnp.where(is_blk, -1.0, 0.0))
        placed = jnp.concatenate([jnp.zeros((SEL_LANE0, tq), F32), sel,
                                  jnp.zeros((LANES - SEL_LANE0 - SEL_LANES, tq), F32)], axis=0)
        sel_ref[0, g] = placed.T.astype(BF16)
    o = jnp.concatenate(o_t, axis=0).T
    gate = jax.nn.sigmoid(gate_ref[0])
    for hh in range(NSA_HEADS):
        cols = slice(hh * NSA_DK, (hh + 1) * NSA_DK)
        ocmp_ref[0, :, cols] = gate[:, hh:hh + 1] * o[:, cols]


def _cmpattn(pb3, kcmp, vcmp, pf3, tq):
    B, S, _ = pb3.shape
    NP = S // CMP_STRIDE
    NS = S // SEL_BLOCK
    HD = NSA_HEADS * NSA_DK
    return pl.pallas_call(
        functools.partial(_cmpattn_kernel, tq=tq, NP=NP, NS=NS),
        grid=(B, S // tq),
        in_specs=[pl.BlockSpec((1, tq, HD), lambda b, i: (b, i, PB_NQ // HD)),
                  pl.BlockSpec((1, NSA_GROUPS, NP, NSA_DK), lambda b, i: (b, 0, 0, 0)),
                  pl.BlockSpec((1, NSA_GROUPS, NSA_DK, NP), lambda b, i: (b, 0, 0, 0)),
                  pl.BlockSpec((1, tq, LANES), lambda b, i: (b, i, PF_NG // LANES))],
        out_specs=[pl.BlockSpec((1, tq, HD), lambda b, i: (b, i, 0)),
                   pl.BlockSpec((1, NSA_GROUPS, tq, LANES), lambda b, i: (b, 0, i, 0))],
        out_shape=[jax.ShapeDtypeStruct((B, S, HD), F32),
                   jax.ShapeDtypeStruct((B, NSA_GROUPS, S, LANES), BF16)],
        compiler_params=_params("arbitrary", "arbitrary"),
        name="cmpattn",
    )(pb3, kcmp, vcmp, pf3)


def _selwin_kernel(q_ref, ks_ref, vs_ref, kw_ref, vw_ref, sel_ref, wmask_ref, dmask_ref, gate_ref, ocmp_ref,
                   o_ref, ksx, vsx, kwx, vwx, *, tq, KC, S):
    WL = WINDOW + tq
    RB = 512
    i = pl.program_id(1)
    t0 = pl.multiple_of(i * tq, tq)
    sel_shift = SEL_BLOCK.bit_length() - 1

    @pl.when(i == 0)
    def _build():
        lane = lax.broadcasted_iota(jnp.int32, (RB, LANES), 1)
        is_head = lane < NSA_DK
        pad_k = jnp.where(lane == PAD_LANE, MASK_BIG, 0.0).astype(BF16)
        for g in range(NSA_GROUPS):
            kwx[g, 0:WINDOW, :] = pad_k[:WINDOW]
            vwx[g, 0:WINDOW, :] = jnp.zeros((WINDOW, LANES), BF16)
        for piece in range(S // RB):
            r0 = piece * RB
            pos = r0 + lax.broadcasted_iota(jnp.int32, (RB, LANES), 0)
            alibi = jnp.where(lane == POS_HI_LANE, jnp.right_shift(pos, sel_shift),
                              jnp.where(lane == POS_LO_LANE, pos & (SEL_BLOCK - 1), 0)).astype(F32)
            onehot = jnp.where(lane - SEL_LANE0 == jnp.right_shift(pos, sel_shift), MASK_BIG, 0.0)
            k_extra_win = alibi.astype(BF16)
            k_extra_sel = (alibi + onehot).astype(BF16)
            v_extra = jnp.where(lane == ONES_LANE, 1.0, 0.0).astype(BF16)
            for g in range(NSA_GROUPS):
                def head_lanes(ref):
                    t = ref[0, r0:r0 + RB, :]
                    return t if g == 0 else jnp.concatenate([t[:, NSA_DK:], t[:, :NSA_DK]], axis=1)
                ksx[g, r0:r0 + RB, :] = jnp.where(is_head, head_lanes(ks_ref), k_extra_sel)
                vsx[g, r0:r0 + RB, :] = jnp.where(is_head, head_lanes(vs_ref), v_extra)
                kwx[g, WINDOW + r0:WINDOW + r0 + RB, :] = jnp.where(is_head, head_lanes(kw_ref), k_extra_win)
                vwx[g, WINDOW + r0:WINDOW + r0 + RB, :] = jnp.where(is_head, head_lanes(vw_ref), v_extra)

    gate = jax.nn.sigmoid(gate_ref[0])
    lane = lax.broadcasted_iota(jnp.int32, (tq, LANES), 1)
    is_head = lane < NSA_DK
    diag_chunk = t0 // KC
    wmask = wmask_ref[...][None]
    dmask = dmask_ref[i % (KC // tq)][None]

    def masked_rows(s, mask):
        n = s.shape[-1]
        return (s.reshape(NSA_R, tq, n) + mask).reshape(NSA_R * tq, n)

    def normalise(acc):
        return acc[:, :NSA_DK] / acc[:, ONES_LANE:ONES_LANE + 1]

    def scores(q, k):
        half = q.shape[0] // 2
        return jnp.concatenate([lax.dot_general(q[:half], k, NT_DIMS, preferred_element_type=F32),
                                lax.dot_general(q[half:], k, NT_DIMS, preferred_element_type=F32)], axis=0)

    def weighted_values(p, v):
        half = p.shape[0] // 2
        p = p.astype(BF16)
        return jnp.concatenate([jnp.dot(p[:half], v, preferred_element_type=F32),
                                jnp.dot(p[half:], v, preferred_element_type=F32)], axis=0)

    def query_rows(g):
        sel = sel_ref[0, g].astype(F32)
        rows = []
        for r in range(NSA_R):
            hh = g * NSA_R + r
            qh = q_ref[0, :, (hh // 2) * LANES:(hh // 2 + 1) * LANES].astype(F32)
            if hh % 2:
                qh = jnp.concatenate([qh[:, NSA_DK:], qh[:, :NSA_DK]], axis=1)
            slope = 2.0 ** -(hh + 1)
            extra = jnp.where(lane == POS_HI_LANE, SEL_BLOCK * slope,
                              jnp.where(lane == POS_LO_LANE, slope, jnp.where(lane == PAD_LANE, -1.0, sel)))
            rows.append(jnp.where(is_head, qh, extra).astype(BF16))
        return jnp.concatenate(rows, axis=0)

    def tile(n_past):
        past = n_past * KC
        for g in range(NSA_GROUPS):
            q = query_rows(g)
            s = masked_rows(scores(q, kwx[g, pl.ds(t0, WL), :]), wmask)
            p = jnp.exp(s - jnp.max(s, axis=-1, keepdims=True))
            o_win = normalise(weighted_values(p, vwx[g, pl.ds(t0, WL), :]))

            s_own = masked_rows(scores(q, ksx[g, past:past + KC, :]), dmask)
            m = jnp.max(s_own, axis=-1, keepdims=True)
            if n_past:
                s_past = scores(q, ksx[g, 0:past, :])
                m = jnp.maximum(m, jnp.max(s_past, axis=-1, keepdims=True))
            acc = weighted_values(jnp.exp(s_own - m), vsx[g, past:past + KC, :])
            if n_past:
                acc = acc + weighted_values(jnp.exp(s_past - m), vsx[g, 0:past, :])
            o_sel = normalise(acc)

            for r in range(NSA_R):
                hh = g * NSA_R + r
                g_sel = gate[:, NSA_HEADS + hh:NSA_HEADS + hh + 1]
                g_win = gate[:, 2 * NSA_HEADS + hh:2 * NSA_HEADS + hh + 1]
                piece = g_sel * o_sel[r * tq:(r + 1) * tq] + g_win * o_win[r * tq:(r + 1) * tq]
                o_ref[0, :, hh * NSA_DK:(hh + 1) * NSA_DK] = (
                    ocmp_ref[0, :, hh * NSA_DK:(hh + 1) * NSA_DK] + piece).astype(BF16)

    for n_past in range(S // KC):
        pl.when(diag_chunk == n_past)(functools.partial(tile, n_past))


def _selwin(pb3, sel, pf3, ocmp, tq, KC):
    B, S, _ = pb3.shape
    HD = NSA_HEADS * NSA_DK
    WL = WINDOW + tq
    tr = jnp.arange(tq)[:, None]
    kc = jnp.arange(WL)[None, :]
    wmask = jnp.where((kc > tr) & (kc <= tr + WINDOW), 0.0, NEG).astype(F32)
    off = (jnp.arange(KC // tq) * tq)[:, None, None]
    dmask = jnp.where(jnp.arange(KC)[None, None, :] <= off + tr[None], 0.0, NEG).astype(F32)
    kv_spec = lambda off: pl.BlockSpec((1, S, LANES), lambda b, i: (b, 0, off // LANES))
    return pl.pallas_call(
        functools.partial(_selwin_kernel, tq=tq, KC=KC, S=S),
        grid=(B, S // tq),
        in_specs=[pl.BlockSpec((1, tq, HD), lambda b, i: (b, i, PB_NQ // HD)),
                  kv_spec(PB_KS), kv_spec(PB_VS), kv_spec(PB_KW), kv_spec(PB_VW),
                  pl.BlockSpec((1, NSA_GROUPS, tq, LANES), lambda b, i: (b, 0, i, 0)),
                  pl.BlockSpec((tq, WL), lambda b, i: (0, 0)),
                  pl.BlockSpec((KC // tq, tq, KC), lambda b, i: (0, 0, 0)),
                  pl.BlockSpec((1, tq, LANES), lambda b, i: (b, i, PF_NG // LANES)),
                  pl.BlockSpec((1, tq, HD), lambda b, i: (b, i, 0))],
        out_specs=pl.BlockSpec((1, tq, HD), lambda b, i: (b, i, 0)),
        out_shape=jax.ShapeDtypeStruct((B, S, HD), BF16),
        scratch_shapes=[pltpu.VMEM((NSA_GROUPS, S, LANES), BF16), pltpu.VMEM((NSA_GROUPS, S, LANES), BF16),
                        pltpu.VMEM((NSA_GROUPS, WINDOW + S, LANES), BF16),
                        pltpu.VMEM((NSA_GROUPS, WINDOW + S, LANES), BF16)],
        compiler_params=_params("arbitrary", "arbitrary"),
        name="selwin",
    )(pb3, pb3, pb3, pb3, pb3, sel, wmask, dmask, pf3, ocmp)


SUBLANES = 8


def _mix_ffn_kernel(x_ref, ya_ref, yb_ref, ga_ref, gb_ref, wa_ref, wb_ref, wo_ref, gn_ref,
                    wup_ref, cw_ref, cb_ref, wd_ref, gf_ref, o_ref, tail_ref, u_ref, *, tm, tf):
    @pl.when(pl.program_id(1) == 0)
    def _():
        tail_ref[...] = jnp.zeros_like(tail_ref)

    y_a = jnp.dot(ya_ref[0], wa_ref[...], preferred_element_type=F32)
    y_b = jnp.dot(yb_ref[0], wb_ref[...], preferred_element_type=F32)
    merged = (jax.nn.sigmoid(ga_ref[0].astype(F32)) * y_a + jax.nn.sigmoid(gb_ref[0].astype(F32)) * y_b)
    x1 = x_ref[0] + jnp.dot(merged.astype(BF16), wo_ref[...], preferred_element_type=F32)
    o_ref[0] = x1
    h = _rms(x1, gn_ref[...]).astype(BF16)

    top = lax.broadcasted_iota(jnp.int32, (SUBLANES, tf), 0)
    for c in range(D_FF // tf):
        cols = slice(c * tf, (c + 1) * tf)
        a = jnp.dot(h, wup_ref[:, cols], preferred_element_type=F32)
        b = jnp.dot(h, wup_ref[:, D_FF + c * tf:D_FF + (c + 1) * tf], preferred_element_type=F32)
        tail = tail_ref[c]
        tail_ref[c] = a[tm - SUBLANES:]
        ac = cb_ref[:, cols] + cw_ref[CONV_W - 1:CONV_W, cols] * a
        for d in range(1, CONV_W):
            sh = pltpu.roll(a, d, axis=0)
            head = jnp.where(top < d, pltpu.roll(tail, d, axis=0), sh[:SUBLANES])
            sh = jnp.concatenate([head, sh[SUBLANES:]], axis=0)
            ac = ac + cw_ref[CONV_W - 1 - d:CONV_W - d, cols] * sh
        u_ref[:, cols] = (jax.nn.gelu(ac) * b).astype(BF16)
    y = jnp.dot(u_ref[...], wd_ref[...], preferred_element_type=F32)
    o_ref[0] = _rms(o_ref[0] + y, gf_ref[...])


def _mix_ffn(x, ya, yb, pb3, wa, wb, wo, gn, w_up, conv_w, conv_b, w_down, gf, tm, tf):
    B, S, _ = x.shape
    HV = RET_HEADS * RET_DV
    HD = NSA_HEADS * NSA_DK
    rows = lambda width, col=0: pl.BlockSpec((1, tm, width), lambda b, i: (b, i, col))
    whole = lambda *shape: pl.BlockSpec(shape, lambda b, i: (0,) * len(shape), pipeline_mode=pl.Buffered(1))
    return pl.pallas_call(
        functools.partial(_mix_ffn_kernel, tm=tm, tf=tf),
        grid=(B, S // tm),
        in_specs=[rows(D_MODEL), rows(HV), rows(HD),
                  rows(D_MODEL, PB_GA // D_MODEL), rows(D_MODEL, PB_GB // D_MODEL),
                  whole(HV, D_MODEL), whole(HD, D_MODEL), whole(D_MODEL, D_MODEL), whole(1, D_MODEL),
                  whole(D_MODEL, 2 * D_FF), whole(CONV_W, D_FF), whole(1, D_FF), whole(D_FF, D_MODEL),
                  whole(1, D_MODEL)],
        out_specs=rows(D_MODEL),
        out_shape=jax.ShapeDtypeStruct((B, S, D_MODEL), F32),
        scratch_shapes=[pltpu.VMEM((D_FF // tf, SUBLANES, tf), F32), pltpu.VMEM((tm, D_FF), BF16)],
        compiler_params=_params("arbitrary", "arbitrary"),
        name="mix_ffn",
    )(x, ya, yb, pb3, pb3, wa, wb, wo, gn, w_up, conv_w, conv_b, w_down, gf)


def _block_diag_groups(w):
    z = jnp.zeros_like(w)
    return jnp.concatenate([jnp.concatenate([w, z], axis=-1), jnp.concatenate([z, w], axis=-1)], axis=-2)


def kernel(x, norm_mix, w_in, ret_gn_g, cmp_pos_k, cmp_w1_k, cmp_b1_k, cmp_w2_k, cmp_pos_v, cmp_w1_v,
           cmp_b1_v, cmp_w2_v, w_ret_o, w_nsa_o, w_out, norm_ffn, w_up, conv_w, conv_b, w_down,
           norm_final):
    B, S, D = x.shape
    assert D == D_MODEL and NSA_GROUPS == 2 and norm_mix.shape[0] == 1
    assert S % 512 == 0 and S >= 1024 and S // CMP_STRIDE <= LANES and S // SEL_BLOCK <= SEL_LANES
    M = B * S
    x2 = x.reshape(M, D)

    w_bf = w_in[0].astype(BF16)
    w_gates = w_bf[:, W_GA:]

    def cmp_weights(pos, w1, b1, w2):
        return (jnp.tile(pos, (1, NSA_GROUPS)),
                _block_diag_groups(w1.reshape(CMP_BLOCK, NSA_DK, CMP_HIDDEN)).astype(BF16),
                jnp.tile(b1, NSA_GROUPS)[None, :],
                _block_diag_groups(w2).astype(BF16))

    posk, w1k, b1k, w2k = cmp_weights(cmp_pos_k[0], cmp_w1_k[0], cmp_b1_k[0], cmp_w2_k[0])
    posv, w1v, b1v, w2v = cmp_weights(cmp_pos_v[0], cmp_w1_v[0], cmp_b1_v[0], cmp_w2_v[0])

    pb, pf, ya = _inproj_ret(x2, norm_mix, w_bf, w_gates, ret_gn_g, S, tm=512, C=256)
    pb3 = pb.reshape(B, S, PB_W)
    pf3 = pf.reshape(B, S, PF_W)
    ya = ya.reshape(B, S, -1)

    kcmp, vcmp = _compress(pf3, posk, posv, w1k, w1v, b1k, b1v, w2k, w2v)
    ocmp, sel = _cmpattn(pb3, kcmp, vcmp, pf3, tq=1024)
    yb = _selwin(pb3, sel, pf3, ocmp, tq=256, KC=512)

    return _mix_ffn(x, ya, yb, pb3, w_ret_o[0].astype(BF16), w_nsa_o[0].astype(BF16),
                    w_out[0].astype(BF16), norm_ffn, w_up[0].astype(BF16), conv_w[0], conv_b,
                    w_down[0].astype(BF16), norm_final[None, :], tm=512, tf=256)
```

```python
import functools
import math

import numpy as np
import jax
import jax.numpy as jnp
from jax import lax
from jax.experimental import pallas as pl
from jax.experimental.pallas import tpu as pltpu

F32 = jnp.float32
BF16 = jnp.bfloat16

D_MODEL = 1024
RET_HEADS = 4
RET_DK = 128
RET_DV = 256
NSA_HEADS = 8
NSA_GROUPS = 2
NSA_R = NSA_HEADS // NSA_GROUPS
NSA_DK = 64
CMP_BLOCK = 32
CMP_STRIDE = 16
CMP_HIDDEN = 256
SEL_BLOCK = 64
SEL_TOPK = 8
WINDOW = 512
D_FF = 2816
CONV_W = 3
EPS = 1e-6
NEG = -1e30
FORCE = 1e9
LOWEST = -3e38

IN_SIZES = (RET_HEADS * RET_DK, RET_HEADS * RET_DK, RET_HEADS * RET_DV, RET_HEADS * RET_DV,
            NSA_HEADS * NSA_DK,
            NSA_GROUPS * NSA_DK, NSA_GROUPS * NSA_DK, NSA_GROUPS * NSA_DK,
            NSA_GROUPS * NSA_DK, NSA_GROUPS * NSA_DK, NSA_GROUPS * NSA_DK,
            3 * NSA_HEADS, D_MODEL, D_MODEL)

LANES = 128
VMEM_LIMIT = 56 * 1024 * 1024

_W_OFF = np.concatenate([[0], np.cumsum(IN_SIZES)]).tolist()
W_NQ, W_KCR, W_KS, W_NG, W_GA = _W_OFF[4], _W_OFF[5], _W_OFF[7], _W_OFF[11], _W_OFF[12]

RT_RQ, RT_RK, RT_RV, RT_RG = 0, 512, 1024, 2048
RT_W = 3072
PB_NQ, PB_KS, PB_VS, PB_KW, PB_VW, PB_GA, PB_GB = 0, 512, 640, 768, 896, 1024, 2048
PB_W = 3072
PF_KCR, PF_VCR, PF_NG = 0, 128, 256
PF_W = 384

NT_DIMS = (((1,), (1,)), ((), ()))
TN_DIMS = (((0,), (0,)), ((), ()))


def _params(*sem):
    return pltpu.CompilerParams(dimension_semantics=sem, vmem_limit_bytes=VMEM_LIMIT)


def _rms(x, g):
    return x * lax.rsqrt(jnp.mean(x * x, axis=-1, keepdims=True) + EPS) * g


def _split3(x):
    hi = x.astype(BF16)
    rest = x - hi.astype(F32)
    mid = rest.astype(BF16)
    return hi, mid, (rest - mid.astype(F32)).astype(BF16)


def _inproj_ret_kernel(x_ref, g_ref, w_ref, wg_ref, gn_ref, pb_ref, pf_ref, ya_ref,
                       stash0, stash1, r_ref, dmask_ref, xi_ref, zeta_ref, *, tm, C, S):
    log_gamma = [float(np.log1p(-np.exp2(np.float32(-5.0 - h)))) for h in range(RET_HEADS)]
    scale = RET_DK ** -0.5
    i = pl.program_id(0)

    @pl.when(i == 0)
    def _():
        diff = (lax.broadcasted_iota(jnp.int32, (C, C), 0)
                - lax.broadcasted_iota(jnp.int32, (C, C), 1)).astype(F32)
        pos_v = lax.broadcasted_iota(jnp.int32, (C, RET_DV), 0).astype(F32)
        pos_k = lax.broadcasted_iota(jnp.int32, (C, RET_DK), 0).astype(F32)
        for h in range(RET_HEADS):
            lg = log_gamma[h]
            dmask_ref[h] = jnp.where(diff >= 0, jnp.exp(lg * jnp.maximum(diff, 0.0)), 0.0) * scale
            xi_ref[h] = jnp.exp(lg * (pos_v + 1.0))
            zeta_ref[h] = jnp.exp(lg * (C - 1.0 - pos_k)) * scale
        stash1[...] = jnp.zeros_like(stash1)
        r_ref[...] = jnp.zeros_like(r_ref)

    seq_start = lax.rem(i + (S // tm) - 1, S // tm) == 0

    def step(new_ref, old_ref):
        hb = _rms(x_ref[...], g_ref[...]).astype(BF16)
        for c0 in range(0, RT_W, 512):
            new_ref[:, c0:c0 + 512] = jnp.dot(
                hb, w_ref[:, c0:c0 + 512], preferred_element_type=F32).astype(BF16)
        def project(w, c0, c1):
            return jnp.dot(hb, w[:, c0:c1], preferred_element_type=F32)

        pb_ref[:, PB_NQ:PB_KS] = (project(w_ref, W_NQ, W_KCR) * (NSA_DK ** -0.5)).astype(BF16)
        pb_ref[:, PB_KS:PB_GA] = project(w_ref, W_KS, W_NG).astype(BF16)
        for c0 in range(0, PB_W - PB_GA, 512):
            pb_ref[:, PB_GA + c0:PB_GA + c0 + 512] = project(wg_ref, c0, c0 + 512).astype(BF16)
        pf_ref[:, PF_KCR:PF_NG] = project(w_ref, W_KCR, W_KS)
        pf_ref[:, PF_NG:PF_W] = project(w_ref, W_NG, W_NG + PF_W - PF_NG)

        for cc in range(tm // C):
            rows = slice(cc * C, (cc + 1) * C)
            for h in range(RET_HEADS):
                lg = log_gamma[h]
                q = old_ref[rows, RT_RQ + h * RET_DK:RT_RQ + (h + 1) * RET_DK]
                k = old_ref[rows, RT_RK + h * RET_DK:RT_RK + (h + 1) * RET_DK]
                v = old_ref[rows, RT_RV + h * RET_DV:RT_RV + (h + 1) * RET_DV]
                g = old_ref[rows, RT_RG + h * RET_DV:RT_RG + (h + 1) * RET_DV].astype(F32)
                s = lax.dot_general(q, k, NT_DIMS, preferred_element_type=F32)
                o = jnp.dot((s * dmask_ref[h]).astype(BF16), v, preferred_element_type=F32)
                state = r_ref[h]
                if cc == 0:
                    state = jnp.where(seq_start, 0.0, state)
                o = o + xi_ref[h] * jnp.dot(q, state.astype(BF16), preferred_element_type=F32)
                kz = (k.astype(F32) * zeta_ref[h]).astype(BF16)
                kv = lax.dot_general(kz, v, TN_DIMS, preferred_element_type=F32)
                r_ref[h] = math.exp(lg * C) * state + kv
                mu = jnp.mean(o, axis=-1, keepdims=True)
                d = o - mu
                var = jnp.mean(d * d, axis=-1, keepdims=True)
                on = d * lax.rsqrt(var + EPS) * gn_ref[:, h * RET_DV:(h + 1) * RET_DV]
                ya_ref[rows, h * RET_DV:(h + 1) * RET_DV] = (g * jax.nn.sigmoid(g) * on).astype(BF16)

    pl.when(lax.rem(i, 2) == 0)(functools.partial(step, stash0, stash1))
    pl.when(lax.rem(i, 2) == 1)(functools.partial(step, stash1, stash0))


def _inproj_ret(x2, g, w, w_gates, gn_g, S, tm, C):
    M = x2.shape[0]
    nt = M // tm
    HV = RET_HEADS * RET_DV
    cur = lambda i: (jnp.minimum(i, nt - 1), 0)
    prev = lambda i: (jnp.maximum(i - 1, 0), 0)
    const = lambda i: (0, 0)
    return pl.pallas_call(
        functools.partial(_inproj_ret_kernel, tm=tm, C=C, S=S),
        grid=(nt + 1,),
        in_specs=[pl.BlockSpec((tm, D_MODEL), cur),
                  pl.BlockSpec((1, D_MODEL), const),
                  pl.BlockSpec(w.shape, const, pipeline_mode=pl.Buffered(1)),
                  pl.BlockSpec(w_gates.shape, const, pipeline_mode=pl.Buffered(1)),
                  pl.BlockSpec((1, HV), const)],
        out_specs=[pl.BlockSpec((tm, PB_W), cur), pl.BlockSpec((tm, PF_W), cur), pl.BlockSpec((tm, HV), prev)],
        out_shape=[jax.ShapeDtypeStruct((M, PB_W), BF16), jax.ShapeDtypeStruct((M, PF_W), F32),
                   jax.ShapeDtypeStruct((M, HV), BF16)],
        scratch_shapes=[pltpu.VMEM((tm, RT_W), BF16), pltpu.VMEM((tm, RT_W), BF16),
                        pltpu.VMEM((RET_HEADS, RET_DK, RET_DV), F32), pltpu.VMEM((RET_HEADS, C, C), F32),
                        pltpu.VMEM((RET_HEADS, C, RET_DV), F32), pltpu.VMEM((RET_HEADS, C, RET_DK), F32)],
        compiler_params=_params("arbitrary"),
        name="inproj_ret",
    )(x2, g, w, w_gates, gn_g)


def _compress_kernel(kin_ref, vin_ref, posk_ref, posv_ref, w1k_ref, w1v_ref, b1k_ref, b1v_ref,
                     w2k_ref, w2v_ref, ko_ref, vo_ref, *, NP):
    GD = NSA_GROUPS * NSA_DK
    npc = CMP_BLOCK // CMP_STRIDE
    row = lax.broadcasted_iota(jnp.int32, (NP, GD), 0)
    for in_ref, pos_ref, w1_ref, b1_ref, w2_ref, out_ref in (
            (kin_ref, posk_ref, w1k_ref, b1k_ref, w2k_ref, ko_ref),
            (vin_ref, posv_ref, w1v_ref, b1v_ref, w2v_ref, vo_ref)):
        parts = []
        for p in range(npc):
            acc = jnp.zeros((NP, NSA_GROUPS * CMP_HIDDEN), F32)
            for i in range(0, CMP_STRIDE, 2):
                j = p * CMP_STRIDE + i
                toks = [(in_ref[0, pl.ds(i + d, NP, stride=CMP_STRIDE), :] + pos_ref[j + d:j + d + 1, :])
                        .astype(BF16) for d in range(2)]
                acc = acc + jnp.dot(jnp.concatenate(toks, axis=1), w1_ref[j:j + 2].reshape(2 * GD, -1),
                                    preferred_element_type=F32)
            parts.append(acc)
        hidden = parts[0]
        for p in range(1, npc):
            hidden = hidden + pltpu.roll(parts[p], NP - p, axis=0)
        act = jax.nn.gelu(hidden + b1_ref[...]).astype(BF16)
        res = jnp.dot(act, w2_ref[...], preferred_element_type=F32)
        res = jnp.where(row < NP - (npc - 1), res, 0.0)
        if out_ref is vo_ref:
            res = res.T
            for g in range(NSA_GROUPS):
                out_ref[0, g] = res[g * NSA_DK:(g + 1) * NSA_DK, :]
        else:
            for g in range(NSA_GROUPS):
                out_ref[0, g] = res[:, g * NSA_DK:(g + 1) * NSA_DK]


def _compress(pf3, posk, posv, w1k, w1v, b1k, b1v, w2k, w2v):
    B, S, _ = pf3.shape
    NP = S // CMP_STRIDE
    GD = NSA_GROUPS * NSA_DK
    GH = NSA_GROUPS * CMP_HIDDEN
    const2 = lambda b: (0, 0)
    const3 = lambda b: (0, 0, 0)
    k_shape = (NSA_GROUPS, NP, NSA_DK)
    v_shape = (NSA_GROUPS, NSA_DK, NP)
    return pl.pallas_call(
        functools.partial(_compress_kernel, NP=NP),
        grid=(B,),
        in_specs=[pl.BlockSpec((1, S, GD), lambda b: (b, 0, PF_KCR // GD)),
                  pl.BlockSpec((1, S, GD), lambda b: (b, 0, PF_VCR // GD)),
                  pl.BlockSpec((CMP_BLOCK, GD), const2), pl.BlockSpec((CMP_BLOCK, GD), const2),
                  pl.BlockSpec((CMP_BLOCK, GD, GH), const3), pl.BlockSpec((CMP_BLOCK, GD, GH), const3),
                  pl.BlockSpec((1, GH), const2), pl.BlockSpec((1, GH), const2),
                  pl.BlockSpec((GH, GD), const2), pl.BlockSpec((GH, GD), const2)],
        out_specs=[pl.BlockSpec((1,) + k_shape, lambda b: (b, 0, 0, 0)),
                   pl.BlockSpec((1,) + v_shape, lambda b: (b, 0, 0, 0))],
        out_shape=[jax.ShapeDtypeStruct((B,) + k_shape, F32), jax.ShapeDtypeStruct((B,) + v_shape, F32)],
        compiler_params=_params("arbitrary"),
        name="compress",
    )(pf3, pf3, posk, posv, w1k, w1v, b1k, b1v, w2k, w2v)


SEL_LANE0 = 64
SEL_LANES = 32
POS_HI_LANE, POS_LO_LANE, PAD_LANE = 96, 97, 98
ONES_LANE = 64
MASK_BIG = 1e30


def _cmpattn_kernel(q_ref, kc_ref, vct_ref, gate_ref, ocmp_ref, sel_ref, *, tq, NP, NS):
    NC = NP - (CMP_BLOCK // CMP_STRIDE - 1)
    tile = pl.program_id(1)

    def run(rows):
        _cmpattn_tile(q_ref, kc_ref, vct_ref, gate_ref, ocmp_ref, sel_ref, tile * tq,
                      tq=tq, NP=rows, NC=NC, NS=NS)

    for v in range(pl.cdiv(NP * CMP_STRIDE, tq)):
        pl.when(tile == v)(functools.partial(run, min(NP, (v + 1) * tq // CMP_STRIDE)))


def _cmpattn_tile(q_ref, kc_ref, vct_ref, gate_ref, ocmp_ref, sel_ref, t0, *, tq, NP, NC, NS):
    tcol = t0 + lax.broadcasted_iota(jnp.int32, (NP, tq), 1)
    cidx = lax.broadcasted_iota(jnp.int32, (NP, tq), 0)
    visible = (tcol >= cidx * CMP_STRIDE + (CMP_BLOCK - 1)) & (cidx < NC)
    mask_add = jnp.where(visible, 0.0, NEG)
    block_end = (lax.broadcasted_iota(jnp.int32, (NP, 1), 0) * CMP_STRIDE + (CMP_BLOCK - 1)).astype(F32)
    sees_any = ((t0 + lax.broadcasted_iota(jnp.int32, (1, tq), 1)) >= CMP_BLOCK - 1).astype(F32)
    jj = lax.broadcasted_iota(jnp.int32, (SEL_LANES, NP), 0)
    cc = lax.broadcasted_iota(jnp.int32, (SEL_LANES, NP), 1)
    overlap_t = ((cc * CMP_STRIDE < jj * SEL_BLOCK + SEL_BLOCK)
                 & (cc * CMP_STRIDE + CMP_BLOCK > jj * SEL_BLOCK) & (jj < NS)).astype(BF16)
    blk = lax.broadcasted_iota(jnp.int32, (SEL_LANES, tq), 0)
    blk_f = blk.astype(F32)
    is_blk = blk < NS
    cur = jnp.right_shift(t0 + lax.broadcasted_iota(jnp.int32, (SEL_LANES, tq), 1),
                          SEL_BLOCK.bit_length() - 1)
    o_t = []
    for g in range(NSA_GROUPS):
        kc_parts = _split3(kc_ref[0, g, :NP, :])
        vct = vct_ref[0, g, :, :NP].astype(BF16)
        psum = jnp.zeros((NP, tq), F32)
        for r in range(NSA_R):
            hh = g * NSA_R + r
            q = q_ref[0, :, hh * NSA_DK:(hh + 1) * NSA_DK]
            s = sum(lax.dot_general(part, q, NT_DIMS, preferred_element_type=F32) for part in kc_parts)
            s = s + (2.0 ** -(hh + 1)) * block_end + mask_add
            e = jnp.exp(s - jnp.max(s, axis=0, keepdims=True))
            p = e * (sees_any / jnp.sum(e, axis=0, keepdims=True))
            o_t.append(jnp.dot(vct, p.astype(BF16), preferred_element_type=F32))
            psum = psum + p
        imp = sum(jnp.dot(overlap_t, part, preferred_element_type=F32) for part in _split3(psum))
        imp = jnp.where((blk == 0) | (blk == cur) | (blk == cur - 1), FORCE, imp)
        imp = jnp.where(blk > cur, -FORCE, imp)
        work = jnp.where(is_blk, imp, LOWEST)
        sel = jnp.zeros((SEL_LANES, tq), F32)
        for _ in range(min(SEL_TOPK, NS)):
            top = jnp.max(work, axis=0, keepdims=True)
            first = jnp.min(jnp.where(work == top, blk_f, float(SEL_LANES)), axis=0, keepdims=True)
            pick = blk_f == first
            sel = jnp.where(pick, 1.0, sel)
            work = jnp.where(pick, LOWEST, work)
        sel = jnp.where(is_blk & (blk <= cur), sel - 1.0, jnp.where(is_blk, -1.0, 0.0))
        placed = jnp.concatenate([jnp.zeros((SEL_LANE0, tq), F32), sel,
                                  jnp.zeros((LANES - SEL_LANE0 - SEL_LANES, tq), F32)], axis=0)
        sel_ref[0, g] = placed.T.astype(BF16)
    o = jnp.concatenate(o_t, axis=0).T
    gate = jax.nn.sigmoid(gate_ref[0])
    for hh in range(NSA_HEADS):
        cols = slice(hh * NSA_DK, (hh + 1) * NSA_DK)
        ocmp_ref[0, :, cols] = gate[:, hh:hh + 1] * o[:, cols]


def _cmpattn(pb3, kcmp, vcmp, pf3, tq):
    B, S, _ = pb3.shape
    NP = S // CMP_STRIDE
    NS = S // SEL_BLOCK
    HD = NSA_HEADS * NSA_DK
    return pl.pallas_call(
        functools.partial(_cmpattn_kernel, tq=tq, NP=NP, NS=NS),
        grid=(B, S // tq),
        in_specs=[pl.BlockSpec((1, tq, HD), lambda b, i: (b, i, PB_NQ // HD)),
                  pl.BlockSpec((1, NSA_GROUPS, NP, NSA_DK), lambda b, i: (b, 0, 0, 0)),
                  pl.BlockSpec((1, NSA_GROUPS, NSA_DK, NP), lambda b, i: (b, 0, 0, 0)),
                  pl.BlockSpec((1, tq, LANES), lambda b, i: (b, i, PF_NG // LANES))],
        out_specs=[pl.BlockSpec((1, tq, HD), lambda b, i: (b, i, 0)),
                   pl.BlockSpec((1, NSA_GROUPS, tq, LANES), lambda b, i: (b, 0, i, 0))],
        out_shape=[jax.ShapeDtypeStruct((B, S, HD), F32),
                   jax.ShapeDtypeStruct((B, NSA_GROUPS, S, LANES), BF16)],
        compiler_params=_params("arbitrary", "arbitrary"),
        name="cmpattn",
    )(pb3, kcmp, vcmp, pf3)


def _selwin_kernel(q_ref, ks_ref, vs_ref, kw_ref, vw_ref, sel_ref, wmask_ref, dmask_ref, gate_ref, ocmp_ref,
                   o_ref, ksx, vsx, kwx, vwx, *, tq, KC, S):
    WL = WINDOW + tq
    RB = 512
    i = pl.program_id(1)
    t0 = pl.multiple_of(i * tq, tq)
    sel_shift = SEL_BLOCK.bit_length() - 1

    @pl.when(i == 0)
    def _build():
        lane = lax.broadcasted_iota(jnp.int32, (RB, LANES), 1)
        is_head = lane < NSA_DK
        pad_k = jnp.where(lane == PAD_LANE, MASK_BIG, 0.0).astype(BF16)
        for g in range(NSA_GROUPS):
            kwx[g, 0:WINDOW, :] = pad_k[:WINDOW]
            vwx[g, 0:WINDOW, :] = jnp.zeros((WINDOW, LANES), BF16)
        for piece in range(S // RB):
            r0 = piece * RB
            pos = r0 + lax.broadcasted_iota(jnp.int32, (RB, LANES), 0)
            alibi = jnp.where(lane == POS_HI_LANE, jnp.right_shift(pos, sel_shift),
                              jnp.where(lane == POS_LO_LANE, pos & (SEL_BLOCK - 1), 0)).astype(F32)
            onehot = jnp.where(lane - SEL_LANE0 == jnp.right_shift(pos, sel_shift), MASK_BIG, 0.0)
            k_extra_win = alibi.astype(BF16)
            k_extra_sel = (alibi + onehot).astype(BF16)
            v_extra = jnp.where(lane == ONES_LANE, 1.0, 0.0).astype(BF16)
            for g in range(NSA_GROUPS):
                def head_lanes(ref):
                    t = ref[0, r0:r0 + RB, :]
                    return t if g == 0 else jnp.concatenate([t[:, NSA_DK:], t[:, :NSA_DK]], axis=1)
                ksx[g, r0:r0 + RB, :] = jnp.where(is_head, head_lanes(ks_ref), k_extra_sel)
                vsx[g, r0:r0 + RB, :] = jnp.where(is_head, head_lanes(vs_ref), v_extra)
                kwx[g, WINDOW + r0:WINDOW + r0 + RB, :] = jnp.where(is_head, head_lanes(kw_ref), k_extra_win)
                vwx[g, WINDOW + r0:WINDOW + r0 + RB, :] = jnp.where(is_head, head_lanes(vw_ref), v_extra)

    gate = jax.nn.sigmoid(gate_ref[0])
    lane = lax.broadcasted_iota(jnp.int32, (tq, LANES), 1)
    is_head = lane < NSA_DK
    diag_chunk = t0 // KC
    wmask = wmask_ref[...][None]
    dmask = dmask_ref[i % (KC // tq)][None]

    def masked_rows(s, mask):
        n = s.shape[-1]
        return (s.reshape(NSA_R, tq, n) + mask).reshape(NSA_R * tq, n)

    def normalise(acc):
        return acc[:, :NSA_DK] / acc[:, ONES_LANE:ONES_LANE + 1]

    def scores(q, k):
        half = q.shape[0] // 2
        return jnp.concatenate([lax.dot_general(q[:half], k, NT_DIMS, preferred_element_type=F32),
                                lax.dot_general(q[half:], k, NT_DIMS, preferred_element_type=F32)], axis=0)

    def weighted_values(p, v):
        half = p.shape[0] // 2
        p = p.astype(BF16)
        return jnp.concatenate([jnp.dot(p[:half], v, preferred_element_type=F32),
                                jnp.dot(p[half:], v, preferred_element_type=F32)], axis=0)

    def query_rows(g):
        sel = sel_ref[0, g].astype(F32)
        rows = []
        for r in range(NSA_R):
            hh = g * NSA_R + r
            qh = q_ref[0, :, (hh // 2) * LANES:(hh // 2 + 1) * LANES].astype(F32)
            if hh % 2:
                qh = jnp.concatenate([qh[:, NSA_DK:], qh[:, :NSA_DK]], axis=1)
            slope = 2.0 ** -(hh + 1)
            extra = jnp.where(lane == POS_HI_LANE, SEL_BLOCK * slope,
                              jnp.where(lane == POS_LO_LANE, slope, jnp.where(lane == PAD_LANE, -1.0, sel)))
            rows.append(jnp.where(is_head, qh, extra).astype(BF16))
        return jnp.concatenate(rows, axis=0)

    def tile(n_past):
        past = n_past * KC
        for g in range(NSA_GROUPS):
            q = query_rows(g)
            s = masked_rows(scores(q, kwx[g, pl.ds(t0, WL), :]), wmask)
            p = jnp.exp(s - jnp.max(s, axis=-1, keepdims=True))
            o_win = normalise(weighted_values(p, vwx[g, pl.ds(t0, WL), :]))

            s_own = masked_rows(scores(q, ksx[g, past:past + KC, :]), dmask)
            m = jnp.max(s_own, axis=-1, keepdims=True)
            if n_past:
                s_past = scores(q, ksx[g, 0:past, :])
                m = jnp.maximum(m, jnp.max(s_past, axis=-1, keepdims=True))
            acc = weighted_values(jnp.exp(s_own - m), vsx[g, past:past + KC, :])
            if n_past:
                acc = acc + weighted_values(jnp.exp(s_past - m), vsx[g, 0:past, :])
            o_sel = normalise(acc)

            for r in range(NSA_R):
                hh = g * NSA_R + r
                g_sel = gate[:, NSA_HEADS + hh:NSA_HEADS + hh + 1]
                g_win = gate[:, 2 * NSA_HEADS + hh:2 * NSA_HEADS + hh + 1]
                piece = g_sel * o_sel[r * tq:(r + 1) * tq] + g_win * o_win[r * tq:(r + 1) * tq]
                o_ref[0, :, hh * NSA_DK:(hh + 1) * NSA_DK] = (
                    ocmp_ref[0, :, hh * NSA_DK:(hh + 1) * NSA_DK] + piece).astype(BF16)

    for n_past in range(S // KC):
        pl.when(diag_chunk == n_past)(functools.partial(tile, n_past))


def _selwin(pb3, sel, pf3, ocmp, tq, KC):
    B, S, _ = pb3.shape
    HD = NSA_HEADS * NSA_DK
    WL = WINDOW + tq
    tr = jnp.arange(tq)[:, None]
    kc = jnp.arange(WL)[None, :]
    wmask = jnp.where((kc > tr) & (kc <= tr + WINDOW), 0.0, NEG).astype(F32)
    off = (jnp.arange(KC // tq) * tq)[:, None, None]
    dmask = jnp.where(jnp.arange(KC)[None, None, :] <= off + tr[None], 0.0, NEG).astype(F32)
    kv_spec = lambda off: pl.BlockSpec((1, S, LANES), lambda b, i: (b, 0, off // LANES))
    return pl.pallas_call(
        functools.partial(_selwin_kernel, tq=tq, KC=KC, S=S),
        grid=(B, S // tq),
        in_specs=[pl.BlockSpec((1, tq, HD), lambda b, i: (b, i, PB_NQ // HD)),
                  kv_spec(PB_KS), kv_spec(PB_VS), kv_spec(PB_KW), kv_spec(PB_VW),
                  pl.BlockSpec((1, NSA_GROUPS, tq, LANES), lambda b, i: (b, 0, i, 0)),
                  pl.BlockSpec((tq, WL), lambda b, i: (0, 0)),
                  pl.BlockSpec((KC // tq, tq, KC), lambda b, i: (0, 0, 0)),
                  pl.BlockSpec((1, tq, LANES), lambda b, i: (b, i, PF_NG // LANES)),
                  pl.BlockSpec((1, tq, HD), lambda b, i: (b, i, 0))],
        out_specs=pl.BlockSpec((1, tq, HD), lambda b, i: (b, i, 0)),
        out_shape=jax.ShapeDtypeStruct((B, S, HD), BF16),
        scratch_shapes=[pltpu.VMEM((NSA_GROUPS, S, LANES), BF16), pltpu.VMEM((NSA_GROUPS, S, LANES), BF16),
                        pltpu.VMEM((NSA_GROUPS, WINDOW + S, LANES), BF16),
                        pltpu.VMEM((NSA_GROUPS, WINDOW + S, LANES), BF16)],
        compiler_params=_params("arbitrary", "arbitrary"),
        name="selwin",
    )(pb3, pb3, pb3, pb3, pb3, sel, wmask, dmask, pf3, ocmp)


SUBLANES = 8


def _mix_ffn_kernel(x_ref, ya_ref, yb_ref, ga_ref, gb_ref, wa_ref, wb_ref, wo_ref, gn_ref,
                    wup_ref, cw_ref, cb_ref, wd_ref, gf_ref, o_ref, tail_ref, u_ref, *, tm, tf):
    @pl.when(pl.program_id(1) == 0)
    def _():
        tail_ref[...] = jnp.zeros_like(tail_ref)

    y_a = jnp.dot(ya_ref[0], wa_ref[...], preferred_element_type=F32)
    y_b = jnp.dot(yb_ref[0], wb_ref[...], preferred_element_type=F32)
    merged = (jax.nn.sigmoid(ga_ref[0].astype(F32)) * y_a + jax.nn.sigmoid(gb_ref[0].astype(F32)) * y_b)
    x1 = x_ref[0] + jnp.dot(merged.astype(BF16), wo_ref[...], preferred_element_type=F32)
    o_ref[0] = x1
    h = _rms(x1, gn_ref[...]).astype(BF16)

    top = lax.broadcasted_iota(jnp.int32, (SUBLANES, tf), 0)
    for c in range(D_FF // tf):
        cols = slice(c * tf, (c + 1) * tf)
        a = jnp.dot(h, wup_ref[:, cols], preferred_element_type=F32)
        b = jnp.dot(h, wup_ref[:, D_FF + c * tf:D_FF + (c + 1) * tf], preferred_element_type=F32)
        tail = tail_ref[c]
        tail_ref[c] = a[tm - SUBLANES:]
        ac = cb_ref[:, cols] + cw_ref[CONV_W - 1:CONV_W, cols] * a
        for d in range(1, CONV_W):
            sh = pltpu.roll(a, d, axis=0)
            head = jnp.where(top < d, pltpu.roll(tail, d, axis=0), sh[:SUBLANES])
            sh = jnp.concatenate([head, sh[SUBLANES:]], axis=0)
            ac = ac + cw_ref[CONV_W - 1 - d:CONV_W - d, cols] * sh
        u_ref[:, cols] = (jax.nn.gelu(ac) * b).astype(BF16)
    y = jnp.dot(u_ref[...], wd_ref[...], preferred_element_type=F32)
    o_ref[0] = _rms(o_ref[0] + y, gf_ref[...])


def _mix_ffn(x, ya, yb, pb3, wa, wb, wo, gn, w_up, conv_w, conv_b, w_down, gf, tm, tf):
    B, S, _ = x.shape
    HV = RET_HEADS * RET_DV
    HD = NSA_HEADS * NSA_DK
    rows = lambda width, col=0: pl.BlockSpec((1, tm, width), lambda b, i: (b, i, col))
    whole = lambda *shape: pl.BlockSpec(shape, lambda b, i: (0,) * len(shape), pipeline_mode=pl.Buffered(1))
    return pl.pallas_call(
        functools.partial(_mix_ffn_kernel, tm=tm, tf=tf),
        grid=(B, S // tm),
        in_specs=[rows(D_MODEL), rows(HV), rows(HD),
                  rows(D_MODEL, PB_GA // D_MODEL), rows(D_MODEL, PB_GB // D_MODEL),
                  whole(HV, D_MODEL), whole(HD, D_MODEL), whole(D_MODEL, D_MODEL), whole(1, D_MODEL),
                  whole(D_MODEL, 2 * D_FF), whole(CONV_W, D_FF), whole(1, D_FF), whole(D_FF, D_MODEL),
                  whole(1, D_MODEL)],
        out_specs=rows(D_MODEL),
        out_shape=jax.ShapeDtypeStruct((B, S, D_MODEL), F32),
        scratch_shapes=[pltpu.VMEM((D_FF // tf, SUBLANES, tf), F32), pltpu.VMEM((tm, D_FF), BF16)],
        compiler_params=_params("arbitrary", "arbitrary"),
        name="mix_ffn",
    )(x, ya, yb, pb3, pb3, wa, wb, wo, gn, w_up, conv_w, conv_b, w_down, gf)


def _block_diag_groups(w):
    z = jnp.zeros_like(w)
    return jnp.concatenate([jnp.concatenate([w, z], axis=-1), jnp.concatenate([z, w], axis=-1)], axis=-2)


def kernel(x, norm_mix, w_in, ret_gn_g, cmp_pos_k, cmp_w1_k, cmp_b1_k, cmp_w2_k, cmp_pos_v, cmp_w1_v,
           cmp_b1_v, cmp_w2_v, w_ret_o, w_nsa_o, w_out, norm_ffn, w_up, conv_w, conv_b, w_down,
           norm_final):
    B, S, D = x.shape
    assert D == D_MODEL and NSA_GROUPS == 2 and norm_mix.shape[0] == 1
    assert S % 512 == 0 and S >= 1024 and S // CMP_STRIDE <= LANES and S // SEL_BLOCK <= SEL_LANES
    M = B * S
    x2 = x.reshape(M, D)

    w_bf = w_in[0].astype(BF16)
    w_gates = w_bf[:, W_GA:]

    def cmp_weights(pos, w1, b1, w2):
        return (jnp.tile(pos, (1, NSA_GROUPS)),
                _block_diag_groups(w1.reshape(CMP_BLOCK, NSA_DK, CMP_HIDDEN)).astype(BF16),
                jnp.tile(b1, NSA_GROUPS)[None, :],
                _block_diag_groups(w2).astype(BF16))

    posk, w1k, b1k, w2k = cmp_weights(cmp_pos_k[0], cmp_w1_k[0], cmp_b1_k[0], cmp_w2_k[0])
    posv, w1v, b1v, w2v = cmp_weights(cmp_pos_v[0], cmp_w1_v[0], cmp_b1_v[0], cmp_w2_v[0])

    pb, pf, ya = _inproj_ret(x2, norm_mix, w_bf, w_gates, ret_gn_g, S, tm=512, C=256)
    pb3 = pb.reshape(B, S, PB_W)
    pf3 = pf.reshape(B, S, PF_W)
    ya = ya.reshape(B, S, -1)

    kcmp, vcmp = _compress(pf3, posk, posv, w1k, w1v, b1k, b1v, w2k, w2v)
    ocmp, sel = _cmpattn(pb3, kcmp, vcmp, pf3, tq=1024)
    yb = _selwin(pb3, sel, pf3, ocmp, tq=256, KC=512)

    return _mix_ffn(x, ya, yb, pb3, w_ret_o[0].astype(BF16), w_nsa_o[0].astype(BF16),
                    w_out[0].astype(BF16), norm_ffn, w_up[0].astype(BF16), conv_w[0], conv_b,
                    w_down[0].astype(BF16), norm_final[None, :], tm=512, tf=256)
```

```python
import functools
import math

import numpy as np
import jax
import jax.numpy as jnp
from jax import lax
from jax.experimental import pallas as pl
from jax.experimental.pallas import tpu as pltpu

F32 = jnp.float32
BF16 = jnp.bfloat16

D_MODEL = 1024
RET_HEADS = 4
RET_DK = 128
RET_DV = 256
NSA_HEADS = 8
NSA_GROUPS = 2
NSA_R = NSA_HEADS // NSA_GROUPS
NSA_DK = 64
CMP_BLOCK = 32
CMP_STRIDE = 16
CMP_HIDDEN = 256
SEL_BLOCK = 64
SEL_TOPK = 8
WINDOW = 512
D_FF = 2816
CONV_W = 3
EPS = 1e-6
NEG = -1e30
FORCE = 1e9
LOWEST = -3e38

IN_SIZES = (RET_HEADS * RET_DK, RET_HEADS * RET_DK, RET_HEADS * RET_DV, RET_HEADS * RET_DV,
            NSA_HEADS * NSA_DK,
            NSA_GROUPS * NSA_DK, NSA_GROUPS * NSA_DK, NSA_GROUPS * NSA_DK,
            NSA_GROUPS * NSA_DK, NSA_GROUPS * NSA_DK, NSA_GROUPS * NSA_DK,
            3 * NSA_HEADS, D_MODEL, D_MODEL)

LANES = 128
VMEM_LIMIT = 56 * 1024 * 1024

_W_OFF = np.concatenate([[0], np.cumsum(IN_SIZES)]).tolist()
W_NQ, W_KCR, W_KS, W_NG, W_GA = _W_OFF[4], _W_OFF[5], _W_OFF[7], _W_OFF[11], _W_OFF[12]

RT_RQ, RT_RK, RT_RV, RT_RG = 0, 512, 1024, 2048
RT_W = 3072
PB_NQ, PB_KS, PB_VS, PB_KW, PB_VW, PB_GA, PB_GB = 0, 512, 640, 768, 896, 1024, 2048
PB_W = 3072
PF_KCR, PF_VCR, PF_NG = 0, 128, 256
PF_W = 384

NT_DIMS = (((1,), (1,)), ((), ()))
TN_DIMS = (((0,), (0,)), ((), ()))


def _params(*sem):
    return pltpu.CompilerParams(dimension_semantics=sem, vmem_limit_bytes=VMEM_LIMIT)


def _rms(x, g):
    return x * lax.rsqrt(jnp.mean(x * x, axis=-1, keepdims=True) + EPS) * g


def _split3(x):
    hi = x.astype(BF16)
    rest = x - hi.astype(F32)
    mid = rest.astype(BF16)
    return hi, mid, (rest - mid.astype(F32)).astype(BF16)


def _inproj_ret_kernel(x_ref, g_ref, w_ref, wg_ref, gn_ref, pb_ref, pf_ref, ya_ref,
                       stash0, stash1, r_ref, dmask_ref, xi_ref, zeta_ref, *, tm, C, S):
    log_gamma = [float(np.log1p(-np.exp2(np.float32(-5.0 - h)))) for h in range(RET_HEADS)]
    scale = RET_DK ** -0.5
    i = pl.program_id(0)

    @pl.when(i == 0)
    def _():
        diff = (lax.broadcasted_iota(jnp.int32, (C, C), 0)
                - lax.broadcasted_iota(jnp.int32, (C, C), 1)).astype(F32)
        pos_v = lax.broadcasted_iota(jnp.int32, (C, RET_DV), 0).astype(F32)
        pos_k = lax.broadcasted_iota(jnp.int32, (C, RET_DK), 0).astype(F32)
        for h in range(RET_HEADS):
            lg = log_gamma[h]
            dmask_ref[h] = jnp.where(diff >= 0, jnp.exp(lg * jnp.maximum(diff, 0.0)), 0.0) * scale
            xi_ref[h] = jnp.exp(lg * (pos_v + 1.0))
            zeta_ref[h] = jnp.exp(lg * (C - 1.0 - pos_k)) * scale
        stash1[...] = jnp.zeros_like(stash1)
        r_ref[...] = jnp.zeros_like(r_ref)

    seq_start = lax.rem(i + (S // tm) - 1, S // tm) == 0

    def step(new_ref, old_ref):
        for cc in range(tm // C):
            rows = slice(cc * C, (cc + 1) * C)
            for h in range(RET_HEADS):
                lg = log_gamma[h]
                q = old_ref[rows, RT_RQ + h * RET_DK:RT_RQ + (h + 1) * RET_DK]
                k = old_ref[rows, RT_RK + h * RET_DK:RT_RK + (h + 1) * RET_DK]
                v = old_ref[rows, RT_RV + h * RET_DV:RT_RV + (h + 1) * RET_DV]
                g = old_ref[rows, RT_RG + h * RET_DV:RT_RG + (h + 1) * RET_DV].astype(F32)
                s = lax.dot_general(q, k, NT_DIMS, preferred_element_type=F32)
                o = jnp.dot((s * dmask_ref[h]).astype(BF16), v, preferred_element_type=F32)
                state = r_ref[h]
                if cc == 0:
                    state = jnp.where(seq_start, 0.0, state)
                o = o + xi_ref[h] * jnp.dot(q, state.astype(BF16), preferred_element_type=F32)
                kz = (k.astype(F32) * zeta_ref[h]).astype(BF16)
                kv = lax.dot_general(kz, v, TN_DIMS, preferred_element_type=F32)
                r_ref[h] = math.exp(lg * C) * state + kv
                mu = jnp.mean(o, axis=-1, keepdims=True)
                d = o - mu
                var = jnp.mean(d * d, axis=-1, keepdims=True)
                on = d * lax.rsqrt(var + EPS) * gn_ref[:, h * RET_DV:(h + 1) * RET_DV]
                ya_ref[rows, h * RET_DV:(h + 1) * RET_DV] = (g * jax.nn.sigmoid(g) * on).astype(BF16)

        hb = _rms(x_ref[...], g_ref[...]).astype(BF16)
        for c0 in range(0, RT_W, 512):
            new_ref[:, c0:c0 + 512] = jnp.dot(
                hb, w_ref[:, c0:c0 + 512], preferred_element_type=F32).astype(BF16)

        def project(w, c0, c1):
            return jnp.dot(hb, w[:, c0:c1], preferred_element_type=F32)

        pb_ref[:, PB_NQ:PB_KS] = (project(w_ref, W_NQ, W_KCR) * (NSA_DK ** -0.5)).astype(BF16)
        pb_ref[:, PB_KS:PB_GA] = project(w_ref, W_KS, W_NG).astype(BF16)
        for c0 in range(0, PB_W - PB_GA, 512):
            pb_ref[:, PB_GA + c0:PB_GA + c0 + 512] = project(wg_ref, c0, c0 + 512).astype(BF16)
        pf_ref[:, PF_KCR:PF_NG] = project(w_ref, W_KCR, W_KS)
        pf_ref[:, PF_NG:PF_W] = project(w_ref, W_NG, W_NG + PF_W - PF_NG)

    pl.when(lax.rem(i, 2) == 0)(functools.partial(step, stash0, stash1))
    pl.when(lax.rem(i, 2) == 1)(functools.partial(step, stash1, stash0))


def _inproj_ret(x2, g, w, w_gates, gn_g, S, tm, C):
    M = x2.shape[0]
    nt = M // tm
    HV = RET_HEADS * RET_DV
    cur = lambda i: (jnp.minimum(i, nt - 1), 0)
    prev = lambda i: (jnp.maximum(i - 1, 0), 0)
    const = lambda i: (0, 0)
    return pl.pallas_call(
        functools.partial(_inproj_ret_kernel, tm=tm, C=C, S=S),
        grid=(nt + 1,),
        in_specs=[pl.BlockSpec((tm, D_MODEL), cur),
                  pl.BlockSpec((1, D_MODEL), const),
                  pl.BlockSpec(w.shape, const, pipeline_mode=pl.Buffered(1)),
                  pl.BlockSpec(w_gates.shape, const, pipeline_mode=pl.Buffered(1)),
                  pl.BlockSpec((1, HV), const)],
        out_specs=[pl.BlockSpec((tm, PB_W), cur), pl.BlockSpec((tm, PF_W), cur), pl.BlockSpec((tm, HV), prev)],
        out_shape=[jax.ShapeDtypeStruct((M, PB_W), BF16), jax.ShapeDtypeStruct((M, PF_W), F32),
                   jax.ShapeDtypeStruct((M, HV), BF16)],
        scratch_shapes=[pltpu.VMEM((tm, RT_W), BF16), pltpu.VMEM((tm, RT_W), BF16),
                        pltpu.VMEM((RET_HEADS, RET_DK, RET_DV), F32), pltpu.VMEM((RET_HEADS, C, C), F32),
                        pltpu.VMEM((RET_HEADS, C, RET_DV), F32), pltpu.VMEM((RET_HEADS, C, RET_DK), F32)],
        compiler_params=_params("arbitrary"),
        name="inproj_ret",
    )(x2, g, w, w_gates, gn_g)


def _compress_kernel(kin_ref, vin_ref, posk_ref, posv_ref, w1k_ref, w1v_ref, b1k_ref, b1v_ref,
                     w2k_ref, w2v_ref, ko_ref, vo_ref, *, NP):
    GD = NSA_GROUPS * NSA_DK
    npc = CMP_BLOCK // CMP_STRIDE
    row = lax.broadcasted_iota(jnp.int32, (NP, GD), 0)
    for in_ref, pos_ref, w1_ref, b1_ref, w2_ref, out_ref in (
            (kin_ref, posk_ref, w1k_ref, b1k_ref, w2k_ref, ko_ref),
            (vin_ref, posv_ref, w1v_ref, b1v_ref, w2v_ref, vo_ref)):
        parts = []
        for p in range(npc):
            acc = jnp.zeros((NP, NSA_GROUPS * CMP_HIDDEN), F32)
            for i in range(0, CMP_STRIDE, 2):
                j = p * CMP_STRIDE + i
                toks = [(in_ref[0, pl.ds(i + d, NP, stride=CMP_STRIDE), :] + pos_ref[j + d:j + d + 1, :])
                        .astype(BF16) for d in range(2)]
                acc = acc + jnp.dot(jnp.concatenate(toks, axis=1), w1_ref[j:j + 2].reshape(2 * GD, -1),
                                    preferred_element_type=F32)
            parts.append(acc)
        hidden = parts[0]
        for p in range(1, npc):
            hidden = hidden + pltpu.roll(parts[p], NP - p, axis=0)
        act = jax.nn.gelu(hidden + b1_ref[...]).astype(BF16)
        res = jnp.dot(act, w2_ref[...], preferred_element_type=F32)
        res = jnp.where(row < NP - (npc - 1), res, 0.0)
        if out_ref is vo_ref:
            res = res.T
            for g in range(NSA_GROUPS):
                out_ref[0, g] = res[g * NSA_DK:(g + 1) * NSA_DK, :]
        else:
            for g in range(NSA_GROUPS):
                out_ref[0, g] = res[:, g * NSA_DK:(g + 1) * NSA_DK]


def _compress(pf3, posk, posv, w1k, w1v, b1k, b1v, w2k, w2v):
    B, S, _ = pf3.shape
    NP = S // CMP_STRIDE
    GD = NSA_GROUPS * NSA_DK
    GH = NSA_GROUPS * CMP_HIDDEN
    const2 = lambda b: (0, 0)
    const3 = lambda b: (0, 0, 0)
    k_shape = (NSA_GROUPS, NP, NSA_DK)
    v_shape = (NSA_GROUPS, NSA_DK, NP)
    return pl.pallas_call(
        functools.partial(_compress_kernel, NP=NP),
        grid=(B,),
        in_specs=[pl.BlockSpec((1, S, GD), lambda b: (b, 0, PF_KCR // GD)),
                  pl.BlockSpec((1, S, GD), lambda b: (b, 0, PF_VCR // GD)),
                  pl.BlockSpec((CMP_BLOCK, GD), const2), pl.BlockSpec((CMP_BLOCK, GD), const2),
                  pl.BlockSpec((CMP_BLOCK, GD, GH), const3), pl.BlockSpec((CMP_BLOCK, GD, GH), const3),
                  pl.BlockSpec((1, GH), const2), pl.BlockSpec((1, GH), const2),
                  pl.BlockSpec((GH, GD), const2), pl.BlockSpec((GH, GD), const2)],
        out_specs=[pl.BlockSpec((1,) + k_shape, lambda b: (b, 0, 0, 0)),
                   pl.BlockSpec((1,) + v_shape, lambda b: (b, 0, 0, 0))],
        out_shape=[jax.ShapeDtypeStruct((B,) + k_shape, F32), jax.ShapeDtypeStruct((B,) + v_shape, F32)],
        compiler_params=_params("arbitrary"),
        name="compress",
    )(pf3, pf3, posk, posv, w1k, w1v, b1k, b1v, w2k, w2v)


SEL_LANE0 = 64
SEL_LANES = 32
POS_HI_LANE, POS_LO_LANE, PAD_LANE = 96, 97, 98
ONES_LANE = 64
MASK_BIG = 1e30


def _cmpattn_kernel(q_ref, kc_ref, vct_ref, gate_ref, ocmp_ref, sel_ref, *, tq, NP, NS):
    NC = NP - (CMP_BLOCK // CMP_STRIDE - 1)
    tile = pl.program_id(1)

    def run(rows):
        _cmpattn_tile(q_ref, kc_ref, vct_ref, gate_ref, ocmp_ref, sel_ref, tile * tq,
                      tq=tq, NP=rows, NC=NC, NS=NS)

    for v in range(pl.cdiv(NP * CMP_STRIDE, tq)):
        pl.when(tile == v)(functools.partial(run, min(NP, (v + 1) * tq // CMP_STRIDE)))


def _cmpattn_tile(q_ref, kc_ref, vct_ref, gate_ref, ocmp_ref, sel_ref, t0, *, tq, NP, NC, NS):
    tcol = t0 + lax.broadcasted_iota(jnp.int32, (NP, tq), 1)
    cidx = lax.broadcasted_iota(jnp.int32, (NP, tq), 0)
    visible = (tcol >= cidx * CMP_STRIDE + (CMP_BLOCK - 1)) & (cidx < NC)
    mask_add = jnp.where(visible, 0.0, NEG)
    block_end = (lax.broadcasted_iota(jnp.int32, (NP, 1), 0) * CMP_STRIDE + (CMP_BLOCK - 1)).astype(F32)
    sees_any = ((t0 + lax.broadcasted_iota(jnp.int32, (1, tq), 1)) >= CMP_BLOCK - 1).astype(F32)
    jj = lax.broadcasted_iota(jnp.int32, (SEL_LANES, NP), 0)
    cc = lax.broadcasted_iota(jnp.int32, (SEL_LANES, NP), 1)
    overlap_t = ((cc * CMP_STRIDE < jj * SEL_BLOCK + SEL_BLOCK)
                 & (cc * CMP_STRIDE + CMP_BLOCK > jj * SEL_BLOCK) & (jj < NS)).astype(BF16)
    blk = lax.broadcasted_iota(jnp.int32, (SEL_LANES, tq), 0)
    blk_f = blk.astype(F32)
    is_blk = blk < NS
    cur = jnp.right_shift(t0 + lax.broadcasted_iota(jnp.int32, (SEL_LANES, tq), 1),
                          SEL_BLOCK.bit_length() - 1)
    o_t = []
    for g in range(NSA_GROUPS):
        kc_parts = _split3(kc_ref[0, g, :NP, :])
        vct = vct_ref[0, g, :, :NP].astype(BF16)
        psum = jnp.zeros((NP, tq), F32)
        for r in range(NSA_R):
            hh = g * NSA_R + r
            q = q_ref[0, :, hh * NSA_DK:(hh + 1) * NSA_DK]
            s = sum(lax.dot_general(part, q, NT_DIMS, preferred_element_type=F32) for part in kc_parts)
            s = s + (2.0 ** -(hh + 1)) * block_end + mask_add
            e = jnp.exp(s - jnp.max(s, axis=0, keepdims=True))
            p = e * (sees_any / jnp.sum(e, axis=0, keepdims=True))
            o_t.append(jnp.dot(vct, p.astype(BF16), preferred_element_type=F32))
            psum = psum + p
        imp = sum(jnp.dot(overlap_t, part, preferred_element_type=F32) for part in _split3(psum))
        imp = jnp.where((blk == 0) | (blk == cur) | (blk == cur - 1), FORCE, imp)
        imp = jnp.where(blk > cur, -FORCE, imp)
        work = jnp.where(is_blk, imp, LOWEST)
        sel = jnp.zeros((SEL_LANES, tq), F32)
        for _ in range(min(SEL_TOPK, NS)):
            top = jnp.max(work, axis=0, keepdims=True)
            first = jnp.min(jnp.where(work == top, blk_f, float(SEL_LANES)), axis=0, keepdims=True)
            pick = blk_f == first
            sel = jnp.where(pick, 1.0, sel)
            work = jnp.where(pick, LOWEST, work)
        sel = jnp.where(is_blk & (blk <= cur), sel - 1.0, jnp.where(is_blk, -1.0, 0.0))
        placed = jnp.concatenate([jnp.zeros((SEL_LANE0, tq), F32), sel,
                                  jnp.zeros((LANES - SEL_LANE0 - SEL_LANES, tq), F32)], axis=0)
        sel_ref[0, g] = placed.T.astype(BF16)
    o = jnp.concatenate(o_t, axis=0).T
    gate = jax.nn.sigmoid(gate_ref[0])
    for hh in range(NSA_HEADS):
        cols = slice(hh * NSA_DK, (hh + 1) * NSA_DK)
        ocmp_ref[0, :, cols] = gate[:, hh:hh + 1] * o[:, cols]


def _cmpattn(pb3, kcmp, vcmp, pf3, tq):
    B, S, _ = pb3.shape
    NP = S // CMP_STRIDE
    NS = S // SEL_BLOCK
    HD = NSA_HEADS * NSA_DK
    return pl.pallas_call(
        functools.partial(_cmpattn_kernel, tq=tq, NP=NP, NS=NS),
        grid=(B, S // tq),
        in_specs=[pl.BlockSpec((1, tq, HD), lambda b, i: (b, i, PB_NQ // HD)),
                  pl.BlockSpec((1, NSA_GROUPS, NP, NSA_DK), lambda b, i: (b, 0, 0, 0)),
                  pl.BlockSpec((1, NSA_GROUPS, NSA_DK, NP), lambda b, i: (b, 0, 0, 0)),
                  pl.BlockSpec((1, tq, LANES), lambda b, i: (b, i, PF_NG // LANES))],
        out_specs=[pl.BlockSpec((1, tq, HD), lambda b, i: (b, i, 0)),
                   pl.BlockSpec((1, NSA_GROUPS, tq, LANES), lambda b, i: (b, 0, i, 0))],
        out_shape=[jax.ShapeDtypeStruct((B, S, HD), F32),
                   jax.ShapeDtypeStruct((B, NSA_GROUPS, S, LANES), BF16)],
        compiler_params=_params("arbitrary", "arbitrary"),
        name="cmpattn",
    )(pb3, kcmp, vcmp, pf3)


def _selwin_kernel(q_ref, ks_ref, vs_ref, kw_ref, vw_ref, sel_ref, wmask_ref, dmask_ref, gate_ref, ocmp_ref,
                   o_ref, ksx, vsx, kwx, vwx, *, tq, KC, S):
    WL = WINDOW + tq
    RB = 512
    i = pl.program_id(1)
    t0 = pl.multiple_of(i * tq, tq)
    sel_shift = SEL_BLOCK.bit_length() - 1

    @pl.when(i == 0)
    def _build():
        lane = lax.broadcasted_iota(jnp.int32, (RB, LANES), 1)
        is_head = lane < NSA_DK
        pad_k = jnp.where(lane == PAD_LANE, MASK_BIG, 0.0).astype(BF16)
        for g in range(NSA_GROUPS):
            kwx[g, 0:WINDOW, :] = pad_k[:WINDOW]
            vwx[g, 0:WINDOW, :] = jnp.zeros((WINDOW, LANES), BF16)
        for piece in range(S // RB):
            r0 = piece * RB
            pos = r0 + lax.broadcasted_iota(jnp.int32, (RB, LANES), 0)
            alibi = jnp.where(lane == POS_HI_LANE, jnp.right_shift(pos, sel_shift),
                              jnp.where(lane == POS_LO_LANE, pos & (SEL_BLOCK - 1), 0)).astype(F32)
            onehot = jnp.where(lane - SEL_LANE0 == jnp.right_shift(pos, sel_shift), MASK_BIG, 0.0)
            k_extra_win = alibi.astype(BF16)
            k_extra_sel = (alibi + onehot).astype(BF16)
            v_extra = jnp.where(lane == ONES_LANE, 1.0, 0.0).astype(BF16)
            for g in range(NSA_GROUPS):
                def head_lanes(ref):
                    t = ref[0, r0:r0 + RB, :]
                    return t if g == 0 else jnp.concatenate([t[:, NSA_DK:], t[:, :NSA_DK]], axis=1)
                ksx[g, r0:r0 + RB, :] = jnp.where(is_head, head_lanes(ks_ref), k_extra_sel)
                vsx[g, r0:r0 + RB, :] = jnp.where(is_head, head_lanes(vs_ref), v_extra)
                kwx[g, WINDOW + r0:WINDOW + r0 + RB, :] = jnp.where(is_head, head_lanes(kw_ref), k_extra_win)
                vwx[g, WINDOW + r0:WINDOW + r0 + RB, :] = jnp.where(is_head, head_lanes(vw_ref), v_extra)

    gate = jax.nn.sigmoid(gate_ref[0])
    lane = lax.broadcasted_iota(jnp.int32, (tq, LANES), 1)
    is_head = lane < NSA_DK
    diag_chunk = t0 // KC
    wmask = wmask_ref[...][None]
    dmask = dmask_ref[i % (KC // tq)][None]

    def masked_rows(s, mask):
        n = s.shape[-1]
        return (s.reshape(NSA_R, tq, n) + mask).reshape(NSA_R * tq, n)

    def normalise(acc):
        return acc[:, :NSA_DK] / acc[:, ONES_LANE:ONES_LANE + 1]

    def scores(q, k):
        half = q.shape[0] // 2
        return jnp.concatenate([lax.dot_general(q[:half], k, NT_DIMS, preferred_element_type=F32),
                                lax.dot_general(q[half:], k, NT_DIMS, preferred_element_type=F32)], axis=0)

    def weighted_values(p, v):
        half = p.shape[0] // 2
        p = p.astype(BF16)
        return jnp.concatenate([jnp.dot(p[:half], v, preferred_element_type=F32),
                                jnp.dot(p[half:], v, preferred_element_type=F32)], axis=0)

    def query_rows(g):
        sel = sel_ref[0, g].astype(F32)
        rows = []
        for r in range(NSA_R):
            hh = g * NSA_R + r
            qh = q_ref[0, :, (hh // 2) * LANES:(hh // 2 + 1) * LANES].astype(F32)
            if hh % 2:
                qh = jnp.concatenate([qh[:, NSA_DK:], qh[:, :NSA_DK]], axis=1)
            slope = 2.0 ** -(hh + 1)
            extra = jnp.where(lane == POS_HI_LANE, SEL_BLOCK * slope,
                              jnp.where(lane == POS_LO_LANE, slope, jnp.where(lane == PAD_LANE, -1.0, sel)))
            rows.append(jnp.where(is_head, qh, extra).astype(BF16))
        return jnp.concatenate(rows, axis=0)

    def tile(n_past):
        past = n_past * KC
        for g in range(NSA_GROUPS):
            q = query_rows(g)
            s = masked_rows(scores(q, kwx[g, pl.ds(t0, WL), :]), wmask)
            p = jnp.exp(s - jnp.max(s, axis=-1, keepdims=True))
            o_win = normalise(weighted_values(p, vwx[g, pl.ds(t0, WL), :]))

            s_own = masked_rows(scores(q, ksx[g, past:past + KC, :]), dmask)
            m = jnp.max(s_own, axis=-1, keepdims=True)
            if n_past:
                s_past = scores(q, ksx[g, 0:past, :])
                m = jnp.maximum(m, jnp.max(s_past, axis=-1, keepdims=True))
            acc = weighted_values(jnp.exp(s_own - m), vsx[g, past:past + KC, :])
            if n_past:
                acc = acc + weighted_values(jnp.exp(s_past - m), vsx[g, 0:past, :])
            o_sel = normalise(acc)

            for r in range(NSA_R):
                hh = g * NSA_R + r
                g_sel = gate[:, NSA_HEADS + hh:NSA_HEADS + hh + 1]
                g_win = gate[:, 2 * NSA_HEADS + hh:2 * NSA_HEADS + hh + 1]
                piece = g_sel * o_sel[r * tq:(r + 1) * tq] + g_win * o_win[r * tq:(r + 1) * tq]
                o_ref[0, :, hh * NSA_DK:(hh + 1) * NSA_DK] = (
                    ocmp_ref[0, :, hh * NSA_DK:(hh + 1) * NSA_DK] + piece).astype(BF16)

    for n_past in range(S // KC):
        pl.when(diag_chunk == n_past)(functools.partial(tile, n_past))


def _selwin(pb3, sel, pf3, ocmp, tq, KC):
    B, S, _ = pb3.shape
    HD = NSA_HEADS * NSA_DK
    WL = WINDOW + tq
    tr = jnp.arange(tq)[:, None]
    kc = jnp.arange(WL)[None, :]
    wmask = jnp.where((kc > tr) & (kc <= tr + WINDOW), 0.0, NEG).astype(F32)
    off = (jnp.arange(KC // tq) * tq)[:, None, None]
    dmask = jnp.where(jnp.arange(KC)[None, None, :] <= off + tr[None], 0.0, NEG).astype(F32)
    kv_spec = lambda off: pl.BlockSpec((1, S, LANES), lambda b, i: (b, 0, off // LANES))
    return pl.pallas_call(
        functools.partial(_selwin_kernel, tq=tq, KC=KC, S=S),
        grid=(B, S // tq),
        in_specs=[pl.BlockSpec((1, tq, HD), lambda b, i: (b, i, PB_NQ // HD)),
                  kv_spec(PB_KS), kv_spec(PB_VS), kv_spec(PB_KW), kv_spec(PB_VW),
                  pl.BlockSpec((1, NSA_GROUPS, tq, LANES), lambda b, i: (b, 0, i, 0)),
                  pl.BlockSpec((tq, WL), lambda b, i: (0, 0)),
                  pl.BlockSpec((KC // tq, tq, KC), lambda b, i: (0, 0, 0)),
                  pl.BlockSpec((1, tq, LANES), lambda b, i: (b, i, PF_NG // LANES)),
                  pl.BlockSpec((1, tq, HD), lambda b, i: (b, i, 0))],
        out_specs=pl.BlockSpec((1, tq, HD), lambda b, i: (b, i, 0)),
        out_shape=jax.ShapeDtypeStruct((B, S, HD), BF16),
        scratch_shapes=[pltpu.VMEM((NSA_GROUPS, S, LANES), BF16), pltpu.VMEM((NSA_GROUPS, S, LANES), BF16),
                        pltpu.VMEM((NSA_GROUPS, WINDOW + S, LANES), BF16),
                        pltpu.VMEM((NSA_GROUPS, WINDOW + S, LANES), BF16)],
        compiler_params=_params("arbitrary", "arbitrary"),
        name="selwin",
    )(pb3, pb3, pb3, pb3, pb3, sel, wmask, dmask, pf3, ocmp)


SUBLANES = 8


def _mix_ffn_kernel(x_ref, ya_ref, yb_ref, ga_ref, gb_ref, wa_ref, wb_ref, wo_ref, gn_ref,
                    wup_ref, cw_ref, cb_ref, wd_ref, gf_ref, o_ref, tail_ref, u_ref, *, tm, tf):
    @pl.when(pl.program_id(1) == 0)
    def _():
        tail_ref[...] = jnp.zeros_like(tail_ref)

    y_a = jnp.dot(ya_ref[0], wa_ref[...], preferred_element_type=F32)
    y_b = jnp.dot(yb_ref[0], wb_ref[...], preferred_element_type=F32)
    merged = (jax.nn.sigmoid(ga_ref[0].astype(F32)) * y_a + jax.nn.sigmoid(gb_ref[0].astype(F32)) * y_b)
    x1 = x_ref[0] + jnp.dot(merged.astype(BF16), wo_ref[...], preferred_element_type=F32)
    o_ref[0] = x1
    h = _rms(x1, gn_ref[...]).astype(BF16)

    top = lax.broadcasted_iota(jnp.int32, (SUBLANES, tf), 0)
    for c in range(D_FF // tf):
        cols = slice(c * tf, (c + 1) * tf)
        a = jnp.dot(h, wup_ref[:, cols], preferred_element_type=F32)
        b = jnp.dot(h, wup_ref[:, D_FF + c * tf:D_FF + (c + 1) * tf], preferred_element_type=F32)
        tail = tail_ref[c]
        tail_ref[c] = a[tm - SUBLANES:]
        ac = cb_ref[:, cols] + cw_ref[CONV_W - 1:CONV_W, cols] * a
        for d in range(1, CONV_W):
            sh = pltpu.roll(a, d, axis=0)
            head = jnp.where(top < d, pltpu.roll(tail, d, axis=0), sh[:SUBLANES])
            sh = jnp.concatenate([head, sh[SUBLANES:]], axis=0)
            ac = ac + cw_ref[CONV_W - 1 - d:CONV_W - d, cols] * sh
        u_ref[:, cols] = (jax.nn.gelu(ac) * b).astype(BF16)
    y = jnp.dot(u_ref[...], wd_ref[...], preferred_element_type=F32)
    o_ref[0] = _rms(o_ref[0] + y, gf_ref[...])


def _mix_ffn(x, ya, yb, pb3, wa, wb, wo, gn, w_up, conv_w, conv_b, w_down, gf, tm, tf):
    B, S, _ = x.shape
    HV = RET_HEADS * RET_DV
    HD = NSA_HEADS * NSA_DK
    rows = lambda width, col=0: pl.BlockSpec((1, tm, width), lambda b, i: (b, i, col))
    whole = lambda *shape: pl.BlockSpec(shape, lambda b, i: (0,) * len(shape), pipeline_mode=pl.Buffered(1))
    return pl.pallas_call(
        functools.partial(_mix_ffn_kernel, tm=tm, tf=tf),
        grid=(B, S // tm),
        in_specs=[rows(D_MODEL), rows(HV), rows(HD),
                  rows(D_MODEL, PB_GA // D_MODEL), rows(D_MODEL, PB_GB // D_MODEL),
                  whole(HV, D_MODEL), whole(HD, D_MODEL), whole(D_MODEL, D_MODEL), whole(1, D_MODEL),
                  whole(D_MODEL, 2 * D_FF), whole(CONV_W, D_FF), whole(1, D_FF), whole(D_FF, D_MODEL),
                  whole(1, D_MODEL)],
        out_specs=rows(D_MODEL),
        out_shape=jax.ShapeDtypeStruct((B, S, D_MODEL), F32),
        scratch_shapes=[pltpu.VMEM((D_FF // tf, SUBLANES, tf), F32), pltpu.VMEM((tm, D_FF), BF16)],
        compiler_params=_params("arbitrary", "arbitrary"),
        name="mix_ffn",
    )(x, ya, yb, pb3, pb3, wa, wb, wo, gn, w_up, conv_w, conv_b, w_down, gf)


def _block_diag_groups(w):
    z = jnp.zeros_like(w)
    return jnp.concatenate([jnp.concatenate([w, z], axis=-1), jnp.concatenate([z, w], axis=-1)], axis=-2)


def kernel(x, norm_mix, w_in, ret_gn_g, cmp_pos_k, cmp_w1_k, cmp_b1_k, cmp_w2_k, cmp_pos_v, cmp_w1_v,
           cmp_b1_v, cmp_w2_v, w_ret_o, w_nsa_o, w_out, norm_ffn, w_up, conv_w, conv_b, w_down,
           norm_final):
    B, S, D = x.shape
    assert D == D_MODEL and NSA_GROUPS == 2 and norm_mix.shape[0] == 1
    assert S % 512 == 0 and S >= 1024 and S // CMP_STRIDE <= LANES and S // SEL_BLOCK <= SEL_LANES
    M = B * S
    x2 = x.reshape(M, D)

    w_bf = w_in[0].astype(BF16)
    w_gates = w_bf[:, W_GA:]

    def cmp_weights(pos, w1, b1, w2):
        return (jnp.tile(pos, (1, NSA_GROUPS)),
                _block_diag_groups(w1.reshape(CMP_BLOCK, NSA_DK, CMP_HIDDEN)).astype(BF16),
                jnp.tile(b1, NSA_GROUPS)[None, :],
                _block_diag_groups(w2).astype(BF16))

    posk, w1k, b1k, w2k = cmp_weights(cmp_pos_k[0], cmp_w1_k[0], cmp_b1_k[0], cmp_w2_k[0])
    posv, w1v, b1v, w2v = cmp_weights(cmp_pos_v[0], cmp_w1_v[0], cmp_b1_v[0], cmp_w2_v[0])

    pb, pf, ya = _inproj_ret(x2, norm_mix, w_bf, w_gates, ret_gn_g, S, tm=512, C=256)
    pb3 = pb.reshape(B, S, PB_W)
    pf3 = pf.reshape(B, S, PF_W)
    ya = ya.reshape(B, S, -1)

    kcmp, vcmp = _compress(pf3, posk, posv, w1k, w1v, b1k, b1v, w2k, w2v)
    ocmp, sel = _cmpattn(pb3, kcmp, vcmp, pf3, tq=1024)
    yb = _selwin(pb3, sel, pf3, ocmp, tq=256, KC=512)

    return _mix_ffn(x, ya, yb, pb3, w_ret_o[0].astype(BF16), w_nsa_o[0].astype(BF16),
                    w_out[0].astype(BF16), norm_ffn, w_up[0].astype(BF16), conv_w[0], conv_b,
                    w_down[0].astype(BF16), norm_final[None, :], tm=512, tf=256)
```

```python
import functools
import math

import numpy as np
import jax
import jax.numpy as jnp
from jax import lax
from jax.experimental import pallas as pl
from jax.experimental.pallas import tpu as pltpu

F32 = jnp.float32
BF16 = jnp.bfloat16

D_MODEL = 1024
RET_HEADS = 4
RET_DK = 128
RET_DV = 256
NSA_HEADS = 8
NSA_GROUPS = 2
NSA_R = NSA_HEADS // NSA_GROUPS
NSA_DK = 64
CMP_BLOCK = 32
CMP_STRIDE = 16
CMP_HIDDEN = 256
SEL_BLOCK = 64
SEL_TOPK = 8
WINDOW = 512
D_FF = 2816
CONV_W = 3
EPS = 1e-6
NEG = -1e30
FORCE = 1e9
LOWEST = -3e38

IN_SIZES = (RET_HEADS * RET_DK, RET_HEADS * RET_DK, RET_HEADS * RET_DV, RET_HEADS * RET_DV,
            NSA_HEADS * NSA_DK,
            NSA_GROUPS * NSA_DK, NSA_GROUPS * NSA_DK, NSA_GROUPS * NSA_DK,
            NSA_GROUPS * NSA_DK, NSA_GROUPS * NSA_DK, NSA_GROUPS * NSA_DK,
            3 * NSA_HEADS, D_MODEL, D_MODEL)

LANES = 128
VMEM_LIMIT = 56 * 1024 * 1024

_W_OFF = np.concatenate([[0], np.cumsum(IN_SIZES)]).tolist()
W_NQ, W_KCR, W_KS, W_NG, W_GA = _W_OFF[4], _W_OFF[5], _W_OFF[7], _W_OFF[11], _W_OFF[12]

RT_RQ, RT_RK, RT_RV, RT_RG = 0, 512, 1024, 2048
RT_W = 3072
PB_NQ, PB_KS, PB_VS, PB_KW, PB_VW, PB_GA, PB_GB = 0, 512, 640, 768, 896, 1024, 2048
PB_W = 3072
PF_KCR, PF_VCR, PF_NG = 0, 128, 256
PF_W = 384

NT_DIMS = (((1,), (1,)), ((), ()))
TN_DIMS = (((0,), (0,)), ((), ()))


def _params(*sem):
    return pltpu.CompilerParams(dimension_semantics=sem, vmem_limit_bytes=VMEM_LIMIT)


def _rms(x, g):
    return x * lax.rsqrt(jnp.mean(x * x, axis=-1, keepdims=True) + EPS) * g


def _split3(x):
    hi = x.astype(BF16)
    rest = x - hi.astype(F32)
    mid = rest.astype(BF16)
    return hi, mid, (rest - mid.astype(F32)).astype(BF16)


def _inproj_ret_kernel(x_ref, g_ref, w_ref, wg_ref, gn_ref, pb_ref, pf_ref, ya_ref,
                       stash0, stash1, r_ref, dmask_ref, xi_ref, zeta_ref, *, tm, C, S):
    log_gamma = [float(np.log1p(-np.exp2(np.float32(-5.0 - h)))) for h in range(RET_HEADS)]
    scale = RET_DK ** -0.5
    i = pl.program_id(0)

    @pl.when(i == 0)
    def _():
        diff = (lax.broadcasted_iota(jnp.int32, (C, C), 0)
                - lax.broadcasted_iota(jnp.int32, (C, C), 1)).astype(F32)
        pos_v = lax.broadcasted_iota(jnp.int32, (C, RET_DV), 0).astype(F32)
        pos_k = lax.broadcasted_iota(jnp.int32, (C, RET_DK), 0).astype(F32)
        for h in range(RET_HEADS):
            lg = log_gamma[h]
            dmask_ref[h] = jnp.where(diff >= 0, jnp.exp(lg * jnp.maximum(diff, 0.0)), 0.0) * scale
            xi_ref[h] = jnp.exp(lg * (pos_v + 1.0))
            zeta_ref[h] = jnp.exp(lg * (C - 1.0 - pos_k)) * scale
        stash1[...] = jnp.zeros_like(stash1)
        r_ref[...] = jnp.zeros_like(r_ref)

    seq_start = lax.rem(i + (S // tm) - 1, S // tm) == 0

    def step(new_ref, old_ref):
        for cc in range(tm // C):
            rows = slice(cc * C, (cc + 1) * C)
            for h in range(RET_HEADS):
                lg = log_gamma[h]
                q = old_ref[rows, RT_RQ + h * RET_DK:RT_RQ + (h + 1) * RET_DK]
                k = old_ref[rows, RT_RK + h * RET_DK:RT_RK + (h + 1) * RET_DK]
                v = old_ref[rows, RT_RV + h * RET_DV:RT_RV + (h + 1) * RET_DV]
                g = old_ref[rows, RT_RG + h * RET_DV:RT_RG + (h + 1) * RET_DV].astype(F32)
                s = lax.dot_general(q, k, NT_DIMS, preferred_element_type=F32)
                o = jnp.dot((s * dmask_ref[h]).astype(BF16), v, preferred_element_type=F32)
                state = r_ref[h]
                if cc == 0:
                    state = jnp.where(seq_start, 0.0, state)
                o = o + xi_ref[h] * jnp.dot(q, state.astype(BF16), preferred_element_type=F32)
                kz = (k.astype(F32) * zeta_ref[h]).astype(BF16)
                kv = lax.dot_general(kz, v, TN_DIMS, preferred_element_type=F32)
                r_ref[h] = math.exp(lg * C) * state + kv
                mu = jnp.mean(o, axis=-1, keepdims=True)
                d = o - mu
                var = jnp.mean(d * d, axis=-1, keepdims=True)
                on = d * lax.rsqrt(var + EPS) * gn_ref[:, h * RET_DV:(h + 1) * RET_DV]
                ya_ref[rows, h * RET_DV:(h + 1) * RET_DV] = (g * jax.nn.sigmoid(g) * on).astype(BF16)

        hb = _rms(x_ref[...], g_ref[...]).astype(BF16)
        for c0 in range(0, RT_W, 512):
            new_ref[:, c0:c0 + 512] = jnp.dot(
                hb, w_ref[:, c0:c0 + 512], preferred_element_type=F32).astype(BF16)

        def project(w, c0, c1):
            return jnp.dot(hb, w[:, c0:c1], preferred_element_type=F32)

        pb_ref[:, PB_NQ:PB_KS] = (project(w_ref, W_NQ, W_KCR) * (NSA_DK ** -0.5)).astype(BF16)
        pb_ref[:, PB_KS:PB_GA] = project(w_ref, W_KS, W_NG).astype(BF16)
        for c0 in range(0, PB_W - PB_GA, 512):
            pb_ref[:, PB_GA + c0:PB_GA + c0 + 512] = project(wg_ref, c0, c0 + 512).astype(BF16)
        pf_ref[:, PF_KCR:PF_NG] = project(w_ref, W_KCR, W_KS)
        pf_ref[:, PF_NG:PF_W] = project(w_ref, W_NG, W_NG + PF_W - PF_NG)

    pl.when(lax.rem(i, 2) == 0)(functools.partial(step, stash0, stash1))
    pl.when(lax.rem(i, 2) == 1)(functools.partial(step, stash1, stash0))


def _inproj_ret(x2, g, w, w_gates, gn_g, S, tm, C):
    M = x2.shape[0]
    nt = M // tm
    HV = RET_HEADS * RET_DV
    cur = lambda i: (jnp.minimum(i, nt - 1), 0)
    prev = lambda i: (jnp.maximum(i - 1, 0), 0)
    const = lambda i: (0, 0)
    return pl.pallas_call(
        functools.partial(_inproj_ret_kernel, tm=tm, C=C, S=S),
        grid=(nt + 1,),
        in_specs=[pl.BlockSpec((tm, D_MODEL), cur),
                  pl.BlockSpec((1, D_MODEL), const),
                  pl.BlockSpec(w.shape, const, pipeline_mode=pl.Buffered(1)),
                  pl.BlockSpec(w_gates.shape, const, pipeline_mode=pl.Buffered(1)),
                  pl.BlockSpec((1, HV), const)],
        out_specs=[pl.BlockSpec((tm, PB_W), cur), pl.BlockSpec((tm, PF_W), cur), pl.BlockSpec((tm, HV), prev)],
        out_shape=[jax.ShapeDtypeStruct((M, PB_W), BF16), jax.ShapeDtypeStruct((M, PF_W), F32),
                   jax.ShapeDtypeStruct((M, HV), BF16)],
        scratch_shapes=[pltpu.VMEM((tm, RT_W), BF16), pltpu.VMEM((tm, RT_W), BF16),
                        pltpu.VMEM((RET_HEADS, RET_DK, RET_DV), F32), pltpu.VMEM((RET_HEADS, C, C), F32),
                        pltpu.VMEM((RET_HEADS, C, RET_DV), F32), pltpu.VMEM((RET_HEADS, C, RET_DK), F32)],
        compiler_params=_params("arbitrary"),
        name="inproj_ret",
    )(x2, g, w, w_gates, gn_g)


def _compress_kernel(kin_ref, vin_ref, posk_ref, posv_ref, w1k_ref, w1v_ref, b1k_ref, b1v_ref,
                     w2k_ref, w2v_ref, ko_ref, vo_ref, *, NP):
    GD = NSA_GROUPS * NSA_DK
    npc = CMP_BLOCK // CMP_STRIDE
    row = lax.broadcasted_iota(jnp.int32, (NP, GD), 0)
    for in_ref, pos_ref, w1_ref, b1_ref, w2_ref, out_ref in (
            (kin_ref, posk_ref, w1k_ref, b1k_ref, w2k_ref, ko_ref),
            (vin_ref, posv_ref, w1v_ref, b1v_ref, w2v_ref, vo_ref)):
        parts = []
        for p in range(npc):
            acc = jnp.zeros((NP, NSA_GROUPS * CMP_HIDDEN), F32)
            for i in range(0, CMP_STRIDE, 2):
                j = p * CMP_STRIDE + i
                toks = [(in_ref[0, pl.ds(i + d, NP, stride=CMP_STRIDE), :] + pos_ref[j + d:j + d + 1, :])
                        .astype(BF16) for d in range(2)]
                acc = acc + jnp.dot(jnp.concatenate(toks, axis=1), w1_ref[j:j + 2].reshape(2 * GD, -1),
                                    preferred_element_type=F32)
            parts.append(acc)
        hidden = parts[0]
        for p in range(1, npc):
            hidden = hidden + pltpu.roll(parts[p], NP - p, axis=0)
        act = jax.nn.gelu(hidden + b1_ref[...]).astype(BF16)
        res = jnp.dot(act, w2_ref[...], preferred_element_type=F32)
        res = jnp.where(row < NP - (npc - 1), res, 0.0)
        if out_ref is vo_ref:
            res = res.T
            for g in range(NSA_GROUPS):
                out_ref[0, g] = res[g * NSA_DK:(g + 1) * NSA_DK, :]
        else:
            for g in range(NSA_GROUPS):
                out_ref[0, g] = res[:, g * NSA_DK:(g + 1) * NSA_DK]


def _compress(pf3, posk, posv, w1k, w1v, b1k, b1v, w2k, w2v):
    B, S, _ = pf3.shape
    NP = S // CMP_STRIDE
    GD = NSA_GROUPS * NSA_DK
    GH = NSA_GROUPS * CMP_HIDDEN
    const2 = lambda b: (0, 0)
    const3 = lambda b: (0, 0, 0)
    k_shape = (NSA_GROUPS, NP, NSA_DK)
    v_shape = (NSA_GROUPS, NSA_DK, NP)
    return pl.pallas_call(
        functools.partial(_compress_kernel, NP=NP),
        grid=(B,),
        in_specs=[pl.BlockSpec((1, S, GD), lambda b: (b, 0, PF_KCR // GD)),
                  pl.BlockSpec((1, S, GD), lambda b: (b, 0, PF_VCR // GD)),
                  pl.BlockSpec((CMP_BLOCK, GD), const2), pl.BlockSpec((CMP_BLOCK, GD), const2),
                  pl.BlockSpec((CMP_BLOCK, GD, GH), const3), pl.BlockSpec((CMP_BLOCK, GD, GH), const3),
                  pl.BlockSpec((1, GH), const2), pl.BlockSpec((1, GH), const2),
                  pl.BlockSpec((GH, GD), const2), pl.BlockSpec((GH, GD), const2)],
        out_specs=[pl.BlockSpec((1,) + k_shape, lambda b: (b, 0, 0, 0)),
                   pl.BlockSpec((1,) + v_shape, lambda b: (b, 0, 0, 0))],
        out_shape=[jax.ShapeDtypeStruct((B,) + k_shape, F32), jax.ShapeDtypeStruct((B,) + v_shape, F32)],
        compiler_params=_params("arbitrary"),
        name="compress",
    )(pf3, pf3, posk, posv, w1k, w1v, b1k, b1v, w2k, w2v)


SEL_LANE0 = 64
SEL_LANES = 32
POS_HI_LANE, POS_LO_LANE, PAD_LANE = 96, 97, 98
ONES_LANE = 64
MASK_BIG = 1e30


def _cmpattn_kernel(q_ref, kc_ref, vct_ref, gate_ref, ocmp_ref, sel_ref, *, tq, NP, NS):
    NC = NP - (CMP_BLOCK // CMP_STRIDE - 1)
    tile = pl.program_id(1)

    def run(rows):
        _cmpattn_tile(q_ref, kc_ref, vct_ref, gate_ref, ocmp_ref, sel_ref, tile * tq,
                      tq=tq, NP=rows, NC=NC, NS=NS)

    for v in range(pl.cdiv(NP * CMP_STRIDE, tq)):
        pl.when(tile == v)(functools.partial(run, min(NP, (v + 1) * tq // CMP_STRIDE)))


def _cmpattn_tile(q_ref, kc_ref, vct_ref, gate_ref, ocmp_ref, sel_ref, t0, *, tq, NP, NC, NS):
    tcol = t0 + lax.broadcasted_iota(jnp.int32, (NP, tq), 1)
    cidx = lax.broadcasted_iota(jnp.int32, (NP, tq), 0)
    visible = (tcol >= cidx * CMP_STRIDE + (CMP_BLOCK - 1)) & (cidx < NC)
    mask_add = jnp.where(visible, 0.0, NEG)
    block_end = (lax.broadcasted_iota(jnp.int32, (NP, 1), 0) * CMP_STRIDE + (CMP_BLOCK - 1)).astype(F32)
    sees_any = ((t0 + lax.broadcasted_iota(jnp.int32, (1, tq), 1)) >= CMP_BLOCK - 1).astype(F32)
    jj = lax.broadcasted_iota(jnp.int32, (SEL_LANES, NP), 0)
    cc = lax.broadcasted_iota(jnp.int32, (SEL_LANES, NP), 1)
    overlap_t = ((cc * CMP_STRIDE < jj * SEL_BLOCK + SEL_BLOCK)
                 & (cc * CMP_STRIDE + CMP_BLOCK > jj * SEL_BLOCK) & (jj < NS)).astype(BF16)
    blk = lax.broadcasted_iota(jnp.int32, (SEL_LANES, tq), 0)
    blk_f = blk.astype(F32)
    is_blk = blk < NS
    cur = jnp.right_shift(t0 + lax.broadcasted_iota(jnp.int32, (SEL_LANES, tq), 1),
                          SEL_BLOCK.bit_length() - 1)
    o_t = []
    for g in range(NSA_GROUPS):
        kc_parts = _split3(kc_ref[0, g, :NP, :])
        vct = vct_ref[0, g, :, :NP].astype(BF16)
        psum = jnp.zeros((NP, tq), F32)
        for r in range(NSA_R):
            hh = g * NSA_R + r
            q = q_ref[0, :, hh * NSA_DK:(hh + 1) * NSA_DK]
            s = sum(lax.dot_general(part, q, NT_DIMS, preferred_element_type=F32) for part in kc_parts)
            s = s + (2.0 ** -(hh + 1)) * block_end + mask_add
            e = jnp.exp(s - jnp.max(s, axis=0, keepdims=True))
            p = e * (sees_any / jnp.sum(e, axis=0, keepdims=True))
            o_t.append(jnp.dot(vct, p.astype(BF16), preferred_element_type=F32))
            psum = psum + p
        imp = sum(jnp.dot(overlap_t, part, preferred_element_type=F32) for part in _split3(psum))
        imp = jnp.where((blk == 0) | (blk == cur) | (blk == cur - 1), FORCE, imp)
        imp = jnp.where(blk > cur, -FORCE, imp)
        work = jnp.where(is_blk, imp, LOWEST)
        sel = jnp.zeros((SEL_LANES, tq), F32)
        for _ in range(min(SEL_TOPK, NS)):
            top = jnp.max(work, axis=0, keepdims=True)
            first = jnp.min(jnp.where(work == top, blk_f, float(SEL_LANES)), axis=0, keepdims=True)
            pick = blk_f == first
            sel = jnp.where(pick, 1.0, sel)
            work = jnp.where(pick, LOWEST, work)
        sel = jnp.where(is_blk & (blk <= cur), sel - 1.0, jnp.where(is_blk, -1.0, 0.0))
        placed = jnp.concatenate([jnp.zeros((SEL_LANE0, tq), F32), sel,
                                  jnp.zeros((LANES - SEL_LANE0 - SEL_LANES, tq), F32)], axis=0)
        sel_ref[0, g] = placed.T.astype(BF16)
    o = jnp.concatenate(o_t, axis=0).T
    gate = jax.nn.sigmoid(gate_ref[0])
    for hh in range(NSA_HEADS):
        cols = slice(hh * NSA_DK, (hh + 1) * NSA_DK)
        ocmp_ref[0, :, cols] = gate[:, hh:hh + 1] * o[:, cols]


def _cmpattn(pb3, kcmp, vcmp, pf3, tq):
    B, S, _ = pb3.shape
    NP = S // CMP_STRIDE
    NS = S // SEL_BLOCK
    HD = NSA_HEADS * NSA_DK
    return pl.pallas_call(
        functools.partial(_cmpattn_kernel, tq=tq, NP=NP, NS=NS),
        grid=(B, S // tq),
        in_specs=[pl.BlockSpec((1, tq, HD), lambda b, i: (b, i, PB_NQ // HD)),
                  pl.BlockSpec((1, NSA_GROUPS, NP, NSA_DK), lambda b, i: (b, 0, 0, 0)),
                  pl.BlockSpec((1, NSA_GROUPS, NSA_DK, NP), lambda b, i: (b, 0, 0, 0)),
                  pl.BlockSpec((1, tq, LANES), lambda b, i: (b, i, PF_NG // LANES))],
        out_specs=[pl.BlockSpec((1, tq, HD), lambda b, i: (b, i, 0)),
                   pl.BlockSpec((1, NSA_GROUPS, tq, LANES), lambda b, i: (b, 0, i, 0))],
        out_shape=[jax.ShapeDtypeStruct((B, S, HD), F32),
                   jax.ShapeDtypeStruct((B, NSA_GROUPS, S, LANES), BF16)],
        compiler_params=_params("arbitrary", "arbitrary"),
        name="cmpattn",
    )(pb3, kcmp, vcmp, pf3)


def _selwin_kernel(q_ref, ks_ref, vs_ref, kw_ref, vw_ref, sel_ref, wmask_ref, dmask_ref, gate_ref, ocmp_ref,
                   o_ref, ksx, vsx, kwx, vwx, *, tq, KC, S):
    WL = WINDOW + tq
    RB = 512
    i = pl.program_id(1)
    t0 = pl.multiple_of(i * tq, tq)
    sel_shift = SEL_BLOCK.bit_length() - 1

    @pl.when(i == 0)
    def _build():
        lane = lax.broadcasted_iota(jnp.int32, (RB, LANES), 1)
        is_head = lane < NSA_DK
        pad_k = jnp.where(lane == PAD_LANE, MASK_BIG, 0.0).astype(BF16)
        for g in range(NSA_GROUPS):
            kwx[g, 0:WINDOW, :] = pad_k[:WINDOW]
            vwx[g, 0:WINDOW, :] = jnp.zeros((WINDOW, LANES), BF16)
        for piece in range(S // RB):
            r0 = piece * RB
            pos = r0 + lax.broadcasted_iota(jnp.int32, (RB, LANES), 0)
            alibi = jnp.where(lane == POS_HI_LANE, jnp.right_shift(pos, sel_shift),
                              jnp.where(lane == POS_LO_LANE, pos & (SEL_BLOCK - 1), 0)).astype(F32)
            onehot = jnp.where(lane - SEL_LANE0 == jnp.right_shift(pos, sel_shift), MASK_BIG, 0.0)
            k_extra_win = alibi.astype(BF16)
            k_extra_sel = (alibi + onehot).astype(BF16)
            v_extra = jnp.where(lane == ONES_LANE, 1.0, 0.0).astype(BF16)
            for g in range(NSA_GROUPS):
                def head_lanes(ref):
                    t = ref[0, r0:r0 + RB, :]
                    return t if g == 0 else jnp.concatenate([t[:, NSA_DK:], t[:, :NSA_DK]], axis=1)
                ksx[g, r0:r0 + RB, :] = jnp.where(is_head, head_lanes(ks_ref), k_extra_sel)
                vsx[g, r0:r0 + RB, :] = jnp.where(is_head, head_lanes(vs_ref), v_extra)
                kwx[g, WINDOW + r0:WINDOW + r0 + RB, :] = jnp.where(is_head, head_lanes(kw_ref), k_extra_win)
                vwx[g, WINDOW + r0:WINDOW + r0 + RB, :] = jnp.where(is_head, head_lanes(vw_ref), v_extra)

    gate = jax.nn.sigmoid(gate_ref[0])
    lane = lax.broadcasted_iota(jnp.int32, (tq, LANES), 1)
    is_head = lane < NSA_DK
    diag_chunk = t0 // KC
    wmask = wmask_ref[...][None]
    dmask = dmask_ref[i % (KC // tq)][None]

    def masked_rows(s, mask):
        n = s.shape[-1]
        return (s.reshape(NSA_R, tq, n) + mask).reshape(NSA_R * tq, n)

    def normalise(acc):
        return acc[:, :NSA_DK] / acc[:, ONES_LANE:ONES_LANE + 1]

    def scores(q, k):
        half = q.shape[0] // 2
        return jnp.concatenate([lax.dot_general(q[:half], k, NT_DIMS, preferred_element_type=F32),
                                lax.dot_general(q[half:], k, NT_DIMS, preferred_element_type=F32)], axis=0)

    def weighted_values(p, v):
        half = p.shape[0] // 2
        p = p.astype(BF16)
        return jnp.concatenate([jnp.dot(p[:half], v, preferred_element_type=F32),
                                jnp.dot(p[half:], v, preferred_element_type=F32)], axis=0)

    def query_rows(g):
        sel = sel_ref[0, g].astype(F32)
        rows = []
        for r in range(NSA_R):
            hh = g * NSA_R + r
            qh = q_ref[0, :, (hh // 2) * LANES:(hh // 2 + 1) * LANES].astype(F32)
            if hh % 2:
                qh = jnp.concatenate([qh[:, NSA_DK:], qh[:, :NSA_DK]], axis=1)
            slope = 2.0 ** -(hh + 1)
            extra = jnp.where(lane == POS_HI_LANE, SEL_BLOCK * slope,
                              jnp.where(lane == POS_LO_LANE, slope, jnp.where(lane == PAD_LANE, -1.0, sel)))
            rows.append(jnp.where(is_head, qh, extra).astype(BF16))
        return jnp.concatenate(rows, axis=0)

    def tile(n_past):
        past = n_past * KC
        for g in range(NSA_GROUPS):
            q = query_rows(g)
            s = masked_rows(scores(q, kwx[g, pl.ds(t0, WL), :]), wmask)
            p = jnp.exp(s - jnp.max(s, axis=-1, keepdims=True))
            o_win = normalise(weighted_values(p, vwx[g, pl.ds(t0, WL), :]))

            s_own = masked_rows(scores(q, ksx[g, past:past + KC, :]), dmask)
            m = jnp.max(s_own, axis=-1, keepdims=True)
            if n_past:
                s_past = scores(q, ksx[g, 0:past, :])
                m = jnp.maximum(m, jnp.max(s_past, axis=-1, keepdims=True))
            acc = weighted_values(jnp.exp(s_own - m), vsx[g, past:past + KC, :])
            if n_past:
                acc = acc + weighted_values(jnp.exp(s_past - m), vsx[g, 0:past, :])
            o_sel = normalise(acc)

            for r in range(NSA_R):
                hh = g * NSA_R + r
                g_sel = gate[:, NSA_HEADS + hh:NSA_HEADS + hh + 1]
                g_win = gate[:, 2 * NSA_HEADS + hh:2 * NSA_HEADS + hh + 1]
                piece = g_sel * o_sel[r * tq:(r + 1) * tq] + g_win * o_win[r * tq:(r + 1) * tq]
                o_ref[0, :, hh * NSA_DK:(hh + 1) * NSA_DK] = (
                    ocmp_ref[0, :, hh * NSA_DK:(hh + 1) * NSA_DK] + piece).astype(BF16)

    for n_past in range(S // KC):
        pl.when(diag_chunk == n_past)(functools.partial(tile, n_past))


def _selwin(pb3, sel, pf3, ocmp, tq, KC):
    B, S, _ = pb3.shape
    HD = NSA_HEADS * NSA_DK
    WL = WINDOW + tq
    tr = jnp.arange(tq)[:, None]
    kc = jnp.arange(WL)[None, :]
    wmask = jnp.where((kc > tr) & (kc <= tr + WINDOW), 0.0, NEG).astype(F32)
    off = (jnp.arange(KC // tq) * tq)[:, None, None]
    dmask = jnp.where(jnp.arange(KC)[None, None, :] <= off + tr[None], 0.0, NEG).astype(F32)
    kv_spec = lambda off: pl.BlockSpec((1, S, LANES), lambda b, i: (b, 0, off // LANES))
    return pl.pallas_call(
        functools.partial(_selwin_kernel, tq=tq, KC=KC, S=S),
        grid=(B, S // tq),
        in_specs=[pl.BlockSpec((1, tq, HD), lambda b, i: (b, i, PB_NQ // HD)),
                  kv_spec(PB_KS), kv_spec(PB_VS), kv_spec(PB_KW), kv_spec(PB_VW),
                  pl.BlockSpec((1, NSA_GROUPS, tq, LANES), lambda b, i: (b, 0, i, 0)),
                  pl.BlockSpec((tq, WL), lambda b, i: (0, 0)),
                  pl.BlockSpec((KC // tq, tq, KC), lambda b, i: (0, 0, 0)),
                  pl.BlockSpec((1, tq, LANES), lambda b, i: (b, i, PF_NG // LANES)),
                  pl.BlockSpec((1, tq, HD), lambda b, i: (b, i, 0))],
        out_specs=pl.BlockSpec((1, tq, HD), lambda b, i: (b, i, 0)),
        out_shape=jax.ShapeDtypeStruct((B, S, HD), BF16),
        scratch_shapes=[pltpu.VMEM((NSA_GROUPS, S, LANES), BF16), pltpu.VMEM((NSA_GROUPS, S, LANES), BF16),
                        pltpu.VMEM((NSA_GROUPS, WINDOW + S, LANES), BF16),
                        pltpu.VMEM((NSA_GROUPS, WINDOW + S, LANES), BF16)],
        compiler_params=_params("arbitrary", "arbitrary"),
        name="selwin",
    )(pb3, pb3, pb3, pb3, pb3, sel, wmask, dmask, pf3, ocmp)


SUBLANES = 8


def _mix_ffn_kernel(x_ref, ya_ref, yb_ref, ga_ref, gb_ref, wa_ref, wb_ref, wo_ref, gn_ref,
                    wup_ref, cw_ref, cb_ref, wd_ref, gf_ref, o_ref, tail_ref, u_ref, *, tm, tf):
    @pl.when(pl.program_id(1) == 0)
    def _():
        tail_ref[...] = jnp.zeros_like(tail_ref)

    y_a = jnp.dot(ya_ref[0], wa_ref[...], preferred_element_type=F32)
    y_b = jnp.dot(yb_ref[0], wb_ref[...], preferred_element_type=F32)
    merged = (jax.nn.sigmoid(ga_ref[0].astype(F32)) * y_a + jax.nn.sigmoid(gb_ref[0].astype(F32)) * y_b)
    x1 = x_ref[0] + jnp.dot(merged.astype(BF16), wo_ref[...], preferred_element_type=F32)
    o_ref[0] = x1
    h = _rms(x1, gn_ref[...]).astype(BF16)

    top = lax.broadcasted_iota(jnp.int32, (SUBLANES, tf), 0)
    for c in range(D_FF // tf):
        cols = slice(c * tf, (c + 1) * tf)
        a = jnp.dot(h, wup_ref[:, cols], preferred_element_type=F32)
        b = jnp.dot(h, wup_ref[:, D_FF + c * tf:D_FF + (c + 1) * tf], preferred_element_type=F32)
        tail = tail_ref[c]
        tail_ref[c] = a[tm - SUBLANES:]
        ac = cb_ref[:, cols] + cw_ref[CONV_W - 1:CONV_W, cols] * a
        for d in range(1, CONV_W):
            sh = pltpu.roll(a, d, axis=0)
            head = jnp.where(top < d, pltpu.roll(tail, d, axis=0), sh[:SUBLANES])
            sh = jnp.concatenate([head, sh[SUBLANES:]], axis=0)
            ac = ac + cw_ref[CONV_W - 1 - d:CONV_W - d, cols] * sh
        u_ref[:, cols] = (jax.nn.gelu(ac) * b).astype(BF16)
    y = jnp.dot(u_ref[...], wd_ref[...], preferred_element_type=F32)
    o_ref[0] = _rms(o_ref[0] + y, gf_ref[...])


def _mix_ffn(x, ya, yb, pb3, wa, wb, wo, gn, w_up, conv_w, conv_b, w_down, gf, tm, tf):
    B, S, _ = x.shape
    HV = RET_HEADS * RET_DV
    HD = NSA_HEADS * NSA_DK
    rows = lambda width, col=0: pl.BlockSpec((1, tm, width), lambda b, i: (b, i, col))
    whole = lambda *shape: pl.BlockSpec(shape, lambda b, i: (0,) * len(shape), pipeline_mode=pl.Buffered(1))
    return pl.pallas_call(
        functools.partial(_mix_ffn_kernel, tm=tm, tf=tf),
        grid=(B, S // tm),
        in_specs=[rows(D_MODEL), rows(HV), rows(HD),
                  rows(D_MODEL, PB_GA // D_MODEL), rows(D_MODEL, PB_GB // D_MODEL),
                  whole(HV, D_MODEL), whole(HD, D_MODEL), whole(D_MODEL, D_MODEL), whole(1, D_MODEL),
                  whole(D_MODEL, 2 * D_FF), whole(CONV_W, D_FF), whole(1, D_FF), whole(D_FF, D_MODEL),
                  whole(1, D_MODEL)],
        out_specs=rows(D_MODEL),
        out_shape=jax.ShapeDtypeStruct((B, S, D_MODEL), F32),
        scratch_shapes=[pltpu.VMEM((D_FF // tf, SUBLANES, tf), F32), pltpu.VMEM((tm, D_FF), BF16)],
        compiler_params=_params("arbitrary", "arbitrary"),
        name="mix_ffn",
    )(x, ya, yb, pb3, pb3, wa, wb, wo, gn, w_up, conv_w, conv_b, w_down, gf)


def _block_diag_groups(w):
    z = jnp.zeros_like(w)
    return jnp.concatenate([jnp.concatenate([w, z], axis=-1), jnp.concatenate([z, w], axis=-1)], axis=-2)


def kernel(x, norm_mix, w_in, ret_gn_g, cmp_pos_k, cmp_w1_k, cmp_b1_k, cmp_w2_k, cmp_pos_v, cmp_w1_v,
           cmp_b1_v, cmp_w2_v, w_ret_o, w_nsa_o, w_out, norm_ffn, w_up, conv_w, conv_b, w_down,
           norm_final):
    B, S, D = x.shape
    assert D == D_MODEL and NSA_GROUPS == 2 and norm_mix.shape[0] == 1
    assert S % 512 == 0 and S >= 1024 and S // CMP_STRIDE <= LANES and S // SEL_BLOCK <= SEL_LANES
    M = B * S
    x2 = x.reshape(M, D)

    w_bf = lax.optimization_barrier(w_in.astype(BF16))[0]
    w_gates = w_bf[:, W_GA:]

    def cmp_weights(pos, w1, b1, w2):
        return (jnp.tile(pos, (1, NSA_GROUPS)),
                _block_diag_groups(w1.reshape(CMP_BLOCK, NSA_DK, CMP_HIDDEN)).astype(BF16),
                jnp.tile(b1, NSA_GROUPS)[None, :],
                _block_diag_groups(w2).astype(BF16))

    posk, w1k, b1k, w2k = cmp_weights(cmp_pos_k[0], cmp_w1_k[0], cmp_b1_k[0], cmp_w2_k[0])
    posv, w1v, b1v, w2v = cmp_weights(cmp_pos_v[0], cmp_w1_v[0], cmp_b1_v[0], cmp_w2_v[0])

    pb, pf, ya = _inproj_ret(x2, norm_mix, w_bf, w_gates, ret_gn_g, S, tm=512, C=256)
    pb3 = pb.reshape(B, S, PB_W)
    pf3 = pf.reshape(B, S, PF_W)
    ya = ya.reshape(B, S, -1)

    kcmp, vcmp = _compress(pf3, posk, posv, w1k, w1v, b1k, b1v, w2k, w2v)
    ocmp, sel = _cmpattn(pb3, kcmp, vcmp, pf3, tq=1024)
    yb = _selwin(pb3, sel, pf3, ocmp, tq=256, KC=512)

    return _mix_ffn(x, ya, yb, pb3, w_ret_o[0].astype(BF16), w_nsa_o[0].astype(BF16),
                    w_out[0].astype(BF16), norm_ffn, w_up[0].astype(BF16), conv_w[0], conv_b,
                    w_down[0].astype(BF16), norm_final[None, :], tm=512, tf=256)
```

```python
import functools
import math

import numpy as np
import jax
import jax.numpy as jnp
from jax import lax
from jax.experimental import pallas as pl
from jax.experimental.pallas import tpu as pltpu

F32 = jnp.float32
BF16 = jnp.bfloat16

D_MODEL = 1024
RET_HEADS = 4
RET_DK = 128
RET_DV = 256
NSA_HEADS = 8
NSA_GROUPS = 2
NSA_R = NSA_HEADS // NSA_GROUPS
NSA_DK = 64
CMP_BLOCK = 32
CMP_STRIDE = 16
CMP_HIDDEN = 256
SEL_BLOCK = 64
SEL_TOPK = 8
WINDOW = 512
D_FF = 2816
CONV_W = 3
EPS = 1e-6
NEG = -1e30
FORCE = 1e9
LOWEST = -3e38

IN_SIZES = (RET_HEADS * RET_DK, RET_HEADS * RET_DK, RET_HEADS * RET_DV, RET_HEADS * RET_DV,
            NSA_HEADS * NSA_DK,
            NSA_GROUPS * NSA_DK, NSA_GROUPS * NSA_DK, NSA_GROUPS * NSA_DK,
            NSA_GROUPS * NSA_DK, NSA_GROUPS * NSA_DK, NSA_GROUPS * NSA_DK,
            3 * NSA_HEADS, D_MODEL, D_MODEL)

LANES = 128
VMEM_LIMIT = 56 * 1024 * 1024

_W_OFF = np.concatenate([[0], np.cumsum(IN_SIZES)]).tolist()
W_NQ, W_KCR, W_KS, W_NG, W_GA = _W_OFF[4], _W_OFF[5], _W_OFF[7], _W_OFF[11], _W_OFF[12]

RT_RQ, RT_RK, RT_RV, RT_RG = 0, 512, 1024, 2048
RT_W = 3072
PB_NQ, PB_KS, PB_VS, PB_KW, PB_VW, PB_GA, PB_GB = 0, 512, 640, 768, 896, 1024, 2048
PB_W = 3072
PF_KCR, PF_VCR, PF_NG = 0, 128, 256
PF_W = 384

NT_DIMS = (((1,), (1,)), ((), ()))
TN_DIMS = (((0,), (0,)), ((), ()))


def _params(*sem):
    return pltpu.CompilerParams(dimension_semantics=sem, vmem_limit_bytes=VMEM_LIMIT)


def _rms(x, g):
    return x * lax.rsqrt(jnp.mean(x * x, axis=-1, keepdims=True) + EPS) * g


def _split3(x):
    hi = x.astype(BF16)
    rest = x - hi.astype(F32)
    mid = rest.astype(BF16)
    return hi, mid, (rest - mid.astype(F32)).astype(BF16)


def _inproj_ret_kernel(x_ref, g_ref, w_ref, wg_ref, gn_ref, pb_ref, pf_ref, ya_ref,
                       stash0, stash1, r_ref, dmask_ref, xi_ref, zeta_ref, *, tm, C, S):
    log_gamma = [float(np.log1p(-np.exp2(np.float32(-5.0 - h)))) for h in range(RET_HEADS)]
    scale = RET_DK ** -0.5
    i = pl.program_id(0)

    @pl.when(i == 0)
    def _():
        diff = (lax.broadcasted_iota(jnp.int32, (C, C), 0)
                - lax.broadcasted_iota(jnp.int32, (C, C), 1)).astype(F32)
        pos_v = lax.broadcasted_iota(jnp.int32, (C, RET_DV), 0).astype(F32)
        pos_k = lax.broadcasted_iota(jnp.int32, (C, RET_DK), 0).astype(F32)
        for h in range(RET_HEADS):
            lg = log_gamma[h]
            dmask_ref[h] = jnp.where(diff >= 0, jnp.exp(lg * jnp.maximum(diff, 0.0)), 0.0) * scale
            xi_ref[h] = jnp.exp(lg * (pos_v + 1.0))
            zeta_ref[h] = jnp.exp(lg * (C - 1.0 - pos_k)) * scale
        stash1[...] = jnp.zeros_like(stash1)
        r_ref[...] = jnp.zeros_like(r_ref)

    seq_start = lax.rem(i + (S // tm) - 1, S // tm) == 0

    def step(new_ref, old_ref):
        for cc in range(tm // C):
            rows = slice(cc * C, (cc + 1) * C)
            for h in range(RET_HEADS):
                lg = log_gamma[h]
                q = old_ref[rows, RT_RQ + h * RET_DK:RT_RQ + (h + 1) * RET_DK]
                k = old_ref[rows, RT_RK + h * RET_DK:RT_RK + (h + 1) * RET_DK]
                v = old_ref[rows, RT_RV + h * RET_DV:RT_RV + (h + 1) * RET_DV]
                g = old_ref[rows, RT_RG + h * RET_DV:RT_RG + (h + 1) * RET_DV].astype(F32)
                s = lax.dot_general(q, k, NT_DIMS, preferred_element_type=F32)
                o = jnp.dot((s * dmask_ref[h]).astype(BF16), v, preferred_element_type=F32)
                state = r_ref[h]
                if cc == 0:
                    state = jnp.where(seq_start, 0.0, state)
                o = o + xi_ref[h] * jnp.dot(q, state.astype(BF16), preferred_element_type=F32)
                kz = (k.astype(F32) * zeta_ref[h]).astype(BF16)
                kv = lax.dot_general(kz, v, TN_DIMS, preferred_element_type=F32)
                r_ref[h] = math.exp(lg * C) * state + kv
                mu = jnp.mean(o, axis=-1, keepdims=True)
                d = o - mu
                var = jnp.mean(d * d, axis=-1, keepdims=True)
                on = d * lax.rsqrt(var + EPS) * gn_ref[:, h * RET_DV:(h + 1) * RET_DV]
                ya_ref[rows, h * RET_DV:(h + 1) * RET_DV] = (g * jax.nn.sigmoid(g) * on).astype(BF16)

        hb = _rms(x_ref[...], g_ref[...]).astype(BF16)
        for c0 in range(0, RT_W, 512):
            new_ref[:, c0:c0 + 512] = jnp.dot(
                hb, w_ref[:, c0:c0 + 512], preferred_element_type=F32).astype(BF16)

        def project(w, c0, c1):
            return jnp.dot(hb, w[:, c0:c1], preferred_element_type=F32)

        pb_ref[:, PB_NQ:PB_KS] = (project(w_ref, W_NQ, W_KCR) * (NSA_DK ** -0.5)).astype(BF16)
        pb_ref[:, PB_KS:PB_GA] = project(w_ref, W_KS, W_NG).astype(BF16)
        for c0 in range(0, PB_W - PB_GA, 512):
            pb_ref[:, PB_GA + c0:PB_GA + c0 + 512] = project(wg_ref, c0, c0 + 512).astype(BF16)
        pf_ref[:, PF_KCR:PF_NG] = project(w_ref, W_KCR, W_KS)
        pf_ref[:, PF_NG:PF_W] = project(w_ref, W_NG, W_NG + PF_W - PF_NG)

    pl.when(lax.rem(i, 2) == 0)(functools.partial(step, stash0, stash1))
    pl.when(lax.rem(i, 2) == 1)(functools.partial(step, stash1, stash0))


def _inproj_ret(x2, g, w, w_gates, gn_g, S, tm, C):
    M = x2.shape[0]
    nt = M // tm
    HV = RET_HEADS * RET_DV
    cur = lambda i: (jnp.minimum(i, nt - 1), 0)
    prev = lambda i: (jnp.maximum(i - 1, 0), 0)
    const = lambda i: (0, 0)
    return pl.pallas_call(
        functools.partial(_inproj_ret_kernel, tm=tm, C=C, S=S),
        grid=(nt + 1,),
        in_specs=[pl.BlockSpec((tm, D_MODEL), cur),
                  pl.BlockSpec((1, D_MODEL), const),
                  pl.BlockSpec(w.shape, const, pipeline_mode=pl.Buffered(1)),
                  pl.BlockSpec(w_gates.shape, const, pipeline_mode=pl.Buffered(1)),
                  pl.BlockSpec((1, HV), const)],
        out_specs=[pl.BlockSpec((tm, PB_W), cur), pl.BlockSpec((tm, PF_W), cur), pl.BlockSpec((tm, HV), prev)],
        out_shape=[jax.ShapeDtypeStruct((M, PB_W), BF16), jax.ShapeDtypeStruct((M, PF_W), F32),
                   jax.ShapeDtypeStruct((M, HV), BF16)],
        scratch_shapes=[pltpu.VMEM((tm, RT_W), BF16), pltpu.VMEM((tm, RT_W), BF16),
                        pltpu.VMEM((RET_HEADS, RET_DK, RET_DV), F32), pltpu.VMEM((RET_HEADS, C, C), F32),
                        pltpu.VMEM((RET_HEADS, C, RET_DV), F32), pltpu.VMEM((RET_HEADS, C, RET_DK), F32)],
        compiler_params=_params("arbitrary"),
        name="inproj_ret",
    )(x2, g, w, w_gates, gn_g)


def _compress_kernel(kin_ref, vin_ref, posk_ref, posv_ref, w1k_ref, w1v_ref, b1k_ref, b1v_ref,
                     w2k_ref, w2v_ref, ko_ref, vo_ref, *, NP):
    GD = NSA_GROUPS * NSA_DK
    npc = CMP_BLOCK // CMP_STRIDE
    row = lax.broadcasted_iota(jnp.int32, (NP, GD), 0)
    for in_ref, pos_ref, w1_ref, b1_ref, w2_ref, out_ref in (
            (kin_ref, posk_ref, w1k_ref, b1k_ref, w2k_ref, ko_ref),
            (vin_ref, posv_ref, w1v_ref, b1v_ref, w2v_ref, vo_ref)):
        parts = []
        for p in range(npc):
            acc = jnp.zeros((NP, NSA_GROUPS * CMP_HIDDEN), F32)
            for i in range(0, CMP_STRIDE, 2):
                j = p * CMP_STRIDE + i
                toks = [(in_ref[0, pl.ds(i + d, NP, stride=CMP_STRIDE), :] + pos_ref[j + d:j + d + 1, :])
                        .astype(BF16) for d in range(2)]
                acc = acc + jnp.dot(jnp.concatenate(toks, axis=1), w1_ref[j:j + 2].reshape(2 * GD, -1),
                                    preferred_element_type=F32)
            parts.append(acc)
        hidden = parts[0]
        for p in range(1, npc):
            hidden = hidden + pltpu.roll(parts[p], NP - p, axis=0)
        act = jax.nn.gelu(hidden + b1_ref[...]).astype(BF16)
        res = jnp.dot(act, w2_ref[...], preferred_element_type=F32)
        res = jnp.where(row < NP - (npc - 1), res, 0.0)
        if out_ref is vo_ref:
            res = res.T
            for g in range(NSA_GROUPS):
                out_ref[0, g] = res[g * NSA_DK:(g + 1) * NSA_DK, :]
        else:
            for g in range(NSA_GROUPS):
                out_ref[0, g] = res[:, g * NSA_DK:(g + 1) * NSA_DK]


def _compress(pf3, posk, posv, w1k, w1v, b1k, b1v, w2k, w2v):
    B, S, _ = pf3.shape
    NP = S // CMP_STRIDE
    GD = NSA_GROUPS * NSA_DK
    GH = NSA_GROUPS * CMP_HIDDEN
    const2 = lambda b: (0, 0)
    const3 = lambda b: (0, 0, 0)
    k_shape = (NSA_GROUPS, NP, NSA_DK)
    v_shape = (NSA_GROUPS, NSA_DK, NP)
    return pl.pallas_call(
        functools.partial(_compress_kernel, NP=NP),
        grid=(B,),
        in_specs=[pl.BlockSpec((1, S, GD), lambda b: (b, 0, PF_KCR // GD)),
                  pl.BlockSpec((1, S, GD), lambda b: (b, 0, PF_VCR // GD)),
                  pl.BlockSpec((CMP_BLOCK, GD), const2), pl.BlockSpec((CMP_BLOCK, GD), const2),
                  pl.BlockSpec((CMP_BLOCK, GD, GH), const3), pl.BlockSpec((CMP_BLOCK, GD, GH), const3),
                  pl.BlockSpec((1, GH), const2), pl.BlockSpec((1, GH), const2),
                  pl.BlockSpec((GH, GD), const2), pl.BlockSpec((GH, GD), const2)],
        out_specs=[pl.BlockSpec((1,) + k_shape, lambda b: (b, 0, 0, 0)),
                   pl.BlockSpec((1,) + v_shape, lambda b: (b, 0, 0, 0))],
        out_shape=[jax.ShapeDtypeStruct((B,) + k_shape, F32), jax.ShapeDtypeStruct((B,) + v_shape, F32)],
        compiler_params=_params("arbitrary"),
        name="compress",
    )(pf3, pf3, posk, posv, w1k, w1v, b1k, b1v, w2k, w2v)


SEL_LANE0 = 64
SEL_LANES = 32
POS_HI_LANE, POS_LO_LANE, PAD_LANE = 96, 97, 98
ONES_LANE = 64
MASK_BIG = 1e30


def _cmpattn_kernel(q_ref, kc_ref, vct_ref, gate_ref, ocmp_ref, sel_ref, *, tq, NP, NS):
    NC = NP - (CMP_BLOCK // CMP_STRIDE - 1)
    tile = pl.program_id(1)

    def run(rows):
        _cmpattn_tile(q_ref, kc_ref, vct_ref, gate_ref, ocmp_ref, sel_ref, tile * tq,
                      tq=tq, NP=rows, NC=NC, NS=NS)

    for v in range(pl.cdiv(NP * CMP_STRIDE, tq)):
        pl.when(tile == v)(functools.partial(run, min(NP, (v + 1) * tq // CMP_STRIDE)))


def _cmpattn_tile(q_ref, kc_ref, vct_ref, gate_ref, ocmp_ref, sel_ref, t0, *, tq, NP, NC, NS):
    tcol = t0 + lax.broadcasted_iota(jnp.int32, (NP, tq), 1)
    cidx = lax.broadcasted_iota(jnp.int32, (NP, tq), 0)
    visible = (tcol >= cidx * CMP_STRIDE + (CMP_BLOCK - 1)) & (cidx < NC)
    mask_add = jnp.where(visible, 0.0, NEG)
    block_end = (lax.broadcasted_iota(jnp.int32, (NP, 1), 0) * CMP_STRIDE + (CMP_BLOCK - 1)).astype(F32)
    sees_any = ((t0 + lax.broadcasted_iota(jnp.int32, (1, tq), 1)) >= CMP_BLOCK - 1).astype(F32)
    jj = lax.broadcasted_iota(jnp.int32, (SEL_LANES, NP), 0)
    cc = lax.broadcasted_iota(jnp.int32, (SEL_LANES, NP), 1)
    overlap_t = ((cc * CMP_STRIDE < jj * SEL_BLOCK + SEL_BLOCK)
                 & (cc * CMP_STRIDE + CMP_BLOCK > jj * SEL_BLOCK) & (jj < NS)).astype(BF16)
    blk = lax.broadcasted_iota(jnp.int32, (SEL_LANES, tq), 0)
    blk_f = blk.astype(F32)
    is_blk = blk < NS
    cur = jnp.right_shift(t0 + lax.broadcasted_iota(jnp.int32, (SEL_LANES, tq), 1),
                          SEL_BLOCK.bit_length() - 1)
    o_t = []
    for g in range(NSA_GROUPS):
        kc_parts = _split3(kc_ref[0, g, :NP, :])
        vct = vct_ref[0, g, :, :NP].astype(BF16)
        psum = jnp.zeros((NP, tq), F32)
        for r in range(NSA_R):
            hh = g * NSA_R + r
            q = q_ref[0, :, hh * NSA_DK:(hh + 1) * NSA_DK]
            s = sum(lax.dot_general(part, q, NT_DIMS, preferred_element_type=F32) for part in kc_parts)
            s = s + (2.0 ** -(hh + 1)) * block_end + mask_add
            e = jnp.exp(s - jnp.max(s, axis=0, keepdims=True))
            p = e * (sees_any / jnp.sum(e, axis=0, keepdims=True))
            o_t.append(jnp.dot(vct, p.astype(BF16), preferred_element_type=F32))
            psum = psum + p
        imp = sum(jnp.dot(overlap_t, part, preferred_element_type=F32) for part in _split3(psum))
        imp = jnp.where((blk == 0) | (blk == cur) | (blk == cur - 1), FORCE, imp)
        imp = jnp.where(blk > cur, -FORCE, imp)
        work = jnp.where(is_blk, imp, LOWEST)
        sel = jnp.zeros((SEL_LANES, tq), F32)
        for _ in range(min(SEL_TOPK, NS)):
            top = jnp.max(work, axis=0, keepdims=True)
            first = jnp.min(jnp.where(work == top, blk_f, float(SEL_LANES)), axis=0, keepdims=True)
            pick = blk_f == first
            sel = jnp.where(pick, 1.0, sel)
            work = jnp.where(pick, LOWEST, work)
        sel = jnp.where(is_blk & (blk <= cur), sel - 1.0, jnp.where(is_blk, -1.0, 0.0))
        placed = jnp.concatenate([jnp.zeros((SEL_LANE0, tq), F32), sel,
                                  jnp.zeros((LANES - SEL_LANE0 - SEL_LANES, tq), F32)], axis=0)
        sel_ref[0, g] = placed.T.astype(BF16)
    o = jnp.concatenate(o_t, axis=0).T
    gate = jax.nn.sigmoid(gate_ref[0])
    for hh in range(NSA_HEADS):
        cols = slice(hh * NSA_DK, (hh + 1) * NSA_DK)
        ocmp_ref[0, :, cols] = gate[:, hh:hh + 1] * o[:, cols]


def _cmpattn(pb3, kcmp, vcmp, pf3, tq):
    B, S, _ = pb3.shape
    NP = S // CMP_STRIDE
    NS = S // SEL_BLOCK
    HD = NSA_HEADS * NSA_DK
    return pl.pallas_call(
        functools.partial(_cmpattn_kernel, tq=tq, NP=NP, NS=NS),
        grid=(B, S // tq),
        in_specs=[pl.BlockSpec((1, tq, HD), lambda b, i: (b, i, PB_NQ // HD)),
                  pl.BlockSpec((1, NSA_GROUPS, NP, NSA_DK), lambda b, i: (b, 0, 0, 0)),
                  pl.BlockSpec((1, NSA_GROUPS, NSA_DK, NP), lambda b, i: (b, 0, 0, 0)),
                  pl.BlockSpec((1, tq, LANES), lambda b, i: (b, i, PF_NG // LANES))],
        out_specs=[pl.BlockSpec((1, tq, HD), lambda b, i: (b, i, 0)),
                   pl.BlockSpec((1, NSA_GROUPS, tq, LANES), lambda b, i: (b, 0, i, 0))],
        out_shape=[jax.ShapeDtypeStruct((B, S, HD), F32),
                   jax.ShapeDtypeStruct((B, NSA_GROUPS, S, LANES), BF16)],
        compiler_params=_params("arbitrary", "arbitrary"),
        name="cmpattn",
    )(pb3, kcmp, vcmp, pf3)


def _selwin_kernel(q_ref, ks_ref, vs_ref, kw_ref, vw_ref, sel_ref, wmask_ref, dmask_ref, gate_ref, ocmp_ref,
                   o_ref, ksx, vsx, kwx, vwx, *, tq, KC, S):
    WL = WINDOW + tq
    RB = 512
    i = pl.program_id(1)
    t0 = pl.multiple_of(i * tq, tq)
    sel_shift = SEL_BLOCK.bit_length() - 1

    @pl.when(i == 0)
    def _build():
        lane = lax.broadcasted_iota(jnp.int32, (RB, LANES), 1)
        is_head = lane < NSA_DK
        pad_k = jnp.where(lane == PAD_LANE, MASK_BIG, 0.0).astype(BF16)
        for g in range(NSA_GROUPS):
            kwx[g, 0:WINDOW, :] = pad_k[:WINDOW]
            vwx[g, 0:WINDOW, :] = jnp.zeros((WINDOW, LANES), BF16)
        for piece in range(S // RB):
            r0 = piece * RB
            pos = r0 + lax.broadcasted_iota(jnp.int32, (RB, LANES), 0)
            alibi = jnp.where(lane == POS_HI_LANE, jnp.right_shift(pos, sel_shift),
                              jnp.where(lane == POS_LO_LANE, pos & (SEL_BLOCK - 1), 0)).astype(F32)
            onehot = jnp.where(lane - SEL_LANE0 == jnp.right_shift(pos, sel_shift), MASK_BIG, 0.0)
            k_extra_win = alibi.astype(BF16)
            k_extra_sel = (alibi + onehot).astype(BF16)
            v_extra = jnp.where(lane == ONES_LANE, 1.0, 0.0).astype(BF16)
            for g in range(NSA_GROUPS):
                def head_lanes(ref):
                    t = ref[0, r0:r0 + RB, :]
                    return t if g == 0 else jnp.concatenate([t[:, NSA_DK:], t[:, :NSA_DK]], axis=1)
                ksx[g, r0:r0 + RB, :] = jnp.where(is_head, head_lanes(ks_ref), k_extra_sel)
                vsx[g, r0:r0 + RB, :] = jnp.where(is_head, head_lanes(vs_ref), v_extra)
                kwx[g, WINDOW + r0:WINDOW + r0 + RB, :] = jnp.where(is_head, head_lanes(kw_ref), k_extra_win)
                vwx[g, WINDOW + r0:WINDOW + r0 + RB, :] = jnp.where(is_head, head_lanes(vw_ref), v_extra)

    gate = jax.nn.sigmoid(gate_ref[0])
    lane = lax.broadcasted_iota(jnp.int32, (tq, LANES), 1)
    is_head = lane < NSA_DK
    diag_chunk = t0 // KC
    wmask = wmask_ref[...][None]
    dmask = dmask_ref[i % (KC // tq)][None]

    def masked_rows(s, mask):
        n = s.shape[-1]
        return (s.reshape(NSA_R, tq, n) + mask).reshape(NSA_R * tq, n)

    def normalise(acc):
        return acc[:, :NSA_DK] / acc[:, ONES_LANE:ONES_LANE + 1]

    def scores(q, k):
        half = q.shape[0] // 2
        return jnp.concatenate([lax.dot_general(q[:half], k, NT_DIMS, preferred_element_type=F32),
                                lax.dot_general(q[half:], k, NT_DIMS, preferred_element_type=F32)], axis=0)

    def weighted_values(p, v):
        half = p.shape[0] // 2
        p = p.astype(BF16)
        return jnp.concatenate([jnp.dot(p[:half], v, preferred_element_type=F32),
                                jnp.dot(p[half:], v, preferred_element_type=F32)], axis=0)

    def query_rows(g):
        sel = sel_ref[0, g].astype(F32)
        rows = []
        for r in range(NSA_R):
            hh = g * NSA_R + r
            qh = q_ref[0, :, (hh // 2) * LANES:(hh // 2 + 1) * LANES].astype(F32)
            if hh % 2:
                qh = jnp.concatenate([qh[:, NSA_DK:], qh[:, :NSA_DK]], axis=1)
            slope = 2.0 ** -(hh + 1)
            extra = jnp.where(lane == POS_HI_LANE, SEL_BLOCK * slope,
                              jnp.where(lane == POS_LO_LANE, slope, jnp.where(lane == PAD_LANE, -1.0, sel)))
            rows.append(jnp.where(is_head, qh, extra).astype(BF16))
        return jnp.concatenate(rows, axis=0)

    def tile(n_past):
        past = n_past * KC
        for g in range(NSA_GROUPS):
            q = query_rows(g)
            s = masked_rows(scores(q, kwx[g, pl.ds(t0, WL), :]), wmask)
            p = jnp.exp(s - jnp.max(s, axis=-1, keepdims=True))
            o_win = normalise(weighted_values(p, vwx[g, pl.ds(t0, WL), :]))

            s_own = masked_rows(scores(q, ksx[g, past:past + KC, :]), dmask)
            m = jnp.max(s_own, axis=-1, keepdims=True)
            if n_past:
                s_past = scores(q, ksx[g, 0:past, :])
                m = jnp.maximum(m, jnp.max(s_past, axis=-1, keepdims=True))
            acc = weighted_values(jnp.exp(s_own - m), vsx[g, past:past + KC, :])
            if n_past:
                acc = acc + weighted_values(jnp.exp(s_past - m), vsx[g, 0:past, :])
            o_sel = normalise(acc)

            for r in range(NSA_R):
                hh = g * NSA_R + r
                g_sel = gate[:, NSA_HEADS + hh:NSA_HEADS + hh + 1]
                g_win = gate[:, 2 * NSA_HEADS + hh:2 * NSA_HEADS + hh + 1]
                piece = g_sel * o_sel[r * tq:(r + 1) * tq] + g_win * o_win[r * tq:(r + 1) * tq]
                o_ref[0, :, hh * NSA_DK:(hh + 1) * NSA_DK] = (
                    ocmp_ref[0, :, hh * NSA_DK:(hh + 1) * NSA_DK] + piece).astype(BF16)

    for n_past in range(S // KC):
        pl.when(diag_chunk == n_past)(functools.partial(tile, n_past))


def _selwin(pb3, sel, pf3, ocmp, tq, KC):
    B, S, _ = pb3.shape
    HD = NSA_HEADS * NSA_DK
    WL = WINDOW + tq
    tr = jnp.arange(tq)[:, None]
    kc = jnp.arange(WL)[None, :]
    wmask = jnp.where((kc > tr) & (kc <= tr + WINDOW), 0.0, NEG).astype(F32)
    off = (jnp.arange(KC // tq) * tq)[:, None, None]
    dmask = jnp.where(jnp.arange(KC)[None, None, :] <= off + tr[None], 0.0, NEG).astype(F32)
    kv_spec = lambda off: pl.BlockSpec((1, S, LANES), lambda b, i: (b, 0, off // LANES))
    return pl.pallas_call(
        functools.partial(_selwin_kernel, tq=tq, KC=KC, S=S),
        grid=(B, S // tq),
        in_specs=[pl.BlockSpec((1, tq, HD), lambda b, i: (b, i, PB_NQ // HD)),
                  kv_spec(PB_KS), kv_spec(PB_VS), kv_spec(PB_KW), kv_spec(PB_VW),
                  pl.BlockSpec((1, NSA_GROUPS, tq, LANES), lambda b, i: (b, 0, i, 0)),
                  pl.BlockSpec((tq, WL), lambda b, i: (0, 0)),
                  pl.BlockSpec((KC // tq, tq, KC), lambda b, i: (0, 0, 0)),
                  pl.BlockSpec((1, tq, LANES), lambda b, i: (b, i, PF_NG // LANES)),
                  pl.BlockSpec((1, tq, HD), lambda b, i: (b, i, 0))],
        out_specs=pl.BlockSpec((1, tq, HD), lambda b, i: (b, i, 0)),
        out_shape=jax.ShapeDtypeStruct((B, S, HD), BF16),
        scratch_shapes=[pltpu.VMEM((NSA_GROUPS, S, LANES), BF16), pltpu.VMEM((NSA_GROUPS, S, LANES), BF16),
                        pltpu.VMEM((NSA_GROUPS, WINDOW + S, LANES), BF16),
                        pltpu.VMEM((NSA_GROUPS, WINDOW + S, LANES), BF16)],
        compiler_params=_params("arbitrary", "arbitrary"),
        name="selwin",
    )(pb3, pb3, pb3, pb3, pb3, sel, wmask, dmask, pf3, ocmp)


SUBLANES = 8


def _mix_ffn_kernel(x_ref, ya_ref, yb_ref, ga_ref, gb_ref, wa_ref, wb_ref, wo_ref, gn_ref,
                    wup_ref, cw_ref, cb_ref, wd_ref, gf_ref, o_ref, tail_ref, u_ref, *, tm, tf):
    @pl.when(pl.program_id(1) == 0)
    def _():
        tail_ref[...] = jnp.zeros_like(tail_ref)

    y_a = jnp.dot(ya_ref[0], wa_ref[...], preferred_element_type=F32)
    y_b = jnp.dot(yb_ref[0], wb_ref[...], preferred_element_type=F32)
    merged = (jax.nn.sigmoid(ga_ref[0].astype(F32)) * y_a + jax.nn.sigmoid(gb_ref[0].astype(F32)) * y_b)
    x1 = x_ref[0] + jnp.dot(merged.astype(BF16), wo_ref[...], preferred_element_type=F32)
    o_ref[0] = x1
    h = _rms(x1, gn_ref[...]).astype(BF16)

    top = lax.broadcasted_iota(jnp.int32, (SUBLANES, tf), 0)
    for c in range(D_FF // tf):
        cols = slice(c * tf, (c + 1) * tf)
        a = jnp.dot(h, wup_ref[:, cols], preferred_element_type=F32)
        b = jnp.dot(h, wup_ref[:, D_FF + c * tf:D_FF + (c + 1) * tf], preferred_element_type=F32)
        tail = tail_ref[c]
        tail_ref[c] = a[tm - SUBLANES:]
        ac = cb_ref[:, cols] + cw_ref[CONV_W - 1:CONV_W, cols] * a
        for d in range(1, CONV_W):
            sh = pltpu.roll(a, d, axis=0)
            head = jnp.where(top < d, pltpu.roll(tail, d, axis=0), sh[:SUBLANES])
            sh = jnp.concatenate([head, sh[SUBLANES:]], axis=0)
            ac = ac + cw_ref[CONV_W - 1 - d:CONV_W - d, cols] * sh
        u_ref[:, cols] = (jax.nn.gelu(ac) * b).astype(BF16)
    y = jnp.dot(u_ref[...], wd_ref[...], preferred_element_type=F32)
    o_ref[0] = _rms(o_ref[0] + y, gf_ref[...])


def _mix_ffn(x, ya, yb, pb3, wa, wb, wo, gn, w_up, conv_w, conv_b, w_down, gf, tm, tf):
    B, S, _ = x.shape
    HV = RET_HEADS * RET_DV
    HD = NSA_HEADS * NSA_DK
    rows = lambda width, col=0: pl.BlockSpec((1, tm, width), lambda b, i: (b, i, col))
    whole = lambda *shape: pl.BlockSpec(shape, lambda b, i: (0,) * len(shape), pipeline_mode=pl.Buffered(1))
    return pl.pallas_call(
        functools.partial(_mix_ffn_kernel, tm=tm, tf=tf),
        grid=(B, S // tm),
        in_specs=[rows(D_MODEL), rows(HV), rows(HD),
                  rows(D_MODEL, PB_GA // D_MODEL), rows(D_MODEL, PB_GB // D_MODEL),
                  whole(HV, D_MODEL), whole(HD, D_MODEL), whole(D_MODEL, D_MODEL), whole(1, D_MODEL),
                  whole(D_MODEL, 2 * D_FF), whole(CONV_W, D_FF), whole(1, D_FF), whole(D_FF, D_MODEL),
                  whole(1, D_MODEL)],
        out_specs=rows(D_MODEL),
        out_shape=jax.ShapeDtypeStruct((B, S, D_MODEL), F32),
        scratch_shapes=[pltpu.VMEM((D_FF // tf, SUBLANES, tf), F32), pltpu.VMEM((tm, D_FF), BF16)],
        compiler_params=_params("arbitrary", "arbitrary"),
        name="mix_ffn",
    )(x, ya, yb, pb3, pb3, wa, wb, wo, gn, w_up, conv_w, conv_b, w_down, gf)


def _block_diag_groups(w):
    z = jnp.zeros_like(w)
    return jnp.concatenate([jnp.concatenate([w, z], axis=-1), jnp.concatenate([z, w], axis=-1)], axis=-2)


def kernel(x, norm_mix, w_in, ret_gn_g, cmp_pos_k, cmp_w1_k, cmp_b1_k, cmp_w2_k, cmp_pos_v, cmp_w1_v,
           cmp_b1_v, cmp_w2_v, w_ret_o, w_nsa_o, w_out, norm_ffn, w_up, conv_w, conv_b, w_down,
           norm_final):
    B, S, D = x.shape
    assert D == D_MODEL and NSA_GROUPS == 2 and norm_mix.shape[0] == 1
    assert S % 512 == 0 and S >= 1024 and S // CMP_STRIDE <= LANES and S // SEL_BLOCK <= SEL_LANES
    M = B * S
    x2 = x.reshape(M, D)

    w_bf = lax.optimization_barrier(w_in.astype(BF16))[0]
    w_gates = w_bf[:, W_GA:]

    def cmp_weights(pos, w1, b1, w2):
        return (jnp.tile(pos, (1, NSA_GROUPS)),
                _block_diag_groups(w1.reshape(CMP_BLOCK, NSA_DK, CMP_HIDDEN)).astype(BF16),
                jnp.tile(b1, NSA_GROUPS)[None, :],
                _block_diag_groups(w2).astype(BF16))

    posk, w1k, b1k, w2k = cmp_weights(cmp_pos_k[0], cmp_w1_k[0], cmp_b1_k[0], cmp_w2_k[0])
    posv, w1v, b1v, w2v = cmp_weights(cmp_pos_v[0], cmp_w1_v[0], cmp_b1_v[0], cmp_w2_v[0])

    pb, pf, ya = _inproj_ret(x2, norm_mix, w_bf, w_gates, ret_gn_g, S, tm=512, C=256)
    pb3 = pb.reshape(B, S, PB_W)
    pf3 = pf.reshape(B, S, PF_W)
    ya = ya.reshape(B, S, -1)

    kcmp, vcmp = _compress(pf3, posk, posv, w1k, w1v, b1k, b1v, w2k, w2v)
    ocmp, sel = _cmpattn(pb3, kcmp, vcmp, pf3, tq=1024)
    yb = _selwin(pb3, sel, pf3, ocmp, tq=128, KC=512)

    return _mix_ffn(x, ya, yb, pb3, w_ret_o[0].astype(BF16), w_nsa_o[0].astype(BF16),
                    w_out[0].astype(BF16), norm_ffn, w_up[0].astype(BF16), conv_w[0], conv_b,
                    w_down[0].astype(BF16), norm_final[None, :], tm=512, tf=256)
```

```python
import functools
import math

import numpy as np
import jax
import jax.numpy as jnp
from jax import lax
from jax.experimental import pallas as pl
from jax.experimental.pallas import tpu as pltpu

F32 = jnp.float32
BF16 = jnp.bfloat16

D_MODEL = 1024
RET_HEADS = 4
RET_DK = 128
RET_DV = 256
NSA_HEADS = 8
NSA_GROUPS = 2
NSA_R = NSA_HEADS // NSA_GROUPS
NSA_DK = 64
CMP_BLOCK = 32
CMP_STRIDE = 16
CMP_HIDDEN = 256
SEL_BLOCK = 64
SEL_TOPK = 8
WINDOW = 512
D_FF = 2816
CONV_W = 3
EPS = 1e-6
NEG = -1e30
FORCE = 1e9
LOWEST = -3e38

IN_SIZES = (RET_HEADS * RET_DK, RET_HEADS * RET_DK, RET_HEADS * RET_DV, RET_HEADS * RET_DV,
            NSA_HEADS * NSA_DK,
            NSA_GROUPS * NSA_DK, NSA_GROUPS * NSA_DK, NSA_GROUPS * NSA_DK,
            NSA_GROUPS * NSA_DK, NSA_GROUPS * NSA_DK, NSA_GROUPS * NSA_DK,
            3 * NSA_HEADS, D_MODEL, D_MODEL)

LANES = 128
VMEM_LIMIT = 56 * 1024 * 1024

_W_OFF = np.concatenate([[0], np.cumsum(IN_SIZES)]).tolist()
W_NQ, W_KCR, W_KS, W_NG, W_GA = _W_OFF[4], _W_OFF[5], _W_OFF[7], _W_OFF[11], _W_OFF[12]

RT_RQ, RT_RK, RT_RV, RT_RG = 0, 512, 1024, 2048
RT_W = 3072
PB_NQ, PB_KS, PB_VS, PB_KW, PB_VW, PB_GA, PB_GB = 0, 512, 640, 768, 896, 1024, 2048
PB_W = 3072
PF_KCR, PF_VCR, PF_NG = 0, 128, 256
PF_W = 384

NT_DIMS = (((1,), (1,)), ((), ()))
TN_DIMS = (((0,), (0,)), ((), ()))


def _params(*sem):
    return pltpu.CompilerParams(dimension_semantics=sem, vmem_limit_bytes=VMEM_LIMIT)


def _rms(x, g):
    return x * lax.rsqrt(jnp.mean(x * x, axis=-1, keepdims=True) + EPS) * g


def _split3(x):
    hi = x.astype(BF16)
    rest = x - hi.astype(F32)
    mid = rest.astype(BF16)
    return hi, mid, (rest - mid.astype(F32)).astype(BF16)


def _inproj_ret_kernel(x_ref, g_ref, w_ref, wg_ref, gn_ref, pb_ref, pf_ref, ya_ref,
                       stash0, stash1, r_ref, dmask_ref, xi_ref, zeta_ref, *, tm, C, S):
    log_gamma = [float(np.log1p(-np.exp2(np.float32(-5.0 - h)))) for h in range(RET_HEADS)]
    scale = RET_DK ** -0.5
    i = pl.program_id(0)

    @pl.when(i == 0)
    def _():
        diff = (lax.broadcasted_iota(jnp.int32, (C, C), 0)
                - lax.broadcasted_iota(jnp.int32, (C, C), 1)).astype(F32)
        pos_v = lax.broadcasted_iota(jnp.int32, (C, RET_DV), 0).astype(F32)
        pos_k = lax.broadcasted_iota(jnp.int32, (C, RET_DK), 0).astype(F32)
        for h in range(RET_HEADS):
            lg = log_gamma[h]
            dmask_ref[h] = jnp.where(diff >= 0, jnp.exp(lg * jnp.maximum(diff, 0.0)), 0.0) * scale
            xi_ref[h] = jnp.exp(lg * (pos_v + 1.0))
            zeta_ref[h] = jnp.exp(lg * (C - 1.0 - pos_k)) * scale
        stash1[...] = jnp.zeros_like(stash1)
        r_ref[...] = jnp.zeros_like(r_ref)

    seq_start = lax.rem(i + (S // tm) - 1, S // tm) == 0

    def step(new_ref, old_ref):
        hb = _rms(x_ref[...], g_ref[...]).astype(BF16)

        def project(w, c0, c1):
            return jnp.dot(hb, w[:, c0:c1], preferred_element_type=F32)

        def stash_cols(c0):
            new_ref[:, c0:c0 + 512] = project(w_ref, c0, c0 + 512).astype(BF16)

        def query_cols():
            pb_ref[:, PB_NQ:PB_KS] = (project(w_ref, W_NQ, W_KCR) * (NSA_DK ** -0.5)).astype(BF16)

        def kv_cols():
            pb_ref[:, PB_KS:PB_GA] = project(w_ref, W_KS, W_NG).astype(BF16)

        def gate_cols(c0):
            pb_ref[:, PB_GA + c0:PB_GA + c0 + 512] = project(wg_ref, c0, c0 + 512).astype(BF16)

        def compress_cols():
            pf_ref[:, PF_KCR:PF_NG] = project(w_ref, W_KCR, W_KS)

        def branch_gate_cols():
            pf_ref[:, PF_NG:PF_W] = project(w_ref, W_NG, W_NG + PF_W - PF_NG)

        def retention(cc, h):
            rows = slice(cc * C, (cc + 1) * C)
            lg = log_gamma[h]
            q = old_ref[rows, RT_RQ + h * RET_DK:RT_RQ + (h + 1) * RET_DK]
            k = old_ref[rows, RT_RK + h * RET_DK:RT_RK + (h + 1) * RET_DK]
            v = old_ref[rows, RT_RV + h * RET_DV:RT_RV + (h + 1) * RET_DV]
            g = old_ref[rows, RT_RG + h * RET_DV:RT_RG + (h + 1) * RET_DV].astype(F32)
            s = lax.dot_general(q, k, NT_DIMS, preferred_element_type=F32)
            o = jnp.dot((s * dmask_ref[h]).astype(BF16), v, preferred_element_type=F32)
            state = r_ref[h]
            if cc == 0:
                state = jnp.where(seq_start, 0.0, state)
            o = o + xi_ref[h] * jnp.dot(q, state.astype(BF16), preferred_element_type=F32)
            kz = (k.astype(F32) * zeta_ref[h]).astype(BF16)
            kv = lax.dot_general(kz, v, TN_DIMS, preferred_element_type=F32)
            r_ref[h] = math.exp(lg * C) * state + kv
            mu = jnp.mean(o, axis=-1, keepdims=True)
            d = o - mu
            var = jnp.mean(d * d, axis=-1, keepdims=True)
            on = d * lax.rsqrt(var + EPS) * gn_ref[:, h * RET_DV:(h + 1) * RET_DV]
            ya_ref[rows, h * RET_DV:(h + 1) * RET_DV] = (g * jax.nn.sigmoid(g) * on).astype(BF16)

        proj = ([functools.partial(stash_cols, c0) for c0 in range(0, RT_W, 512)]
                + [query_cols, kv_cols] + [functools.partial(gate_cols, c0) for c0 in range(0, PB_W - PB_GA, 512)]
                + [compress_cols, branch_gate_cols])
        ret = [functools.partial(retention, cc, h) for cc in range(tm // C) for h in range(RET_HEADS)]
        per = -(-len(proj) // len(ret))
        for n, job in enumerate(ret):
            job()
            for p in proj[n * per:(n + 1) * per]:
                p()

    pl.when(lax.rem(i, 2) == 0)(functools.partial(step, stash0, stash1))
    pl.when(lax.rem(i, 2) == 1)(functools.partial(step, stash1, stash0))


def _inproj_ret(x2, g, w, w_gates, gn_g, S, tm, C):
    M = x2.shape[0]
    nt = M // tm
    HV = RET_HEADS * RET_DV
    cur = lambda i: (jnp.minimum(i, nt - 1), 0)
    prev = lambda i: (jnp.maximum(i - 1, 0), 0)
    const = lambda i: (0, 0)
    return pl.pallas_call(
        functools.partial(_inproj_ret_kernel, tm=tm, C=C, S=S),
        grid=(nt + 1,),
        in_specs=[pl.BlockSpec((tm, D_MODEL), cur),
                  pl.BlockSpec((1, D_MODEL), const),
                  pl.BlockSpec(w.shape, const, pipeline_mode=pl.Buffered(1)),
                  pl.BlockSpec(w_gates.shape, const, pipeline_mode=pl.Buffered(1)),
                  pl.BlockSpec((1, HV), const)],
        out_specs=[pl.BlockSpec((tm, PB_W), cur), pl.BlockSpec((tm, PF_W), cur), pl.BlockSpec((tm, HV), prev)],
        out_shape=[jax.ShapeDtypeStruct((M, PB_W), BF16), jax.ShapeDtypeStruct((M, PF_W), F32),
                   jax.ShapeDtypeStruct((M, HV), BF16)],
        scratch_shapes=[pltpu.VMEM((tm, RT_W), BF16), pltpu.VMEM((tm, RT_W), BF16),
                        pltpu.VMEM((RET_HEADS, RET_DK, RET_DV), F32), pltpu.VMEM((RET_HEADS, C, C), F32),
                        pltpu.VMEM((RET_HEADS, C, RET_DV), F32), pltpu.VMEM((RET_HEADS, C, RET_DK), F32)],
        compiler_params=_params("arbitrary"),
        name="inproj_ret",
    )(x2, g, w, w_gates, gn_g)


def _compress_kernel(kin_ref, vin_ref, posk_ref, posv_ref, w1k_ref, w1v_ref, b1k_ref, b1v_ref,
                     w2k_ref, w2v_ref, ko_ref, vo_ref, *, NP):
    GD = NSA_GROUPS * NSA_DK
    npc = CMP_BLOCK // CMP_STRIDE
    row = lax.broadcasted_iota(jnp.int32, (NP, GD), 0)
    for in_ref, pos_ref, w1_ref, b1_ref, w2_ref, out_ref in (
            (kin_ref, posk_ref, w1k_ref, b1k_ref, w2k_ref, ko_ref),
            (vin_ref, posv_ref, w1v_ref, b1v_ref, w2v_ref, vo_ref)):
        parts = []
        for p in range(npc):
            acc = jnp.zeros((NP, NSA_GROUPS * CMP_HIDDEN), F32)
            for i in range(0, CMP_STRIDE, 2):
                j = p * CMP_STRIDE + i
                toks = [(in_ref[0, pl.ds(i + d, NP, stride=CMP_STRIDE), :] + pos_ref[j + d:j + d + 1, :])
                        .astype(BF16) for d in range(2)]
                acc = acc + jnp.dot(jnp.concatenate(toks, axis=1), w1_ref[j:j + 2].reshape(2 * GD, -1),
                                    preferred_element_type=F32)
            parts.append(acc)
        hidden = parts[0]
        for p in range(1, npc):
            hidden = hidden + pltpu.roll(parts[p], NP - p, axis=0)
        act = jax.nn.gelu(hidden + b1_ref[...]).astype(BF16)
        res = jnp.dot(act, w2_ref[...], preferred_element_type=F32)
        res = jnp.where(row < NP - (npc - 1), res, 0.0)
        if out_ref is vo_ref:
            res = res.T
            for g in range(NSA_GROUPS):
                out_ref[0, g] = res[g * NSA_DK:(g + 1) * NSA_DK, :]
        else:
            for g in range(NSA_GROUPS):
                out_ref[0, g] = res[:, g * NSA_DK:(g + 1) * NSA_DK]


def _compress(pf3, posk, posv, w1k, w1v, b1k, b1v, w2k, w2v):
    B, S, _ = pf3.shape
    NP = S // CMP_STRIDE
    GD = NSA_GROUPS * NSA_DK
    GH = NSA_GROUPS * CMP_HIDDEN
    const2 = lambda b: (0, 0)
    const3 = lambda b: (0, 0, 0)
    k_shape = (NSA_GROUPS, NP, NSA_DK)
    v_shape = (NSA_GROUPS, NSA_DK, NP)
    return pl.pallas_call(
        functools.partial(_compress_kernel, NP=NP),
        grid=(B,),
        in_specs=[pl.BlockSpec((1, S, GD), lambda b: (b, 0, PF_KCR // GD)),
                  pl.BlockSpec((1, S, GD), lambda b: (b, 0, PF_VCR // GD)),
                  pl.BlockSpec((CMP_BLOCK, GD), const2), pl.BlockSpec((CMP_BLOCK, GD), const2),
                  pl.BlockSpec((CMP_BLOCK, GD, GH), const3), pl.BlockSpec((CMP_BLOCK, GD, GH), const3),
                  pl.BlockSpec((1, GH), const2), pl.BlockSpec((1, GH), const2),
                  pl.BlockSpec((GH, GD), const2), pl.BlockSpec((GH, GD), const2)],
        out_specs=[pl.BlockSpec((1,) + k_shape, lambda b: (b, 0, 0, 0)),
                   pl.BlockSpec((1,) + v_shape, lambda b: (b, 0, 0, 0))],
        out_shape=[jax.ShapeDtypeStruct((B,) + k_shape, F32), jax.ShapeDtypeStruct((B,) + v_shape, F32)],
        compiler_params=_params("arbitrary"),
        name="compress",
    )(pf3, pf3, posk, posv, w1k, w1v, b1k, b1v, w2k, w2v)


SEL_LANE0 = 64
SEL_LANES = 32
POS_HI_LANE, POS_LO_LANE, PAD_LANE = 96, 97, 98
ONES_LANE = 64
MASK_BIG = 1e30


def _cmpattn_kernel(q_ref, kc_ref, vct_ref, gate_ref, ocmp_ref, sel_ref, *, tq, NP, NS):
    NC = NP - (CMP_BLOCK // CMP_STRIDE - 1)
    tile = pl.program_id(1)

    def run(rows):
        _cmpattn_tile(q_ref, kc_ref, vct_ref, gate_ref, ocmp_ref, sel_ref, tile * tq,
                      tq=tq, NP=rows, NC=NC, NS=NS)

    for v in range(pl.cdiv(NP * CMP_STRIDE, tq)):
        pl.when(tile == v)(functools.partial(run, min(NP, (v + 1) * tq // CMP_STRIDE)))


def _cmpattn_tile(q_ref, kc_ref, vct_ref, gate_ref, ocmp_ref, sel_ref, t0, *, tq, NP, NC, NS):
    tcol = t0 + lax.broadcasted_iota(jnp.int32, (NP, tq), 1)
    cidx = lax.broadcasted_iota(jnp.int32, (NP, tq), 0)
    visible = (tcol >= cidx * CMP_STRIDE + (CMP_BLOCK - 1)) & (cidx < NC)
    mask_add = jnp.where(visible, 0.0, NEG)
    block_end = (lax.broadcasted_iota(jnp.int32, (NP, 1), 0) * CMP_STRIDE + (CMP_BLOCK - 1)).astype(F32)
    sees_any = ((t0 + lax.broadcasted_iota(jnp.int32, (1, tq), 1)) >= CMP_BLOCK - 1).astype(F32)
    jj = lax.broadcasted_iota(jnp.int32, (SEL_LANES, NP), 0)
    cc = lax.broadcasted_iota(jnp.int32, (SEL_LANES, NP), 1)
    overlap_t = ((cc * CMP_STRIDE < jj * SEL_BLOCK + SEL_BLOCK)
                 & (cc * CMP_STRIDE + CMP_BLOCK > jj * SEL_BLOCK) & (jj < NS)).astype(BF16)
    blk = lax.broadcasted_iota(jnp.int32, (SEL_LANES, tq), 0)
    blk_f = blk.astype(F32)
    is_blk = blk < NS
    cur = jnp.right_shift(t0 + lax.broadcasted_iota(jnp.int32, (SEL_LANES, tq), 1),
                          SEL_BLOCK.bit_length() - 1)
    o_t = []
    for g in range(NSA_GROUPS):
        kc_parts = _split3(kc_ref[0, g, :NP, :])
        vct = vct_ref[0, g, :, :NP].astype(BF16)
        psum = jnp.zeros((NP, tq), F32)
        for r in range(NSA_R):
            hh = g * NSA_R + r
            q = q_ref[0, :, hh * NSA_DK:(hh + 1) * NSA_DK]
            s = sum(lax.dot_general(part, q, NT_DIMS, preferred_element_type=F32) for part in kc_parts)
            s = s + (2.0 ** -(hh + 1)) * block_end + mask_add
            e = jnp.exp(s - jnp.max(s, axis=0, keepdims=True))
            p = e * (sees_any / jnp.sum(e, axis=0, keepdims=True))
            o_t.append(jnp.dot(vct, p.astype(BF16), preferred_element_type=F32))
            psum = psum + p
        imp = sum(jnp.dot(overlap_t, part, preferred_element_type=F32) for part in _split3(psum))
        imp = jnp.where((blk == 0) | (blk == cur) | (blk == cur - 1), FORCE, imp)
        imp = jnp.where(blk > cur, -FORCE, imp)
        work = jnp.where(is_blk, imp, LOWEST)
        sel = jnp.zeros((SEL_LANES, tq), F32)
        for _ in range(min(SEL_TOPK, NS)):
            top = jnp.max(work, axis=0, keepdims=True)
            first = jnp.min(jnp.where(work == top, blk_f, float(SEL_LANES)), axis=0, keepdims=True)
            pick = blk_f == first
            sel = jnp.where(pick, 1.0, sel)
            work = jnp.where(pick, LOWEST, work)
        sel = jnp.where(is_blk & (blk <= cur), sel - 1.0, jnp.where(is_blk, -1.0, 0.0))
        placed = jnp.concatenate([jnp.zeros((SEL_LANE0, tq), F32), sel,
                                  jnp.zeros((LANES - SEL_LANE0 - SEL_LANES, tq), F32)], axis=0)
        sel_ref[0, g] = placed.T.astype(BF16)
    o = jnp.concatenate(o_t, axis=0).T
    gate = jax.nn.sigmoid(gate_ref[0])
    for hh in range(NSA_HEADS):
        cols = slice(hh * NSA_DK, (hh + 1) * NSA_DK)
        ocmp_ref[0, :, cols] = gate[:, hh:hh + 1] * o[:, cols]


def _cmpattn(pb3, kcmp, vcmp, pf3, tq):
    B, S, _ = pb3.shape
    NP = S // CMP_STRIDE
    NS = S // SEL_BLOCK
    HD = NSA_HEADS * NSA_DK
    return pl.pallas_call(
        functools.partial(_cmpattn_kernel, tq=tq, NP=NP, NS=NS),
        grid=(B, S // tq),
        in_specs=[pl.BlockSpec((1, tq, HD), lambda b, i: (b, i, PB_NQ // HD)),
                  pl.BlockSpec((1, NSA_GROUPS, NP, NSA_DK), lambda b, i: (b, 0, 0, 0)),
                  pl.BlockSpec((1, NSA_GROUPS, NSA_DK, NP), lambda b, i: (b, 0, 0, 0)),
                  pl.BlockSpec((1, tq, LANES), lambda b, i: (b, i, PF_NG // LANES))],
        out_specs=[pl.BlockSpec((1, tq, HD), lambda b, i: (b, i, 0)),
                   pl.BlockSpec((1, NSA_GROUPS, tq, LANES), lambda b, i: (b, 0, i, 0))],
        out_shape=[jax.ShapeDtypeStruct((B, S, HD), F32),
                   jax.ShapeDtypeStruct((B, NSA_GROUPS, S, LANES), BF16)],
        compiler_params=_params("arbitrary", "arbitrary"),
        name="cmpattn",
    )(pb3, kcmp, vcmp, pf3)


def _selwin_kernel(q_ref, ks_ref, vs_ref, kw_ref, vw_ref, sel_ref, wmask_ref, dmask_ref, gate_ref, ocmp_ref,
                   o_ref, ksx, vsx, kwx, vwx, *, tq, KC, S):
    WL = WINDOW + tq
    RB = 512
    i = pl.program_id(1)
    t0 = pl.multiple_of(i * tq, tq)
    sel_shift = SEL_BLOCK.bit_length() - 1

    @pl.when(i == 0)
    def _build():
        lane = lax.broadcasted_iota(jnp.int32, (RB, LANES), 1)
        is_head = lane < NSA_DK
        pad_k = jnp.where(lane == PAD_LANE, MASK_BIG, 0.0).astype(BF16)
        for g in range(NSA_GROUPS):
            kwx[g, 0:WINDOW, :] = pad_k[:WINDOW]
            vwx[g, 0:WINDOW, :] = jnp.zeros((WINDOW, LANES), BF16)
        for piece in range(S // RB):
            r0 = piece * RB
            pos = r0 + lax.broadcasted_iota(jnp.int32, (RB, LANES), 0)
            alibi = jnp.where(lane == POS_HI_LANE, jnp.right_shift(pos, sel_shift),
                              jnp.where(lane == POS_LO_LANE, pos & (SEL_BLOCK - 1), 0)).astype(F32)
            onehot = jnp.where(lane - SEL_LANE0 == jnp.right_shift(pos, sel_shift), MASK_BIG, 0.0)
            k_extra_win = alibi.astype(BF16)
            k_extra_sel = (alibi + onehot).astype(BF16)
            v_extra = jnp.where(lane == ONES_LANE, 1.0, 0.0).astype(BF16)
            for g in range(NSA_GROUPS):
                def head_lanes(ref):
                    t = ref[0, r0:r0 + RB, :]
                    return t if g == 0 else jnp.concatenate([t[:, NSA_DK:], t[:, :NSA_DK]], axis=1)
                ksx[g, r0:r0 + RB, :] = jnp.where(is_head, head_lanes(ks_ref), k_extra_sel)
                vsx[g, r0:r0 + RB, :] = jnp.where(is_head, head_lanes(vs_ref), v_extra)
                kwx[g, WINDOW + r0:WINDOW + r0 + RB, :] = jnp.where(is_head, head_lanes(kw_ref), k_extra_win)
                vwx[g, WINDOW + r0:WINDOW + r0 + RB, :] = jnp.where(is_head, head_lanes(vw_ref), v_extra)

    gate = jax.nn.sigmoid(gate_ref[0])
    lane = lax.broadcasted_iota(jnp.int32, (tq, LANES), 1)
    is_head = lane < NSA_DK
    diag_chunk = t0 // KC
    wmask = wmask_ref[...][None]
    dmask = dmask_ref[i % (KC // tq)][None]

    def masked_rows(s, mask):
        n = s.shape[-1]
        return (s.reshape(NSA_R, tq, n) + mask).reshape(NSA_R * tq, n)

    def normalise(acc):
        return acc[:, :NSA_DK] / acc[:, ONES_LANE:ONES_LANE + 1]

    def scores(q, k):
        half = q.shape[0] // 2
        return jnp.concatenate([lax.dot_general(q[:half], k, NT_DIMS, preferred_element_type=F32),
                                lax.dot_general(q[half:], k, NT_DIMS, preferred_element_type=F32)], axis=0)

    def weighted_values(p, v):
        half = p.shape[0] // 2
        p = p.astype(BF16)
        return jnp.concatenate([jnp.dot(p[:half], v, preferred_element_type=F32),
                                jnp.dot(p[half:], v, preferred_element_type=F32)], axis=0)

    def query_rows(g):
        sel = sel_ref[0, g].astype(F32)
        rows = []
        for r in range(NSA_R):
            hh = g * NSA_R + r
            qh = q_ref[0, :, (hh // 2) * LANES:(hh // 2 + 1) * LANES].astype(F32)
            if hh % 2:
                qh = jnp.concatenate([qh[:, NSA_DK:], qh[:, :NSA_DK]], axis=1)
            slope = 2.0 ** -(hh + 1)
            extra = jnp.where(lane == POS_HI_LANE, SEL_BLOCK * slope,
                              jnp.where(lane == POS_LO_LANE, slope, jnp.where(lane == PAD_LANE, -1.0, sel)))
            rows.append(jnp.where(is_head, qh, extra).astype(BF16))
        return jnp.concatenate(rows, axis=0)

    def tile(n_past):
        past = n_past * KC
        for g in range(NSA_GROUPS):
            q = query_rows(g)
            s = masked_rows(scores(q, kwx[g, pl.ds(t0, WL), :]), wmask)
            p = jnp.exp(s - jnp.max(s, axis=-1, keepdims=True))
            o_win = normalise(weighted_values(p, vwx[g, pl.ds(t0, WL), :]))

            s_own = masked_rows(scores(q, ksx[g, past:past + KC, :]), dmask)
            m = jnp.max(s_own, axis=-1, keepdims=True)
            if n_past:
                s_past = scores(q, ksx[g, 0:past, :])
                m = jnp.maximum(m, jnp.max(s_past, axis=-1, keepdims=True))
            acc = weighted_values(jnp.exp(s_own - m), vsx[g, past:past + KC, :])
            if n_past:
                acc = acc + weighted_values(jnp.exp(s_past - m), vsx[g, 0:past, :])
            o_sel = normalise(acc)

            for r in range(NSA_R):
                hh = g * NSA_R + r
                g_sel = gate[:, NSA_HEADS + hh:NSA_HEADS + hh + 1]
                g_win = gate[:, 2 * NSA_HEADS + hh:2 * NSA_HEADS + hh + 1]
                piece = g_sel * o_sel[r * tq:(r + 1) * tq] + g_win * o_win[r * tq:(r + 1) * tq]
                o_ref[0, :, hh * NSA_DK:(hh + 1) * NSA_DK] = (
                    ocmp_ref[0, :, hh * NSA_DK:(hh + 1) * NSA_DK] + piece).astype(BF16)

    for n_past in range(S // KC):
        pl.when(diag_chunk == n_past)(functools.partial(tile, n_past))


def _selwin(pb3, sel, pf3, ocmp, tq, KC):
    B, S, _ = pb3.shape
    HD = NSA_HEADS * NSA_DK
    WL = WINDOW + tq
    tr = jnp.arange(tq)[:, None]
    kc = jnp.arange(WL)[None, :]
    wmask = jnp.where((kc > tr) & (kc <= tr + WINDOW), 0.0, NEG).astype(F32)
    off = (jnp.arange(KC // tq) * tq)[:, None, None]
    dmask = jnp.where(jnp.arange(KC)[None, None, :] <= off + tr[None], 0.0, NEG).astype(F32)
    kv_spec = lambda off: pl.BlockSpec((1, S, LANES), lambda b, i: (b, 0, off // LANES))
    return pl.pallas_call(
        functools.partial(_selwin_kernel, tq=tq, KC=KC, S=S),
        grid=(B, S // tq),
        in_specs=[pl.BlockSpec((1, tq, HD), lambda b, i: (b, i, PB_NQ // HD)),
                  kv_spec(PB_KS), kv_spec(PB_VS), kv_spec(PB_KW), kv_spec(PB_VW),
                  pl.BlockSpec((1, NSA_GROUPS, tq, LANES), lambda b, i: (b, 0, i, 0)),
                  pl.BlockSpec((tq, WL), lambda b, i: (0, 0)),
                  pl.BlockSpec((KC // tq, tq, KC), lambda b, i: (0, 0, 0)),
                  pl.BlockSpec((1, tq, LANES), lambda b, i: (b, i, PF_NG // LANES)),
                  pl.BlockSpec((1, tq, HD), lambda b, i: (b, i, 0))],
        out_specs=pl.BlockSpec((1, tq, HD), lambda b, i: (b, i, 0)),
        out_shape=jax.ShapeDtypeStruct((B, S, HD), BF16),
        scratch_shapes=[pltpu.VMEM((NSA_GROUPS, S, LANES), BF16), pltpu.VMEM((NSA_GROUPS, S, LANES), BF16),
                        pltpu.VMEM((NSA_GROUPS, WINDOW + S, LANES), BF16),
                        pltpu.VMEM((NSA_GROUPS, WINDOW + S, LANES), BF16)],
        compiler_params=_params("arbitrary", "arbitrary"),
        name="selwin",
    )(pb3, pb3, pb3, pb3, pb3, sel, wmask, dmask, pf3, ocmp)


SUBLANES = 8


def _mix_ffn_kernel(x_ref, ya_ref, yb_ref, ga_ref, gb_ref, wa_ref, wb_ref, wo_ref, gn_ref,
                    wup_ref, cw_ref, cb_ref, wd_ref, gf_ref, o_ref, tail_ref, u_ref, *, tm, tf):
    @pl.when(pl.program_id(1) == 0)
    def _():
        tail_ref[...] = jnp.zeros_like(tail_ref)

    y_a = jnp.dot(ya_ref[0], wa_ref[...], preferred_element_type=F32)
    y_b = jnp.dot(yb_ref[0], wb_ref[...], preferred_element_type=F32)
    merged = (jax.nn.sigmoid(ga_ref[0].astype(F32)) * y_a + jax.nn.sigmoid(gb_ref[0].astype(F32)) * y_b)
    x1 = x_ref[0] + jnp.dot(merged.astype(BF16), wo_ref[...], preferred_element_type=F32)
    o_ref[0] = x1
    h = _rms(x1, gn_ref[...]).astype(BF16)

    top = lax.broadcasted_iota(jnp.int32, (SUBLANES, tf), 0)
    for c in range(D_FF // tf):
        cols = slice(c * tf, (c + 1) * tf)
        a = jnp.dot(h, wup_ref[:, cols], preferred_element_type=F32)
        b = jnp.dot(h, wup_ref[:, D_FF + c * tf:D_FF + (c + 1) * tf], preferred_element_type=F32)
        tail = tail_ref[c]
        tail_ref[c] = a[tm - SUBLANES:]
        ac = cb_ref[:, cols] + cw_ref[CONV_W - 1:CONV_W, cols] * a
        for d in range(1, CONV_W):
            sh = pltpu.roll(a, d, axis=0)
            head = jnp.where(top < d, pltpu.roll(tail, d, axis=0), sh[:SUBLANES])
            sh = jnp.concatenate([head, sh[SUBLANES:]], axis=0)
            ac = ac + cw_ref[CONV_W - 1 - d:CONV_W - d, cols] * sh
        u_ref[:, cols] = (jax.nn.gelu(ac) * b).astype(BF16)
    y = jnp.dot(u_ref[...], wd_ref[...], preferred_element_type=F32)
    o_ref[0] = _rms(o_ref[0] + y, gf_ref[...])


def _mix_ffn(x, ya, yb, pb3, wa, wb, wo, gn, w_up, conv_w, conv_b, w_down, gf, tm, tf):
    B, S, _ = x.shape
    HV = RET_HEADS * RET_DV
    HD = NSA_HEADS * NSA_DK
    rows = lambda width, col=0: pl.BlockSpec((1, tm, width), lambda b, i: (b, i, col))
    whole = lambda *shape: pl.BlockSpec(shape, lambda b, i: (0,) * len(shape), pipeline_mode=pl.Buffered(1))
    return pl.pallas_call(
        functools.partial(_mix_ffn_kernel, tm=tm, tf=tf),
        grid=(B, S // tm),
        in_specs=[rows(D_MODEL), rows(HV), rows(HD),
                  rows(D_MODEL, PB_GA // D_MODEL), rows(D_MODEL, PB_GB // D_MODEL),
                  whole(HV, D_MODEL), whole(HD, D_MODEL), whole(D_MODEL, D_MODEL), whole(1, D_MODEL),
                  whole(D_MODEL, 2 * D_FF), whole(CONV_W, D_FF), whole(1, D_FF), whole(D_FF, D_MODEL),
                  whole(1, D_MODEL)],
        out_specs=rows(D_MODEL),
        out_shape=jax.ShapeDtypeStruct((B, S, D_MODEL), F32),
        scratch_shapes=[pltpu.VMEM((D_FF // tf, SUBLANES, tf), F32), pltpu.VMEM((tm, D_FF), BF16)],
        compiler_params=_params("arbitrary", "arbitrary"),
        name="mix_ffn",
    )(x, ya, yb, pb3, pb3, wa, wb, wo, gn, w_up, conv_w, conv_b, w_down, gf)


def _block_diag_groups(w):
    z = jnp.zeros_like(w)
    return jnp.concatenate([jnp.concatenate([w, z], axis=-1), jnp.concatenate([z, w], axis=-1)], axis=-2)


def kernel(x, norm_mix, w_in, ret_gn_g, cmp_pos_k, cmp_w1_k, cmp_b1_k, cmp_w2_k, cmp_pos_v, cmp_w1_v,
           cmp_b1_v, cmp_w2_v, w_ret_o, w_nsa_o, w_out, norm_ffn, w_up, conv_w, conv_b, w_down,
           norm_final):
    B, S, D = x.shape
    assert D == D_MODEL and NSA_GROUPS == 2 and norm_mix.shape[0] == 1
    assert S % 512 == 0 and S >= 1024 and S // CMP_STRIDE <= LANES and S // SEL_BLOCK <= SEL_LANES
    M = B * S
    x2 = x.reshape(M, D)

    w_bf = lax.optimization_barrier(w_in.astype(BF16))[0]
    w_gates = w_bf[:, W_GA:]

    def cmp_weights(pos, w1, b1, w2):
        return (jnp.tile(pos, (1, NSA_GROUPS)),
                _block_diag_groups(w1.reshape(CMP_BLOCK, NSA_DK, CMP_HIDDEN)).astype(BF16),
                jnp.tile(b1, NSA_GROUPS)[None, :],
                _block_diag_groups(w2).astype(BF16))

    posk, w1k, b1k, w2k = cmp_weights(cmp_pos_k[0], cmp_w1_k[0], cmp_b1_k[0], cmp_w2_k[0])
    posv, w1v, b1v, w2v = cmp_weights(cmp_pos_v[0], cmp_w1_v[0], cmp_b1_v[0], cmp_w2_v[0])

    pb, pf, ya = _inproj_ret(x2, norm_mix, w_bf, w_gates, ret_gn_g, S, tm=512, C=256)
    pb3 = pb.reshape(B, S, PB_W)
    pf3 = pf.reshape(B, S, PF_W)
    ya = ya.reshape(B, S, -1)

    kcmp, vcmp = _compress(pf3, posk, posv, w1k, w1v, b1k, b1v, w2k, w2v)
    ocmp, sel = _cmpattn(pb3, kcmp, vcmp, pf3, tq=1024)
    yb = _selwin(pb3, sel, pf3, ocmp, tq=256, KC=512)

    return _mix_ffn(x, ya, yb, pb3, w_ret_o[0].astype(BF16), w_nsa_o[0].astype(BF16),
                    w_out[0].astype(BF16), norm_ffn, w_up[0].astype(BF16), conv_w[0], conv_b,
                    w_down[0].astype(BF16), norm_final[None, :], tm=512, tf=256)
```

```python
import functools
import math

import numpy as np
import jax
import jax.numpy as jnp
from jax import lax
from jax.experimental import pallas as pl
from jax.experimental.pallas import tpu as pltpu

F32 = jnp.float32
BF16 = jnp.bfloat16

D_MODEL = 1024
RET_HEADS = 4
RET_DK = 128
RET_DV = 256
NSA_HEADS = 8
NSA_GROUPS = 2
NSA_R = NSA_HEADS // NSA_GROUPS
NSA_DK = 64
CMP_BLOCK = 32
CMP_STRIDE = 16
CMP_HIDDEN = 256
SEL_BLOCK = 64
SEL_TOPK = 8
WINDOW = 512
D_FF = 2816
CONV_W = 3
EPS = 1e-6
NEG = -1e30
FORCE = 1e9
LOWEST = -3e38

IN_SIZES = (RET_HEADS * RET_DK, RET_HEADS * RET_DK, RET_HEADS * RET_DV, RET_HEADS * RET_DV,
            NSA_HEADS * NSA_DK,
            NSA_GROUPS * NSA_DK, NSA_GROUPS * NSA_DK, NSA_GROUPS * NSA_DK,
            NSA_GROUPS * NSA_DK, NSA_GROUPS * NSA_DK, NSA_GROUPS * NSA_DK,
            3 * NSA_HEADS, D_MODEL, D_MODEL)

LANES = 128
VMEM_LIMIT = 56 * 1024 * 1024

_W_OFF = np.concatenate([[0], np.cumsum(IN_SIZES)]).tolist()
W_NQ, W_KCR, W_KS, W_NG, W_GA = _W_OFF[4], _W_OFF[5], _W_OFF[7], _W_OFF[11], _W_OFF[12]

RT_RQ, RT_RK, RT_RV, RT_RG = 0, 512, 1024, 2048
RT_W = 3072
PB_NQ, PB_KS, PB_VS, PB_KW, PB_VW, PB_GA, PB_GB = 0, 512, 640, 768, 896, 1024, 2048
PB_W = 3072
PF_KCR, PF_VCR, PF_NG = 0, 128, 256
PF_W = 384

NT_DIMS = (((1,), (1,)), ((), ()))
TN_DIMS = (((0,), (0,)), ((), ()))


def _params(*sem):
    return pltpu.CompilerParams(dimension_semantics=sem, vmem_limit_bytes=VMEM_LIMIT)


def _rms(x, g):
    return x * lax.rsqrt(jnp.mean(x * x, axis=-1, keepdims=True) + EPS) * g


def _split3(x):
    hi = x.astype(BF16)
    rest = x - hi.astype(F32)
    mid = rest.astype(BF16)
    return hi, mid, (rest - mid.astype(F32)).astype(BF16)


def _inproj_ret_kernel(x_ref, g_ref, w_ref, wg_ref, gn_ref, pb_ref, pf_ref, ya_ref,
                       stash0, stash1, r_ref, dmask_ref, xi_ref, zeta_ref, *, tm, C, S):
    log_gamma = [float(np.log1p(-np.exp2(np.float32(-5.0 - h)))) for h in range(RET_HEADS)]
    scale = RET_DK ** -0.5
    i = pl.program_id(0)

    @pl.when(i == 0)
    def _():
        diff = (lax.broadcasted_iota(jnp.int32, (C, C), 0)
                - lax.broadcasted_iota(jnp.int32, (C, C), 1)).astype(F32)
        pos_v = lax.broadcasted_iota(jnp.int32, (C, RET_DV), 0).astype(F32)
        pos_k = lax.broadcasted_iota(jnp.int32, (C, RET_DK), 0).astype(F32)
        for h in range(RET_HEADS):
            lg = log_gamma[h]
            dmask_ref[h] = jnp.where(diff >= 0, jnp.exp(lg * jnp.maximum(diff, 0.0)), 0.0) * scale
            xi_ref[h] = jnp.exp(lg * (pos_v + 1.0))
            zeta_ref[h] = jnp.exp(lg * (C - 1.0 - pos_k)) * scale
        stash1[...] = jnp.zeros_like(stash1)
        r_ref[...] = jnp.zeros_like(r_ref)

    seq_start = lax.rem(i + (S // tm) - 1, S // tm) == 0

    def step(new_ref, old_ref):
        hb = _rms(x_ref[...], g_ref[...]).astype(BF16)

        def project(w, c0, c1):
            return jnp.dot(hb, w[:, c0:c1], preferred_element_type=F32)

        def stash_cols(c0):
            new_ref[:, c0:c0 + 512] = project(w_ref, c0, c0 + 512).astype(BF16)

        def query_cols():
            pb_ref[:, PB_NQ:PB_KS] = (project(w_ref, W_NQ, W_KCR) * (NSA_DK ** -0.5)).astype(BF16)

        def kv_cols():
            pb_ref[:, PB_KS:PB_GA] = project(w_ref, W_KS, W_NG).astype(BF16)

        def gate_cols(c0):
            pb_ref[:, PB_GA + c0:PB_GA + c0 + 512] = project(wg_ref, c0, c0 + 512).astype(BF16)

        def compress_cols():
            pf_ref[:, PF_KCR:PF_NG] = project(w_ref, W_KCR, W_KS)

        def branch_gate_cols():
            pf_ref[:, PF_NG:PF_W] = project(w_ref, W_NG, W_NG + PF_W - PF_NG)

        def retention(cc, h):
            rows = slice(cc * C, (cc + 1) * C)
            lg = log_gamma[h]
            q = old_ref[rows, RT_RQ + h * RET_DK:RT_RQ + (h + 1) * RET_DK]
            k = old_ref[rows, RT_RK + h * RET_DK:RT_RK + (h + 1) * RET_DK]
            v = old_ref[rows, RT_RV + h * RET_DV:RT_RV + (h + 1) * RET_DV]
            g = old_ref[rows, RT_RG + h * RET_DV:RT_RG + (h + 1) * RET_DV].astype(F32)
            s = lax.dot_general(q, k, NT_DIMS, preferred_element_type=F32)
            o = jnp.dot((s * dmask_ref[h]).astype(BF16), v, preferred_element_type=F32)
            state = r_ref[h]
            if cc == 0:
                state = jnp.where(seq_start, 0.0, state)
            o = o + xi_ref[h] * jnp.dot(q, state.astype(BF16), preferred_element_type=F32)
            kz = (k.astype(F32) * zeta_ref[h]).astype(BF16)
            kv = lax.dot_general(kz, v, TN_DIMS, preferred_element_type=F32)
            r_ref[h] = math.exp(lg * C) * state + kv
            mu = jnp.mean(o, axis=-1, keepdims=True)
            d = o - mu
            var = jnp.mean(d * d, axis=-1, keepdims=True)
            on = d * lax.rsqrt(var + EPS) * gn_ref[:, h * RET_DV:(h + 1) * RET_DV]
            ya_ref[rows, h * RET_DV:(h + 1) * RET_DV] = (g * jax.nn.sigmoid(g) * on).astype(BF16)

        proj = ([functools.partial(stash_cols, c0) for c0 in range(0, RT_W, 512)]
                + [query_cols, kv_cols] + [functools.partial(gate_cols, c0) for c0 in range(0, PB_W - PB_GA, 512)]
                + [compress_cols, branch_gate_cols])
        ret = [functools.partial(retention, cc, h) for cc in range(tm // C) for h in range(RET_HEADS)]
        per = -(-len(proj) // len(ret))
        for n, job in enumerate(ret):
            job()
            for p in proj[n * per:(n + 1) * per]:
                p()

    pl.when(lax.rem(i, 2) == 0)(functools.partial(step, stash0, stash1))
    pl.when(lax.rem(i, 2) == 1)(functools.partial(step, stash1, stash0))


def _inproj_ret(x2, g, w, w_gates, gn_g, S, tm, C):
    M = x2.shape[0]
    nt = M // tm
    HV = RET_HEADS * RET_DV
    cur = lambda i: (jnp.minimum(i, nt - 1), 0)
    prev = lambda i: (jnp.maximum(i - 1, 0), 0)
    const = lambda i: (0, 0)
    return pl.pallas_call(
        functools.partial(_inproj_ret_kernel, tm=tm, C=C, S=S),
        grid=(nt + 1,),
        in_specs=[pl.BlockSpec((tm, D_MODEL), cur),
                  pl.BlockSpec((1, D_MODEL), const),
                  pl.BlockSpec(w.shape, const, pipeline_mode=pl.Buffered(1)),
                  pl.BlockSpec(w_gates.shape, const, pipeline_mode=pl.Buffered(1)),
                  pl.BlockSpec((1, HV), const)],
        out_specs=[pl.BlockSpec((tm, PB_W), cur), pl.BlockSpec((tm, PF_W), cur), pl.BlockSpec((tm, HV), prev)],
        out_shape=[jax.ShapeDtypeStruct((M, PB_W), BF16), jax.ShapeDtypeStruct((M, PF_W), F32),
                   jax.ShapeDtypeStruct((M, HV), BF16)],
        scratch_shapes=[pltpu.VMEM((tm, RT_W), BF16), pltpu.VMEM((tm, RT_W), BF16),
                        pltpu.VMEM((RET_HEADS, RET_DK, RET_DV), F32), pltpu.VMEM((RET_HEADS, C, C), F32),
                        pltpu.VMEM((RET_HEADS, C, RET_DV), F32), pltpu.VMEM((RET_HEADS, C, RET_DK), F32)],
        compiler_params=_params("arbitrary"),
        name="inproj_ret",
    )(x2, g, w, w_gates, gn_g)


def _compress_kernel(kin_ref, vin_ref, posk_ref, posv_ref, w1k_ref, w1v_ref, b1k_ref, b1v_ref,
                     w2k_ref, w2v_ref, ko_ref, vo_ref, *, NP):
    GD = NSA_GROUPS * NSA_DK
    npc = CMP_BLOCK // CMP_STRIDE
    row = lax.broadcasted_iota(jnp.int32, (NP, GD), 0)
    for in_ref, pos_ref, w1_ref, b1_ref, w2_ref, out_ref in (
            (kin_ref, posk_ref, w1k_ref, b1k_ref, w2k_ref, ko_ref),
            (vin_ref, posv_ref, w1v_ref, b1v_ref, w2v_ref, vo_ref)):
        parts = []
        for p in range(npc):
            acc = jnp.zeros((NP, NSA_GROUPS * CMP_HIDDEN), F32)
            for i in range(0, CMP_STRIDE, 2):
                j = p * CMP_STRIDE + i
                toks = [(in_ref[0, pl.ds(i + d, NP, stride=CMP_STRIDE), :] + pos_ref[j + d:j + d + 1, :])
                        .astype(BF16) for d in range(2)]
                acc = acc + jnp.dot(jnp.concatenate(toks, axis=1), w1_ref[j:j + 2].reshape(2 * GD, -1),
                                    preferred_element_type=F32)
            parts.append(acc)
        hidden = parts[0]
        for p in range(1, npc):
            hidden = hidden + pltpu.roll(parts[p], NP - p, axis=0)
        act = jax.nn.gelu(hidden + b1_ref[...]).astype(BF16)
        res = jnp.dot(act, w2_ref[...], preferred_element_type=F32)
        res = jnp.where(row < NP - (npc - 1), res, 0.0)
        if out_ref is vo_ref:
            res = res.T
            for g in range(NSA_GROUPS):
                out_ref[0, g] = res[g * NSA_DK:(g + 1) * NSA_DK, :]
        else:
            for g in range(NSA_GROUPS):
                out_ref[0, g] = res[:, g * NSA_DK:(g + 1) * NSA_DK]


def _compress(pf3, posk, posv, w1k, w1v, b1k, b1v, w2k, w2v):
    B, S, _ = pf3.shape
    NP = S // CMP_STRIDE
    GD = NSA_GROUPS * NSA_DK
    GH = NSA_GROUPS * CMP_HIDDEN
    const2 = lambda b: (0, 0)
    const3 = lambda b: (0, 0, 0)
    k_shape = (NSA_GROUPS, NP, NSA_DK)
    v_shape = (NSA_GROUPS, NSA_DK, NP)
    return pl.pallas_call(
        functools.partial(_compress_kernel, NP=NP),
        grid=(B,),
        in_specs=[pl.BlockSpec((1, S, GD), lambda b: (b, 0, PF_KCR // GD)),
                  pl.BlockSpec((1, S, GD), lambda b: (b, 0, PF_VCR // GD)),
                  pl.BlockSpec((CMP_BLOCK, GD), const2), pl.BlockSpec((CMP_BLOCK, GD), const2),
                  pl.BlockSpec((CMP_BLOCK, GD, GH), const3), pl.BlockSpec((CMP_BLOCK, GD, GH), const3),
                  pl.BlockSpec((1, GH), const2), pl.BlockSpec((1, GH), const2),
                  pl.BlockSpec((GH, GD), const2), pl.BlockSpec((GH, GD), const2)],
        out_specs=[pl.BlockSpec((1,) + k_shape, lambda b: (b, 0, 0, 0)),
                   pl.BlockSpec((1,) + v_shape, lambda b: (b, 0, 0, 0))],
        out_shape=[jax.ShapeDtypeStruct((B,) + k_shape, F32), jax.ShapeDtypeStruct((B,) + v_shape, F32)],
        compiler_params=_params("arbitrary"),
        name="compress",
    )(pf3, pf3, posk, posv, w1k, w1v, b1k, b1v, w2k, w2v)


SEL_LANE0 = 64
SEL_LANES = 32
POS_HI_LANE, POS_LO_LANE, PAD_LANE = 96, 97, 98
ONES_LANE = 64
MASK_BIG = 1e30


def _cmpattn_kernel(q_ref, kc_ref, vct_ref, gate_ref, ocmp_ref, sel_ref, *, tq, NP, NS):
    NC = NP - (CMP_BLOCK // CMP_STRIDE - 1)
    tile = pl.program_id(1)

    def run(rows):
        _cmpattn_tile(q_ref, kc_ref, vct_ref, gate_ref, ocmp_ref, sel_ref, tile * tq,
                      tq=tq, NP=rows, NC=NC, NS=NS)

    for v in range(pl.cdiv(NP * CMP_STRIDE, tq)):
        pl.when(tile == v)(functools.partial(run, min(NP, (v + 1) * tq // CMP_STRIDE)))


def _cmpattn_tile(q_ref, kc_ref, vct_ref, gate_ref, ocmp_ref, sel_ref, t0, *, tq, NP, NC, NS):
    tcol = t0 + lax.broadcasted_iota(jnp.int32, (NP, tq), 1)
    cidx = lax.broadcasted_iota(jnp.int32, (NP, tq), 0)
    visible = (tcol >= cidx * CMP_STRIDE + (CMP_BLOCK - 1)) & (cidx < NC)
    mask_add = jnp.where(visible, 0.0, NEG)
    block_end = (lax.broadcasted_iota(jnp.int32, (NP, 1), 0) * CMP_STRIDE + (CMP_BLOCK - 1)).astype(F32)
    sees_any = ((t0 + lax.broadcasted_iota(jnp.int32, (1, tq), 1)) >= CMP_BLOCK - 1).astype(F32)
    jj = lax.broadcasted_iota(jnp.int32, (SEL_LANES, NP), 0)
    cc = lax.broadcasted_iota(jnp.int32, (SEL_LANES, NP), 1)
    overlap_t = ((cc * CMP_STRIDE < jj * SEL_BLOCK + SEL_BLOCK)
                 & (cc * CMP_STRIDE + CMP_BLOCK > jj * SEL_BLOCK) & (jj < NS)).astype(BF16)
    blk = lax.broadcasted_iota(jnp.int32, (SEL_LANES, tq), 0)
    blk_f = blk.astype(F32)
    is_blk = blk < NS
    cur = jnp.right_shift(t0 + lax.broadcasted_iota(jnp.int32, (SEL_LANES, tq), 1),
                          SEL_BLOCK.bit_length() - 1)
    o_t = []
    for g in range(NSA_GROUPS):
        kc_parts = _split3(kc_ref[0, g, :NP, :])
        vct = vct_ref[0, g, :, :NP].astype(BF16)
        psum = jnp.zeros((NP, tq), F32)
        for r in range(NSA_R):
            hh = g * NSA_R + r
            q = q_ref[0, :, hh * NSA_DK:(hh + 1) * NSA_DK]
            s = sum(lax.dot_general(part, q, NT_DIMS, preferred_element_type=F32) for part in kc_parts)
            s = s + (2.0 ** -(hh + 1)) * block_end + mask_add
            e = jnp.exp(s - jnp.max(s, axis=0, keepdims=True))
            p = e * (sees_any / jnp.sum(e, axis=0, keepdims=True))
            o_t.append(jnp.dot(vct, p.astype(BF16), preferred_element_type=F32))
            psum = psum + p
        imp = sum(jnp.dot(overlap_t, part, preferred_element_type=F32) for part in _split3(psum))
        imp = jnp.where((blk == 0) | (blk == cur) | (blk == cur - 1), FORCE, imp)
        imp = jnp.where(blk > cur, -FORCE, imp)
        work = jnp.where(is_blk, imp, LOWEST)
        sel = jnp.zeros((SEL_LANES, tq), F32)
        for _ in range(min(SEL_TOPK, NS)):
            top = jnp.max(work, axis=0, keepdims=True)
            first = jnp.min(jnp.where(work == top, blk_f, float(SEL_LANES)), axis=0, keepdims=True)
            pick = blk_f == first
            sel = jnp.where(pick, 1.0, sel)
            work = jnp.where(pick, LOWEST, work)
        sel = jnp.where(is_blk & (blk <= cur), sel - 1.0, jnp.where(is_blk, -1.0, 0.0))
        placed = jnp.concatenate([jnp.zeros((SEL_LANE0, tq), F32), sel,
                                  jnp.zeros((LANES - SEL_LANE0 - SEL_LANES, tq), F32)], axis=0)
        sel_ref[0, g] = placed.T.astype(BF16)
    o = jnp.concatenate(o_t, axis=0).T
    gate = jax.nn.sigmoid(gate_ref[0])
    for hh in range(NSA_HEADS):
        cols = slice(hh * NSA_DK, (hh + 1) * NSA_DK)
        ocmp_ref[0, :, cols] = gate[:, hh:hh + 1] * o[:, cols]


def _cmpattn(pb3, kcmp, vcmp, pf3, tq):
    B, S, _ = pb3.shape
    NP = S // CMP_STRIDE
    NS = S // SEL_BLOCK
    HD = NSA_HEADS * NSA_DK
    return pl.pallas_call(
        functools.partial(_cmpattn_kernel, tq=tq, NP=NP, NS=NS),
        grid=(B, S // tq),
        in_specs=[pl.BlockSpec((1, tq, HD), lambda b, i: (b, i, PB_NQ // HD)),
                  pl.BlockSpec((1, NSA_GROUPS, NP, NSA_DK), lambda b, i: (b, 0, 0, 0)),
                  pl.BlockSpec((1, NSA_GROUPS, NSA_DK, NP), lambda b, i: (b, 0, 0, 0)),
                  pl.BlockSpec((1, tq, LANES), lambda b, i: (b, i, PF_NG // LANES))],
        out_specs=[pl.BlockSpec((1, tq, HD), lambda b, i: (b, i, 0)),
                   pl.BlockSpec((1, NSA_GROUPS, tq, LANES), lambda b, i: (b, 0, i, 0))],
        out_shape=[jax.ShapeDtypeStruct((B, S, HD), F32),
                   jax.ShapeDtypeStruct((B, NSA_GROUPS, S, LANES), BF16)],
        compiler_params=_params("arbitrary", "arbitrary"),
        name="cmpattn",
    )(pb3, kcmp, vcmp, pf3)


def _selwin_kernel(q_ref, ks_ref, vs_ref, kw_ref, vw_ref, sel_ref, wmask_ref, dmask_ref, gate_ref, ocmp_ref,
                   o_ref, ksx, vsx, kwx, vwx, *, tq, KC, S):
    WL = WINDOW + tq
    RB = 512
    i = pl.program_id(1)
    t0 = pl.multiple_of(i * tq, tq)
    sel_shift = SEL_BLOCK.bit_length() - 1

    @pl.when(i == 0)
    def _build():
        lane = lax.broadcasted_iota(jnp.int32, (RB, LANES), 1)
        is_head = lane < NSA_DK
        pad_k = jnp.where(lane == PAD_LANE, MASK_BIG, 0.0).astype(BF16)
        for g in range(NSA_GROUPS):
            kwx[g, 0:WINDOW, :] = pad_k[:WINDOW]
            vwx[g, 0:WINDOW, :] = jnp.zeros((WINDOW, LANES), BF16)
        for piece in range(S // RB):
            r0 = piece * RB
            pos = r0 + lax.broadcasted_iota(jnp.int32, (RB, LANES), 0)
            alibi = jnp.where(lane == POS_HI_LANE, jnp.right_shift(pos, sel_shift),
                              jnp.where(lane == POS_LO_LANE, pos & (SEL_BLOCK - 1), 0)).astype(F32)
            onehot = jnp.where(lane - SEL_LANE0 == jnp.right_shift(pos, sel_shift), MASK_BIG, 0.0)
            k_extra_win = alibi.astype(BF16)
            k_extra_sel = (alibi + onehot).astype(BF16)
            v_extra = jnp.where(lane == ONES_LANE, 1.0, 0.0).astype(BF16)
            for g in range(NSA_GROUPS):
                def head_lanes(ref):
                    t = ref[0, r0:r0 + RB, :]
                    return t if g == 0 else jnp.concatenate([t[:, NSA_DK:], t[:, :NSA_DK]], axis=1)
                ksx[g, r0:r0 + RB, :] = jnp.where(is_head, head_lanes(ks_ref), k_extra_sel)
                vsx[g, r0:r0 + RB, :] = jnp.where(is_head, head_lanes(vs_ref), v_extra)
                kwx[g, WINDOW + r0:WINDOW + r0 + RB, :] = jnp.where(is_head, head_lanes(kw_ref), k_extra_win)
                vwx[g, WINDOW + r0:WINDOW + r0 + RB, :] = jnp.where(is_head, head_lanes(vw_ref), v_extra)

    gate = jax.nn.sigmoid(gate_ref[0])
    lane = lax.broadcasted_iota(jnp.int32, (tq, LANES), 1)
    is_head = lane < NSA_DK
    diag_chunk = t0 // KC
    wmask = wmask_ref[...][None]
    dmask = dmask_ref[i % (KC // tq)][None]

    def masked_rows(s, mask):
        n = s.shape[-1]
        return (s.reshape(NSA_R, tq, n) + mask).reshape(NSA_R * tq, n)

    def normalise(acc):
        return acc[:, :NSA_DK] / acc[:, ONES_LANE:ONES_LANE + 1]

    def scores(q, k):
        half = q.shape[0] // 2
        return jnp.concatenate([lax.dot_general(q[:half], k, NT_DIMS, preferred_element_type=F32),
                                lax.dot_general(q[half:], k, NT_DIMS, preferred_element_type=F32)], axis=0)

    def weighted_values(p, v):
        half = p.shape[0] // 2
        p = p.astype(BF16)
        return jnp.concatenate([jnp.dot(p[:half], v, preferred_element_type=F32),
                                jnp.dot(p[half:], v, preferred_element_type=F32)], axis=0)

    def query_rows(g):
        sel = sel_ref[0, g].astype(F32)
        rows = []
        for r in range(NSA_R):
            hh = g * NSA_R + r
            qh = q_ref[0, :, (hh // 2) * LANES:(hh // 2 + 1) * LANES].astype(F32)
            if hh % 2:
                qh = jnp.concatenate([qh[:, NSA_DK:], qh[:, :NSA_DK]], axis=1)
            slope = 2.0 ** -(hh + 1)
            extra = jnp.where(lane == POS_HI_LANE, SEL_BLOCK * slope,
                              jnp.where(lane == POS_LO_LANE, slope, jnp.where(lane == PAD_LANE, -1.0, sel)))
            rows.append(jnp.where(is_head, qh, extra).astype(BF16))
        return jnp.concatenate(rows, axis=0)

    def tile(n_past):
        past = n_past * KC
        for g in range(NSA_GROUPS):
            q = query_rows(g)
            s = masked_rows(scores(q, kwx[g, pl.ds(t0, WL), :]), wmask)
            p = jnp.exp(s - jnp.max(s, axis=-1, keepdims=True))
            o_win = normalise(weighted_values(p, vwx[g, pl.ds(t0, WL), :]))

            if n_past:
                s_past = scores(q, ksx[g, 0:past, :])
            s_own = masked_rows(scores(q, ksx[g, past:past + KC, :]), dmask)
            m = jnp.max(s_own, axis=-1, keepdims=True)
            if n_past:
                m = jnp.maximum(m, jnp.max(s_past, axis=-1, keepdims=True))
                acc = weighted_values(jnp.exp(s_past - m), vsx[g, 0:past, :])
                acc = acc + weighted_values(jnp.exp(s_own - m), vsx[g, past:past + KC, :])
            else:
                acc = weighted_values(jnp.exp(s_own - m), vsx[g, past:past + KC, :])
            o_sel = normalise(acc)

            for r in range(NSA_R):
                hh = g * NSA_R + r
                g_sel = gate[:, NSA_HEADS + hh:NSA_HEADS + hh + 1]
                g_win = gate[:, 2 * NSA_HEADS + hh:2 * NSA_HEADS + hh + 1]
                piece = g_sel * o_sel[r * tq:(r + 1) * tq] + g_win * o_win[r * tq:(r + 1) * tq]
                o_ref[0, :, hh * NSA_DK:(hh + 1) * NSA_DK] = (
                    ocmp_ref[0, :, hh * NSA_DK:(hh + 1) * NSA_DK] + piece).astype(BF16)

    for n_past in range(S // KC):
        pl.when(diag_chunk == n_past)(functools.partial(tile, n_past))


def _selwin(pb3, sel, pf3, ocmp, tq, KC):
    B, S, _ = pb3.shape
    HD = NSA_HEADS * NSA_DK
    WL = WINDOW + tq
    tr = jnp.arange(tq)[:, None]
    kc = jnp.arange(WL)[None, :]
    wmask = jnp.where((kc > tr) & (kc <= tr + WINDOW), 0.0, NEG).astype(F32)
    off = (jnp.arange(KC // tq) * tq)[:, None, None]
    dmask = jnp.where(jnp.arange(KC)[None, None, :] <= off + tr[None], 0.0, NEG).astype(F32)
    kv_spec = lambda off: pl.BlockSpec((1, S, LANES), lambda b, i: (b, 0, off // LANES))
    return pl.pallas_call(
        functools.partial(_selwin_kernel, tq=tq, KC=KC, S=S),
        grid=(B, S // tq),
        in_specs=[pl.BlockSpec((1, tq, HD), lambda b, i: (b, i, PB_NQ // HD)),
                  kv_spec(PB_KS), kv_spec(PB_VS), kv_spec(PB_KW), kv_spec(PB_VW),
                  pl.BlockSpec((1, NSA_GROUPS, tq, LANES), lambda b, i: (b, 0, i, 0)),
                  pl.BlockSpec((tq, WL), lambda b, i: (0, 0)),
                  pl.BlockSpec((KC // tq, tq, KC), lambda b, i: (0, 0, 0)),
                  pl.BlockSpec((1, tq, LANES), lambda b, i: (b, i, PF_NG // LANES)),
                  pl.BlockSpec((1, tq, HD), lambda b, i: (b, i, 0))],
        out_specs=pl.BlockSpec((1, tq, HD), lambda b, i: (b, i, 0)),
        out_shape=jax.ShapeDtypeStruct((B, S, HD), BF16),
        scratch_shapes=[pltpu.VMEM((NSA_GROUPS, S, LANES), BF16), pltpu.VMEM((NSA_GROUPS, S, LANES), BF16),
                        pltpu.VMEM((NSA_GROUPS, WINDOW + S, LANES), BF16),
                        pltpu.VMEM((NSA_GROUPS, WINDOW + S, LANES), BF16)],
        compiler_params=_params("arbitrary", "arbitrary"),
        name="selwin",
    )(pb3, pb3, pb3, pb3, pb3, sel, wmask, dmask, pf3, ocmp)


SUBLANES = 8


def _mix_ffn_kernel(x_ref, ya_ref, yb_ref, ga_ref, gb_ref, wa_ref, wb_ref, wo_ref, gn_ref,
                    wup_ref, cw_ref, cb_ref, wd_ref, gf_ref, o_ref, tail_ref, u_ref, *, tm, tf):
    @pl.when(pl.program_id(1) == 0)
    def _():
        tail_ref[...] = jnp.zeros_like(tail_ref)

    y_a = jnp.dot(ya_ref[0], wa_ref[...], preferred_element_type=F32)
    y_b = jnp.dot(yb_ref[0], wb_ref[...], preferred_element_type=F32)
    merged = (jax.nn.sigmoid(ga_ref[0].astype(F32)) * y_a + jax.nn.sigmoid(gb_ref[0].astype(F32)) * y_b)
    x1 = x_ref[0] + jnp.dot(merged.astype(BF16), wo_ref[...], preferred_element_type=F32)
    o_ref[0] = x1
    h = _rms(x1, gn_ref[...]).astype(BF16)

    top = lax.broadcasted_iota(jnp.int32, (SUBLANES, tf), 0)
    for c in range(D_FF // tf):
        cols = slice(c * tf, (c + 1) * tf)
        a = jnp.dot(h, wup_ref[:, cols], preferred_element_type=F32)
        b = jnp.dot(h, wup_ref[:, D_FF + c * tf:D_FF + (c + 1) * tf], preferred_element_type=F32)
        tail = tail_ref[c]
        tail_ref[c] = a[tm - SUBLANES:]
        ac = cb_ref[:, cols] + cw_ref[CONV_W - 1:CONV_W, cols] * a
        for d in range(1, CONV_W):
            sh = pltpu.roll(a, d, axis=0)
            head = jnp.where(top < d, pltpu.roll(tail, d, axis=0), sh[:SUBLANES])
            sh = jnp.concatenate([head, sh[SUBLANES:]], axis=0)
            ac = ac + cw_ref[CONV_W - 1 - d:CONV_W - d, cols] * sh
        u_ref[:, cols] = (jax.nn.gelu(ac) * b).astype(BF16)
    y = jnp.dot(u_ref[...], wd_ref[...], preferred_element_type=F32)
    o_ref[0] = _rms(o_ref[0] + y, gf_ref[...])


def _mix_ffn(x, ya, yb, pb3, wa, wb, wo, gn, w_up, conv_w, conv_b, w_down, gf, tm, tf):
    B, S, _ = x.shape
    HV = RET_HEADS * RET_DV
    HD = NSA_HEADS * NSA_DK
    rows = lambda width, col=0: pl.BlockSpec((1, tm, width), lambda b, i: (b, i, col))
    whole = lambda *shape: pl.BlockSpec(shape, lambda b, i: (0,) * len(shape), pipeline_mode=pl.Buffered(1))
    return pl.pallas_call(
        functools.partial(_mix_ffn_kernel, tm=tm, tf=tf),
        grid=(B, S // tm),
        in_specs=[rows(D_MODEL), rows(HV), rows(HD),
                  rows(D_MODEL, PB_GA // D_MODEL), rows(D_MODEL, PB_GB // D_MODEL),
                  whole(HV, D_MODEL), whole(HD, D_MODEL), whole(D_MODEL, D_MODEL), whole(1, D_MODEL),
                  whole(D_MODEL, 2 * D_FF), whole(CONV_W, D_FF), whole(1, D_FF), whole(D_FF, D_MODEL),
                  whole(1, D_MODEL)],
        out_specs=rows(D_MODEL),
        out_shape=jax.ShapeDtypeStruct((B, S, D_MODEL), F32),
        scratch_shapes=[pltpu.VMEM((D_FF // tf, SUBLANES, tf), F32), pltpu.VMEM((tm, D_FF), BF16)],
        compiler_params=_params("arbitrary", "arbitrary"),
        name="mix_ffn",
    )(x, ya, yb, pb3, pb3, wa, wb, wo, gn, w_up, conv_w, conv_b, w_down, gf)


def _block_diag_groups(w):
    z = jnp.zeros_like(w)
    return jnp.concatenate([jnp.concatenate([w, z], axis=-1), jnp.concatenate([z, w], axis=-1)], axis=-2)


def kernel(x, norm_mix, w_in, ret_gn_g, cmp_pos_k, cmp_w1_k, cmp_b1_k, cmp_w2_k, cmp_pos_v, cmp_w1_v,
           cmp_b1_v, cmp_w2_v, w_ret_o, w_nsa_o, w_out, norm_ffn, w_up, conv_w, conv_b, w_down,
           norm_final):
    B, S, D = x.shape
    assert D == D_MODEL and NSA_GROUPS == 2 and norm_mix.shape[0] == 1
    assert S % 512 == 0 and S >= 1024 and S // CMP_STRIDE <= LANES and S // SEL_BLOCK <= SEL_LANES
    M = B * S
    x2 = x.reshape(M, D)

    w_bf = lax.optimization_barrier(w_in.astype(BF16))[0]
    w_gates = w_bf[:, W_GA:]

    def cmp_weights(pos, w1, b1, w2):
        return (jnp.tile(pos, (1, NSA_GROUPS)),
                _block_diag_groups(w1.reshape(CMP_BLOCK, NSA_DK, CMP_HIDDEN)).astype(BF16),
                jnp.tile(b1, NSA_GROUPS)[None, :],
                _block_diag_groups(w2).astype(BF16))

    posk, w1k, b1k, w2k = cmp_weights(cmp_pos_k[0], cmp_w1_k[0], cmp_b1_k[0], cmp_w2_k[0])
    posv, w1v, b1v, w2v = cmp_weights(cmp_pos_v[0], cmp_w1_v[0], cmp_b1_v[0], cmp_w2_v[0])

    pb, pf, ya = _inproj_ret(x2, norm_mix, w_bf, w_gates, ret_gn_g, S, tm=512, C=256)
    pb3 = pb.reshape(B, S, PB_W)
    pf3 = pf.reshape(B, S, PF_W)
    ya = ya.reshape(B, S, -1)

    kcmp, vcmp = _compress(pf3, posk, posv, w1k, w1v, b1k, b1v, w2k, w2v)
    ocmp, sel = _cmpattn(pb3, kcmp, vcmp, pf3, tq=1024)
    yb = _selwin(pb3, sel, pf3, ocmp, tq=256, KC=512)

    return _mix_ffn(x, ya, yb, pb3, w_ret_o[0].astype(BF16), w_nsa_o[0].astype(BF16),
                    w_out[0].astype(BF16), norm_ffn, w_up[0].astype(BF16), conv_w[0], conv_b,
                    w_down[0].astype(BF16), norm_final[None, :], tm=512, tf=256)
```

```python
import functools
import math

import numpy as np
import jax
import jax.numpy as jnp
from jax import lax
from jax.experimental import pallas as pl
from jax.experimental.pallas import tpu as pltpu

F32 = jnp.float32
BF16 = jnp.bfloat16

D_MODEL = 1024
RET_HEADS = 4
RET_DK = 128
RET_DV = 256
NSA_HEADS = 8
NSA_GROUPS = 2
NSA_R = NSA_HEADS // NSA_GROUPS
NSA_DK = 64
CMP_BLOCK = 32
CMP_STRIDE = 16
CMP_HIDDEN = 256
SEL_BLOCK = 64
SEL_TOPK = 8
WINDOW = 512
D_FF = 2816
CONV_W = 3
EPS = 1e-6
NEG = -1e30
FORCE = 1e9
LOWEST = -3e38

IN_SIZES = (RET_HEADS * RET_DK, RET_HEADS * RET_DK, RET_HEADS * RET_DV, RET_HEADS * RET_DV,
            NSA_HEADS * NSA_DK,
            NSA_GROUPS * NSA_DK, NSA_GROUPS * NSA_DK, NSA_GROUPS * NSA_DK,
            NSA_GROUPS * NSA_DK, NSA_GROUPS * NSA_DK, NSA_GROUPS * NSA_DK,
            3 * NSA_HEADS, D_MODEL, D_MODEL)

LANES = 128
VMEM_LIMIT = 56 * 1024 * 1024

_W_OFF = np.concatenate([[0], np.cumsum(IN_SIZES)]).tolist()
W_NQ, W_KCR, W_KS, W_NG, W_GA = _W_OFF[4], _W_OFF[5], _W_OFF[7], _W_OFF[11], _W_OFF[12]

RT_RQ, RT_RK, RT_RV, RT_RG = 0, 512, 1024, 2048
RT_W = 3072
PB_NQ, PB_KS, PB_VS, PB_KW, PB_VW, PB_GA, PB_GB = 0, 512, 640, 768, 896, 1024, 2048
PB_W = 3072
PF_KCR, PF_VCR, PF_NG = 0, 128, 256
PF_W = 384

NT_DIMS = (((1,), (1,)), ((), ()))
TN_DIMS = (((0,), (0,)), ((), ()))


def _params(*sem):
    return pltpu.CompilerParams(dimension_semantics=sem, vmem_limit_bytes=VMEM_LIMIT)


def _rms(x, g):
    return x * lax.rsqrt(jnp.mean(x * x, axis=-1, keepdims=True) + EPS) * g


def _split3(x):
    hi = x.astype(BF16)
    rest = x - hi.astype(F32)
    mid = rest.astype(BF16)
    return hi, mid, (rest - mid.astype(F32)).astype(BF16)


def _inproj_ret_kernel(x_ref, g_ref, w_ref, wg_ref, gn_ref, pb_ref, pf_ref, ya_ref,
                       stash0, stash1, r_ref, dmask_ref, xi_ref, zeta_ref, *, tm, C, S):
    log_gamma = [float(np.log1p(-np.exp2(np.float32(-5.0 - h)))) for h in range(RET_HEADS)]
    scale = RET_DK ** -0.5
    i = pl.program_id(0)

    @pl.when(i == 0)
    def _():
        diff = (lax.broadcasted_iota(jnp.int32, (C, C), 0)
                - lax.broadcasted_iota(jnp.int32, (C, C), 1)).astype(F32)
        pos_v = lax.broadcasted_iota(jnp.int32, (C, RET_DV), 0).astype(F32)
        pos_k = lax.broadcasted_iota(jnp.int32, (C, RET_DK), 0).astype(F32)
        for h in range(RET_HEADS):
            lg = log_gamma[h]
            dmask_ref[h] = jnp.where(diff >= 0, jnp.exp(lg * jnp.maximum(diff, 0.0)), 0.0) * scale
            xi_ref[h] = jnp.exp(lg * (pos_v + 1.0))
            zeta_ref[h] = jnp.exp(lg * (C - 1.0 - pos_k)) * scale
        stash1[...] = jnp.zeros_like(stash1)
        r_ref[...] = jnp.zeros_like(r_ref)

    seq_start = lax.rem(i + (S // tm) - 1, S // tm) == 0

    def step(new_ref, old_ref):
        hb = _rms(x_ref[...], g_ref[...]).astype(BF16)

        def project(w, c0, c1):
            return jnp.dot(hb, w[:, c0:c1], preferred_element_type=F32)

        def stash_cols(c0):
            new_ref[:, c0:c0 + 512] = project(w_ref, c0, c0 + 512).astype(BF16)

        def query_cols():
            pb_ref[:, PB_NQ:PB_KS] = (project(w_ref, W_NQ, W_KCR) * (NSA_DK ** -0.5)).astype(BF16)

        def kv_cols():
            pb_ref[:, PB_KS:PB_GA] = project(w_ref, W_KS, W_NG).astype(BF16)

        def gate_cols(c0):
            pb_ref[:, PB_GA + c0:PB_GA + c0 + 512] = project(wg_ref, c0, c0 + 512).astype(BF16)

        def compress_cols():
            pf_ref[:, PF_KCR:PF_NG] = project(w_ref, W_KCR, W_KS)

        def branch_gate_cols():
            pf_ref[:, PF_NG:PF_W] = project(w_ref, W_NG, W_NG + PF_W - PF_NG)

        def retention(cc, h):
            rows = slice(cc * C, (cc + 1) * C)
            lg = log_gamma[h]
            q = old_ref[rows, RT_RQ + h * RET_DK:RT_RQ + (h + 1) * RET_DK]
            k = old_ref[rows, RT_RK + h * RET_DK:RT_RK + (h + 1) * RET_DK]
            v = old_ref[rows, RT_RV + h * RET_DV:RT_RV + (h + 1) * RET_DV]
            g = old_ref[rows, RT_RG + h * RET_DV:RT_RG + (h + 1) * RET_DV].astype(F32)
            s = lax.dot_general(q, k, NT_DIMS, preferred_element_type=F32)
            o = jnp.dot((s * dmask_ref[h]).astype(BF16), v, preferred_element_type=F32)
            state = r_ref[h]
            if cc == 0:
                state = jnp.where(seq_start, 0.0, state)
            o = o + xi_ref[h] * jnp.dot(q, state.astype(BF16), preferred_element_type=F32)
            kz = (k.astype(F32) * zeta_ref[h]).astype(BF16)
            kv = lax.dot_general(kz, v, TN_DIMS, preferred_element_type=F32)
            r_ref[h] = math.exp(lg * C) * state + kv
            mu = jnp.mean(o, axis=-1, keepdims=True)
            d = o - mu
            var = jnp.mean(d * d, axis=-1, keepdims=True)
            on = d * lax.rsqrt(var + EPS) * gn_ref[:, h * RET_DV:(h + 1) * RET_DV]
            ya_ref[rows, h * RET_DV:(h + 1) * RET_DV] = (g * jax.nn.sigmoid(g) * on).astype(BF16)

        proj = ([functools.partial(stash_cols, c0) for c0 in range(0, RT_W, 512)]
                + [query_cols, kv_cols] + [functools.partial(gate_cols, c0) for c0 in range(0, PB_W - PB_GA, 512)]
                + [compress_cols, branch_gate_cols])
        ret = [functools.partial(retention, cc, h) for cc in range(tm // C) for h in range(RET_HEADS)]
        per = -(-len(proj) // len(ret))
        for n, job in enumerate(ret):
            job()
            for p in proj[n * per:(n + 1) * per]:
                p()

    pl.when(lax.rem(i, 2) == 0)(functools.partial(step, stash0, stash1))
    pl.when(lax.rem(i, 2) == 1)(functools.partial(step, stash1, stash0))


def _inproj_ret(x2, g, w, w_gates, gn_g, S, tm, C):
    M = x2.shape[0]
    nt = M // tm
    HV = RET_HEADS * RET_DV
    cur = lambda i: (jnp.minimum(i, nt - 1), 0)
    prev = lambda i: (jnp.maximum(i - 1, 0), 0)
    const = lambda i: (0, 0)
    return pl.pallas_call(
        functools.partial(_inproj_ret_kernel, tm=tm, C=C, S=S),
        grid=(nt + 1,),
        in_specs=[pl.BlockSpec((tm, D_MODEL), cur),
                  pl.BlockSpec((1, D_MODEL), const),
                  pl.BlockSpec(w.shape, const, pipeline_mode=pl.Buffered(1)),
                  pl.BlockSpec(w_gates.shape, const, pipeline_mode=pl.Buffered(1)),
                  pl.BlockSpec((1, HV), const)],
        out_specs=[pl.BlockSpec((tm, PB_W), cur), pl.BlockSpec((tm, PF_W), cur), pl.BlockSpec((tm, HV), prev)],
        out_shape=[jax.ShapeDtypeStruct((M, PB_W), BF16), jax.ShapeDtypeStruct((M, PF_W), F32),
                   jax.ShapeDtypeStruct((M, HV), BF16)],
        scratch_shapes=[pltpu.VMEM((tm, RT_W), BF16), pltpu.VMEM((tm, RT_W), BF16),
                        pltpu.VMEM((RET_HEADS, RET_DK, RET_DV), F32), pltpu.VMEM((RET_HEADS, C, C), F32),
                        pltpu.VMEM((RET_HEADS, C, RET_DV), F32), pltpu.VMEM((RET_HEADS, C, RET_DK), F32)],
        compiler_params=_params("arbitrary"),
        name="inproj_ret",
    )(x2, g, w, w_gates, gn_g)


def _compress_kernel(kin_ref, vin_ref, posk_ref, posv_ref, w1k_ref, w1v_ref, b1k_ref, b1v_ref,
                     w2k_ref, w2v_ref, ko_ref, vo_ref, *, NP):
    GD = NSA_GROUPS * NSA_DK
    npc = CMP_BLOCK // CMP_STRIDE
    row = lax.broadcasted_iota(jnp.int32, (NP, GD), 0)
    for in_ref, pos_ref, w1_ref, b1_ref, w2_ref, out_ref in (
            (kin_ref, posk_ref, w1k_ref, b1k_ref, w2k_ref, ko_ref),
            (vin_ref, posv_ref, w1v_ref, b1v_ref, w2v_ref, vo_ref)):
        parts = []
        for p in range(npc):
            acc = jnp.zeros((NP, NSA_GROUPS * CMP_HIDDEN), F32)
            for i in range(0, CMP_STRIDE, 2):
                j = p * CMP_STRIDE + i
                toks = [(in_ref[0, pl.ds(i + d, NP, stride=CMP_STRIDE), :] + pos_ref[j + d:j + d + 1, :])
                        .astype(BF16) for d in range(2)]
                acc = acc + jnp.dot(jnp.concatenate(toks, axis=1), w1_ref[j:j + 2].reshape(2 * GD, -1),
                                    preferred_element_type=F32)
            parts.append(acc)
        hidden = parts[0]
        for p in range(1, npc):
            hidden = hidden + pltpu.roll(parts[p], NP - p, axis=0)
        act = jax.nn.gelu(hidden + b1_ref[...]).astype(BF16)
        res = jnp.dot(act, w2_ref[...], preferred_element_type=F32)
        res = jnp.where(row < NP - (npc - 1), res, 0.0)
        if out_ref is vo_ref:
            res = res.T
            for g in range(NSA_GROUPS):
                out_ref[0, g] = res[g * NSA_DK:(g + 1) * NSA_DK, :]
        else:
            for g in range(NSA_GROUPS):
                out_ref[0, g] = res[:, g * NSA_DK:(g + 1) * NSA_DK]


SEL_LANE0 = 64
SEL_LANES = 32
POS_HI_LANE, POS_LO_LANE, PAD_LANE = 96, 97, 98
ONES_LANE = 64
MASK_BIG = 1e30


def _cmpattn_kernel(kin_ref, vin_ref, posk_ref, posv_ref, w1k_ref, w1v_ref, b1k_ref, b1v_ref, w2k_ref, w2v_ref,
                    q_ref, gate_ref, ocmp_ref, sel_ref, kc_ref, vct_ref, *, tq, NP, NS):
    NC = NP - (CMP_BLOCK // CMP_STRIDE - 1)
    tile = pl.program_id(1)

    @pl.when(tile == 0)
    def _():
        _compress_kernel(kin_ref, vin_ref, posk_ref, posv_ref, w1k_ref, w1v_ref, b1k_ref, b1v_ref,
                         w2k_ref, w2v_ref, kc_ref, vct_ref, NP=NP)

    def run(rows):
        _cmpattn_tile(q_ref, kc_ref, vct_ref, gate_ref, ocmp_ref, sel_ref, tile * tq,
                      tq=tq, NP=rows, NC=NC, NS=NS)

    for v in range(pl.cdiv(NP * CMP_STRIDE, tq)):
        pl.when(tile == v)(functools.partial(run, min(NP, (v + 1) * tq // CMP_STRIDE)))


def _cmpattn_tile(q_ref, kc_ref, vct_ref, gate_ref, ocmp_ref, sel_ref, t0, *, tq, NP, NC, NS):
    tcol = t0 + lax.broadcasted_iota(jnp.int32, (NP, tq), 1)
    cidx = lax.broadcasted_iota(jnp.int32, (NP, tq), 0)
    visible = (tcol >= cidx * CMP_STRIDE + (CMP_BLOCK - 1)) & (cidx < NC)
    mask_add = jnp.where(visible, 0.0, NEG)
    block_end = (lax.broadcasted_iota(jnp.int32, (NP, 1), 0) * CMP_STRIDE + (CMP_BLOCK - 1)).astype(F32)
    sees_any = ((t0 + lax.broadcasted_iota(jnp.int32, (1, tq), 1)) >= CMP_BLOCK - 1).astype(F32)
    jj = lax.broadcasted_iota(jnp.int32, (SEL_LANES, NP), 0)
    cc = lax.broadcasted_iota(jnp.int32, (SEL_LANES, NP), 1)
    overlap_t = ((cc * CMP_STRIDE < jj * SEL_BLOCK + SEL_BLOCK)
                 & (cc * CMP_STRIDE + CMP_BLOCK > jj * SEL_BLOCK) & (jj < NS)).astype(BF16)
    blk = lax.broadcasted_iota(jnp.int32, (SEL_LANES, tq), 0)
    blk_f = blk.astype(F32)
    is_blk = blk < NS
    cur = jnp.right_shift(t0 + lax.broadcasted_iota(jnp.int32, (SEL_LANES, tq), 1),
                          SEL_BLOCK.bit_length() - 1)
    o_t = []
    for g in range(NSA_GROUPS):
        kc_parts = _split3(kc_ref[0, g, :NP, :])
        vct = vct_ref[0, g, :, :NP].astype(BF16)
        psum = jnp.zeros((NP, tq), F32)
        for r in range(NSA_R):
            hh = g * NSA_R + r
            q = q_ref[0, :, hh * NSA_DK:(hh + 1) * NSA_DK]
            s = sum(lax.dot_general(part, q, NT_DIMS, preferred_element_type=F32) for part in kc_parts)
            s = s + (2.0 ** -(hh + 1)) * block_end + mask_add
            e = jnp.exp(s - jnp.max(s, axis=0, keepdims=True))
            p = e * (sees_any / jnp.sum(e, axis=0, keepdims=True))
            o_t.append(jnp.dot(vct, p.astype(BF16), preferred_element_type=F32))
            psum = psum + p
        imp = sum(jnp.dot(overlap_t, part, preferred_element_type=F32) for part in _split3(psum))
        imp = jnp.where((blk == 0) | (blk == cur) | (blk == cur - 1), FORCE, imp)
        imp = jnp.where(blk > cur, -FORCE, imp)
        work = jnp.where(is_blk, imp, LOWEST)
        sel = jnp.zeros((SEL_LANES, tq), F32)
        for _ in range(min(SEL_TOPK, NS)):
            top = jnp.max(work, axis=0, keepdims=True)
            first = jnp.min(jnp.where(work == top, blk_f, float(SEL_LANES)), axis=0, keepdims=True)
            pick = blk_f == first
            sel = jnp.where(pick, 1.0, sel)
            work = jnp.where(pick, LOWEST, work)
        sel = jnp.where(is_blk & (blk <= cur), sel - 1.0, jnp.where(is_blk, -1.0, 0.0))
        placed = jnp.concatenate([jnp.zeros((SEL_LANE0, tq), F32), sel,
                                  jnp.zeros((LANES - SEL_LANE0 - SEL_LANES, tq), F32)], axis=0)
        sel_ref[0, g] = placed.T.astype(BF16)
    o = jnp.concatenate(o_t, axis=0).T
    gate = jax.nn.sigmoid(gate_ref[0])
    for hh in range(NSA_HEADS):
        cols = slice(hh * NSA_DK, (hh + 1) * NSA_DK)
        ocmp_ref[0, :, cols] = gate[:, hh:hh + 1] * o[:, cols]


def _cmpattn(pb3, pf3, posk, posv, w1k, w1v, b1k, b1v, w2k, w2v, tq):
    B, S, _ = pb3.shape
    NP = S // CMP_STRIDE
    NS = S // SEL_BLOCK
    HD = NSA_HEADS * NSA_DK
    GD = NSA_GROUPS * NSA_DK
    GH = NSA_GROUPS * CMP_HIDDEN
    const2 = lambda b, i: (0, 0)
    const3 = lambda b, i: (0, 0, 0)
    return pl.pallas_call(
        functools.partial(_cmpattn_kernel, tq=tq, NP=NP, NS=NS),
        grid=(B, S // tq),
        in_specs=[pl.BlockSpec((1, S, GD), lambda b, i: (b, 0, PF_KCR // GD)),
                  pl.BlockSpec((1, S, GD), lambda b, i: (b, 0, PF_VCR // GD)),
                  pl.BlockSpec((CMP_BLOCK, GD), const2), pl.BlockSpec((CMP_BLOCK, GD), const2),
                  pl.BlockSpec((CMP_BLOCK, GD, GH), const3), pl.BlockSpec((CMP_BLOCK, GD, GH), const3),
                  pl.BlockSpec((1, GH), const2), pl.BlockSpec((1, GH), const2),
                  pl.BlockSpec((GH, GD), const2), pl.BlockSpec((GH, GD), const2),
                  pl.BlockSpec((1, tq, HD), lambda b, i: (b, i, PB_NQ // HD)),
                  pl.BlockSpec((1, tq, LANES), lambda b, i: (b, i, PF_NG // LANES))],
        out_specs=[pl.BlockSpec((1, tq, HD), lambda b, i: (b, i, 0)),
                   pl.BlockSpec((1, NSA_GROUPS, tq, LANES), lambda b, i: (b, 0, i, 0))],
        out_shape=[jax.ShapeDtypeStruct((B, S, HD), F32),
                   jax.ShapeDtypeStruct((B, NSA_GROUPS, S, LANES), BF16)],
        scratch_shapes=[pltpu.VMEM((1, NSA_GROUPS, NP, NSA_DK), F32), pltpu.VMEM((1, NSA_GROUPS, NSA_DK, NP), F32)],
        compiler_params=_params("arbitrary", "arbitrary"),
        name="cmpattn",
    )(pf3, pf3, posk, posv, w1k, w1v, b1k, b1v, w2k, w2v, pb3, pf3)


def _selwin_kernel(q_ref, ks_ref, vs_ref, kw_ref, vw_ref, sel_ref, wmask_ref, dmask_ref, gate_ref, ocmp_ref,
                   o_ref, ksx, vsx, kwx, vwx, *, tq, KC, S):
    WL = WINDOW + tq
    RB = 512
    i = pl.program_id(1)
    t0 = pl.multiple_of(i * tq, tq)
    sel_shift = SEL_BLOCK.bit_length() - 1

    @pl.when(i == 0)
    def _build():
        lane = lax.broadcasted_iota(jnp.int32, (RB, LANES), 1)
        is_head = lane < NSA_DK
        pad_k = jnp.where(lane == PAD_LANE, MASK_BIG, 0.0).astype(BF16)
        for g in range(NSA_GROUPS):
            kwx[g, 0:WINDOW, :] = pad_k[:WINDOW]
            vwx[g, 0:WINDOW, :] = jnp.zeros((WINDOW, LANES), BF16)
        for piece in range(S // RB):
            r0 = piece * RB
            pos = r0 + lax.broadcasted_iota(jnp.int32, (RB, LANES), 0)
            alibi = jnp.where(lane == POS_HI_LANE, jnp.right_shift(pos, sel_shift),
                              jnp.where(lane == POS_LO_LANE, pos & (SEL_BLOCK - 1), 0)).astype(F32)
            onehot = jnp.where(lane - SEL_LANE0 == jnp.right_shift(pos, sel_shift), MASK_BIG, 0.0)
            k_extra_win = alibi.astype(BF16)
            k_extra_sel = (alibi + onehot).astype(BF16)
            v_extra = jnp.where(lane == ONES_LANE, 1.0, 0.0).astype(BF16)
            for g in range(NSA_GROUPS):
                def head_lanes(ref):
                    t = ref[0, r0:r0 + RB, :]
                    return t if g == 0 else jnp.concatenate([t[:, NSA_DK:], t[:, :NSA_DK]], axis=1)
                ksx[g, r0:r0 + RB, :] = jnp.where(is_head, head_lanes(ks_ref), k_extra_sel)
                vsx[g, r0:r0 + RB, :] = jnp.where(is_head, head_lanes(vs_ref), v_extra)
                kwx[g, WINDOW + r0:WINDOW + r0 + RB, :] = jnp.where(is_head, head_lanes(kw_ref), k_extra_win)
                vwx[g, WINDOW + r0:WINDOW + r0 + RB, :] = jnp.where(is_head, head_lanes(vw_ref), v_extra)

    gate = jax.nn.sigmoid(gate_ref[0])
    lane = lax.broadcasted_iota(jnp.int32, (tq, LANES), 1)
    is_head = lane < NSA_DK
    diag_chunk = t0 // KC
    wmask = wmask_ref[...][None]
    dmask = dmask_ref[i % (KC // tq)][None]

    def masked_rows(s, mask):
        n = s.shape[-1]
        return (s.reshape(NSA_R, tq, n) + mask).reshape(NSA_R * tq, n)

    def normalise(acc):
        return acc[:, :NSA_DK] / acc[:, ONES_LANE:ONES_LANE + 1]

    def scores(q, k):
        half = q.shape[0] // 2
        return jnp.concatenate([lax.dot_general(q[:half], k, NT_DIMS, preferred_element_type=F32),
                                lax.dot_general(q[half:], k, NT_DIMS, preferred_element_type=F32)], axis=0)

    def weighted_values(p, v):
        half = p.shape[0] // 2
        p = p.astype(BF16)
        return jnp.concatenate([jnp.dot(p[:half], v, preferred_element_type=F32),
                                jnp.dot(p[half:], v, preferred_element_type=F32)], axis=0)

    def query_rows(g):
        sel = sel_ref[0, g].astype(F32)
        rows = []
        for r in range(NSA_R):
            hh = g * NSA_R + r
            qh = q_ref[0, :, (hh // 2) * LANES:(hh // 2 + 1) * LANES].astype(F32)
            if hh % 2:
                qh = jnp.concatenate([qh[:, NSA_DK:], qh[:, :NSA_DK]], axis=1)
            slope = 2.0 ** -(hh + 1)
            extra = jnp.where(lane == POS_HI_LANE, SEL_BLOCK * slope,
                              jnp.where(lane == POS_LO_LANE, slope, jnp.where(lane == PAD_LANE, -1.0, sel)))
            rows.append(jnp.where(is_head, qh, extra).astype(BF16))
        return jnp.concatenate(rows, axis=0)

    def tile(n_past):
        past = n_past * KC
        for g in range(NSA_GROUPS):
            q = query_rows(g)
            s = masked_rows(scores(q, kwx[g, pl.ds(t0, WL), :]), wmask)
            p = jnp.exp(s - jnp.max(s, axis=-1, keepdims=True))
            o_win = normalise(weighted_values(p, vwx[g, pl.ds(t0, WL), :]))

            if n_past:
                s_past = scores(q, ksx[g, 0:past, :])
            s_own = masked_rows(scores(q, ksx[g, past:past + KC, :]), dmask)
            m = jnp.max(s_own, axis=-1, keepdims=True)
            if n_past:
                m = jnp.maximum(m, jnp.max(s_past, axis=-1, keepdims=True))
                acc = weighted_values(jnp.exp(s_past - m), vsx[g, 0:past, :])
                acc = acc + weighted_values(jnp.exp(s_own - m), vsx[g, past:past + KC, :])
            else:
                acc = weighted_values(jnp.exp(s_own - m), vsx[g, past:past + KC, :])
            o_sel = normalise(acc)

            for r in range(NSA_R):
                hh = g * NSA_R + r
                g_sel = gate[:, NSA_HEADS + hh:NSA_HEADS + hh + 1]
                g_win = gate[:, 2 * NSA_HEADS + hh:2 * NSA_HEADS + hh + 1]
                piece = g_sel * o_sel[r * tq:(r + 1) * tq] + g_win * o_win[r * tq:(r + 1) * tq]
                o_ref[0, :, hh * NSA_DK:(hh + 1) * NSA_DK] = (
                    ocmp_ref[0, :, hh * NSA_DK:(hh + 1) * NSA_DK] + piece).astype(BF16)

    for n_past in range(S // KC):
        pl.when(diag_chunk == n_past)(functools.partial(tile, n_past))


def _selwin(pb3, sel, pf3, ocmp, tq, KC):
    B, S, _ = pb3.shape
    HD = NSA_HEADS * NSA_DK
    WL = WINDOW + tq
    tr = jnp.arange(tq)[:, None]
    kc = jnp.arange(WL)[None, :]
    wmask = jnp.where((kc > tr) & (kc <= tr + WINDOW), 0.0, NEG).astype(F32)
    off = (jnp.arange(KC // tq) * tq)[:, None, None]
    dmask = jnp.where(jnp.arange(KC)[None, None, :] <= off + tr[None], 0.0, NEG).astype(F32)
    kv_spec = lambda off: pl.BlockSpec((1, S, LANES), lambda b, i: (b, 0, off // LANES))
    return pl.pallas_call(
        functools.partial(_selwin_kernel, tq=tq, KC=KC, S=S),
        grid=(B, S // tq),
        in_specs=[pl.BlockSpec((1, tq, HD), lambda b, i: (b, i, PB_NQ // HD)),
                  kv_spec(PB_KS), kv_spec(PB_VS), kv_spec(PB_KW), kv_spec(PB_VW),
                  pl.BlockSpec((1, NSA_GROUPS, tq, LANES), lambda b, i: (b, 0, i, 0)),
                  pl.BlockSpec((tq, WL), lambda b, i: (0, 0)),
                  pl.BlockSpec((KC // tq, tq, KC), lambda b, i: (0, 0, 0)),
                  pl.BlockSpec((1, tq, LANES), lambda b, i: (b, i, PF_NG // LANES)),
                  pl.BlockSpec((1, tq, HD), lambda b, i: (b, i, 0))],
        out_specs=pl.BlockSpec((1, tq, HD), lambda b, i: (b, i, 0)),
        out_shape=jax.ShapeDtypeStruct((B, S, HD), BF16),
        scratch_shapes=[pltpu.VMEM((NSA_GROUPS, S, LANES), BF16), pltpu.VMEM((NSA_GROUPS, S, LANES), BF16),
                        pltpu.VMEM((NSA_GROUPS, WINDOW + S, LANES), BF16),
                        pltpu.VMEM((NSA_GROUPS, WINDOW + S, LANES), BF16)],
        compiler_params=_params("arbitrary", "arbitrary"),
        name="selwin",
    )(pb3, pb3, pb3, pb3, pb3, sel, wmask, dmask, pf3, ocmp)


SUBLANES = 8


def _mix_ffn_kernel(x_ref, ya_ref, yb_ref, ga_ref, gb_ref, wa_ref, wb_ref, wo_ref, gn_ref,
                    wup_ref, cw_ref, cb_ref, wd_ref, gf_ref, o_ref, tail_ref, u_ref, *, tm, tf):
    @pl.when(pl.program_id(1) == 0)
    def _():
        tail_ref[...] = jnp.zeros_like(tail_ref)

    y_a = jnp.dot(ya_ref[0], wa_ref[...], preferred_element_type=F32)
    y_b = jnp.dot(yb_ref[0], wb_ref[...], preferred_element_type=F32)
    merged = (jax.nn.sigmoid(ga_ref[0].astype(F32)) * y_a + jax.nn.sigmoid(gb_ref[0].astype(F32)) * y_b)
    x1 = x_ref[0] + jnp.dot(merged.astype(BF16), wo_ref[...], preferred_element_type=F32)
    o_ref[0] = x1
    h = _rms(x1, gn_ref[...]).astype(BF16)

    top = lax.broadcasted_iota(jnp.int32, (SUBLANES, tf), 0)
    for c in range(D_FF // tf):
        cols = slice(c * tf, (c + 1) * tf)
        a = jnp.dot(h, wup_ref[:, cols], preferred_element_type=F32)
        b = jnp.dot(h, wup_ref[:, D_FF + c * tf:D_FF + (c + 1) * tf], preferred_element_type=F32)
        tail = tail_ref[c]
        tail_ref[c] = a[tm - SUBLANES:]
        ac = cb_ref[:, cols] + cw_ref[CONV_W - 1:CONV_W, cols] * a
        for d in range(1, CONV_W):
            sh = pltpu.roll(a, d, axis=0)
            head = jnp.where(top < d, pltpu.roll(tail, d, axis=0), sh[:SUBLANES])
            sh = jnp.concatenate([head, sh[SUBLANES:]], axis=0)
            ac = ac + cw_ref[CONV_W - 1 - d:CONV_W - d, cols] * sh
        u_ref[:, cols] = (jax.nn.gelu(ac) * b).astype(BF16)
    y = jnp.dot(u_ref[...], wd_ref[...], preferred_element_type=F32)
    o_ref[0] = _rms(o_ref[0] + y, gf_ref[...])


def _mix_ffn(x, ya, yb, pb3, wa, wb, wo, gn, w_up, conv_w, conv_b, w_down, gf, tm, tf):
    B, S, _ = x.shape
    HV = RET_HEADS * RET_DV
    HD = NSA_HEADS * NSA_DK
    rows = lambda width, col=0: pl.BlockSpec((1, tm, width), lambda b, i: (b, i, col))
    whole = lambda *shape: pl.BlockSpec(shape, lambda b, i: (0,) * len(shape), pipeline_mode=pl.Buffered(1))
    return pl.pallas_call(
        functools.partial(_mix_ffn_kernel, tm=tm, tf=tf),
        grid=(B, S // tm),
        in_specs=[rows(D_MODEL), rows(HV), rows(HD),
                  rows(D_MODEL, PB_GA // D_MODEL), rows(D_MODEL, PB_GB // D_MODEL),
                  whole(HV, D_MODEL), whole(HD, D_MODEL), whole(D_MODEL, D_MODEL), whole(1, D_MODEL),
                  whole(D_MODEL, 2 * D_FF), whole(CONV_W, D_FF), whole(1, D_FF), whole(D_FF, D_MODEL),
                  whole(1, D_MODEL)],
        out_specs=rows(D_MODEL),
        out_shape=jax.ShapeDtypeStruct((B, S, D_MODEL), F32),
        scratch_shapes=[pltpu.VMEM((D_FF // tf, SUBLANES, tf), F32), pltpu.VMEM((tm, D_FF), BF16)],
        compiler_params=_params("arbitrary", "arbitrary"),
        name="mix_ffn",
    )(x, ya, yb, pb3, pb3, wa, wb, wo, gn, w_up, conv_w, conv_b, w_down, gf)


def _block_diag_groups(w):
    z = jnp.zeros_like(w)
    return jnp.concatenate([jnp.concatenate([w, z], axis=-1), jnp.concatenate([z, w], axis=-1)], axis=-2)


def kernel(x, norm_mix, w_in, ret_gn_g, cmp_pos_k, cmp_w1_k, cmp_b1_k, cmp_w2_k, cmp_pos_v, cmp_w1_v,
           cmp_b1_v, cmp_w2_v, w_ret_o, w_nsa_o, w_out, norm_ffn, w_up, conv_w, conv_b, w_down,
           norm_final):
    B, S, D = x.shape
    assert D == D_MODEL and NSA_GROUPS == 2 and norm_mix.shape[0] == 1
    assert S % 512 == 0 and S >= 1024 and S // CMP_STRIDE <= LANES and S // SEL_BLOCK <= SEL_LANES
    M = B * S
    x2 = x.reshape(M, D)

    w_bf = lax.optimization_barrier(w_in.astype(BF16))[0]
    w_gates = w_bf[:, W_GA:]

    def cmp_weights(pos, w1, b1, w2):
        return (jnp.tile(pos, (1, NSA_GROUPS)),
                _block_diag_groups(w1.reshape(CMP_BLOCK, NSA_DK, CMP_HIDDEN)).astype(BF16),
                jnp.tile(b1, NSA_GROUPS)[None, :],
                _block_diag_groups(w2).astype(BF16))

    posk, w1k, b1k, w2k = cmp_weights(cmp_pos_k[0], cmp_w1_k[0], cmp_b1_k[0], cmp_w2_k[0])
    posv, w1v, b1v, w2v = cmp_weights(cmp_pos_v[0], cmp_w1_v[0], cmp_b1_v[0], cmp_w2_v[0])

    pb, pf, ya = _inproj_ret(x2, norm_mix, w_bf, w_gates, ret_gn_g, S, tm=512, C=256)
    pb3 = pb.reshape(B, S, PB_W)
    pf3 = pf.reshape(B, S, PF_W)
    ya = ya.reshape(B, S, -1)

    ocmp, sel = _cmpattn(pb3, pf3, posk, posv, w1k, w1v, b1k, b1v, w2k, w2v, tq=1024)
    yb = _selwin(pb3, sel, pf3, ocmp, tq=256, KC=512)

    return _mix_ffn(x, ya, yb, pb3, w_ret_o[0].astype(BF16), w_nsa_o[0].astype(BF16),
                    w_out[0].astype(BF16), norm_ffn, w_up[0].astype(BF16), conv_w[0], conv_b,
                    w_down[0].astype(BF16), norm_final[None, :], tm=512, tf=256)
```
